```python
import math
import jax, jax.numpy as jnp
from jax import lax
import numpy as np

D_MODEL = 2048
BATCH = 8
SEQ = 2048
DEPTH = 2

HEAD_DIM = 64
SWA_HEADS = 16
SWA_KV_HEADS = 2
SWA_WINDOW = 128
DSA_HEADS = 8
DSA_KV_RANK = 128
IDX_HEADS = 8
IDX_DIM = 64
DSA_TOPK_MAX = 256
QBLOCK = 128
RET_HEADS = 8
RET_CHUNK = 128
A_WIDTH = SWA_HEADS * HEAD_DIM
B_WIDTH = DSA_HEADS * HEAD_DIM
C_WIDTH = RET_HEADS * HEAD_DIM
MIX_WIDTH = A_WIDTH + B_WIDTH + C_WIDTH
MIX_COLS = (
    A_WIDTH, SWA_KV_HEADS * HEAD_DIM, SWA_KV_HEADS * HEAD_DIM,
    B_WIDTH, DSA_KV_RANK, IDX_HEADS * IDX_DIM, IDX_DIM, IDX_HEADS,
    C_WIDTH, C_WIDTH, C_WIDTH, C_WIDTH,
)
MIX_IN_WIDTH = sum(MIX_COLS)
MIX_SPLITS = [int(v) for v in np.cumsum(MIX_COLS)[:-1]]
D_FF = 5504
N_MOD = 9
EPS = 1e-6

kernel_name = "hymba_style_swa_dsa_retention_macaron"


def rms_norm(x, g):
    xf = x.astype(jnp.float32)
    y = xf * lax.rsqrt(jnp.mean(xf * xf, axis=-1, keepdims=True) + EPS)
    return (y * g.astype(jnp.float32)).astype(x.dtype)


def alibi_slopes(n):
    return jnp.asarray(2.0 ** (-8.0 * np.arange(1, n + 1) / n), dtype=jnp.float32)


def swiglu(h, w_in, w_out):
    gate, up = jnp.split(h @ w_in, 2, axis=-1)
    return (jax.nn.silu(gate) * up) @ w_out


def swa_attention(q, k, v, sinks):
    b, s, _, dh = q.shape
    w = SWA_WINDOW
    nb = s // w
    g = SWA_HEADS // SWA_KV_HEADS
    f32 = jnp.float32
    qb = q.reshape(b, nb, w, SWA_KV_HEADS, g, dh)

    def band(t):
        prev = jnp.pad(t, ((0, 0), (w, 0), (0, 0), (0, 0)))[:, :s]
        return jnp.concatenate([prev.reshape(b, nb, w, SWA_KV_HEADS, dh),
                                t.reshape(b, nb, w, SWA_KV_HEADS, dh)], axis=2)

    kb, vb = band(k), band(v)
    i = jnp.arange(w)[:, None]
    j = jnp.arange(2 * w)[None, :]
    dist = i - j + w
    key_pos = jnp.arange(nb)[:, None, None] * w - w + j[None]
    valid = (dist >= 0) & (dist < w) & (key_pos >= 0)
    slopes = alibi_slopes(SWA_HEADS).reshape(SWA_KV_HEADS, g)[:, :, None, None, None]
    scores = jnp.einsum('bnihgd,bnjhd->bhgnij', qb, kb).astype(f32) * dh ** -0.5
    scores = jnp.where(valid, scores - slopes * dist.astype(f32), -jnp.inf)
    sink = sinks.astype(f32).reshape(SWA_KV_HEADS, g)[:, :, None, None, None]
    m = jnp.maximum(scores.max(axis=-1, keepdims=True), sink)
    e = jnp.exp(scores - m)
    p = e / (e.sum(axis=-1, keepdims=True) + jnp.exp(sink - m))
    out = jnp.einsum('bhgnij,bnjhd->bnihgd', p.astype(v.dtype), vb)
    return out.reshape(b, s, SWA_HEADS * dh)


def dsa_attention(q, ckv, iq, ik, iw, w_uk, w_uv):
    b, s, _, dh = q.shape
    f32 = jnp.float32
    topk = min(DSA_TOPK_MAX, s // 4)
    nb = s // QBLOCK
    slopes = alibi_slopes(DSA_HEADS)[None, :, None, None]
    q_lat = jnp.einsum('bshd,rhd->bshr', q, w_uk)
    key_pos = jnp.arange(s)

    def to_blocks(t):
        return jnp.moveaxis(t.reshape((b, nb, QBLOCK) + t.shape[2:]), 1, 0)

    def one_block(args):
        blk, ql, iqb, iwb = args
        t = blk * QBLOCK + jnp.arange(QBLOCK)
        logit = jnp.einsum('bthd,bsd->bths', iqb, ik).astype(f32) * IDX_DIM ** -0.5
        score = jnp.einsum('bth,bths->bts', iwb.astype(f32) * IDX_HEADS ** -0.5, jax.nn.relu(logit))
        score = jnp.where(key_pos[None, None, :] <= t[None, :, None], score, -jnp.inf)
        _, idx = lax.top_k(score, topk)
        c_sel = jax.vmap(lambda cc, ii: cc[ii])(ckv, idx)
        att = jnp.einsum('bthr,btkr->bhtk', ql, c_sel).astype(f32) * dh ** -0.5
        dist = (t[:, None] - idx).astype(f32)
        att = att - slopes * dist[:, None]
        att = jnp.where((idx <= t[:, None])[:, None], att, -jnp.inf)
        p = jax.nn.softmax(att, axis=-1)
        return jnp.einsum('bhtk,btkr->bthr', p.astype(c_sel.dtype), c_sel)

    o = lax.map(one_block, (jnp.arange(nb), to_blocks(q_lat), to_blocks(iq), to_blocks(iw)))
    o = jnp.moveaxis(o, 0, 1).reshape(b, s, DSA_HEADS, DSA_KV_RANK)
    return jnp.einsum('bshr,rhd->bshd', o, w_uv).reshape(b, s, DSA_HEADS * dh)


def retention(q, k, v, gate, norm_g):
    b, s, h, dh = q.shape
    f32 = jnp.float32
    cs = RET_CHUNK
    nc = s // cs
    lg = jnp.log(1.0 - 2.0 ** (-5.0 - jnp.arange(h, dtype=f32)))
    i = jnp.arange(cs, dtype=f32)
    diff = i[:, None] - i[None, :]
    inner_decay = jnp.where(diff >= 0, jnp.exp(lg[:, None, None] * jnp.maximum(diff, 0.0)), 0.0)
    q_decay = jnp.exp(lg[None, :] * (i[:, None] + 1.0))[None, :, :, None]
    k_decay = jnp.exp(lg[None, :] * (cs - 1.0 - i[:, None]))[None, :, :, None]
    chunk_decay = jnp.exp(lg * cs)[None, :, None, None]

    def to_chunks(t):
        return jnp.moveaxis(t.reshape(b, nc, cs, h, dh), 1, 0)

    def step(state, xs):
        qc, kc, vc = (t.astype(f32) for t in xs)
        kc = kc * dh ** -0.5
        inner = jnp.einsum('bihd,bjhd->bhij', qc, kc) * inner_decay
        out = jnp.einsum('bhij,bjhd->bihd', inner, vc)
        out = out + jnp.einsum('bihd,bhde->bihe', qc, state) * q_decay
        state = state * chunk_decay + jnp.einsum('bjhd,bjhe->bhde', kc * k_decay, vc)
        return state, out

    state0 = jnp.zeros((b, h, dh, dh), f32)
    _, o = lax.scan(step, state0, (to_chunks(q), to_chunks(k), to_chunks(v)))
    o = jnp.moveaxis(o, 0, 1).reshape(b, s, h, dh)
    mu = jnp.mean(o, axis=-1, keepdims=True)
    var = jnp.mean(jnp.square(o - mu), axis=-1, keepdims=True)
    o = ((o - mu) * lax.rsqrt(var + EPS)).reshape(b, s, h * dh) * norm_g.astype(f32)
    return (o * jax.nn.silu(gate.astype(f32))).astype(q.dtype)


def token_mixer(h, w_in, w_out, sinks, kv_norm, w_uk, w_uv, ret_norm):
    b, s, _ = h.shape
    aq, ak, av, bq, bkv, biq, bik, biw, cq, ck, cv, cg = jnp.split(h @ w_in, MIX_SPLITS, axis=-1)

    def heads(t, n):
        return t.reshape(b, s, n, HEAD_DIM)

    oa = swa_attention(heads(aq, SWA_HEADS), heads(ak, SWA_KV_HEADS), heads(av, SWA_KV_HEADS), sinks)
    ob = dsa_attention(heads(bq, DSA_HEADS), rms_norm(bkv, kv_norm),
                       biq.reshape(b, s, IDX_HEADS, IDX_DIM), bik, biw, w_uk, w_uv)
    oc = retention(heads(cq, RET_HEADS), heads(ck, RET_HEADS), heads(cv, RET_HEADS), cg, ret_norm)
    return jnp.concatenate([oa, ob, oc], axis=-1) @ w_out


def sandwich(x, fn, g_pre, g_post, shift, scale, gate, res_w):
    h = rms_norm(x, g_pre) * (1.0 + scale[:, None]) + shift[:, None]
    y = rms_norm(fn(h), g_post)
    return x + res_w * gate[:, None] * y


def setup_inputs(seed: int = 0) -> dict:
    key = jax.random.key(seed)
    ks = jax.random.split(key, 17)
    f32 = jnp.float32
    nrm = lambda k, shp, sc: jax.random.normal(k, shp, f32) * sc
    return {
        "x": nrm(ks[0], (BATCH, SEQ, D_MODEL), 1.0),
        "c": nrm(ks[1], (BATCH, D_MODEL), 1.0),
        "ada_w": nrm(ks[2], (DEPTH, D_MODEL, N_MOD * D_MODEL), 0.5 * D_MODEL ** -0.5),
        "ada_b": nrm(ks[3], (DEPTH, N_MOD * D_MODEL), 0.01),
        "norm_g": 1.0 + nrm(ks[4], (DEPTH, 6, D_MODEL), 0.05),
        "ffn1_w_in": nrm(ks[5], (DEPTH, D_MODEL, 2 * D_FF), D_MODEL ** -0.5),
        "ffn1_w_out": nrm(ks[6], (DEPTH, D_FF, D_MODEL), D_FF ** -0.5),
        "ffn2_w_in": nrm(ks[7], (DEPTH, D_MODEL, 2 * D_FF), D_MODEL ** -0.5),
        "ffn2_w_out": nrm(ks[8], (DEPTH, D_FF, D_MODEL), D_FF ** -0.5),
        "mix_w_in": nrm(ks[9], (DEPTH, D_MODEL, MIX_IN_WIDTH), D_MODEL ** -0.5),
        "mix_w_out": nrm(ks[10], (DEPTH, MIX_WIDTH, D_MODEL), MIX_WIDTH ** -0.5),
        "swa_sinks": nrm(ks[11], (DEPTH, SWA_HEADS), 1.0),
        "dsa_kv_norm": 1.0 + nrm(ks[12], (DEPTH, DSA_KV_RANK), 0.05),
        "dsa_w_uk": nrm(ks[13], (DEPTH, DSA_KV_RANK, DSA_HEADS, HEAD_DIM), DSA_KV_RANK ** -0.5),
        "dsa_w_uv": nrm(ks[14], (DEPTH, DSA_KV_RANK, DSA_HEADS, HEAD_DIM), DSA_KV_RANK ** -0.5),
        "ret_norm": 1.0 + nrm(ks[15], (DEPTH, C_WIDTH), 0.05),
    }


def reference(x, c, ada_w, ada_b, norm_g, ffn1_w_in, ffn1_w_out, ffn2_w_in, ffn2_w_out,
              mix_w_in, mix_w_out, swa_sinks, dsa_kv_norm, dsa_w_uk, dsa_w_uv, ret_norm):
    cond = jax.nn.silu(c)
    for l in range(DEPTH):
        mod = cond @ ada_w[l] + ada_b[l]
        sh1, sc1, gt1, sh2, sc2, gt2, sh3, sc3, gt3 = jnp.split(mod, N_MOD, axis=-1)
        x = sandwich(x, lambda h: swiglu(h, ffn1_w_in[l], ffn1_w_out[l]),
                     norm_g[l, 0], norm_g[l, 1], sh1, sc1, gt1, 0.5)
        x = sandwich(x, lambda h: token_mixer(h, mix_w_in[l], mix_w_out[l], swa_sinks[l], dsa_kv_norm[l],
                                              dsa_w_uk[l], dsa_w_uv[l], ret_norm[l]),
                     norm_g[l, 2], norm_g[l, 3], sh2, sc2, gt2, 1.0)
        x = sandwich(x, lambda h: swiglu(h, ffn2_w_in[l], ffn2_w_out[l]),
                     norm_g[l, 4], norm_g[l, 5], sh3, sc3, gt3, 0.5)
    return x
```

```python
import functools
import math

import jax
import jax.numpy as jnp
from jax import lax
from jax.experimental import pallas as pl
from jax.experimental.pallas import tpu as pltpu

F32 = jnp.float32
BF16 = jnp.bfloat16

LANES = 128
HEAD_DIM = 64
SWA_HEADS = 16
SWA_KV_HEADS = 2
BLOCK = 128
DSA_HEADS = 8
DSA_RANK = 128
IDX_HEADS = 8
IDX_DIM = 64
DSA_TOPK_MAX = 256
RET_HEADS = 8
N_MOD = 9
EPS = 1e-6
NEG = -1e30

A_WIDTH = SWA_HEADS * HEAD_DIM
B_WIDTH = DSA_HEADS * HEAD_DIM
C_WIDTH = RET_HEADS * HEAD_DIM
KV_WIDTH = SWA_KV_HEADS * HEAD_DIM

COL_AQ = 0
COL_BQ = COL_AQ + A_WIDTH
COL_BIQ = COL_BQ + B_WIDTH
COL_CQ = COL_BIQ + IDX_HEADS * IDX_DIM
COL_CK = COL_CQ + C_WIDTH
COL_CV = COL_CK + C_WIDTH
COL_CG = COL_CV + C_WIDTH
COL_AKV = COL_CG + C_WIDTH
COL_BKV = COL_AKV + 2 * KV_WIDTH
COL_BIK = COL_BKV + DSA_RANK
MIX_PAD_WIDTH = COL_BIK + LANES

DSA_KCHUNK = 256
BISECT_ITERS = 30
VMEM_LIMIT = 56 * 1024 * 1024


def _dot(a, b):
    return jnp.dot(a, b, preferred_element_type=F32)


def _dot_nt(a, b):
    return lax.dot_general(a, b, (((1,), (1,)), ((), ())), preferred_element_type=F32)


def _dot_tn(a, b):
    return lax.dot_general(a, b, (((0,), (0,)), ((), ())), preferred_element_type=F32)


def _silu(x):
    return x / (1.0 + jnp.exp(-x))


def _rms(x, g):
    return x * lax.rsqrt(jnp.mean(x * x, axis=-1, keepdims=True) + EPS) * g


def _alibi_slope(h, n):
    return 2.0 ** (-8.0 * (h + 1) / n)


def _ada_kernel(c_ref, w_ref, b_ref, o_ref):
    cond = _silu(c_ref[...]).astype(BF16)
    o_ref[...] = _dot(cond, w_ref[...].astype(BF16)) + b_ref[...]


def _ada_mod(c, ada_w, ada_b):
    depth, d, n = ada_w.shape
    b = c.shape[0]
    tn = 1024
    return pl.pallas_call(
        _ada_kernel,
        grid=(depth, n // tn),
        in_specs=[
            pl.BlockSpec((b, d), lambda l, j: (0, 0)),
            pl.BlockSpec((None, d, tn), lambda l, j: (l, 0, j)),
            pl.BlockSpec((None, 1, tn), lambda l, j: (l, 0, j)),
        ],
        out_specs=pl.BlockSpec((None, b, tn), lambda l, j: (l, 0, j)),
        out_shape=jax.ShapeDtypeStruct((depth, b, n), F32),
        compiler_params=pltpu.CompilerParams(
            dimension_semantics=("arbitrary", "arbitrary"), vmem_limit_bytes=VMEM_LIMIT),
        name="ada_mod",
    )(c, ada_w, ada_b.reshape(depth, 1, n))


def _mod_spec(layer, k, d, grid_rank):
    if grid_rank == 2:
        return pl.BlockSpec((None, None, None, 1, d), lambda b, i: (layer, b, k, 0, 0))
    return pl.BlockSpec((None, None, None, 1, d), lambda b, i, j: (layer, b, k, 0, 0))


def _gain_spec(layer, k, d, grid_rank):
    if grid_rank == 2:
        return pl.BlockSpec((None, None, 1, d), lambda b, i: (layer, k, 0, 0))
    return pl.BlockSpec((None, None, 1, d), lambda b, i, j: (layer, k, 0, 0))


def _ffn_kernel(res_w, x_ref, sh_ref, sc_ref, gt_ref, gpre_ref, gpost_ref, wg_ref, wu_ref, wo_ref,
                o_ref, h_scr, acc_scr):
    f = pl.program_id(2)

    @pl.when(f == 0)
    def _():
        h = _rms(x_ref[...], gpre_ref[...]) * (1.0 + sc_ref[...]) + sh_ref[...]
        h_scr[...] = h.astype(BF16)
        acc_scr[...] = jnp.zeros_like(acc_scr)

    h = h_scr[...]
    act = _silu(_dot(h, wg_ref[...])) * _dot(h, wu_ref[...])
    acc_scr[...] += _dot(act.astype(BF16), wo_ref[...])

    @pl.when(f == pl.num_programs(2) - 1)
    def _():
        y = _rms(acc_scr[...], gpost_ref[...])
        o_ref[...] = x_ref[...] + res_w * gt_ref[...] * y


def _ffn_block(x, mod, norm_g, layer, sub, wg, wu, wo, res_w, tm, tf):
    b, s, d = x.shape
    fp = wg.shape[1]
    mk = 3 * sub
    return pl.pallas_call(
        functools.partial(_ffn_kernel, res_w),
        grid=(b, s // tm, fp // tf),
        in_specs=[
            pl.BlockSpec((None, tm, d), lambda b, i, j: (b, i, 0)),
            _mod_spec(layer, mk, d, 3), _mod_spec(layer, mk + 1, d, 3), _mod_spec(layer, mk + 2, d, 3),
            _gain_spec(layer, 2 * sub, d, 3), _gain_spec(layer, 2 * sub + 1, d, 3),
            pl.BlockSpec((d, tf), lambda b, i, j: (0, j)),
            pl.BlockSpec((d, tf), lambda b, i, j: (0, j)),
            pl.BlockSpec((tf, d), lambda b, i, j: (j, 0)),
        ],
        out_specs=pl.BlockSpec((None, tm, d), lambda b, i, j: (b, i, 0)),
        out_shape=jax.ShapeDtypeStruct((b, s, d), F32),
        scratch_shapes=[pltpu.VMEM((tm, d), BF16), pltpu.VMEM((tm, d), F32)],
        compiler_params=pltpu.CompilerParams(
            dimension_semantics=("parallel", "parallel", "arbitrary"), vmem_limit_bytes=VMEM_LIMIT),
        name=f"ffn_l{layer}_s{sub}",
    )(x, mod, mod, mod, norm_g, norm_g, wg, wu, wo)


def _mix_in_kernel(x_ref, sh_ref, sc_ref, gpre_ref, w_ref, o_ref, h_scr):
    @pl.when(pl.program_id(2) == 0)
    def _():
        h = _rms(x_ref[...], gpre_ref[...]) * (1.0 + sc_ref[...]) + sh_ref[...]
        h_scr[...] = h.astype(BF16)

    o_ref[...] = _dot(h_scr[...], w_ref[...])


def _mix_in(x, mod, norm_g, layer, w, tm, tn):
    b, s, d = x.shape
    n = w.shape[1]
    return pl.pallas_call(
        _mix_in_kernel,
        grid=(b, s // tm, n // tn),
        in_specs=[
            pl.BlockSpec((None, tm, d), lambda b, i, j: (b, i, 0)),
            _mod_spec(layer, 3, d, 3), _mod_spec(layer, 4, d, 3),
            _gain_spec(layer, 2, d, 3),
            pl.BlockSpec((d, tn), lambda b, i, j: (0, j)),
        ],
        out_specs=pl.BlockSpec((None, tm, tn), lambda b, i, j: (b, i, j)),
        out_shape=jax.ShapeDtypeStruct((b, s, n), F32),
        scratch_shapes=[pltpu.VMEM((tm, d), BF16)],
        compiler_params=pltpu.CompilerParams(
            dimension_semantics=("parallel", "parallel", "arbitrary"), vmem_limit_bytes=VMEM_LIMIT),
        name=f"mix_in_l{layer}",
    )(x, mod, mod, norm_g, w)


def _lane_halves(x, first_half_holds_data):
    lane = lax.broadcasted_iota(jnp.int32, x.shape, 1)
    if first_half_holds_data:
        lo = jnp.where(lane < HEAD_DIM, x, 0.0)
        return lo, pltpu.roll(lo, HEAD_DIM, 1)
    hi = jnp.where(lane >= HEAD_DIM, x, 0.0)
    return pltpu.roll(hi, HEAD_DIM, 1), hi


def _swa_kernel(sinks_ref, q_ref, kvc_ref, kvp_ref, o_ref):
    n = pl.program_id(1)
    w = BLOCK
    kcat = jnp.concatenate([kvp_ref[:, :KV_WIDTH], kvc_ref[:, :KV_WIDTH]], axis=0)
    vcat = jnp.concatenate([kvp_ref[:, KV_WIDTH:], kvc_ref[:, KV_WIDTH:]], axis=0)
    k_halves = [tuple(t.astype(BF16) for t in _lane_halves(kcat, hk == 0)) for hk in range(SWA_KV_HEADS)]
    v_halves = [tuple(t.astype(BF16) for t in _lane_halves(vcat, hk == 0)) for hk in range(SWA_KV_HEADS)]

    i = lax.broadcasted_iota(jnp.int32, (w, 2 * w), 0)
    j = lax.broadcasted_iota(jnp.int32, (w, 2 * w), 1)
    dist = i - j + w
    valid = (dist >= 0) & (dist < w) & ((n - 1) * w + j >= 0)
    distf = dist.astype(F32)
    group = SWA_HEADS // SWA_KV_HEADS
    for tile in range(SWA_HEADS // 2):
        q = q_ref[:, tile * LANES:(tile + 1) * LANES].astype(BF16)
        hk = (2 * tile) // group
        out = None
        for par in range(2):
            head = 2 * tile + par
            sc = _dot_nt(q, k_halves[hk][par]) * HEAD_DIM ** -0.5
            sc = jnp.where(valid, sc - _alibi_slope(head, SWA_HEADS) * distf, -jnp.inf)
            sink = sinks_ref[head]
            m = jnp.maximum(jnp.max(sc, axis=-1, keepdims=True), sink)
            e = jnp.exp(sc - m)
            p = e / (jnp.sum(e, axis=-1, keepdims=True) + jnp.exp(sink - m))
            pv = _dot(p.astype(BF16), v_halves[hk][par])
            out = pv if out is None else out + pv
        o_ref[:, tile * LANES:(tile + 1) * LANES] = out.astype(o_ref.dtype)


def _swa(proj, sinks, layer):
    b, s, _ = proj.shape
    nb = s // BLOCK
    kv_blk = COL_AKV // (2 * KV_WIDTH)
    return pl.pallas_call(
        _swa_kernel,
        grid=(b, nb),
        in_specs=[
            pl.BlockSpec(memory_space=pltpu.SMEM),
            pl.BlockSpec((None, BLOCK, A_WIDTH), lambda b, n: (b, n, COL_AQ // A_WIDTH)),
            pl.BlockSpec((None, BLOCK, 2 * KV_WIDTH), lambda b, n: (b, n, kv_blk)),
            pl.BlockSpec((None, BLOCK, 2 * KV_WIDTH), lambda b, n: (b, jnp.maximum(n - 1, 0), kv_blk)),
        ],
        out_specs=pl.BlockSpec((None, BLOCK, A_WIDTH), lambda b, n: (b, n, 0)),
        out_shape=jax.ShapeDtypeStruct((b, s, A_WIDTH), BF16),
        compiler_params=pltpu.CompilerParams(
            dimension_semantics=("parallel", "arbitrary"), vmem_limit_bytes=VMEM_LIMIT),
        name=f"swa_l{layer}",
    )(sinks, proj, proj, proj)


def _dsa_kernel(topk, q_ref, iq_ref, kv_ref, ikw_ref, kvn_ref, wuk_ref, wuv_ref, o_ref,
                ckv_scr, iklo_scr, ikhi_scr, score_scr, bias_scr, qlat_scr, m_scr, l_scr, acc_scr):
    n = pl.program_id(1)
    qb = BLOCK
    kc = DSA_KCHUNK
    nchunks = lax.div(n * qb, kc) + 1

    @pl.when(n == 0)
    def _():
        ckv_scr[...] = _rms(kv_ref[...], kvn_ref[...]).astype(BF16)
        lo, hi = _lane_halves(ikw_ref[...], True)
        iklo_scr[...] = lo.astype(BF16)
        ikhi_scr[...] = hi.astype(BF16)

    row0 = pl.multiple_of(n * qb, qb)
    t_pos = n * qb + lax.broadcasted_iota(jnp.int32, (qb, kc), 0)
    key_in_chunk = lax.broadcasted_iota(jnp.int32, (qb, kc), 1)

    iw = ikw_ref[pl.ds(row0, qb), :]
    iw_scale = IDX_HEADS ** -0.5 * IDX_DIM ** -0.5
    w_cols = [iw[:, IDX_DIM + h:IDX_DIM + h + 1] * iw_scale for h in range(IDX_HEADS)]

    def score_chunk(c, carry):
        off = pl.multiple_of(c * kc, kc)
        ik_lo = iklo_scr[pl.ds(off, kc), :]
        ik_hi = ikhi_scr[pl.ds(off, kc), :]
        sc = jnp.zeros((qb, kc), F32)
        for tile in range(IDX_HEADS // 2):
            iq = iq_ref[:, tile * LANES:(tile + 1) * LANES].astype(BF16)
            sc = sc + w_cols[2 * tile] * jnp.maximum(_dot_nt(iq, ik_lo), 0.0)
            sc = sc + w_cols[2 * tile + 1] * jnp.maximum(_dot_nt(iq, ik_hi), 0.0)
        causal = (off + key_in_chunk) <= t_pos
        score_scr[c] = jnp.where(causal, sc, -jnp.inf)
        return carry

    lax.fori_loop(0, nchunks, score_chunk, 0)

    def count_ge(thr):
        def body(c, cnt):
            return cnt + jnp.sum(jnp.where(score_scr[c] >= thr, 1.0, 0.0), axis=-1, keepdims=True)
        return lax.fori_loop(0, nchunks, body, jnp.zeros((qb, 1), F32))

    def max_below(bound, strict):
        def body(c, mx):
            x = score_scr[c]
            keep = (x < bound) if strict else (x <= bound)
            return jnp.maximum(mx, jnp.max(jnp.where(keep, x, -jnp.inf), axis=-1, keepdims=True))
        return lax.fori_loop(0, nchunks, body, jnp.full((qb, 1), -jnp.inf, F32))

    @pl.when(n * qb + qb <= topk)
    def _():
        def body(c, carry):
            bias_scr[c] = jnp.where(score_scr[c] == -jnp.inf, NEG, 0.0)
            return carry
        lax.fori_loop(0, nchunks, body, 0)

    @pl.when(n * qb + qb > topk)
    def _():
        kf = float(topk)

        def minmax(c, carry):
            mn, mx = carry
            x = score_scr[c]
            mn = jnp.minimum(mn, jnp.min(jnp.where(x == -jnp.inf, jnp.inf, x), axis=-1, keepdims=True))
            return mn, jnp.maximum(mx, jnp.max(x, axis=-1, keepdims=True))

        lo, hi = lax.fori_loop(0, nchunks, minmax,
                               (jnp.full((qb, 1), jnp.inf, F32), jnp.full((qb, 1), -jnp.inf, F32)))

        def bisect(_, carry):
            lo, hi = carry
            mid = 0.5 * (lo + hi)
            ge = count_ge(mid) >= kf
            return jnp.where(ge, mid, lo), jnp.where(ge, hi, mid)

        lo, hi = lax.fori_loop(0, BISECT_ITERS, bisect, (lo, hi))

        def walk_cond(carry):
            return carry[1] > 0.0

        def walk(carry):
            thr, _ = carry
            short = count_ge(thr) < kf
            thr = jnp.where(short, max_below(thr, True), thr)
            return thr, jnp.max(jnp.where(short, 1.0, 0.0))

        thr, _ = lax.while_loop(walk_cond, walk, (max_below(hi, False), jnp.float32(1.0)))

        def count_gt(c, cnt):
            return cnt + jnp.sum(jnp.where(score_scr[c] > thr, 1.0, 0.0), axis=-1, keepdims=True)

        need = kf - lax.fori_loop(0, nchunks, count_gt, jnp.zeros((qb, 1), F32))

        r = lax.broadcasted_iota(jnp.int32, (LANES, LANES), 0)
        col = lax.broadcasted_iota(jnp.int32, (LANES, LANES), 1)
        tri = jnp.where(r <= col, 1.0, 0.0).astype(BF16)
        ones = jnp.ones((LANES, LANES), BF16)

        def select(c, seen):
            x = score_scr[c]
            parts = []
            for t in range(kc // LANES):
                xt = x[:, t * LANES:(t + 1) * LANES]
                eq = xt == thr
                eqb = jnp.where(eq, 1.0, 0.0).astype(BF16)
                rank = seen + _dot(eqb, tri)
                seen = seen + _dot(eqb, ones)
                sel = (xt > thr) | (eq & (rank <= need))
                parts.append(jnp.where(sel, 0.0, NEG))
            bias_scr[c] = jnp.concatenate(parts, axis=1)
            return seen

        lax.fori_loop(0, nchunks, select, jnp.zeros((qb, LANES), F32))

    for h in range(DSA_HEADS):
        tile = h // 2
        q = q_ref[:, tile * LANES:(tile + 1) * LANES].astype(BF16)
        qlat_scr[h] = _dot(q, wuk_ref[h]).astype(BF16)
    m_scr[...] = jnp.full(m_scr.shape, NEG, F32)
    l_scr[...] = jnp.zeros_like(l_scr)
    acc_scr[...] = jnp.zeros_like(acc_scr)

    def attend(c, carry):
        off = pl.multiple_of(c * kc, kc)
        ckv = ckv_scr[pl.ds(off, kc), :]
        bias = bias_scr[c]
        dist = (t_pos - (off + key_in_chunk)).astype(F32)
        for h in range(DSA_HEADS):
            s = _dot_nt(qlat_scr[h], ckv) * HEAD_DIM ** -0.5 - _alibi_slope(h, DSA_HEADS) * dist + bias
            m_old = m_scr[h]
            m_new = jnp.maximum(m_old, jnp.max(s, axis=-1, keepdims=True))
            p = jnp.exp(s - m_new)
            alpha = jnp.exp(m_old - m_new)
            l_scr[h] = alpha * l_scr[h] + jnp.sum(p, axis=-1, keepdims=True)
            acc_scr[h] = alpha * acc_scr[h] + _dot(p.astype(BF16), ckv)
            m_scr[h] = m_new
        return carry

    lax.fori_loop(0, nchunks, attend, 0)

    for tile in range(DSA_HEADS // 2):
        out = None
        for par in range(2):
            h = 2 * tile + par
            o = (acc_scr[h] / l_scr[h]).astype(BF16)
            part = _dot(o, wuv_ref[h])
            out = part if out is None else out + part
        o_ref[:, tile * LANES:(tile + 1) * LANES] = out.astype(o_ref.dtype)


def _dsa(proj, kv_norm, wuk_pad, wuv_pad, layer):
    b, s, _ = proj.shape
    nb = s // BLOCK
    topk = min(DSA_TOPK_MAX, s // 4)
    assert topk % BLOCK == 0
    nck = s // DSA_KCHUNK
    return pl.pallas_call(
        functools.partial(_dsa_kernel, topk),
        grid=(b, nb),
        in_specs=[
            pl.BlockSpec((None, BLOCK, B_WIDTH), lambda b, n: (b, n, COL_BQ // B_WIDTH)),
            pl.BlockSpec((None, BLOCK, B_WIDTH), lambda b, n: (b, n, COL_BIQ // B_WIDTH)),
            pl.BlockSpec((None, s, DSA_RANK), lambda b, n: (b, 0, COL_BKV // DSA_RANK)),
            pl.BlockSpec((None, s, LANES), lambda b, n: (b, 0, COL_BIK // LANES)),
            pl.BlockSpec((1, DSA_RANK), lambda b, n: (0, 0)),
            pl.BlockSpec((DSA_HEADS, LANES, DSA_RANK), lambda b, n: (0, 0, 0)),
            pl.BlockSpec((DSA_HEADS, DSA_RANK, LANES), lambda b, n: (0, 0, 0)),
        ],
        out_specs=pl.BlockSpec((None, BLOCK, B_WIDTH), lambda b, n: (b, n, 0)),
        out_shape=jax.ShapeDtypeStruct((b, s, B_WIDTH), BF16),
        scratch_shapes=[
            pltpu.VMEM((s, DSA_RANK), BF16),
            pltpu.VMEM((s, LANES), BF16),
            pltpu.VMEM((s, LANES), BF16),
            pltpu.VMEM((nck, BLOCK, DSA_KCHUNK), F32),
            pltpu.VMEM((nck, BLOCK, DSA_KCHUNK), F32),
            pltpu.VMEM((DSA_HEADS, BLOCK, DSA_RANK), BF16),
            pltpu.VMEM((DSA_HEADS, BLOCK, 1), F32),
            pltpu.VMEM((DSA_HEADS, BLOCK, 1), F32),
            pltpu.VMEM((DSA_HEADS, BLOCK, DSA_RANK), F32),
        ],
        compiler_params=pltpu.CompilerParams(
            dimension_semantics=("parallel", "arbitrary"), vmem_limit_bytes=VMEM_LIMIT),
        name=f"dsa_l{layer}",
    )(proj, proj, proj, proj, kv_norm, wuk_pad, wuv_pad)


def _ret_kernel(q_ref, k_ref, v_ref, g_ref, gn_ref, o_ref, state_scr):
    cs = BLOCK

    @pl.when(pl.program_id(1) == 0)
    def _():
        state_scr[...] = jnp.zeros_like(state_scr)

    row = lax.broadcasted_iota(jnp.int32, (cs, LANES), 0)
    lane = lax.broadcasted_iota(jnp.int32, (cs, LANES), 1)
    first = lane < HEAD_DIM
    rowf = row.astype(F32)
    diff = (row - lane).astype(F32)
    same_head = (row < HEAD_DIM) == first
    seg_mean = jnp.where(same_head, 1.0 / HEAD_DIM, 0.0).astype(BF16)

    def seg_mean_dot(x):
        x_hi = x.astype(BF16)
        x_lo = (x - x_hi.astype(F32)).astype(BF16)
        return _dot(x_hi, seg_mean) + _dot(x_lo, seg_mean)

    for tile in range(RET_HEADS // 2):
        cols = slice(tile * LANES, (tile + 1) * LANES)
        lg = [math.log(1.0 - 2.0 ** (-5.0 - (2 * tile + par))) for par in range(2)]
        lg_lane = jnp.where(first, lg[0], lg[1])
        q = q_ref[:, cols].astype(BF16)
        k = k_ref[:, cols] * HEAD_DIM ** -0.5
        v = v_ref[:, cols]
        state = state_scr[tile]
        out = _dot(q, state.astype(BF16)) * jnp.exp(lg_lane * (rowf + 1.0))
        for par in range(2):
            keep = first if par == 0 else ~first
            decay = jnp.where(diff >= 0, jnp.exp(lg[par] * jnp.maximum(diff, 0.0)), 0.0)
            inner = _dot_nt(q, jnp.where(keep, k, 0.0).astype(BF16)) * decay
            out = out + _dot(inner.astype(BF16), jnp.where(keep, v, 0.0).astype(BF16))
        k_dec = (k * jnp.exp(lg_lane * (cs - 1.0 - rowf))).astype(BF16)
        kv = _dot_tn(k_dec, v.astype(BF16))
        state_scr[tile] = state * jnp.exp(lg_lane * cs) + jnp.where(same_head, kv, 0.0)

        mu = seg_mean_dot(out)
        cen = out - mu
        var = seg_mean_dot(cen * cen)
        y = cen * lax.rsqrt(var + EPS) * gn_ref[:, cols]
        o_ref[:, cols] = (y * _silu(g_ref[:, cols])).astype(o_ref.dtype)


def _retention(proj, ret_norm, layer):
    b, s, _ = proj.shape
    nc = s // BLOCK
    spec = lambda col: pl.BlockSpec((None, BLOCK, C_WIDTH), lambda b, c: (b, c, col // C_WIDTH))
    return pl.pallas_call(
        _ret_kernel,
        grid=(b, nc),
        in_specs=[spec(COL_CQ), spec(COL_CK), spec(COL_CV), spec(COL_CG),
                  pl.BlockSpec((1, C_WIDTH), lambda b, c: (0, 0))],
        out_specs=pl.BlockSpec((None, BLOCK, C_WIDTH), lambda b, c: (b, c, 0)),
        out_shape=jax.ShapeDtypeStruct((b, s, C_WIDTH), BF16),
        scratch_shapes=[pltpu.VMEM((RET_HEADS // 2, LANES, LANES), F32)],
        compiler_params=pltpu.CompilerParams(
            dimension_semantics=("parallel", "arbitrary"), vmem_limit_bytes=VMEM_LIMIT),
        name=f"ret_l{layer}",
    )(proj, proj, proj, proj, ret_norm)


def _mix_out_kernel(x_ref, gt_ref, gpost_ref, oa_ref, ob_ref, oc_ref, w_ref, o_ref):
    y = _dot(oa_ref[...], w_ref[:A_WIDTH, :])
    y = y + _dot(ob_ref[...], w_ref[A_WIDTH:A_WIDTH + B_WIDTH, :])
    y = y + _dot(oc_ref[...], w_ref[A_WIDTH + B_WIDTH:, :])
    o_ref[...] = x_ref[...] + gt_ref[...] * _rms(y, gpost_ref[...])


def _mix_out(x, mod, norm_g, layer, oa, ob, oc, w, tm):
    b, s, d = x.shape
    row = lambda width: pl.BlockSpec((None, tm, width), lambda b, i: (b, i, 0))
    return pl.pallas_call(
        _mix_out_kernel,
        grid=(b, s // tm),
        in_specs=[
            row(d), _mod_spec(layer, 5, d, 2), _gain_spec(layer, 3, d, 2),
            row(A_WIDTH), row(B_WIDTH), row(C_WIDTH),
            pl.BlockSpec(w.shape, lambda b, i: (0, 0)),
        ],
        out_specs=row(d),
        out_shape=jax.ShapeDtypeStruct((b, s, d), F32),
        compiler_params=pltpu.CompilerParams(
            dimension_semantics=("parallel", "parallel"), vmem_limit_bytes=VMEM_LIMIT),
        name=f"mix_out_l{layer}",
    )(x, mod, norm_g, oa, ob, oc, w)


def _prep_ffn(w_in, w_out, tf):
    d_ff = w_out.shape[0]
    pad = (-d_ff) % tf
    wg = jnp.pad(w_in[:, :d_ff], ((0, 0), (0, pad))).astype(BF16)
    wu = jnp.pad(w_in[:, d_ff:], ((0, 0), (0, pad))).astype(BF16)
    wo = jnp.pad(w_out, ((0, pad), (0, 0))).astype(BF16)
    return wg, wu, wo


def _prep_mix_in(w):
    sizes = [A_WIDTH, KV_WIDTH, KV_WIDTH, B_WIDTH, DSA_RANK, IDX_HEADS * IDX_DIM, IDX_DIM, IDX_HEADS,
             C_WIDTH, C_WIDTH, C_WIDTH, C_WIDTH]
    starts = [0]
    for sz in sizes:
        starts.append(starts[-1] + sz)
    aq, ak, av, bq, bkv, biq, bik, biw, cq, ck, cv, cg = [w[:, a:a + sz] for a, sz in zip(starts, sizes)]
    tail = jnp.zeros((w.shape[0], LANES - IDX_DIM - IDX_HEADS), w.dtype)
    return jnp.concatenate([aq, bq, biq, cq, ck, cv, cg, ak, av, bkv, bik, biw, tail], axis=1).astype(BF16)


def _prep_dsa_up(w_uk, w_uv):
    r, h, dh = w_uk.shape
    uk = jnp.transpose(w_uk, (1, 2, 0))
    uv = jnp.transpose(w_uv, (1, 0, 2))
    odd = (jnp.arange(h) % 2 == 1)[:, None, None]
    zk = jnp.zeros_like(uk)
    zv = jnp.zeros_like(uv)
    uk_pad = jnp.where(odd, jnp.concatenate([zk, uk], axis=1), jnp.concatenate([uk, zk], axis=1))
    uv_pad = jnp.where(odd, jnp.concatenate([zv, uv], axis=2), jnp.concatenate([uv, zv], axis=2))
    return uk_pad.astype(BF16), uv_pad.astype(BF16)


FFN_TM = 512
FFN_TF = 512
MIX_TM = 512
MIX_TN = 512


def kernel(x, c, ada_w, ada_b, norm_g, ffn1_w_in, ffn1_w_out, ffn2_w_in, ffn2_w_out,
           mix_w_in, mix_w_out, swa_sinks, dsa_kv_norm, dsa_w_uk, dsa_w_uv, ret_norm):
    depth = ada_w.shape[0]
    b, s, d = x.shape
    assert s % FFN_TM == 0 and s % DSA_KCHUNK == 0 and d % LANES == 0
    mod = _ada_mod(c, ada_w, ada_b).reshape(depth, b, N_MOD, 1, d)
    gains = norm_g.reshape(depth, norm_g.shape[1], 1, d)
    for l in range(depth):
        wg, wu, wo = _prep_ffn(ffn1_w_in[l], ffn1_w_out[l], FFN_TF)
        x = _ffn_block(x, mod, gains, l, 0, wg, wu, wo, 0.5, FFN_TM, FFN_TF)

        proj = _mix_in(x, mod, gains, l, _prep_mix_in(mix_w_in[l]), MIX_TM, MIX_TN)
        oa = _swa(proj, swa_sinks[l], l)
        uk_pad, uv_pad = _prep_dsa_up(dsa_w_uk[l], dsa_w_uv[l])
        ob = _dsa(proj, dsa_kv_norm[l].reshape(1, -1), uk_pad, uv_pad, l)
        oc = _retention(proj, ret_norm[l].reshape(1, -1), l)
        x = _mix_out(x, mod, gains, l, oa, ob, oc, mix_w_out[l].astype(BF16), MIX_TM)

        wg, wu, wo = _prep_ffn(ffn2_w_in[l], ffn2_w_out[l], FFN_TF)
        x = _ffn_block(x, mod, gains, l, 2, wg, wu, wo, 0.5, FFN_TM, FFN_TF)
    return x
```

```python
import functools
import math

import jax
import jax.numpy as jnp
from jax import lax
from jax.experimental import pallas as pl
from jax.experimental.pallas import tpu as pltpu

F32 = jnp.float32
BF16 = jnp.bfloat16

LANES = 128
HEAD_DIM = 64
SWA_HEADS = 16
SWA_KV_HEADS = 2
BLOCK = 128
DSA_HEADS = 8
DSA_RANK = 128
IDX_HEADS = 8
IDX_DIM = 64
DSA_TOPK_MAX = 256
RET_HEADS = 8
N_MOD = 9
EPS = 1e-6
NEG = -1e30

A_WIDTH = SWA_HEADS * HEAD_DIM
B_WIDTH = DSA_HEADS * HEAD_DIM
C_WIDTH = RET_HEADS * HEAD_DIM
KV_WIDTH = SWA_KV_HEADS * HEAD_DIM

COL_AQ = 0
COL_BQ = COL_AQ + A_WIDTH
COL_BIQ = COL_BQ + B_WIDTH
COL_CQ = COL_BIQ + IDX_HEADS * IDX_DIM
COL_CK = COL_CQ + C_WIDTH
COL_CV = COL_CK + C_WIDTH
COL_CG = COL_CV + C_WIDTH
COL_AKV = COL_CG + C_WIDTH
COL_BKV = COL_AKV + 2 * KV_WIDTH
COL_BIK = COL_BKV + DSA_RANK
MIX_PAD_WIDTH = COL_BIK + LANES

DSA_KCHUNK = 256
BISECT_ITERS = 24
VMEM_LIMIT = 56 * 1024 * 1024


def _dot(a, b):
    return jnp.dot(a, b, preferred_element_type=F32)


def _dot_nt(a, b):
    return lax.dot_general(a, b, (((1,), (1,)), ((), ())), preferred_element_type=F32)


def _dot_tn(a, b):
    return lax.dot_general(a, b, (((0,), (0,)), ((), ())), preferred_element_type=F32)


def _silu(x):
    return x / (1.0 + jnp.exp(-x))


def _rms(x, g):
    return x * lax.rsqrt(jnp.mean(x * x, axis=-1, keepdims=True) + EPS) * g


def _alibi_slope(h, n):
    return 2.0 ** (-8.0 * (h + 1) / n)


def _ada_kernel(c_ref, w_ref, b_ref, o_ref):
    cond = _silu(c_ref[...]).astype(BF16)
    o_ref[...] = _dot(cond, w_ref[...].astype(BF16)) + b_ref[...]


def _ada_mod(c, ada_w, ada_b):
    depth, d, n = ada_w.shape
    b = c.shape[0]
    tn = 1024
    return pl.pallas_call(
        _ada_kernel,
        grid=(depth, n // tn),
        in_specs=[
            pl.BlockSpec((b, d), lambda l, j: (0, 0)),
            pl.BlockSpec((None, d, tn), lambda l, j: (l, 0, j)),
            pl.BlockSpec((None, 1, tn), lambda l, j: (l, 0, j)),
        ],
        out_specs=pl.BlockSpec((None, b, tn), lambda l, j: (l, 0, j)),
        out_shape=jax.ShapeDtypeStruct((depth, b, n), F32),
        compiler_params=pltpu.CompilerParams(
            dimension_semantics=("arbitrary", "arbitrary"), vmem_limit_bytes=VMEM_LIMIT),
        name="ada_mod",
    )(c, ada_w, ada_b.reshape(depth, 1, n))


def _mod_spec(layer, k, d, grid_rank):
    if grid_rank == 2:
        return pl.BlockSpec((None, None, None, 1, d), lambda b, i: (layer, b, k, 0, 0))
    return pl.BlockSpec((None, None, None, 1, d), lambda b, i, j: (layer, b, k, 0, 0))


def _gain_spec(layer, k, d, grid_rank):
    if grid_rank == 2:
        return pl.BlockSpec((None, None, 1, d), lambda b, i: (layer, k, 0, 0))
    return pl.BlockSpec((None, None, 1, d), lambda b, i, j: (layer, k, 0, 0))


def _ffn_kernel(res_w, x_ref, sh_ref, sc_ref, gt_ref, gpre_ref, gpost_ref, wg_ref, wu_ref, wo_ref,
                o_ref, h_scr, acc_scr):
    f = pl.program_id(2)

    @pl.when(f == 0)
    def _():
        h = _rms(x_ref[...], gpre_ref[...]) * (1.0 + sc_ref[...]) + sh_ref[...]
        h_scr[...] = h.astype(BF16)
        acc_scr[...] = jnp.zeros_like(acc_scr)

    h = h_scr[...]
    act = _silu(_dot(h, wg_ref[...])) * _dot(h, wu_ref[...])
    acc_scr[...] += _dot(act.astype(BF16), wo_ref[...])

    @pl.when(f == pl.num_programs(2) - 1)
    def _():
        y = _rms(acc_scr[...], gpost_ref[...])
        o_ref[...] = x_ref[...] + res_w * gt_ref[...] * y


def _ffn_block(x, mod, norm_g, layer, sub, wg, wu, wo, res_w, tm, tf):
    b, s, d = x.shape
    fp = wg.shape[1]
    mk = 3 * sub
    return pl.pallas_call(
        functools.partial(_ffn_kernel, res_w),
        grid=(b, s // tm, fp // tf),
        in_specs=[
            pl.BlockSpec((None, tm, d), lambda b, i, j: (b, i, 0)),
            _mod_spec(layer, mk, d, 3), _mod_spec(layer, mk + 1, d, 3), _mod_spec(layer, mk + 2, d, 3),
            _gain_spec(layer, 2 * sub, d, 3), _gain_spec(layer, 2 * sub + 1, d, 3),
            pl.BlockSpec((d, tf), lambda b, i, j: (0, j)),
            pl.BlockSpec((d, tf), lambda b, i, j: (0, j)),
            pl.BlockSpec((tf, d), lambda b, i, j: (j, 0)),
        ],
        out_specs=pl.BlockSpec((None, tm, d), lambda b, i, j: (b, i, 0)),
        out_shape=jax.ShapeDtypeStruct((b, s, d), F32),
        scratch_shapes=[pltpu.VMEM((tm, d), BF16), pltpu.VMEM((tm, d), F32)],
        compiler_params=pltpu.CompilerParams(
            dimension_semantics=("parallel", "parallel", "arbitrary"), vmem_limit_bytes=VMEM_LIMIT),
        name=f"ffn_l{layer}_s{sub}",
    )(x, mod, mod, mod, norm_g, norm_g, wg, wu, wo)


def _mix_in_kernel(x_ref, sh_ref, sc_ref, gpre_ref, w_ref, o_ref, h_scr):
    @pl.when(pl.program_id(2) == 0)
    def _():
        h = _rms(x_ref[...], gpre_ref[...]) * (1.0 + sc_ref[...]) + sh_ref[...]
        h_scr[...] = h.astype(BF16)

    o_ref[...] = _dot(h_scr[...], w_ref[...])


def _mix_in(x, mod, norm_g, layer, w, tm, tn):
    b, s, d = x.shape
    n = w.shape[1]
    return pl.pallas_call(
        _mix_in_kernel,
        grid=(b, s // tm, n // tn),
        in_specs=[
            pl.BlockSpec((None, tm, d), lambda b, i, j: (b, i, 0)),
            _mod_spec(layer, 3, d, 3), _mod_spec(layer, 4, d, 3),
            _gain_spec(layer, 2, d, 3),
            pl.BlockSpec((d, tn), lambda b, i, j: (0, j)),
        ],
        out_specs=pl.BlockSpec((None, tm, tn), lambda b, i, j: (b, i, j)),
        out_shape=jax.ShapeDtypeStruct((b, s, n), F32),
        scratch_shapes=[pltpu.VMEM((tm, d), BF16)],
        compiler_params=pltpu.CompilerParams(
            dimension_semantics=("parallel", "parallel", "arbitrary"), vmem_limit_bytes=VMEM_LIMIT),
        name=f"mix_in_l{layer}",
    )(x, mod, mod, norm_g, w)


def _lane_halves(x, first_half_holds_data):
    lane = lax.broadcasted_iota(jnp.int32, x.shape, 1)
    if first_half_holds_data:
        lo = jnp.where(lane < HEAD_DIM, x, 0.0)
        return lo, pltpu.roll(lo, HEAD_DIM, 1)
    hi = jnp.where(lane >= HEAD_DIM, x, 0.0)
    return pltpu.roll(hi, HEAD_DIM, 1), hi


def _swa_kernel(sinks_ref, q_ref, kvc_ref, kvp_ref, o_ref):
    n = pl.program_id(1)
    w = BLOCK
    kcat = jnp.concatenate([kvp_ref[:, :KV_WIDTH], kvc_ref[:, :KV_WIDTH]], axis=0)
    vcat = jnp.concatenate([kvp_ref[:, KV_WIDTH:], kvc_ref[:, KV_WIDTH:]], axis=0)
    k_halves = [tuple(t.astype(BF16) for t in _lane_halves(kcat, hk == 0)) for hk in range(SWA_KV_HEADS)]
    v_halves = [tuple(t.astype(BF16) for t in _lane_halves(vcat, hk == 0)) for hk in range(SWA_KV_HEADS)]

    i = lax.broadcasted_iota(jnp.int32, (w, 2 * w), 0)
    j = lax.broadcasted_iota(jnp.int32, (w, 2 * w), 1)
    dist = i - j + w
    valid = (dist >= 0) & (dist < w) & ((n - 1) * w + j >= 0)
    distf = dist.astype(F32)
    group = SWA_HEADS // SWA_KV_HEADS
    for tile in range(SWA_HEADS // 2):
        q = q_ref[:, tile * LANES:(tile + 1) * LANES].astype(BF16)
        hk = (2 * tile) // group
        out = None
        for par in range(2):
            head = 2 * tile + par
            sc = _dot_nt(q, k_halves[hk][par]) * HEAD_DIM ** -0.5
            sc = jnp.where(valid, sc - _alibi_slope(head, SWA_HEADS) * distf, -jnp.inf)
            sink = sinks_ref[head]
            m = jnp.maximum(jnp.max(sc, axis=-1, keepdims=True), sink)
            e = jnp.exp(sc - m)
            p = e / (jnp.sum(e, axis=-1, keepdims=True) + jnp.exp(sink - m))
            pv = _dot(p.astype(BF16), v_halves[hk][par])
            out = pv if out is None else out + pv
        o_ref[:, tile * LANES:(tile + 1) * LANES] = out.astype(o_ref.dtype)


def _swa(proj, sinks, layer):
    b, s, _ = proj.shape
    nb = s // BLOCK
    kv_blk = COL_AKV // (2 * KV_WIDTH)
    return pl.pallas_call(
        _swa_kernel,
        grid=(b, nb),
        in_specs=[
            pl.BlockSpec(memory_space=pltpu.SMEM),
            pl.BlockSpec((None, BLOCK, A_WIDTH), lambda b, n: (b, n, COL_AQ // A_WIDTH)),
            pl.BlockSpec((None, BLOCK, 2 * KV_WIDTH), lambda b, n: (b, n, kv_blk)),
            pl.BlockSpec((None, BLOCK, 2 * KV_WIDTH), lambda b, n: (b, jnp.maximum(n - 1, 0), kv_blk)),
        ],
        out_specs=pl.BlockSpec((None, BLOCK, A_WIDTH), lambda b, n: (b, n, 0)),
        out_shape=jax.ShapeDtypeStruct((b, s, A_WIDTH), BF16),
        compiler_params=pltpu.CompilerParams(
            dimension_semantics=("parallel", "arbitrary"), vmem_limit_bytes=VMEM_LIMIT),
        name=f"swa_l{layer}",
    )(sinks, proj, proj, proj)


def _dsa_kernel(topk, q_ref, iq_ref, kv_ref, ikw_ref, kvn_ref, wuk_ref, wuv_ref, o_ref,
                kaug_scr, iklo_scr, ikhi_scr, score_scr, bias_scr, qaug_scr, s_scr, mrun_scr, m_scr,
                lpart_scr, acc_scr):
    n = pl.program_id(1)
    qb = BLOCK
    kc = DSA_KCHUNK
    nchunks = lax.div(n * qb, kc) + 1

    @pl.when(n == 0)
    def _():
        s_len = kaug_scr.shape[0]
        pos = lax.broadcasted_iota(jnp.int32, (s_len, LANES), 0)
        lane = lax.broadcasted_iota(jnp.int32, (s_len, LANES), 1)
        pos_tile = jnp.where(lane == 0, pos >> 8, jnp.where(lane == 1, pos & 255, 0)).astype(F32)
        kaug_scr[:, :DSA_RANK] = _rms(kv_ref[...], kvn_ref[...]).astype(BF16)
        kaug_scr[:, DSA_RANK:] = pos_tile.astype(BF16)
        lo, hi = _lane_halves(ikw_ref[...], True)
        iklo_scr[...] = lo.astype(BF16)
        ikhi_scr[...] = hi.astype(BF16)

    row0 = pl.multiple_of(n * qb, qb)
    t_pos = n * qb + lax.broadcasted_iota(jnp.int32, (qb, kc), 0)
    key_in_chunk = lax.broadcasted_iota(jnp.int32, (qb, kc), 1)

    iw = ikw_ref[pl.ds(row0, qb), :]
    iw_scale = IDX_HEADS ** -0.5 * IDX_DIM ** -0.5
    w_cols = [iw[:, IDX_DIM + h:IDX_DIM + h + 1] * iw_scale for h in range(IDX_HEADS)]

    def score_chunk(c, carry):
        off = pl.multiple_of(c * kc, kc)
        ik_lo = iklo_scr[pl.ds(off, kc), :]
        ik_hi = ikhi_scr[pl.ds(off, kc), :]
        sc = jnp.zeros((qb, kc), F32)
        for tile in range(IDX_HEADS // 2):
            iq = iq_ref[:, tile * LANES:(tile + 1) * LANES].astype(BF16)
            sc = sc + w_cols[2 * tile] * jnp.maximum(_dot_nt(iq, ik_lo), 0.0)
            sc = sc + w_cols[2 * tile + 1] * jnp.maximum(_dot_nt(iq, ik_hi), 0.0)
        causal = (off + key_in_chunk) <= t_pos
        score_scr[c] = jnp.where(causal, sc, -jnp.inf)
        return carry

    lax.fori_loop(0, nchunks, score_chunk, 0)

    def fold_lanes(x, op):
        out = x[:, :LANES]
        for t in range(1, kc // LANES):
            out = op(out, x[:, t * LANES:(t + 1) * LANES])
        return out

    def count_ge(thr, strict=False):
        def body(c, cnt):
            x = score_scr[c]
            hit = (x > thr) if strict else (x >= thr)
            return cnt + fold_lanes(jnp.where(hit, 1.0, 0.0), jnp.add)
        part = lax.fori_loop(0, nchunks, body, jnp.zeros((qb, LANES), F32))
        return jnp.sum(part, axis=-1, keepdims=True)

    def max_below(bound, strict):
        def body(c, mx):
            x = score_scr[c]
            keep = (x < bound) if strict else (x <= bound)
            return jnp.maximum(mx, fold_lanes(jnp.where(keep, x, -jnp.inf), jnp.maximum))
        part = lax.fori_loop(0, nchunks, body, jnp.full((qb, LANES), -jnp.inf, F32))
        return jnp.max(part, axis=-1, keepdims=True)

    @pl.when(n * qb + qb <= topk)
    def _():
        def body(c, carry):
            bias_scr[c] = jnp.where(score_scr[c] == -jnp.inf, NEG, 0.0)
            return carry
        lax.fori_loop(0, nchunks, body, 0)

    @pl.when(n * qb + qb > topk)
    def _():
        kf = float(topk)

        def minmax(c, carry):
            mn, mx = carry
            x = score_scr[c]
            mn = jnp.minimum(mn, fold_lanes(jnp.where(x == -jnp.inf, jnp.inf, x), jnp.minimum))
            return mn, jnp.maximum(mx, fold_lanes(x, jnp.maximum))

        lo, hi = lax.fori_loop(0, nchunks, minmax,
                               (jnp.full((qb, LANES), jnp.inf, F32), jnp.full((qb, LANES), -jnp.inf, F32)))
        lo = jnp.min(lo, axis=-1, keepdims=True)
        hi = jnp.max(hi, axis=-1, keepdims=True)

        def bisect(_, carry):
            lo, hi = carry
            mid = 0.5 * (lo + hi)
            ge = count_ge(mid) >= kf
            return jnp.where(ge, mid, lo), jnp.where(ge, hi, mid)

        lo, hi = lax.fori_loop(0, BISECT_ITERS, bisect, (lo, hi))

        def walk_cond(carry):
            return carry[1] > 0.0

        def walk(carry):
            thr, _ = carry
            short = count_ge(thr) < kf
            thr = jnp.where(short, max_below(thr, True), thr)
            return thr, jnp.max(jnp.where(short, 1.0, 0.0))

        thr, _ = lax.while_loop(walk_cond, walk, (max_below(hi, False), jnp.float32(1.0)))

        need = kf - count_ge(thr, strict=True)

        r = lax.broadcasted_iota(jnp.int32, (LANES, LANES), 0)
        col = lax.broadcasted_iota(jnp.int32, (LANES, LANES), 1)
        tri = jnp.where(r <= col, 1.0, 0.0).astype(BF16)
        ones = jnp.ones((LANES, LANES), BF16)

        def select(c, seen):
            x = score_scr[c]
            parts = []
            for t in range(kc // LANES):
                xt = x[:, t * LANES:(t + 1) * LANES]
                eq = xt == thr
                eqb = jnp.where(eq, 1.0, 0.0).astype(BF16)
                rank = seen + _dot(eqb, tri)
                seen = seen + _dot(eqb, ones)
                sel = (xt > thr) | (eq & (rank <= need))
                parts.append(jnp.where(sel, 0.0, NEG))
            bias_scr[c] = jnp.concatenate(parts, axis=1)
            return seen

        lax.fori_loop(0, nchunks, select, jnp.zeros((qb, LANES), F32))

    lane = lax.broadcasted_iota(jnp.int32, (qb, LANES), 1)
    for h in range(DSA_HEADS):
        tile = h // 2
        rows = slice(h * qb, (h + 1) * qb)
        q = q_ref[:, tile * LANES:(tile + 1) * LANES].astype(BF16)
        slope = _alibi_slope(h, DSA_HEADS)
        qaug_scr[rows, :DSA_RANK] = (_dot(q, wuk_ref[h]) * HEAD_DIM ** -0.5).astype(BF16)
        qaug_scr[rows, DSA_RANK:] = jnp.where(lane == 0, 256.0 * slope,
                                              jnp.where(lane == 1, slope, 0.0)).astype(BF16)
    mrun_scr[...] = jnp.full(mrun_scr.shape, NEG, F32)

    def logits(c, carry):
        off = pl.multiple_of(c * kc, kc)
        s = _dot_nt(qaug_scr[...], kaug_scr[pl.ds(off, kc), :])
        bias = bias_scr[c]
        for h in range(DSA_HEADS):
            rows = slice(h * qb, (h + 1) * qb)
            sh = s[rows] + bias
            s_scr[c, rows, :] = sh
            mrun_scr[rows, :] = jnp.maximum(mrun_scr[rows, :], fold_lanes(sh, jnp.maximum))
        return carry

    lax.fori_loop(0, nchunks, logits, 0)
    m_scr[...] = jnp.max(mrun_scr[...], axis=-1, keepdims=True)
    lpart_scr[...] = jnp.zeros_like(lpart_scr)
    acc_scr[...] = jnp.zeros_like(acc_scr)

    def attend(c, carry):
        off = pl.multiple_of(c * kc, kc)
        p = jnp.exp(s_scr[c] - m_scr[...])
        lpart_scr[...] += fold_lanes(p, jnp.add)
        acc_scr[...] += _dot(p.astype(BF16), kaug_scr[pl.ds(off, kc), :DSA_RANK])
        return carry

    lax.fori_loop(0, nchunks, attend, 0)

    o_all = (acc_scr[...] / jnp.sum(lpart_scr[...], axis=-1, keepdims=True)).astype(BF16)
    for tile in range(DSA_HEADS // 2):
        h = 2 * tile
        out = _dot(o_all[h * qb:(h + 1) * qb], wuv_ref[h]) + _dot(o_all[(h + 1) * qb:(h + 2) * qb], wuv_ref[h + 1])
        o_ref[:, tile * LANES:(tile + 1) * LANES] = out.astype(o_ref.dtype)


def _dsa(proj, kv_norm, wuk_pad, wuv_pad, layer):
    b, s, _ = proj.shape
    nb = s // BLOCK
    topk = min(DSA_TOPK_MAX, s // 4)
    assert topk % BLOCK == 0
    nck = s // DSA_KCHUNK
    rows = DSA_HEADS * BLOCK
    return pl.pallas_call(
        functools.partial(_dsa_kernel, topk),
        grid=(b, nb),
        in_specs=[
            pl.BlockSpec((None, BLOCK, B_WIDTH), lambda b, n: (b, n, COL_BQ // B_WIDTH)),
            pl.BlockSpec((None, BLOCK, B_WIDTH), lambda b, n: (b, n, COL_BIQ // B_WIDTH)),
            pl.BlockSpec((None, s, DSA_RANK), lambda b, n: (b, 0, COL_BKV // DSA_RANK)),
            pl.BlockSpec((None, s, LANES), lambda b, n: (b, 0, COL_BIK // LANES)),
            pl.BlockSpec((1, DSA_RANK), lambda b, n: (0, 0)),
            pl.BlockSpec((DSA_HEADS, LANES, DSA_RANK), lambda b, n: (0, 0, 0)),
            pl.BlockSpec((DSA_HEADS, DSA_RANK, LANES), lambda b, n: (0, 0, 0)),
        ],
        out_specs=pl.BlockSpec((None, BLOCK, B_WIDTH), lambda b, n: (b, n, 0)),
        out_shape=jax.ShapeDtypeStruct((b, s, B_WIDTH), BF16),
        scratch_shapes=[
            pltpu.VMEM((s, DSA_RANK + LANES), BF16),
            pltpu.VMEM((s, LANES), BF16),
            pltpu.VMEM((s, LANES), BF16),
            pltpu.VMEM((nck, BLOCK, DSA_KCHUNK), F32),
            pltpu.VMEM((nck, BLOCK, DSA_KCHUNK), F32),
            pltpu.VMEM((rows, DSA_RANK + LANES), BF16),
            pltpu.VMEM((nck, rows, DSA_KCHUNK), F32),
            pltpu.VMEM((rows, LANES), F32),
            pltpu.VMEM((rows, 1), F32),
            pltpu.VMEM((rows, LANES), F32),
            pltpu.VMEM((rows, DSA_RANK), F32),
        ],
        compiler_params=pltpu.CompilerParams(
            dimension_semantics=("parallel", "arbitrary"), vmem_limit_bytes=VMEM_LIMIT),
        name=f"dsa_l{layer}",
    )(proj, proj, proj, proj, kv_norm, wuk_pad, wuv_pad)


def _ret_kernel(q_ref, k_ref, v_ref, g_ref, gn_ref, o_ref, state_scr):
    cs = BLOCK

    @pl.when(pl.program_id(1) == 0)
    def _():
        state_scr[...] = jnp.zeros_like(state_scr)

    row = lax.broadcasted_iota(jnp.int32, (cs, LANES), 0)
    lane = lax.broadcasted_iota(jnp.int32, (cs, LANES), 1)
    first = lane < HEAD_DIM
    rowf = row.astype(F32)
    diff = (row - lane).astype(F32)
    same_head = (row < HEAD_DIM) == first
    seg_mean = jnp.where(same_head, 1.0 / HEAD_DIM, 0.0).astype(BF16)

    def seg_mean_dot(x):
        x_hi = x.astype(BF16)
        x_lo = (x - x_hi.astype(F32)).astype(BF16)
        return _dot(x_hi, seg_mean) + _dot(x_lo, seg_mean)

    for tile in range(RET_HEADS // 2):
        cols = slice(tile * LANES, (tile + 1) * LANES)
        lg = [math.log(1.0 - 2.0 ** (-5.0 - (2 * tile + par))) for par in range(2)]
        lg_lane = jnp.where(first, lg[0], lg[1])
        q = q_ref[:, cols].astype(BF16)
        k = k_ref[:, cols] * HEAD_DIM ** -0.5
        v = v_ref[:, cols]
        state = state_scr[tile]
        out = _dot(q, state.astype(BF16)) * jnp.exp(lg_lane * (rowf + 1.0))
        for par in range(2):
            keep = first if par == 0 else ~first
            decay = jnp.where(diff >= 0, jnp.exp(lg[par] * jnp.maximum(diff, 0.0)), 0.0)
            inner = _dot_nt(q, jnp.where(keep, k, 0.0).astype(BF16)) * decay
            out = out + _dot(inner.astype(BF16), jnp.where(keep, v, 0.0).astype(BF16))
        k_dec = (k * jnp.exp(lg_lane * (cs - 1.0 - rowf))).astype(BF16)
        kv = _dot_tn(k_dec, v.astype(BF16))
        state_scr[tile] = state * jnp.exp(lg_lane * cs) + jnp.where(same_head, kv, 0.0)

        mu = seg_mean_dot(out)
        cen = out - mu
        var = seg_mean_dot(cen * cen)
        y = cen * lax.rsqrt(var + EPS) * gn_ref[:, cols]
        o_ref[:, cols] = (y * _silu(g_ref[:, cols])).astype(o_ref.dtype)


def _retention(proj, ret_norm, layer):
    b, s, _ = proj.shape
    nc = s // BLOCK
    spec = lambda col: pl.BlockSpec((None, BLOCK, C_WIDTH), lambda b, c: (b, c, col // C_WIDTH))
    return pl.pallas_call(
        _ret_kernel,
        grid=(b, nc),
        in_specs=[spec(COL_CQ), spec(COL_CK), spec(COL_CV), spec(COL_CG),
                  pl.BlockSpec((1, C_WIDTH), lambda b, c: (0, 0))],
        out_specs=pl.BlockSpec((None, BLOCK, C_WIDTH), lambda b, c: (b, c, 0)),
        out_shape=jax.ShapeDtypeStruct((b, s, C_WIDTH), BF16),
        scratch_shapes=[pltpu.VMEM((RET_HEADS // 2, LANES, LANES), F32)],
        compiler_params=pltpu.CompilerParams(
            dimension_semantics=("parallel", "arbitrary"), vmem_limit_bytes=VMEM_LIMIT),
        name=f"ret_l{layer}",
    )(proj, proj, proj, proj, ret_norm)


def _mix_out_kernel(x_ref, gt_ref, gpost_ref, oa_ref, ob_ref, oc_ref, w_ref, o_ref):
    y = _dot(oa_ref[...], w_ref[:A_WIDTH, :])
    y = y + _dot(ob_ref[...], w_ref[A_WIDTH:A_WIDTH + B_WIDTH, :])
    y = y + _dot(oc_ref[...], w_ref[A_WIDTH + B_WIDTH:, :])
    o_ref[...] = x_ref[...] + gt_ref[...] * _rms(y, gpost_ref[...])


def _mix_out(x, mod, norm_g, layer, oa, ob, oc, w, tm):
    b, s, d = x.shape
    row = lambda width: pl.BlockSpec((None, tm, width), lambda b, i: (b, i, 0))
    return pl.pallas_call(
        _mix_out_kernel,
        grid=(b, s // tm),
        in_specs=[
            row(d), _mod_spec(layer, 5, d, 2), _gain_spec(layer, 3, d, 2),
            row(A_WIDTH), row(B_WIDTH), row(C_WIDTH),
            pl.BlockSpec(w.shape, lambda b, i: (0, 0)),
        ],
        out_specs=row(d),
        out_shape=jax.ShapeDtypeStruct((b, s, d), F32),
        compiler_params=pltpu.CompilerParams(
            dimension_semantics=("parallel", "parallel"), vmem_limit_bytes=VMEM_LIMIT),
        name=f"mix_out_l{layer}",
    )(x, mod, norm_g, oa, ob, oc, w)


def _prep_ffn(w_in, w_out, tf):
    d_ff = w_out.shape[0]
    pad = (-d_ff) % tf
    wg = jnp.pad(w_in[:, :d_ff], ((0, 0), (0, pad))).astype(BF16)
    wu = jnp.pad(w_in[:, d_ff:], ((0, 0), (0, pad))).astype(BF16)
    wo = jnp.pad(w_out, ((0, pad), (0, 0))).astype(BF16)
    return wg, wu, wo


def _prep_mix_in(w):
    sizes = [A_WIDTH, KV_WIDTH, KV_WIDTH, B_WIDTH, DSA_RANK, IDX_HEADS * IDX_DIM, IDX_DIM, IDX_HEADS,
             C_WIDTH, C_WIDTH, C_WIDTH, C_WIDTH]
    starts = [0]
    for sz in sizes:
        starts.append(starts[-1] + sz)
    aq, ak, av, bq, bkv, biq, bik, biw, cq, ck, cv, cg = [w[:, a:a + sz] for a, sz in zip(starts, sizes)]
    tail = jnp.zeros((w.shape[0], LANES - IDX_DIM - IDX_HEADS), w.dtype)
    return jnp.concatenate([aq, bq, biq, cq, ck, cv, cg, ak, av, bkv, bik, biw, tail], axis=1).astype(BF16)


def _prep_dsa_up(w_uk, w_uv):
    r, h, dh = w_uk.shape
    uk = jnp.transpose(w_uk, (1, 2, 0))
    uv = jnp.transpose(w_uv, (1, 0, 2))
    odd = (jnp.arange(h) % 2 == 1)[:, None, None]
    zk = jnp.zeros_like(uk)
    zv = jnp.zeros_like(uv)
    uk_pad = jnp.where(odd, jnp.concatenate([zk, uk], axis=1), jnp.concatenate([uk, zk], axis=1))
    uv_pad = jnp.where(odd, jnp.concatenate([zv, uv], axis=2), jnp.concatenate([uv, zv], axis=2))
    return uk_pad.astype(BF16), uv_pad.astype(BF16)


FFN_TM = 512
FFN_TF = 512
MIX_TM = 512
MIX_TN = 512


def kernel(x, c, ada_w, ada_b, norm_g, ffn1_w_in, ffn1_w_out, ffn2_w_in, ffn2_w_out,
           mix_w_in, mix_w_out, swa_sinks, dsa_kv_norm, dsa_w_uk, dsa_w_uv, ret_norm):
    depth = ada_w.shape[0]
    b, s, d = x.shape
    assert s % FFN_TM == 0 and s % DSA_KCHUNK == 0 and d % LANES == 0
    mod = _ada_mod(c, ada_w, ada_b).reshape(depth, b, N_MOD, 1, d)
    gains = norm_g.reshape(depth, norm_g.shape[1], 1, d)
    for l in range(depth):
        wg, wu, wo = _prep_ffn(ffn1_w_in[l], ffn1_w_out[l], FFN_TF)
        x = _ffn_block(x, mod, gains, l, 0, wg, wu, wo, 0.5, FFN_TM, FFN_TF)

        proj = _mix_in(x, mod, gains, l, _prep_mix_in(mix_w_in[l]), MIX_TM, MIX_TN)
        oa = _swa(proj, swa_sinks[l], l)
        uk_pad, uv_pad = _prep_dsa_up(dsa_w_uk[l], dsa_w_uv[l])
        ob = _dsa(proj, dsa_kv_norm[l].reshape(1, -1), uk_pad, uv_pad, l)
        oc = _retention(proj, ret_norm[l].reshape(1, -1), l)
        x = _mix_out(x, mod, gains, l, oa, ob, oc, mix_w_out[l].astype(BF16), MIX_TM)

        wg, wu, wo = _prep_ffn(ffn2_w_in[l], ffn2_w_out[l], FFN_TF)
        x = _ffn_block(x, mod, gains, l, 2, wg, wu, wo, 0.5, FFN_TM, FFN_TF)
    return x
```

```python
import functools
import math

import jax
import jax.numpy as jnp
from jax import lax
from jax.experimental import pallas as pl
from jax.experimental.pallas import tpu as pltpu

F32 = jnp.float32
BF16 = jnp.bfloat16

LANES = 128
HEAD_DIM = 64
SWA_HEADS = 16
SWA_KV_HEADS = 2
BLOCK = 128
DSA_HEADS = 8
DSA_RANK = 128
IDX_HEADS = 8
IDX_DIM = 64
DSA_TOPK_MAX = 256
RET_HEADS = 8
N_MOD = 9
EPS = 1e-6
NEG = -1e30

A_WIDTH = SWA_HEADS * HEAD_DIM
B_WIDTH = DSA_HEADS * HEAD_DIM
C_WIDTH = RET_HEADS * HEAD_DIM
KV_WIDTH = SWA_KV_HEADS * HEAD_DIM

COL_AQ = 0
COL_BQ = COL_AQ + A_WIDTH
COL_BIQ = COL_BQ + B_WIDTH
COL_CQ = COL_BIQ + IDX_HEADS * IDX_DIM
COL_CK = COL_CQ + C_WIDTH
COL_CV = COL_CK + C_WIDTH
COL_CG = COL_CV + C_WIDTH
COL_AKV = COL_CG + C_WIDTH
COL_BKV = COL_AKV + 2 * KV_WIDTH
COL_BIK = COL_BKV + DSA_RANK
MIX_PAD_WIDTH = COL_BIK + LANES

DSA_KCHUNK = 256
BISECT_ITERS = 24
VMEM_LIMIT = 56 * 1024 * 1024


def _dot(a, b):
    return jnp.dot(a, b, preferred_element_type=F32)


def _dot_nt(a, b):
    return lax.dot_general(a, b, (((1,), (1,)), ((), ())), preferred_element_type=F32)


def _dot_tn(a, b):
    return lax.dot_general(a, b, (((0,), (0,)), ((), ())), preferred_element_type=F32)


def _silu(x):
    return x / (1.0 + jnp.exp(-x))


def _rms(x, g):
    return x * lax.rsqrt(jnp.mean(x * x, axis=-1, keepdims=True) + EPS) * g


def _alibi_slope(h, n):
    return 2.0 ** (-8.0 * (h + 1) / n)


def _ada_kernel(c_ref, w_ref, b_ref, o_ref):
    cond = _silu(c_ref[...]).astype(BF16)
    o_ref[...] = _dot(cond, w_ref[...].astype(BF16)) + b_ref[...]


def _ada_mod(c, ada_w, ada_b):
    depth, d, n = ada_w.shape
    b = c.shape[0]
    tn = 1024
    return pl.pallas_call(
        _ada_kernel,
        grid=(depth, n // tn),
        in_specs=[
            pl.BlockSpec((b, d), lambda l, j: (0, 0)),
            pl.BlockSpec((None, d, tn), lambda l, j: (l, 0, j)),
            pl.BlockSpec((None, 1, tn), lambda l, j: (l, 0, j)),
        ],
        out_specs=pl.BlockSpec((None, b, tn), lambda l, j: (l, 0, j)),
        out_shape=jax.ShapeDtypeStruct((depth, b, n), F32),
        compiler_params=pltpu.CompilerParams(
            dimension_semantics=("arbitrary", "arbitrary"), vmem_limit_bytes=VMEM_LIMIT),
        name="ada_mod",
    )(c, ada_w, ada_b.reshape(depth, 1, n))


def _mod_spec(layer, k, d, grid_rank):
    if grid_rank == 2:
        return pl.BlockSpec((None, None, None, 1, d), lambda b, i: (layer, b, k, 0, 0))
    return pl.BlockSpec((None, None, None, 1, d), lambda b, i, j: (layer, b, k, 0, 0))


def _gain_spec(layer, k, d, grid_rank):
    if grid_rank == 2:
        return pl.BlockSpec((None, None, 1, d), lambda b, i: (layer, k, 0, 0))
    return pl.BlockSpec((None, None, 1, d), lambda b, i, j: (layer, k, 0, 0))


def _ffn_kernel(res_w, x_ref, sh_ref, sc_ref, gt_ref, gpre_ref, gpost_ref, wg_ref, wu_ref, wo_ref,
                o_ref, h_scr, acc_scr):
    f = pl.program_id(2)

    @pl.when(f == 0)
    def _():
        h = _rms(x_ref[...], gpre_ref[...]) * (1.0 + sc_ref[...]) + sh_ref[...]
        h_scr[...] = h.astype(BF16)
        acc_scr[...] = jnp.zeros_like(acc_scr)

    h = h_scr[...]
    act = _silu(_dot(h, wg_ref[...])) * _dot(h, wu_ref[...])
    acc_scr[...] += _dot(act.astype(BF16), wo_ref[...])

    @pl.when(f == pl.num_programs(2) - 1)
    def _():
        y = _rms(acc_scr[...], gpost_ref[...])
        o_ref[...] = x_ref[...] + res_w * gt_ref[...] * y


def _ffn_block(x, mod, norm_g, layer, sub, wg, wu, wo, res_w, tm, tf):
    b, s, d = x.shape
    fp = wg.shape[1]
    mk = 3 * sub
    return pl.pallas_call(
        functools.partial(_ffn_kernel, res_w),
        grid=(b, s // tm, fp // tf),
        in_specs=[
            pl.BlockSpec((None, tm, d), lambda b, i, j: (b, i, 0)),
            _mod_spec(layer, mk, d, 3), _mod_spec(layer, mk + 1, d, 3), _mod_spec(layer, mk + 2, d, 3),
            _gain_spec(layer, 2 * sub, d, 3), _gain_spec(layer, 2 * sub + 1, d, 3),
            pl.BlockSpec((d, tf), lambda b, i, j: (0, j)),
            pl.BlockSpec((d, tf), lambda b, i, j: (0, j)),
            pl.BlockSpec((tf, d), lambda b, i, j: (j, 0)),
        ],
        out_specs=pl.BlockSpec((None, tm, d), lambda b, i, j: (b, i, 0)),
        out_shape=jax.ShapeDtypeStruct((b, s, d), F32),
        scratch_shapes=[pltpu.VMEM((tm, d), BF16), pltpu.VMEM((tm, d), F32)],
        compiler_params=pltpu.CompilerParams(
            dimension_semantics=("parallel", "parallel", "arbitrary"), vmem_limit_bytes=VMEM_LIMIT),
        name=f"ffn_l{layer}_s{sub}",
    )(x, mod, mod, mod, norm_g, norm_g, wg, wu, wo)


def _mix_in_kernel(x_ref, sh_ref, sc_ref, gpre_ref, w_ref, o_ref, h_scr):
    @pl.when(pl.program_id(2) == 0)
    def _():
        h = _rms(x_ref[...], gpre_ref[...]) * (1.0 + sc_ref[...]) + sh_ref[...]
        h_scr[...] = h.astype(BF16)

    o_ref[...] = _dot(h_scr[...], w_ref[...])


def _mix_in(x, mod, norm_g, layer, w, tm, tn):
    b, s, d = x.shape
    n = w.shape[1]
    return pl.pallas_call(
        _mix_in_kernel,
        grid=(b, s // tm, n // tn),
        in_specs=[
            pl.BlockSpec((None, tm, d), lambda b, i, j: (b, i, 0)),
            _mod_spec(layer, 3, d, 3), _mod_spec(layer, 4, d, 3),
            _gain_spec(layer, 2, d, 3),
            pl.BlockSpec((d, tn), lambda b, i, j: (0, j)),
        ],
        out_specs=pl.BlockSpec((None, tm, tn), lambda b, i, j: (b, i, j)),
        out_shape=jax.ShapeDtypeStruct((b, s, n), F32),
        scratch_shapes=[pltpu.VMEM((tm, d), BF16)],
        compiler_params=pltpu.CompilerParams(
            dimension_semantics=("parallel", "parallel", "arbitrary"), vmem_limit_bytes=VMEM_LIMIT),
        name=f"mix_in_l{layer}",
    )(x, mod, mod, norm_g, w)


def _lane_halves(x, first_half_holds_data):
    lane = lax.broadcasted_iota(jnp.int32, x.shape, 1)
    if first_half_holds_data:
        lo = jnp.where(lane < HEAD_DIM, x, 0.0)
        return lo, pltpu.roll(lo, HEAD_DIM, 1)
    hi = jnp.where(lane >= HEAD_DIM, x, 0.0)
    return pltpu.roll(hi, HEAD_DIM, 1), hi


def _swa_kernel(sinks_ref, q_ref, kvc_ref, kvp_ref, o_ref):
    n = pl.program_id(1)
    w = BLOCK
    kcat = jnp.concatenate([kvp_ref[:, :KV_WIDTH], kvc_ref[:, :KV_WIDTH]], axis=0)
    vcat = jnp.concatenate([kvp_ref[:, KV_WIDTH:], kvc_ref[:, KV_WIDTH:]], axis=0)
    k_halves = [tuple(t.astype(BF16) for t in _lane_halves(kcat, hk == 0)) for hk in range(SWA_KV_HEADS)]
    v_halves = [tuple(t.astype(BF16) for t in _lane_halves(vcat, hk == 0)) for hk in range(SWA_KV_HEADS)]

    bands = SWA_HEADS // SWA_KV_HEADS // 2
    rows = bands * w
    r = lax.broadcasted_iota(jnp.int32, (rows, 2 * w), 0)
    j = lax.broadcasted_iota(jnp.int32, (rows, 2 * w), 1)
    dist = (r & (w - 1)) - j + w
    valid = (dist >= 0) & (dist < w) & ((n - 1) * w + j >= 0)
    band_step = _alibi_slope(2, SWA_HEADS) / _alibi_slope(0, SWA_HEADS)
    band_scale = jnp.ones((rows, 2 * w), F32)
    for band in range(1, bands):
        band_scale = jnp.where(r >= band * w, band_step ** band, band_scale)
    dist_scaled = dist.astype(F32) * band_scale
    band_col = lax.broadcasted_iota(jnp.int32, (rows, 1), 0)
    for hk in range(SWA_KV_HEADS):
        q = jnp.concatenate([q_ref[:, (bands * hk + band) * LANES:(bands * hk + band + 1) * LANES]
                             for band in range(bands)], axis=0).astype(BF16)
        out = None
        for par in range(2):
            head0 = 2 * bands * hk + par
            sink = jnp.full((rows, 1), sinks_ref[head0], F32)
            for band in range(1, bands):
                sink = jnp.where(band_col >= band * w, sinks_ref[head0 + 2 * band], sink)
            sc = _dot_nt(q, k_halves[hk][par]) * HEAD_DIM ** -0.5
            sc = jnp.where(valid, sc - _alibi_slope(head0, SWA_HEADS) * dist_scaled, -jnp.inf)
            m = jnp.maximum(jnp.max(sc, axis=-1, keepdims=True), sink)
            e = jnp.exp(sc - m)
            p = e / (jnp.sum(e, axis=-1, keepdims=True) + jnp.exp(sink - m))
            pv = _dot(p.astype(BF16), v_halves[hk][par])
            out = pv if out is None else out + pv
        for band in range(bands):
            tile = bands * hk + band
            o_ref[:, tile * LANES:(tile + 1) * LANES] = out[band * w:(band + 1) * w].astype(o_ref.dtype)


def _swa(proj, sinks, layer):
    b, s, _ = proj.shape
    nb = s // BLOCK
    kv_blk = COL_AKV // (2 * KV_WIDTH)
    return pl.pallas_call(
        _swa_kernel,
        grid=(b, nb),
        in_specs=[
            pl.BlockSpec(memory_space=pltpu.SMEM),
            pl.BlockSpec((None, BLOCK, A_WIDTH), lambda b, n: (b, n, COL_AQ // A_WIDTH)),
            pl.BlockSpec((None, BLOCK, 2 * KV_WIDTH), lambda b, n: (b, n, kv_blk)),
            pl.BlockSpec((None, BLOCK, 2 * KV_WIDTH), lambda b, n: (b, jnp.maximum(n - 1, 0), kv_blk)),
        ],
        out_specs=pl.BlockSpec((None, BLOCK, A_WIDTH), lambda b, n: (b, n, 0)),
        out_shape=jax.ShapeDtypeStruct((b, s, A_WIDTH), BF16),
        compiler_params=pltpu.CompilerParams(
            dimension_semantics=("parallel", "arbitrary"), vmem_limit_bytes=VMEM_LIMIT),
        name=f"swa_l{layer}",
    )(sinks, proj, proj, proj)


def _dsa_kernel(topk, q_ref, iq_ref, kv_ref, ikw_ref, kvn_ref, wuk_ref, wuv_ref, o_ref,
                kaug_scr, iklo_scr, ikhi_scr, score_scr, bias_scr, qaug_scr, s_scr, mrun_scr, m_scr,
                lpart_scr, acc_scr):
    n = pl.program_id(1)
    qb = BLOCK
    kc = DSA_KCHUNK
    nchunks = lax.div(n * qb, kc) + 1

    @pl.when(n == 0)
    def _():
        s_len = kaug_scr.shape[0]
        pos = lax.broadcasted_iota(jnp.int32, (s_len, LANES), 0)
        lane = lax.broadcasted_iota(jnp.int32, (s_len, LANES), 1)
        pos_tile = jnp.where(lane == 0, pos >> 8, jnp.where(lane == 1, pos & 255, 0)).astype(F32)
        kaug_scr[:, :DSA_RANK] = _rms(kv_ref[...], kvn_ref[...]).astype(BF16)
        kaug_scr[:, DSA_RANK:] = pos_tile.astype(BF16)
        lo, hi = _lane_halves(ikw_ref[...], True)
        iklo_scr[...] = lo.astype(BF16)
        ikhi_scr[...] = hi.astype(BF16)

    row0 = pl.multiple_of(n * qb, qb)
    t_pos = n * qb + lax.broadcasted_iota(jnp.int32, (qb, kc), 0)
    key_in_chunk = lax.broadcasted_iota(jnp.int32, (qb, kc), 1)

    iw = ikw_ref[pl.ds(row0, qb), :]
    iw_scale = IDX_HEADS ** -0.5 * IDX_DIM ** -0.5
    w_cols = [iw[:, IDX_DIM + h:IDX_DIM + h + 1] * iw_scale for h in range(IDX_HEADS)]

    def score_chunk(c, carry):
        off = pl.multiple_of(c * kc, kc)
        ik_lo = iklo_scr[pl.ds(off, kc), :]
        ik_hi = ikhi_scr[pl.ds(off, kc), :]
        sc = jnp.zeros((qb, kc), F32)
        for tile in range(IDX_HEADS // 2):
            iq = iq_ref[:, tile * LANES:(tile + 1) * LANES].astype(BF16)
            sc = sc + w_cols[2 * tile] * jnp.maximum(_dot_nt(iq, ik_lo), 0.0)
            sc = sc + w_cols[2 * tile + 1] * jnp.maximum(_dot_nt(iq, ik_hi), 0.0)
        causal = (off + key_in_chunk) <= t_pos
        score_scr[c] = jnp.where(causal, sc, -jnp.inf)
        return carry

    lax.fori_loop(0, nchunks, score_chunk, 0)

    def fold_lanes(x, op):
        out = x[:, :LANES]
        for t in range(1, kc // LANES):
            out = op(out, x[:, t * LANES:(t + 1) * LANES])
        return out

    def count_ge(thr, strict=False):
        def body(c, cnt):
            x = score_scr[c]
            hit = (x > thr) if strict else (x >= thr)
            return cnt + fold_lanes(jnp.where(hit, 1.0, 0.0), jnp.add)
        part = lax.fori_loop(0, nchunks, body, jnp.zeros((qb, LANES), F32))
        return jnp.sum(part, axis=-1, keepdims=True)

    def max_below(bound, strict):
        def body(c, mx):
            x = score_scr[c]
            keep = (x < bound) if strict else (x <= bound)
            return jnp.maximum(mx, fold_lanes(jnp.where(keep, x, -jnp.inf), jnp.maximum))
        part = lax.fori_loop(0, nchunks, body, jnp.full((qb, LANES), -jnp.inf, F32))
        return jnp.max(part, axis=-1, keepdims=True)

    @pl.when(n * qb + qb <= topk)
    def _():
        def body(c, carry):
            bias_scr[c] = jnp.where(score_scr[c] == -jnp.inf, NEG, 0.0)
            return carry
        lax.fori_loop(0, nchunks, body, 0)

    @pl.when(n * qb + qb > topk)
    def _():
        kf = float(topk)

        def minmax(c, carry):
            mn, mx = carry
            x = score_scr[c]
            mn = jnp.minimum(mn, fold_lanes(jnp.where(x == -jnp.inf, jnp.inf, x), jnp.minimum))
            return mn, jnp.maximum(mx, fold_lanes(x, jnp.maximum))

        lo, hi = lax.fori_loop(0, nchunks, minmax,
                               (jnp.full((qb, LANES), jnp.inf, F32), jnp.full((qb, LANES), -jnp.inf, F32)))
        lo = jnp.min(lo, axis=-1, keepdims=True)
        hi = jnp.max(hi, axis=-1, keepdims=True)

        def bisect(_, carry):
            lo, hi = carry
            mid = 0.5 * (lo + hi)
            ge = count_ge(mid) >= kf
            return jnp.where(ge, mid, lo), jnp.where(ge, hi, mid)

        lo, hi = lax.fori_loop(0, BISECT_ITERS, bisect, (lo, hi))

        def walk_cond(carry):
            return carry[1] > 0.0

        def walk(carry):
            thr, _ = carry
            short = count_ge(thr) < kf
            thr = jnp.where(short, max_below(thr, True), thr)
            return thr, jnp.max(jnp.where(short, 1.0, 0.0))

        thr, _ = lax.while_loop(walk_cond, walk, (max_below(hi, False), jnp.float32(1.0)))

        need = kf - count_ge(thr, strict=True)

        r = lax.broadcasted_iota(jnp.int32, (LANES, LANES), 0)
        col = lax.broadcasted_iota(jnp.int32, (LANES, LANES), 1)
        tri = jnp.where(r <= col, 1.0, 0.0).astype(BF16)
        ones = jnp.ones((LANES, LANES), BF16)

        def select(c, seen):
            x = score_scr[c]
            parts = []
            for t in range(kc // LANES):
                xt = x[:, t * LANES:(t + 1) * LANES]
                eq = xt == thr
                eqb = jnp.where(eq, 1.0, 0.0).astype(BF16)
                rank = seen + _dot(eqb, tri)
                seen = seen + _dot(eqb, ones)
                sel = (xt > thr) | (eq & (rank <= need))
                parts.append(jnp.where(sel, 0.0, NEG))
            bias_scr[c] = jnp.concatenate(parts, axis=1)
            return seen

        lax.fori_loop(0, nchunks, select, jnp.zeros((qb, LANES), F32))

    lane = lax.broadcasted_iota(jnp.int32, (qb, LANES), 1)
    for h in range(DSA_HEADS):
        tile = h // 2
        rows = slice(h * qb, (h + 1) * qb)
        q = q_ref[:, tile * LANES:(tile + 1) * LANES].astype(BF16)
        slope = _alibi_slope(h, DSA_HEADS)
        qaug_scr[rows, :DSA_RANK] = (_dot(q, wuk_ref[h]) * HEAD_DIM ** -0.5).astype(BF16)
        qaug_scr[rows, DSA_RANK:] = jnp.where(lane == 0, 256.0 * slope,
                                              jnp.where(lane == 1, slope, 0.0)).astype(BF16)
    mrun_scr[...] = jnp.full(mrun_scr.shape, NEG, F32)

    def logits(c, carry):
        off = pl.multiple_of(c * kc, kc)
        s = _dot_nt(qaug_scr[...], kaug_scr[pl.ds(off, kc), :])
        bias = bias_scr[c]
        for h in range(DSA_HEADS):
            rows = slice(h * qb, (h + 1) * qb)
            sh = s[rows] + bias
            s_scr[c, rows, :] = sh
            mrun_scr[rows, :] = jnp.maximum(mrun_scr[rows, :], fold_lanes(sh, jnp.maximum))
        return carry

    lax.fori_loop(0, nchunks, logits, 0)
    m_scr[...] = jnp.max(mrun_scr[...], axis=-1, keepdims=True)
    lpart_scr[...] = jnp.zeros_like(lpart_scr)
    acc_scr[...] = jnp.zeros_like(acc_scr)

    def attend(c, carry):
        off = pl.multiple_of(c * kc, kc)
        p = jnp.exp(s_scr[c] - m_scr[...])
        lpart_scr[...] += fold_lanes(p, jnp.add)
        acc_scr[...] += _dot(p.astype(BF16), kaug_scr[pl.ds(off, kc), :DSA_RANK])
        return carry

    lax.fori_loop(0, nchunks, attend, 0)

    o_all = (acc_scr[...] / jnp.sum(lpart_scr[...], axis=-1, keepdims=True)).astype(BF16)
    for tile in range(DSA_HEADS // 2):
        h = 2 * tile
        out = _dot(o_all[h * qb:(h + 1) * qb], wuv_ref[h]) + _dot(o_all[(h + 1) * qb:(h + 2) * qb], wuv_ref[h + 1])
        o_ref[:, tile * LANES:(tile + 1) * LANES] = out.astype(o_ref.dtype)


def _dsa(proj, kv_norm, wuk_pad, wuv_pad, layer):
    b, s, _ = proj.shape
    nb = s // BLOCK
    topk = min(DSA_TOPK_MAX, s // 4)
    assert topk % BLOCK == 0
    nck = s // DSA_KCHUNK
    rows = DSA_HEADS * BLOCK
    return pl.pallas_call(
        functools.partial(_dsa_kernel, topk),
        grid=(b, nb),
        in_specs=[
            pl.BlockSpec((None, BLOCK, B_WIDTH), lambda b, n: (b, n, COL_BQ // B_WIDTH)),
            pl.BlockSpec((None, BLOCK, B_WIDTH), lambda b, n: (b, n, COL_BIQ // B_WIDTH)),
            pl.BlockSpec((None, s, DSA_RANK), lambda b, n: (b, 0, COL_BKV // DSA_RANK)),
            pl.BlockSpec((None, s, LANES), lambda b, n: (b, 0, COL_BIK // LANES)),
            pl.BlockSpec((1, DSA_RANK), lambda b, n: (0, 0)),
            pl.BlockSpec((DSA_HEADS, LANES, DSA_RANK), lambda b, n: (0, 0, 0)),
            pl.BlockSpec((DSA_HEADS, DSA_RANK, LANES), lambda b, n: (0, 0, 0)),
        ],
        out_specs=pl.BlockSpec((None, BLOCK, B_WIDTH), lambda b, n: (b, n, 0)),
        out_shape=jax.ShapeDtypeStruct((b, s, B_WIDTH), BF16),
        scratch_shapes=[
            pltpu.VMEM((s, DSA_RANK + LANES), BF16),
            pltpu.VMEM((s, LANES), BF16),
            pltpu.VMEM((s, LANES), BF16),
            pltpu.VMEM((nck, BLOCK, DSA_KCHUNK), F32),
            pltpu.VMEM((nck, BLOCK, DSA_KCHUNK), F32),
            pltpu.VMEM((rows, DSA_RANK + LANES), BF16),
            pltpu.VMEM((nck, rows, DSA_KCHUNK), F32),
            pltpu.VMEM((rows, LANES), F32),
            pltpu.VMEM((rows, 1), F32),
            pltpu.VMEM((rows, LANES), F32),
            pltpu.VMEM((rows, DSA_RANK), F32),
        ],
        compiler_params=pltpu.CompilerParams(
            dimension_semantics=("parallel", "arbitrary"), vmem_limit_bytes=VMEM_LIMIT),
        name=f"dsa_l{layer}",
    )(proj, proj, proj, proj, kv_norm, wuk_pad, wuv_pad)


def _ret_kernel(q_ref, k_ref, v_ref, g_ref, gn_ref, o_ref, state_scr):
    cs = BLOCK

    @pl.when(pl.program_id(1) == 0)
    def _():
        state_scr[...] = jnp.zeros_like(state_scr)

    row = lax.broadcasted_iota(jnp.int32, (cs, LANES), 0)
    lane = lax.broadcasted_iota(jnp.int32, (cs, LANES), 1)
    first = lane < HEAD_DIM
    rowf = row.astype(F32)
    diff = (row - lane).astype(F32)
    same_head = (row < HEAD_DIM) == first
    seg_mean = jnp.where(same_head, 1.0 / HEAD_DIM, 0.0).astype(BF16)

    def seg_mean_dot(x):
        x_hi = x.astype(BF16)
        x_lo = (x - x_hi.astype(F32)).astype(BF16)
        return _dot(x_hi, seg_mean) + _dot(x_lo, seg_mean)

    tiles = RET_HEADS // 2
    outs = []
    for bi in range(q_ref.shape[0]):
        for tile in range(tiles):
            cols = slice(tile * LANES, (tile + 1) * LANES)
            lg = [math.log(1.0 - 2.0 ** (-5.0 - (2 * tile + par))) for par in range(2)]
            lg_lane = jnp.where(first, lg[0], lg[1])
            q = q_ref[bi, :, cols].astype(BF16)
            k = k_ref[bi, :, cols] * HEAD_DIM ** -0.5
            v = v_ref[bi, :, cols]
            state = state_scr[bi, tile]
            out = _dot(q, state.astype(BF16)) * jnp.exp(lg_lane * (rowf + 1.0))
            for par in range(2):
                keep = first if par == 0 else ~first
                decay = jnp.where(diff >= 0, jnp.exp(lg[par] * jnp.maximum(diff, 0.0)), 0.0)
                inner = _dot_nt(q, jnp.where(keep, k, 0.0).astype(BF16)) * decay
                out = out + _dot(inner.astype(BF16), jnp.where(keep, v, 0.0).astype(BF16))
            k_dec = (k * jnp.exp(lg_lane * (cs - 1.0 - rowf))).astype(BF16)
            kv = _dot_tn(k_dec, v.astype(BF16))
            state_scr[bi, tile] = state * jnp.exp(lg_lane * cs) + jnp.where(same_head, kv, 0.0)
            outs.append(out)

    out = jnp.concatenate(outs, axis=0)
    mu = seg_mean_dot(out)
    cen = out - mu
    var = seg_mean_dot(cen * cen)
    yn = cen * lax.rsqrt(var + EPS)
    for bi in range(q_ref.shape[0]):
        for tile in range(tiles):
            cols = slice(tile * LANES, (tile + 1) * LANES)
            y = yn[(bi * tiles + tile) * cs:(bi * tiles + tile + 1) * cs] * gn_ref[:, cols]
            o_ref[bi, :, cols] = (y * _silu(g_ref[bi, :, cols])).astype(o_ref.dtype)


RET_BATCH = 4


def _retention(proj, ret_norm, layer):
    b, s, _ = proj.shape
    nc = s // BLOCK
    rb = math.gcd(b, RET_BATCH)
    spec = lambda col: pl.BlockSpec((rb, BLOCK, C_WIDTH), lambda b, c: (b, c, col // C_WIDTH))
    return pl.pallas_call(
        _ret_kernel,
        grid=(b // rb, nc),
        in_specs=[spec(COL_CQ), spec(COL_CK), spec(COL_CV), spec(COL_CG),
                  pl.BlockSpec((1, C_WIDTH), lambda b, c: (0, 0))],
        out_specs=pl.BlockSpec((rb, BLOCK, C_WIDTH), lambda b, c: (b, c, 0)),
        out_shape=jax.ShapeDtypeStruct((b, s, C_WIDTH), BF16),
        scratch_shapes=[pltpu.VMEM((rb, RET_HEADS // 2, LANES, LANES), F32)],
        compiler_params=pltpu.CompilerParams(
            dimension_semantics=("parallel", "arbitrary"), vmem_limit_bytes=VMEM_LIMIT),
        name=f"ret_l{layer}",
    )(proj, proj, proj, proj, ret_norm)


def _mix_out_kernel(x_ref, gt_ref, gpost_ref, oa_ref, ob_ref, oc_ref, w_ref, o_ref):
    y = _dot(oa_ref[...], w_ref[:A_WIDTH, :])
    y = y + _dot(ob_ref[...], w_ref[A_WIDTH:A_WIDTH + B_WIDTH, :])
    y = y + _dot(oc_ref[...], w_ref[A_WIDTH + B_WIDTH:, :])
    o_ref[...] = x_ref[...] + gt_ref[...] * _rms(y, gpost_ref[...])


def _mix_out(x, mod, norm_g, layer, oa, ob, oc, w, tm):
    b, s, d = x.shape
    row = lambda width: pl.BlockSpec((None, tm, width), lambda b, i: (b, i, 0))
    return pl.pallas_call(
        _mix_out_kernel,
        grid=(b, s // tm),
        in_specs=[
            row(d), _mod_spec(layer, 5, d, 2), _gain_spec(layer, 3, d, 2),
            row(A_WIDTH), row(B_WIDTH), row(C_WIDTH),
            pl.BlockSpec(w.shape, lambda b, i: (0, 0)),
        ],
        out_specs=row(d),
        out_shape=jax.ShapeDtypeStruct((b, s, d), F32),
        compiler_params=pltpu.CompilerParams(
            dimension_semantics=("parallel", "parallel"), vmem_limit_bytes=VMEM_LIMIT),
        name=f"mix_out_l{layer}",
    )(x, mod, norm_g, oa, ob, oc, w)


def _prep_ffn(w_in, w_out, tf):
    d_ff = w_out.shape[0]
    pad = (-d_ff) % tf
    wg = jnp.pad(w_in[:, :d_ff], ((0, 0), (0, pad))).astype(BF16)
    wu = jnp.pad(w_in[:, d_ff:], ((0, 0), (0, pad))).astype(BF16)
    wo = jnp.pad(w_out, ((0, pad), (0, 0))).astype(BF16)
    return wg, wu, wo


def _prep_mix_in(w):
    sizes = [A_WIDTH, KV_WIDTH, KV_WIDTH, B_WIDTH, DSA_RANK, IDX_HEADS * IDX_DIM, IDX_DIM, IDX_HEADS,
             C_WIDTH, C_WIDTH, C_WIDTH, C_WIDTH]
    starts = [0]
    for sz in sizes:
        starts.append(starts[-1] + sz)
    aq, ak, av, bq, bkv, biq, bik, biw, cq, ck, cv, cg = [w[:, a:a + sz] for a, sz in zip(starts, sizes)]
    tail = jnp.zeros((w.shape[0], LANES - IDX_DIM - IDX_HEADS), w.dtype)
    return jnp.concatenate([aq, bq, biq, cq, ck, cv, cg, ak, av, bkv, bik, biw, tail], axis=1).astype(BF16)


def _prep_dsa_up(w_uk, w_uv):
    r, h, dh = w_uk.shape
    uk = jnp.transpose(w_uk, (1, 2, 0))
    uv = jnp.transpose(w_uv, (1, 0, 2))
    odd = (jnp.arange(h) % 2 == 1)[:, None, None]
    zk = jnp.zeros_like(uk)
    zv = jnp.zeros_like(uv)
    uk_pad = jnp.where(odd, jnp.concatenate([zk, uk], axis=1), jnp.concatenate([uk, zk], axis=1))
    uv_pad = jnp.where(odd, jnp.concatenate([zv, uv], axis=2), jnp.concatenate([uv, zv], axis=2))
    return uk_pad.astype(BF16), uv_pad.astype(BF16)


FFN_TM = 512
FFN_TF = 512
MIX_IN_TM = 1024
MIX_IN_TN = 1536
MIX_OUT_TM = 512


def kernel(x, c, ada_w, ada_b, norm_g, ffn1_w_in, ffn1_w_out, ffn2_w_in, ffn2_w_out,
           mix_w_in, mix_w_out, swa_sinks, dsa_kv_norm, dsa_w_uk, dsa_w_uv, ret_norm):
    depth = ada_w.shape[0]
    b, s, d = x.shape
    assert s % FFN_TM == 0 and s % DSA_KCHUNK == 0 and d % LANES == 0
    mod = _ada_mod(c, ada_w, ada_b).reshape(depth, b, N_MOD, 1, d)
    gains = norm_g.reshape(depth, norm_g.shape[1], 1, d)
    for l in range(depth):
        wg, wu, wo = _prep_ffn(ffn1_w_in[l], ffn1_w_out[l], FFN_TF)
        x = _ffn_block(x, mod, gains, l, 0, wg, wu, wo, 0.5, FFN_TM, FFN_TF)

        proj = _mix_in(x, mod, gains, l, _prep_mix_in(mix_w_in[l]), MIX_IN_TM, MIX_IN_TN)
        oa = _swa(proj, swa_sinks[l], l)
        uk_pad, uv_pad = _prep_dsa_up(dsa_w_uk[l], dsa_w_uv[l])
        ob = _dsa(proj, dsa_kv_norm[l].reshape(1, -1), uk_pad, uv_pad, l)
        oc = _retention(proj, ret_norm[l].reshape(1, -1), l)
        x = _mix_out(x, mod, gains, l, oa, ob, oc, mix_w_out[l].astype(BF16), MIX_OUT_TM)

        wg, wu, wo = _prep_ffn(ffn2_w_in[l], ffn2_w_out[l], FFN_TF)
        x = _ffn_block(x, mod, gains, l, 2, wg, wu, wo, 0.5, FFN_TM, FFN_TF)
    return x
```

```python
import functools
import math

import jax
import jax.numpy as jnp
from jax import lax
from jax.experimental import pallas as pl
from jax.experimental.pallas import tpu as pltpu

F32 = jnp.float32
BF16 = jnp.bfloat16

LANES = 128
HEAD_DIM = 64
SWA_HEADS = 16
SWA_KV_HEADS = 2
BLOCK = 128
DSA_HEADS = 8
DSA_RANK = 128
IDX_HEADS = 8
IDX_DIM = 64
DSA_TOPK_MAX = 256
RET_HEADS = 8
N_MOD = 9
EPS = 1e-6
NEG = -1e30

A_WIDTH = SWA_HEADS * HEAD_DIM
B_WIDTH = DSA_HEADS * HEAD_DIM
C_WIDTH = RET_HEADS * HEAD_DIM
KV_WIDTH = SWA_KV_HEADS * HEAD_DIM

COL_AQ = 0
COL_BQ = COL_AQ + A_WIDTH
COL_BIQ = COL_BQ + B_WIDTH
COL_CQ = COL_BIQ + IDX_HEADS * IDX_DIM
COL_CK = COL_CQ + C_WIDTH
COL_CV = COL_CK + C_WIDTH
COL_CG = COL_CV + C_WIDTH
COL_AKV = COL_CG + C_WIDTH
COL_BKV = COL_AKV + 2 * KV_WIDTH
COL_BIK = COL_BKV + DSA_RANK
MIX_PAD_WIDTH = COL_BIK + LANES

DSA_QBLOCK = 256
DSA_KCHUNK = 256
BISECT_ITERS = 24
VMEM_LIMIT = 56 * 1024 * 1024


def _dot(a, b):
    return jnp.dot(a, b, preferred_element_type=F32)


def _dot_nt(a, b):
    return lax.dot_general(a, b, (((1,), (1,)), ((), ())), preferred_element_type=F32)


def _dot_tn(a, b):
    return lax.dot_general(a, b, (((0,), (0,)), ((), ())), preferred_element_type=F32)


def _silu(x):
    return x / (1.0 + jnp.exp(-x))


def _rms(x, g):
    return x * lax.rsqrt(jnp.mean(x * x, axis=-1, keepdims=True) + EPS) * g


def _alibi_slope(h, n):
    return 2.0 ** (-8.0 * (h + 1) / n)


def _ada_kernel(c_ref, w_ref, b_ref, o_ref):
    cond = _silu(c_ref[...]).astype(BF16)
    o_ref[...] = _dot(cond, w_ref[...].astype(BF16)) + b_ref[...]


def _ada_mod(c, ada_w, ada_b):
    depth, d, n = ada_w.shape
    b = c.shape[0]
    tn = 1024
    return pl.pallas_call(
        _ada_kernel,
        grid=(depth, n // tn),
        in_specs=[
            pl.BlockSpec((b, d), lambda l, j: (0, 0)),
            pl.BlockSpec((None, d, tn), lambda l, j: (l, 0, j)),
            pl.BlockSpec((None, 1, tn), lambda l, j: (l, 0, j)),
        ],
        out_specs=pl.BlockSpec((None, b, tn), lambda l, j: (l, 0, j)),
        out_shape=jax.ShapeDtypeStruct((depth, b, n), F32),
        compiler_params=pltpu.CompilerParams(
            dimension_semantics=("arbitrary", "arbitrary"), vmem_limit_bytes=VMEM_LIMIT),
        name="ada_mod",
    )(c, ada_w, ada_b.reshape(depth, 1, n))


def _mod_spec(layer, k, d, grid_rank):
    if grid_rank == 2:
        return pl.BlockSpec((None, None, None, 1, d), lambda b, i: (layer, b, k, 0, 0))
    return pl.BlockSpec((None, None, None, 1, d), lambda b, i, j: (layer, b, k, 0, 0))


def _gain_spec(layer, k, d, grid_rank):
    if grid_rank == 2:
        return pl.BlockSpec((None, None, 1, d), lambda b, i: (layer, k, 0, 0))
    return pl.BlockSpec((None, None, 1, d), lambda b, i, j: (layer, k, 0, 0))


def _ffn_kernel(res_w, x_ref, sh_ref, sc_ref, gt_ref, gpre_ref, gpost_ref, wg_ref, wu_ref, wo_ref,
                o_ref, h_scr, acc_scr):
    f = pl.program_id(2)

    @pl.when(f == 0)
    def _():
        h = _rms(x_ref[...], gpre_ref[...]) * (1.0 + sc_ref[...]) + sh_ref[...]
        h_scr[...] = h.astype(BF16)
        acc_scr[...] = jnp.zeros_like(acc_scr)

    h = h_scr[...]
    act = _silu(_dot(h, wg_ref[...])) * _dot(h, wu_ref[...])
    acc_scr[...] += _dot(act.astype(BF16), wo_ref[...])

    @pl.when(f == pl.num_programs(2) - 1)
    def _():
        y = _rms(acc_scr[...], gpost_ref[...])
        o_ref[...] = x_ref[...] + res_w * gt_ref[...] * y


def _ffn_block(x, mod, norm_g, layer, sub, wg, wu, wo, res_w, tm, tf):
    b, s, d = x.shape
    fp = wg.shape[1]
    mk = 3 * sub
    return pl.pallas_call(
        functools.partial(_ffn_kernel, res_w),
        grid=(b, s // tm, fp // tf),
        in_specs=[
            pl.BlockSpec((None, tm, d), lambda b, i, j: (b, i, 0)),
            _mod_spec(layer, mk, d, 3), _mod_spec(layer, mk + 1, d, 3), _mod_spec(layer, mk + 2, d, 3),
            _gain_spec(layer, 2 * sub, d, 3), _gain_spec(layer, 2 * sub + 1, d, 3),
            pl.BlockSpec((d, tf), lambda b, i, j: (0, j)),
            pl.BlockSpec((d, tf), lambda b, i, j: (0, j)),
            pl.BlockSpec((tf, d), lambda b, i, j: (j, 0)),
        ],
        out_specs=pl.BlockSpec((None, tm, d), lambda b, i, j: (b, i, 0)),
        out_shape=jax.ShapeDtypeStruct((b, s, d), F32),
        scratch_shapes=[pltpu.VMEM((tm, d), BF16), pltpu.VMEM((tm, d), F32)],
        compiler_params=pltpu.CompilerParams(
            dimension_semantics=("parallel", "parallel", "arbitrary"), vmem_limit_bytes=VMEM_LIMIT),
        name=f"ffn_l{layer}_s{sub}",
    )(x, mod, mod, mod, norm_g, norm_g, wg, wu, wo)


def _mix_in_kernel(x_ref, sh_ref, sc_ref, gpre_ref, w_ref, o_ref, h_scr):
    @pl.when(pl.program_id(2) == 0)
    def _():
        h = _rms(x_ref[...], gpre_ref[...]) * (1.0 + sc_ref[...]) + sh_ref[...]
        h_scr[...] = h.astype(BF16)

    o_ref[...] = _dot(h_scr[...], w_ref[...])


def _mix_in(x, mod, norm_g, layer, w, tm, tn):
    b, s, d = x.shape
    n = w.shape[1]
    return pl.pallas_call(
        _mix_in_kernel,
        grid=(b, s // tm, n // tn),
        in_specs=[
            pl.BlockSpec((None, tm, d), lambda b, i, j: (b, i, 0)),
            _mod_spec(layer, 3, d, 3), _mod_spec(layer, 4, d, 3),
            _gain_spec(layer, 2, d, 3),
            pl.BlockSpec((d, tn), lambda b, i, j: (0, j)),
        ],
        out_specs=pl.BlockSpec((None, tm, tn), lambda b, i, j: (b, i, j)),
        out_shape=jax.ShapeDtypeStruct((b, s, n), F32),
        scratch_shapes=[pltpu.VMEM((tm, d), BF16)],
        compiler_params=pltpu.CompilerParams(
            dimension_semantics=("parallel", "parallel", "arbitrary"), vmem_limit_bytes=VMEM_LIMIT),
        name=f"mix_in_l{layer}",
    )(x, mod, mod, norm_g, w)


def _lane_halves(x, first_half_holds_data):
    lane = lax.broadcasted_iota(jnp.int32, x.shape, 1)
    if first_half_holds_data:
        lo = jnp.where(lane < HEAD_DIM, x, 0.0)
        return lo, pltpu.roll(lo, HEAD_DIM, 1)
    hi = jnp.where(lane >= HEAD_DIM, x, 0.0)
    return pltpu.roll(hi, HEAD_DIM, 1), hi


def _swa_kernel(sinks_ref, q_ref, kvc_ref, kvp_ref, o_ref):
    n = pl.program_id(1)
    w = BLOCK
    kcat = jnp.concatenate([kvp_ref[:, :KV_WIDTH], kvc_ref[:, :KV_WIDTH]], axis=0)
    vcat = jnp.concatenate([kvp_ref[:, KV_WIDTH:], kvc_ref[:, KV_WIDTH:]], axis=0)
    k_halves = [tuple(t.astype(BF16) for t in _lane_halves(kcat, hk == 0)) for hk in range(SWA_KV_HEADS)]
    v_halves = [tuple(t.astype(BF16) for t in _lane_halves(vcat, hk == 0)) for hk in range(SWA_KV_HEADS)]

    bands = SWA_HEADS // SWA_KV_HEADS // 2
    rows = bands * w
    r = lax.broadcasted_iota(jnp.int32, (rows, 2 * w), 0)
    j = lax.broadcasted_iota(jnp.int32, (rows, 2 * w), 1)
    dist = (r & (w - 1)) - j + w
    valid = (dist >= 0) & (dist < w) & ((n - 1) * w + j >= 0)
    band_step = _alibi_slope(2, SWA_HEADS) / _alibi_slope(0, SWA_HEADS)
    band_scale = jnp.ones((rows, 2 * w), F32)
    for band in range(1, bands):
        band_scale = jnp.where(r >= band * w, band_step ** band, band_scale)
    dist_scaled = dist.astype(F32) * band_scale
    band_col = lax.broadcasted_iota(jnp.int32, (rows, 1), 0)
    for hk in range(SWA_KV_HEADS):
        q = jnp.concatenate([q_ref[:, (bands * hk + band) * LANES:(bands * hk + band + 1) * LANES]
                             for band in range(bands)], axis=0).astype(BF16)
        out = None
        for par in range(2):
            head0 = 2 * bands * hk + par
            sink = jnp.full((rows, 1), sinks_ref[head0], F32)
            for band in range(1, bands):
                sink = jnp.where(band_col >= band * w, sinks_ref[head0 + 2 * band], sink)
            sc = _dot_nt(q, k_halves[hk][par]) * HEAD_DIM ** -0.5
            sc = jnp.where(valid, sc - _alibi_slope(head0, SWA_HEADS) * dist_scaled, -jnp.inf)
            m = jnp.maximum(jnp.max(sc, axis=-1, keepdims=True), sink)
            e = jnp.exp(sc - m)
            p = e / (jnp.sum(e, axis=-1, keepdims=True) + jnp.exp(sink - m))
            pv = _dot(p.astype(BF16), v_halves[hk][par])
            out = pv if out is None else out + pv
        for band in range(bands):
            tile = bands * hk + band
            o_ref[:, tile * LANES:(tile + 1) * LANES] = out[band * w:(band + 1) * w].astype(o_ref.dtype)


def _swa(proj, sinks, layer):
    b, s, _ = proj.shape
    nb = s // BLOCK
    kv_blk = COL_AKV // (2 * KV_WIDTH)
    return pl.pallas_call(
        _swa_kernel,
        grid=(b, nb),
        in_specs=[
            pl.BlockSpec(memory_space=pltpu.SMEM),
            pl.BlockSpec((None, BLOCK, A_WIDTH), lambda b, n: (b, n, COL_AQ // A_WIDTH)),
            pl.BlockSpec((None, BLOCK, 2 * KV_WIDTH), lambda b, n: (b, n, kv_blk)),
            pl.BlockSpec((None, BLOCK, 2 * KV_WIDTH), lambda b, n: (b, jnp.maximum(n - 1, 0), kv_blk)),
        ],
        out_specs=pl.BlockSpec((None, BLOCK, A_WIDTH), lambda b, n: (b, n, 0)),
        out_shape=jax.ShapeDtypeStruct((b, s, A_WIDTH), BF16),
        compiler_params=pltpu.CompilerParams(
            dimension_semantics=("parallel", "arbitrary"), vmem_limit_bytes=VMEM_LIMIT),
        name=f"swa_l{layer}",
    )(sinks, proj, proj, proj)


def _dsa_kernel(topk, q_ref, iq_ref, kv_ref, ikw_ref, kvn_ref, wuk_ref, wuv_ref, o_ref,
                ckv_scr, ik_scr, iqh_scr, score_scr, bias_scr, qlat_scr, s_scr, mrun_scr, m_scr,
                lpart_scr, acc_scr):
    n = pl.program_id(1)
    qb = DSA_QBLOCK
    kc = DSA_KCHUNK
    nchunks = lax.div(n * qb + qb - 1, kc) + 1

    @pl.when(n == 0)
    def _():
        ckv_scr[...] = _rms(kv_ref[...], kvn_ref[...]).astype(BF16)
        ik_scr[...] = ikw_ref[...].astype(BF16)

    row0 = pl.multiple_of(n * qb, qb)

    iw_t = ikw_ref[pl.ds(row0, qb), :].T
    iw_scale = IDX_HEADS ** -0.5 * IDX_DIM ** -0.5
    w_rows = [iw_t[IDX_DIM + h:IDX_DIM + h + 1, :] * iw_scale for h in range(IDX_HEADS)]
    lane = lax.broadcasted_iota(jnp.int32, (qb, LANES), 1)
    for h in range(IDX_HEADS):
        iq = iq_ref[:, (h // 2) * LANES:(h // 2 + 1) * LANES]
        if h % 2 == 0:
            iq = jnp.where(lane < IDX_DIM, iq, 0.0)
        else:
            iq = pltpu.roll(jnp.where(lane >= IDX_DIM, iq, 0.0), IDX_DIM, 1)
        iqh_scr[h] = iq.astype(BF16)

    key_row = lax.broadcasted_iota(jnp.int32, (kc, qb), 0)
    t_pos = n * qb + lax.broadcasted_iota(jnp.int32, (kc, qb), 1)

    def score_chunk(c, carry):
        off = pl.multiple_of(c * kc, kc)
        ik = ik_scr[pl.ds(off, kc), :]
        sc = jnp.zeros((kc, qb), F32)
        for h in range(IDX_HEADS):
            sc = sc + w_rows[h] * jnp.maximum(_dot_nt(ik, iqh_scr[h]), 0.0)
        score_scr[c] = jnp.where(off + key_row <= t_pos, sc, -jnp.inf)
        return carry

    lax.fori_loop(0, nchunks, score_chunk, 0)

    sub = 8

    def fold_rows(x, op):
        out = x[:sub]
        for g in range(1, x.shape[0] // sub):
            out = op(out, x[g * sub:(g + 1) * sub])
        return out

    def fold_lanes(x, op):
        out = x[:, :LANES]
        for t in range(1, x.shape[1] // LANES):
            out = op(out, x[:, t * LANES:(t + 1) * LANES])
        return out

    def count_ge(thr, strict=False):
        def body(c, cnt):
            x = score_scr[c]
            hit = (x > thr) if strict else (x >= thr)
            return cnt + fold_rows(jnp.where(hit, 1.0, 0.0), jnp.add)
        part = lax.fori_loop(0, nchunks, body, jnp.zeros((sub, qb), F32))
        return jnp.sum(part, axis=0, keepdims=True)

    def max_below(bound, strict):
        def body(c, mx):
            x = score_scr[c]
            keep = (x < bound) if strict else (x <= bound)
            return jnp.maximum(mx, fold_rows(jnp.where(keep, x, -jnp.inf), jnp.maximum))
        part = lax.fori_loop(0, nchunks, body, jnp.full((sub, qb), -jnp.inf, F32))
        return jnp.max(part, axis=0, keepdims=True)

    def store_bias(c, t, bias_t):
        for g in range(qb // LANES):
            bias_scr[c, g * LANES:(g + 1) * LANES, t * LANES:(t + 1) * LANES] = (
                bias_t[:, g * LANES:(g + 1) * LANES].T)

    @pl.when(n * qb + qb <= topk)
    def _():
        def body(c, carry):
            x = score_scr[c]
            for t in range(kc // LANES):
                store_bias(c, t, jnp.where(x[t * LANES:(t + 1) * LANES] == -jnp.inf, NEG, 0.0))
            return carry
        lax.fori_loop(0, nchunks, body, 0)

    @pl.when(n * qb + qb > topk)
    def _():
        kf = float(topk)

        def minmax(c, carry):
            mn, mx = carry
            x = score_scr[c]
            mn = jnp.minimum(mn, fold_rows(jnp.where(x == -jnp.inf, jnp.inf, x), jnp.minimum))
            return mn, jnp.maximum(mx, fold_rows(x, jnp.maximum))

        lo, hi = lax.fori_loop(0, nchunks, minmax,
                               (jnp.full((sub, qb), jnp.inf, F32), jnp.full((sub, qb), -jnp.inf, F32)))
        lo = jnp.min(lo, axis=0, keepdims=True)
        hi = jnp.max(hi, axis=0, keepdims=True)

        def bisect(_, carry):
            lo, hi = carry
            mid = 0.5 * (lo + hi)
            ge = count_ge(mid) >= kf
            return jnp.where(ge, mid, lo), jnp.where(ge, hi, mid)

        lo, hi = lax.fori_loop(0, BISECT_ITERS, bisect, (lo, hi))

        def walk_cond(carry):
            return carry[1] > 0.0

        def walk(carry):
            thr, _ = carry
            short = count_ge(thr) < kf
            thr = jnp.where(short, max_below(thr, True), thr)
            return thr, jnp.max(jnp.where(short, 1.0, 0.0))

        thr, _ = lax.while_loop(walk_cond, walk, (max_below(hi, False), jnp.float32(1.0)))

        need = kf - count_ge(thr, strict=True)

        r = lax.broadcasted_iota(jnp.int32, (LANES, LANES), 0)
        col = lax.broadcasted_iota(jnp.int32, (LANES, LANES), 1)
        tri = jnp.where(col <= r, 1.0, 0.0).astype(BF16)

        def select(c, seen):
            x = score_scr[c]
            for t in range(kc // LANES):
                xt = x[t * LANES:(t + 1) * LANES]
                eq = xt == thr
                eqf = jnp.where(eq, 1.0, 0.0)
                rank = seen + _dot(tri, eqf.astype(BF16))
                seen = seen + jnp.sum(eqf, axis=0, keepdims=True)
                sel = (xt > thr) | (eq & (rank <= need))
                store_bias(c, t, jnp.where(sel, 0.0, NEG))
            return seen

        lax.fori_loop(0, nchunks, select, jnp.zeros((1, qb), F32))

    for h in range(DSA_HEADS):
        tile = h // 2
        rows = slice(h * qb, (h + 1) * qb)
        q = q_ref[:, tile * LANES:(tile + 1) * LANES].astype(BF16)
        qlat_scr[rows, :] = (_dot(q, wuk_ref[h]) * HEAD_DIM ** -0.5).astype(BF16)
    mrun_scr[...] = jnp.full(mrun_scr.shape, NEG, F32)
    key_lane = lax.broadcasted_iota(jnp.int32, (1, kc), 1)

    def logits(c, carry):
        off = pl.multiple_of(c * kc, kc)
        s = _dot_nt(qlat_scr[...], ckv_scr[pl.ds(off, kc), :])
        bias = bias_scr[c]
        key_pos = (off + key_lane).astype(F32)
        for h in range(DSA_HEADS):
            rows = slice(h * qb, (h + 1) * qb)
            sh = s[rows] + (bias + _alibi_slope(h, DSA_HEADS) * key_pos)
            s_scr[c, rows, :] = sh
            mrun_scr[rows, :] = jnp.maximum(mrun_scr[rows, :], fold_lanes(sh, jnp.maximum))
        return carry

    lax.fori_loop(0, nchunks, logits, 0)
    m_scr[...] = jnp.broadcast_to(jnp.max(mrun_scr[...], axis=-1, keepdims=True), m_scr.shape)
    lpart_scr[...] = jnp.zeros_like(lpart_scr)
    acc_scr[...] = jnp.zeros_like(acc_scr)

    def attend(c, carry):
        off = pl.multiple_of(c * kc, kc)
        m = m_scr[...]
        p = [jnp.exp(s_scr[c, :, t * LANES:(t + 1) * LANES] - m) for t in range(kc // LANES)]
        lpart_scr[...] += functools.reduce(jnp.add, p)
        acc_scr[...] += _dot(jnp.concatenate(p, axis=1).astype(BF16), ckv_scr[pl.ds(off, kc), :])
        return carry

    lax.fori_loop(0, nchunks, attend, 0)

    o_all = (acc_scr[...] / jnp.sum(lpart_scr[...], axis=-1, keepdims=True)).astype(BF16)
    for tile in range(DSA_HEADS // 2):
        h = 2 * tile
        out = _dot(o_all[h * qb:(h + 1) * qb], wuv_ref[h]) + _dot(o_all[(h + 1) * qb:(h + 2) * qb], wuv_ref[h + 1])
        o_ref[:, tile * LANES:(tile + 1) * LANES] = out.astype(o_ref.dtype)


def _dsa(proj, kv_norm, wuk_pad, wuv_pad, layer):
    b, s, _ = proj.shape
    qb = DSA_QBLOCK
    nb = s // qb
    topk = min(DSA_TOPK_MAX, s // 4)
    assert topk % qb == 0
    nck = s // DSA_KCHUNK
    rows = DSA_HEADS * qb
    return pl.pallas_call(
        functools.partial(_dsa_kernel, topk),
        grid=(b, nb),
        in_specs=[
            pl.BlockSpec((None, qb, B_WIDTH), lambda b, n: (b, n, COL_BQ // B_WIDTH)),
            pl.BlockSpec((None, qb, B_WIDTH), lambda b, n: (b, n, COL_BIQ // B_WIDTH)),
            pl.BlockSpec((None, s, DSA_RANK), lambda b, n: (b, 0, COL_BKV // DSA_RANK)),
            pl.BlockSpec((None, s, LANES), lambda b, n: (b, 0, COL_BIK // LANES)),
            pl.BlockSpec((1, DSA_RANK), lambda b, n: (0, 0)),
            pl.BlockSpec((DSA_HEADS, LANES, DSA_RANK), lambda b, n: (0, 0, 0)),
            pl.BlockSpec((DSA_HEADS, DSA_RANK, LANES), lambda b, n: (0, 0, 0)),
        ],
        out_specs=pl.BlockSpec((None, qb, B_WIDTH), lambda b, n: (b, n, 0)),
        out_shape=jax.ShapeDtypeStruct((b, s, B_WIDTH), BF16),
        scratch_shapes=[
            pltpu.VMEM((s, DSA_RANK), BF16),
            pltpu.VMEM((s, LANES), BF16),
            pltpu.VMEM((IDX_HEADS, qb, LANES), BF16),
            pltpu.VMEM((nck, DSA_KCHUNK, qb), F32),
            pltpu.VMEM((nck, qb, DSA_KCHUNK), F32),
            pltpu.VMEM((rows, DSA_RANK), BF16),
            pltpu.VMEM((nck, rows, DSA_KCHUNK), F32),
            pltpu.VMEM((rows, LANES), F32),
            pltpu.VMEM((rows, LANES), F32),
            pltpu.VMEM((rows, LANES), F32),
            pltpu.VMEM((rows, DSA_RANK), F32),
        ],
        compiler_params=pltpu.CompilerParams(
            dimension_semantics=("parallel", "arbitrary"), vmem_limit_bytes=VMEM_LIMIT),
        name=f"dsa_l{layer}",
    )(proj, proj, proj, proj, kv_norm, wuk_pad, wuv_pad)


def _ret_kernel(q_ref, k_ref, v_ref, g_ref, gn_ref, o_ref, state_scr):
    cs = BLOCK

    @pl.when(pl.program_id(1) == 0)
    def _():
        state_scr[...] = jnp.zeros_like(state_scr)

    row = lax.broadcasted_iota(jnp.int32, (cs, LANES), 0)
    lane = lax.broadcasted_iota(jnp.int32, (cs, LANES), 1)
    first = lane < HEAD_DIM
    rowf = row.astype(F32)
    diff = (row - lane).astype(F32)
    same_head = (row < HEAD_DIM) == first
    seg_mean = jnp.where(same_head, 1.0 / HEAD_DIM, 0.0).astype(BF16)

    def seg_mean_dot(x):
        x_hi = x.astype(BF16)
        x_lo = (x - x_hi.astype(F32)).astype(BF16)
        return _dot(x_hi, seg_mean) + _dot(x_lo, seg_mean)

    tiles = RET_HEADS // 2
    outs = []
    for bi in range(q_ref.shape[0]):
        for tile in range(tiles):
            cols = slice(tile * LANES, (tile + 1) * LANES)
            lg = [math.log(1.0 - 2.0 ** (-5.0 - (2 * tile + par))) for par in range(2)]
            lg_lane = jnp.where(first, lg[0], lg[1])
            q = q_ref[bi, :, cols].astype(BF16)
            k = k_ref[bi, :, cols] * HEAD_DIM ** -0.5
            v = v_ref[bi, :, cols]
            state = state_scr[bi, tile]
            out = _dot(q, state.astype(BF16)) * jnp.exp(lg_lane * (rowf + 1.0))
            for par in range(2):
                keep = first if par == 0 else ~first
                decay = jnp.where(diff >= 0, jnp.exp(lg[par] * jnp.maximum(diff, 0.0)), 0.0)
                inner = _dot_nt(q, jnp.where(keep, k, 0.0).astype(BF16)) * decay
                out = out + _dot(inner.astype(BF16), jnp.where(keep, v, 0.0).astype(BF16))
            k_dec = (k * jnp.exp(lg_lane * (cs - 1.0 - rowf))).astype(BF16)
            kv = _dot_tn(k_dec, v.astype(BF16))
            state_scr[bi, tile] = state * jnp.exp(lg_lane * cs) + jnp.where(same_head, kv, 0.0)
            outs.append(out)

    out = jnp.concatenate(outs, axis=0)
    mu = seg_mean_dot(out)
    cen = out - mu
    var = seg_mean_dot(cen * cen)
    yn = cen * lax.rsqrt(var + EPS)
    for bi in range(q_ref.shape[0]):
        for tile in range(tiles):
            cols = slice(tile * LANES, (tile + 1) * LANES)
            y = yn[(bi * tiles + tile) * cs:(bi * tiles + tile + 1) * cs] * gn_ref[:, cols]
            o_ref[bi, :, cols] = (y * _silu(g_ref[bi, :, cols])).astype(o_ref.dtype)


RET_BATCH = 4


def _retention(proj, ret_norm, layer):
    b, s, _ = proj.shape
    nc = s // BLOCK
    rb = math.gcd(b, RET_BATCH)
    spec = lambda col: pl.BlockSpec((rb, BLOCK, C_WIDTH), lambda b, c: (b, c, col // C_WIDTH))
    return pl.pallas_call(
        _ret_kernel,
        grid=(b // rb, nc),
        in_specs=[spec(COL_CQ), spec(COL_CK), spec(COL_CV), spec(COL_CG),
                  pl.BlockSpec((1, C_WIDTH), lambda b, c: (0, 0))],
        out_specs=pl.BlockSpec((rb, BLOCK, C_WIDTH), lambda b, c: (b, c, 0)),
        out_shape=jax.ShapeDtypeStruct((b, s, C_WIDTH), BF16),
        scratch_shapes=[pltpu.VMEM((rb, RET_HEADS // 2, LANES, LANES), F32)],
        compiler_params=pltpu.CompilerParams(
            dimension_semantics=("parallel", "arbitrary"), vmem_limit_bytes=VMEM_LIMIT),
        name=f"ret_l{layer}",
    )(proj, proj, proj, proj, ret_norm)


def _mix_out_kernel(x_ref, gt_ref, gpost_ref, oa_ref, ob_ref, oc_ref, w_ref, o_ref):
    y = _dot(oa_ref[...], w_ref[:A_WIDTH, :])
    y = y + _dot(ob_ref[...], w_ref[A_WIDTH:A_WIDTH + B_WIDTH, :])
    y = y + _dot(oc_ref[...], w_ref[A_WIDTH + B_WIDTH:, :])
    o_ref[...] = x_ref[...] + gt_ref[...] * _rms(y, gpost_ref[...])


def _mix_out(x, mod, norm_g, layer, oa, ob, oc, w, tm):
    b, s, d = x.shape
    row = lambda width: pl.BlockSpec((None, tm, width), lambda b, i: (b, i, 0))
    return pl.pallas_call(
        _mix_out_kernel,
        grid=(b, s // tm),
        in_specs=[
            row(d), _mod_spec(layer, 5, d, 2), _gain_spec(layer, 3, d, 2),
            row(A_WIDTH), row(B_WIDTH), row(C_WIDTH),
            pl.BlockSpec(w.shape, lambda b, i: (0, 0)),
        ],
        out_specs=row(d),
        out_shape=jax.ShapeDtypeStruct((b, s, d), F32),
        compiler_params=pltpu.CompilerParams(
            dimension_semantics=("parallel", "parallel"), vmem_limit_bytes=VMEM_LIMIT),
        name=f"mix_out_l{layer}",
    )(x, mod, norm_g, oa, ob, oc, w)


def _prep_ffn(w_in, w_out, tf):
    d_ff = w_out.shape[0]
    pad = (-d_ff) % tf
    wg = jnp.pad(w_in[:, :d_ff], ((0, 0), (0, pad))).astype(BF16)
    wu = jnp.pad(w_in[:, d_ff:], ((0, 0), (0, pad))).astype(BF16)
    wo = jnp.pad(w_out, ((0, pad), (0, 0))).astype(BF16)
    return wg, wu, wo


def _prep_mix_in(w):
    sizes = [A_WIDTH, KV_WIDTH, KV_WIDTH, B_WIDTH, DSA_RANK, IDX_HEADS * IDX_DIM, IDX_DIM, IDX_HEADS,
             C_WIDTH, C_WIDTH, C_WIDTH, C_WIDTH]
    starts = [0]
    for sz in sizes:
        starts.append(starts[-1] + sz)
    aq, ak, av, bq, bkv, biq, bik, biw, cq, ck, cv, cg = [w[:, a:a + sz] for a, sz in zip(starts, sizes)]
    tail = jnp.zeros((w.shape[0], LANES - IDX_DIM - IDX_HEADS), w.dtype)
    return jnp.concatenate([aq, bq, biq, cq, ck, cv, cg, ak, av, bkv, bik, biw, tail], axis=1).astype(BF16)


def _prep_dsa_up(w_uk, w_uv):
    r, h, dh = w_uk.shape
    uk = jnp.transpose(w_uk, (1, 2, 0))
    uv = jnp.transpose(w_uv, (1, 0, 2))
    odd = (jnp.arange(h) % 2 == 1)[:, None, None]
    zk = jnp.zeros_like(uk)
    zv = jnp.zeros_like(uv)
    uk_pad = jnp.where(odd, jnp.concatenate([zk, uk], axis=1), jnp.concatenate([uk, zk], axis=1))
    uv_pad = jnp.where(odd, jnp.concatenate([zv, uv], axis=2), jnp.concatenate([uv, zv], axis=2))
    return uk_pad.astype(BF16), uv_pad.astype(BF16)


FFN_TM = 512
FFN_TF = 512
MIX_IN_TM = 1024
MIX_IN_TN = 1536
MIX_OUT_TM = 512


def kernel(x, c, ada_w, ada_b, norm_g, ffn1_w_in, ffn1_w_out, ffn2_w_in, ffn2_w_out,
           mix_w_in, mix_w_out, swa_sinks, dsa_kv_norm, dsa_w_uk, dsa_w_uv, ret_norm):
    depth = ada_w.shape[0]
    b, s, d = x.shape
    assert s % FFN_TM == 0 and s % DSA_KCHUNK == 0 and d % LANES == 0
    mod = _ada_mod(c, ada_w, ada_b).reshape(depth, b, N_MOD, 1, d)
    gains = norm_g.reshape(depth, norm_g.shape[1], 1, d)
    for l in range(depth):
        wg, wu, wo = _prep_ffn(ffn1_w_in[l], ffn1_w_out[l], FFN_TF)
        x = _ffn_block(x, mod, gains, l, 0, wg, wu, wo, 0.5, FFN_TM, FFN_TF)

        proj = _mix_in(x, mod, gains, l, _prep_mix_in(mix_w_in[l]), MIX_IN_TM, MIX_IN_TN)
        oa = _swa(proj, swa_sinks[l], l)
        uk_pad, uv_pad = _prep_dsa_up(dsa_w_uk[l], dsa_w_uv[l])
        ob = _dsa(proj, dsa_kv_norm[l].reshape(1, -1), uk_pad, uv_pad, l)
        oc = _retention(proj, ret_norm[l].reshape(1, -1), l)
        x = _mix_out(x, mod, gains, l, oa, ob, oc, mix_w_out[l].astype(BF16), MIX_OUT_TM)

        wg, wu, wo = _prep_ffn(ffn2_w_in[l], ffn2_w_out[l], FFN_TF)
        x = _ffn_block(x, mod, gains, l, 2, wg, wu, wo, 0.5, FFN_TM, FFN_TF)
    return x
```

```python
import functools
import math

import jax
import jax.numpy as jnp
from jax import lax
from jax.experimental import pallas as pl
from jax.experimental.pallas import tpu as pltpu

F32 = jnp.float32
BF16 = jnp.bfloat16

LANES = 128
HEAD_DIM = 64
SWA_HEADS = 16
SWA_KV_HEADS = 2
BLOCK = 128
DSA_HEADS = 8
DSA_RANK = 128
IDX_HEADS = 8
IDX_DIM = 64
DSA_TOPK_MAX = 256
RET_HEADS = 8
N_MOD = 9
EPS = 1e-6
NEG = -1e30

A_WIDTH = SWA_HEADS * HEAD_DIM
B_WIDTH = DSA_HEADS * HEAD_DIM
C_WIDTH = RET_HEADS * HEAD_DIM
KV_WIDTH = SWA_KV_HEADS * HEAD_DIM

COL_AQ = 0
COL_BQ = COL_AQ + A_WIDTH
COL_BIQ = COL_BQ + B_WIDTH
COL_CQ = COL_BIQ + IDX_HEADS * IDX_DIM
COL_CK = COL_CQ + C_WIDTH
COL_CV = COL_CK + C_WIDTH
COL_CG = COL_CV + C_WIDTH
COL_AKV = COL_CG + C_WIDTH
COL_BKV = COL_AKV + 2 * KV_WIDTH
COL_BIK = COL_BKV + DSA_RANK
MIX_PAD_WIDTH = COL_BIK + LANES

DSA_QBLOCK = 256
DSA_KCHUNK = 256
BISECT_ITERS = 24
VMEM_LIMIT = 56 * 1024 * 1024


def _dot(a, b):
    return jnp.dot(a, b, preferred_element_type=F32)


def _dot_nt(a, b):
    return lax.dot_general(a, b, (((1,), (1,)), ((), ())), preferred_element_type=F32)


def _dot_tn(a, b):
    return lax.dot_general(a, b, (((0,), (0,)), ((), ())), preferred_element_type=F32)


def _silu(x):
    return x / (1.0 + jnp.exp(-x))


def _rms(x, g):
    return x * lax.rsqrt(jnp.mean(x * x, axis=-1, keepdims=True) + EPS) * g


def _alibi_slope(h, n):
    return 2.0 ** (-8.0 * (h + 1) / n)


def _ada_kernel(c_ref, w_ref, b_ref, o_ref):
    cond = _silu(c_ref[...]).astype(BF16)
    o_ref[...] = _dot(cond, w_ref[...].astype(BF16)) + b_ref[...]


def _ada_mod(c, ada_w, ada_b):
    depth, d, n = ada_w.shape
    b = c.shape[0]
    tn = 1024
    return pl.pallas_call(
        _ada_kernel,
        grid=(depth, n // tn),
        in_specs=[
            pl.BlockSpec((b, d), lambda l, j: (0, 0)),
            pl.BlockSpec((None, d, tn), lambda l, j: (l, 0, j)),
            pl.BlockSpec((None, 1, tn), lambda l, j: (l, 0, j)),
        ],
        out_specs=pl.BlockSpec((None, b, tn), lambda l, j: (l, 0, j)),
        out_shape=jax.ShapeDtypeStruct((depth, b, n), F32),
        compiler_params=pltpu.CompilerParams(
            dimension_semantics=("arbitrary", "arbitrary"), vmem_limit_bytes=VMEM_LIMIT),
        name="ada_mod",
    )(c, ada_w, ada_b.reshape(depth, 1, n))


def _mod_spec(layer, k, d, grid_rank):
    if grid_rank == 2:
        return pl.BlockSpec((None, None, None, 1, d), lambda b, i: (layer, b, k, 0, 0))
    return pl.BlockSpec((None, None, None, 1, d), lambda b, i, j: (layer, b, k, 0, 0))


def _gain_spec(layer, k, d, grid_rank):
    if grid_rank == 2:
        return pl.BlockSpec((None, None, 1, d), lambda b, i: (layer, k, 0, 0))
    return pl.BlockSpec((None, None, 1, d), lambda b, i, j: (layer, k, 0, 0))


ROW_CHUNK = 16


def _row_chunks(n_rows, body):
    def step(i, carry):
        body(pl.ds(pl.multiple_of(i * ROW_CHUNK, ROW_CHUNK), ROW_CHUNK))
        return carry
    lax.fori_loop(0, n_rows // ROW_CHUNK, step, 0, unroll=True)


def _inv_rms_rows(x_ref, inv_ref):
    def body(r):
        x = x_ref[r, :]
        inv_ref[r, :] = lax.rsqrt(jnp.mean(x * x, axis=-1, keepdims=True) + EPS)
    _row_chunks(x_ref.shape[0], body)


def _prenorm_mod_rows(x_ref, gain_ref, shift_ref, inv_ref, h_ref):
    _inv_rms_rows(x_ref, inv_ref)

    def body(r):
        h_ref[r, :] = (x_ref[r, :] * inv_ref[r, :] * gain_ref[...] + shift_ref[...]).astype(h_ref.dtype)
    _row_chunks(x_ref.shape[0], body)


def _postnorm_residual_rows(y_ref, x_ref, gain_ref, inv_ref, o_ref):
    _inv_rms_rows(y_ref, inv_ref)

    def body(r):
        o_ref[r, :] = x_ref[r, :] + y_ref[r, :] * inv_ref[r, :] * gain_ref[...]
    _row_chunks(y_ref.shape[0], body)


def _ffn_kernel(res_w, tail, x_ref, sh_ref, sc_ref, gt_ref, gpre_ref, gpost_ref, wg_ref, wu_ref, wo_ref,
                o_ref, h_scr, acc_scr, gain_scr, inv_scr):
    f = pl.program_id(2)
    last = pl.num_programs(2) - 1
    tf = wo_ref.shape[0]

    @pl.when(f == 0)
    def _():
        gain_scr[...] = gpre_ref[...] * (1.0 + sc_ref[...])
        _prenorm_mod_rows(x_ref, gain_scr, sh_ref, inv_scr, h_scr)
        acc_scr[...] = jnp.zeros_like(acc_scr)

    def step(lo):
        h = h_scr[...]
        act = _silu(_dot(h, wg_ref[:, lo:])) * _dot(h, wu_ref[:, lo:])
        acc_scr[...] += _dot(act.astype(BF16), wo_ref[lo:, :])

    @pl.when(f < last)
    def _():
        step(0)

    @pl.when(f == last)
    def _():
        step(tf - tail)
        gain_scr[...] = res_w * gt_ref[...] * gpost_ref[...]
        _postnorm_residual_rows(acc_scr, x_ref, gain_scr, inv_scr, o_ref)


def _ffn_block(x, mod, norm_g, layer, sub, w_in, w_out, res_w, tm, tf):
    b, s, d = x.shape
    d_ff = w_out.shape[0]
    nf = pl.cdiv(d_ff, tf)
    tail = d_ff - (nf - 1) * tf
    assert tail % LANES == 0 and d_ff >= tf
    mk = 3 * sub
    start = lambda j, base: pl.multiple_of(base + jnp.minimum(j * tf, d_ff - tf), LANES)
    col = lambda base: (lambda b, i, j: (0, start(j, base)))
    return pl.pallas_call(
        functools.partial(_ffn_kernel, res_w, tail),
        grid=(b, s // tm, nf),
        in_specs=[
            pl.BlockSpec((None, tm, d), lambda b, i, j: (b, i, 0)),
            _mod_spec(layer, mk, d, 3), _mod_spec(layer, mk + 1, d, 3), _mod_spec(layer, mk + 2, d, 3),
            _gain_spec(layer, 2 * sub, d, 3), _gain_spec(layer, 2 * sub + 1, d, 3),
            pl.BlockSpec((pl.Element(d), pl.Element(tf)), col(0)),
            pl.BlockSpec((pl.Element(d), pl.Element(tf)), col(d_ff)),
            pl.BlockSpec((pl.Element(tf), pl.Element(d)), lambda b, i, j: (start(j, 0), 0)),
        ],
        out_specs=pl.BlockSpec((None, tm, d), lambda b, i, j: (b, i, 0)),
        out_shape=jax.ShapeDtypeStruct((b, s, d), F32),
        scratch_shapes=[pltpu.VMEM((tm, d), BF16), pltpu.VMEM((tm, d), F32), pltpu.VMEM((1, d), F32),
                        pltpu.VMEM((tm, 1), F32)],
        compiler_params=pltpu.CompilerParams(
            dimension_semantics=("parallel", "parallel", "arbitrary"), vmem_limit_bytes=VMEM_LIMIT),
        name=f"ffn_l{layer}_s{sub}",
    )(x, mod, mod, mod, norm_g, norm_g, w_in, w_in, w_out)


def _mix_in_kernel(x_ref, sh_ref, sc_ref, gpre_ref, w_ref, o_ref, h_scr, gain_scr, inv_scr):
    @pl.when(pl.program_id(2) == 0)
    def _():
        gain_scr[...] = gpre_ref[...] * (1.0 + sc_ref[...])
        _prenorm_mod_rows(x_ref, gain_scr, sh_ref, inv_scr, h_scr)

    o_ref[...] = _dot(h_scr[...], w_ref[...])


def _mix_in(x, mod, norm_g, layer, w, tm, tn):
    b, s, d = x.shape
    n = w.shape[1]
    return pl.pallas_call(
        _mix_in_kernel,
        grid=(b, s // tm, n // tn),
        in_specs=[
            pl.BlockSpec((None, tm, d), lambda b, i, j: (b, i, 0)),
            _mod_spec(layer, 3, d, 3), _mod_spec(layer, 4, d, 3),
            _gain_spec(layer, 2, d, 3),
            pl.BlockSpec((d, tn), lambda b, i, j: (0, j)),
        ],
        out_specs=pl.BlockSpec((None, tm, tn), lambda b, i, j: (b, i, j)),
        out_shape=jax.ShapeDtypeStruct((b, s, n), F32),
        scratch_shapes=[pltpu.VMEM((tm, d), BF16), pltpu.VMEM((1, d), F32), pltpu.VMEM((tm, 1), F32)],
        compiler_params=pltpu.CompilerParams(
            dimension_semantics=("parallel", "parallel", "arbitrary"), vmem_limit_bytes=VMEM_LIMIT),
        name=f"mix_in_l{layer}",
    )(x, mod, mod, norm_g, w)


def _lane_halves(x, first_half_holds_data):
    lane = lax.broadcasted_iota(jnp.int32, x.shape, 1)
    if first_half_holds_data:
        lo = jnp.where(lane < HEAD_DIM, x, 0.0)
        return lo, pltpu.roll(lo, HEAD_DIM, 1)
    hi = jnp.where(lane >= HEAD_DIM, x, 0.0)
    return pltpu.roll(hi, HEAD_DIM, 1), hi


def _swa_kernel(sinks_ref, q_ref, kvc_ref, kvp_ref, o_ref):
    n = pl.program_id(1)
    w = BLOCK
    kcat = jnp.concatenate([kvp_ref[:, :KV_WIDTH], kvc_ref[:, :KV_WIDTH]], axis=0)
    vcat = jnp.concatenate([kvp_ref[:, KV_WIDTH:], kvc_ref[:, KV_WIDTH:]], axis=0)
    k_halves = [tuple(t.astype(BF16) for t in _lane_halves(kcat, hk == 0)) for hk in range(SWA_KV_HEADS)]
    v_halves = [tuple(t.astype(BF16) for t in _lane_halves(vcat, hk == 0)) for hk in range(SWA_KV_HEADS)]

    bands = SWA_HEADS // SWA_KV_HEADS // 2
    rows = bands * w
    r = lax.broadcasted_iota(jnp.int32, (rows, 2 * w), 0)
    j = lax.broadcasted_iota(jnp.int32, (rows, 2 * w), 1)
    dist = (r & (w - 1)) - j + w
    valid = (dist >= 0) & (dist < w) & ((n - 1) * w + j >= 0)
    band_step = _alibi_slope(2, SWA_HEADS) / _alibi_slope(0, SWA_HEADS)
    band_scale = jnp.ones((rows, 2 * w), F32)
    for band in range(1, bands):
        band_scale = jnp.where(r >= band * w, band_step ** band, band_scale)
    dist_scaled = dist.astype(F32) * band_scale
    band_col = lax.broadcasted_iota(jnp.int32, (rows, 1), 0)
    for hk in range(SWA_KV_HEADS):
        q = jnp.concatenate([q_ref[:, (bands * hk + band) * LANES:(bands * hk + band + 1) * LANES]
                             for band in range(bands)], axis=0).astype(BF16)
        out = None
        for par in range(2):
            head0 = 2 * bands * hk + par
            sink = jnp.full((rows, 1), sinks_ref[head0], F32)
            for band in range(1, bands):
                sink = jnp.where(band_col >= band * w, sinks_ref[head0 + 2 * band], sink)
            sc = _dot_nt(q, k_halves[hk][par]) * HEAD_DIM ** -0.5
            sc = jnp.where(valid, sc - _alibi_slope(head0, SWA_HEADS) * dist_scaled, -jnp.inf)
            m = jnp.maximum(jnp.max(sc, axis=-1, keepdims=True), sink)
            e = jnp.exp(sc - m)
            p = e / (jnp.sum(e, axis=-1, keepdims=True) + jnp.exp(sink - m))
            pv = _dot(p.astype(BF16), v_halves[hk][par])
            out = pv if out is None else out + pv
        for band in range(bands):
            tile = bands * hk + band
            o_ref[:, tile * LANES:(tile + 1) * LANES] = out[band * w:(band + 1) * w].astype(o_ref.dtype)


def _swa(proj, sinks, layer):
    b, s, _ = proj.shape
    nb = s // BLOCK
    kv_blk = COL_AKV // (2 * KV_WIDTH)
    return pl.pallas_call(
        _swa_kernel,
        grid=(b, nb),
        in_specs=[
            pl.BlockSpec(memory_space=pltpu.SMEM),
            pl.BlockSpec((None, BLOCK, A_WIDTH), lambda b, n: (b, n, COL_AQ // A_WIDTH)),
            pl.BlockSpec((None, BLOCK, 2 * KV_WIDTH), lambda b, n: (b, n, kv_blk)),
            pl.BlockSpec((None, BLOCK, 2 * KV_WIDTH), lambda b, n: (b, jnp.maximum(n - 1, 0), kv_blk)),
        ],
        out_specs=pl.BlockSpec((None, BLOCK, A_WIDTH), lambda b, n: (b, n, 0)),
        out_shape=jax.ShapeDtypeStruct((b, s, A_WIDTH), BF16),
        compiler_params=pltpu.CompilerParams(
            dimension_semantics=("parallel", "arbitrary"), vmem_limit_bytes=VMEM_LIMIT),
        name=f"swa_l{layer}",
    )(sinks, proj, proj, proj)


def _dsa_kernel(topk, q_ref, iq_ref, kv_ref, ikw_ref, kvn_ref, wuk_ref, wuv_ref, o_ref,
                ckv_scr, ik_scr, iqh_scr, score_scr, bias_scr, qlat_scr, s_scr, mrun_scr, m_scr,
                lpart_scr, acc_scr):
    n = pl.program_id(1)
    qb = DSA_QBLOCK
    kc = DSA_KCHUNK
    nchunks = lax.div(n * qb + qb - 1, kc) + 1

    @pl.when(n == 0)
    def _():
        ckv_scr[...] = _rms(kv_ref[...], kvn_ref[...]).astype(BF16)
        ik_scr[...] = ikw_ref[...].astype(BF16)

    row0 = pl.multiple_of(n * qb, qb)

    iw_t = ikw_ref[pl.ds(row0, qb), :].T
    iw_scale = IDX_HEADS ** -0.5 * IDX_DIM ** -0.5
    w_rows = [iw_t[IDX_DIM + h:IDX_DIM + h + 1, :] * iw_scale for h in range(IDX_HEADS)]
    lane = lax.broadcasted_iota(jnp.int32, (qb, LANES), 1)
    for h in range(IDX_HEADS):
        iq = iq_ref[:, (h // 2) * LANES:(h // 2 + 1) * LANES]
        if h % 2 == 0:
            iq = jnp.where(lane < IDX_DIM, iq, 0.0)
        else:
            iq = pltpu.roll(jnp.where(lane >= IDX_DIM, iq, 0.0), IDX_DIM, 1)
        iqh_scr[h] = iq.astype(BF16)

    key_row = lax.broadcasted_iota(jnp.int32, (kc, qb), 0)
    t_pos = n * qb + lax.broadcasted_iota(jnp.int32, (kc, qb), 1)

    def score_chunk(c, carry):
        off = pl.multiple_of(c * kc, kc)
        ik = ik_scr[pl.ds(off, kc), :]
        sc = jnp.zeros((kc, qb), F32)
        for h in range(IDX_HEADS):
            sc = sc + w_rows[h] * jnp.maximum(_dot_nt(ik, iqh_scr[h]), 0.0)
        score_scr[c] = jnp.where(off + key_row <= t_pos, sc, -jnp.inf)
        return carry

    lax.fori_loop(0, nchunks, score_chunk, 0)

    sub = 8

    def fold_rows(x, op):
        out = x[:sub]
        for g in range(1, x.shape[0] // sub):
            out = op(out, x[g * sub:(g + 1) * sub])
        return out

    def fold_lanes(x, op):
        out = x[:, :LANES]
        for t in range(1, x.shape[1] // LANES):
            out = op(out, x[:, t * LANES:(t + 1) * LANES])
        return out

    def count_ge(thr, strict=False):
        def body(c, cnt):
            x = score_scr[c]
            hit = (x > thr) if strict else (x >= thr)
            return cnt + fold_rows(jnp.where(hit, 1.0, 0.0), jnp.add)
        part = lax.fori_loop(0, nchunks, body, jnp.zeros((sub, qb), F32))
        return jnp.sum(part, axis=0, keepdims=True)

    def max_below(bound, strict):
        def body(c, mx):
            x = score_scr[c]
            keep = (x < bound) if strict else (x <= bound)
            return jnp.maximum(mx, fold_rows(jnp.where(keep, x, -jnp.inf), jnp.maximum))
        part = lax.fori_loop(0, nchunks, body, jnp.full((sub, qb), -jnp.inf, F32))
        return jnp.max(part, axis=0, keepdims=True)

    def store_bias(c, t, bias_t):
        for g in range(qb // LANES):
            bias_scr[c, g * LANES:(g + 1) * LANES, t * LANES:(t + 1) * LANES] = (
                bias_t[:, g * LANES:(g + 1) * LANES].T)

    @pl.when(n * qb + qb <= topk)
    def _():
        def body(c, carry):
            x = score_scr[c]
            for t in range(kc // LANES):
                store_bias(c, t, jnp.where(x[t * LANES:(t + 1) * LANES] == -jnp.inf, NEG, 0.0))
            return carry
        lax.fori_loop(0, nchunks, body, 0)

    @pl.when(n * qb + qb > topk)
    def _():
        kf = float(topk)

        def minmax(c, carry):
            mn, mx = carry
            x = score_scr[c]
            mn = jnp.minimum(mn, fold_rows(jnp.where(x == -jnp.inf, jnp.inf, x), jnp.minimum))
            return mn, jnp.maximum(mx, fold_rows(x, jnp.maximum))

        lo, hi = lax.fori_loop(0, nchunks, minmax,
                               (jnp.full((sub, qb), jnp.inf, F32), jnp.full((sub, qb), -jnp.inf, F32)))
        lo = jnp.min(lo, axis=0, keepdims=True)
        hi = jnp.max(hi, axis=0, keepdims=True)

        def bisect(_, carry):
            lo, hi = carry
            mid = 0.5 * (lo + hi)
            ge = count_ge(mid) >= kf
            return jnp.where(ge, mid, lo), jnp.where(ge, hi, mid)

        lo, hi = lax.fori_loop(0, BISECT_ITERS, bisect, (lo, hi))

        def walk_cond(carry):
            return carry[1] > 0.0

        def walk(carry):
            thr, _ = carry
            short = count_ge(thr) < kf
            thr = jnp.where(short, max_below(thr, True), thr)
            return thr, jnp.max(jnp.where(short, 1.0, 0.0))

        thr, _ = lax.while_loop(walk_cond, walk, (max_below(hi, False), jnp.float32(1.0)))

        need = kf - count_ge(thr, strict=True)

        r = lax.broadcasted_iota(jnp.int32, (LANES, LANES), 0)
        col = lax.broadcasted_iota(jnp.int32, (LANES, LANES), 1)
        tri = jnp.where(col <= r, 1.0, 0.0).astype(BF16)

        def select(c, seen):
            x = score_scr[c]
            for t in range(kc // LANES):
                xt = x[t * LANES:(t + 1) * LANES]
                eq = xt == thr
                eqf = jnp.where(eq, 1.0, 0.0)
                rank = seen + _dot(tri, eqf.astype(BF16))
                seen = seen + jnp.sum(eqf, axis=0, keepdims=True)
                sel = (xt > thr) | (eq & (rank <= need))
                store_bias(c, t, jnp.where(sel, 0.0, NEG))
            return seen

        lax.fori_loop(0, nchunks, select, jnp.zeros((1, qb), F32))

    for h in range(DSA_HEADS):
        tile = h // 2
        rows = slice(h * qb, (h + 1) * qb)
        q = q_ref[:, tile * LANES:(tile + 1) * LANES].astype(BF16)
        qlat_scr[rows, :] = (_dot(q, wuk_ref[h]) * HEAD_DIM ** -0.5).astype(BF16)
    mrun_scr[...] = jnp.full(mrun_scr.shape, NEG, F32)
    key_lane = lax.broadcasted_iota(jnp.int32, (1, kc), 1)

    def logits(c, carry):
        off = pl.multiple_of(c * kc, kc)
        s = _dot_nt(qlat_scr[...], ckv_scr[pl.ds(off, kc), :])
        bias = bias_scr[c]
        key_pos = (off + key_lane).astype(F32)
        for h in range(DSA_HEADS):
            rows = slice(h * qb, (h + 1) * qb)
            sh = s[rows] + (bias + _alibi_slope(h, DSA_HEADS) * key_pos)
            s_scr[c, rows, :] = sh
            mrun_scr[rows, :] = jnp.maximum(mrun_scr[rows, :], fold_lanes(sh, jnp.maximum))
        return carry

    lax.fori_loop(0, nchunks, logits, 0)
    m_scr[...] = jnp.broadcast_to(jnp.max(mrun_scr[...], axis=-1, keepdims=True), m_scr.shape)
    lpart_scr[...] = jnp.zeros_like(lpart_scr)
    acc_scr[...] = jnp.zeros_like(acc_scr)

    def attend(c, carry):
        off = pl.multiple_of(c * kc, kc)
        m = m_scr[...]
        p = [jnp.exp(s_scr[c, :, t * LANES:(t + 1) * LANES] - m) for t in range(kc // LANES)]
        lpart_scr[...] += functools.reduce(jnp.add, p)
        acc_scr[...] += _dot(jnp.concatenate(p, axis=1).astype(BF16), ckv_scr[pl.ds(off, kc), :])
        return carry

    lax.fori_loop(0, nchunks, attend, 0)

    o_all = (acc_scr[...] / jnp.sum(lpart_scr[...], axis=-1, keepdims=True)).astype(BF16)
    for tile in range(DSA_HEADS // 2):
        h = 2 * tile
        out = _dot(o_all[h * qb:(h + 1) * qb], wuv_ref[h]) + _dot(o_all[(h + 1) * qb:(h + 2) * qb], wuv_ref[h + 1])
        o_ref[:, tile * LANES:(tile + 1) * LANES] = out.astype(o_ref.dtype)


def _dsa(proj, kv_norm, wuk_pad, wuv_pad, layer):
    b, s, _ = proj.shape
    qb = DSA_QBLOCK
    nb = s // qb
    topk = min(DSA_TOPK_MAX, s // 4)
    assert topk % qb == 0
    nck = s // DSA_KCHUNK
    rows = DSA_HEADS * qb
    return pl.pallas_call(
        functools.partial(_dsa_kernel, topk),
        grid=(b, nb),
        in_specs=[
            pl.BlockSpec((None, qb, B_WIDTH), lambda b, n: (b, n, COL_BQ // B_WIDTH)),
            pl.BlockSpec((None, qb, B_WIDTH), lambda b, n: (b, n, COL_BIQ // B_WIDTH)),
            pl.BlockSpec((None, s, DSA_RANK), lambda b, n: (b, 0, COL_BKV // DSA_RANK)),
            pl.BlockSpec((None, s, LANES), lambda b, n: (b, 0, COL_BIK // LANES)),
            pl.BlockSpec((1, DSA_RANK), lambda b, n: (0, 0)),
            pl.BlockSpec((DSA_HEADS, LANES, DSA_RANK), lambda b, n: (0, 0, 0)),
            pl.BlockSpec((DSA_HEADS, DSA_RANK, LANES), lambda b, n: (0, 0, 0)),
        ],
        out_specs=pl.BlockSpec((None, qb, B_WIDTH), lambda b, n: (b, n, 0)),
        out_shape=jax.ShapeDtypeStruct((b, s, B_WIDTH), BF16),
        scratch_shapes=[
            pltpu.VMEM((s, DSA_RANK), BF16),
            pltpu.VMEM((s, LANES), BF16),
            pltpu.VMEM((IDX_HEADS, qb, LANES), BF16),
            pltpu.VMEM((nck, DSA_KCHUNK, qb), F32),
            pltpu.VMEM((nck, qb, DSA_KCHUNK), F32),
            pltpu.VMEM((rows, DSA_RANK), BF16),
            pltpu.VMEM((nck, rows, DSA_KCHUNK), F32),
            pltpu.VMEM((rows, LANES), F32),
            pltpu.VMEM((rows, LANES), F32),
            pltpu.VMEM((rows, LANES), F32),
            pltpu.VMEM((rows, DSA_RANK), F32),
        ],
        compiler_params=pltpu.CompilerParams(
            dimension_semantics=("parallel", "arbitrary"), vmem_limit_bytes=VMEM_LIMIT),
        name=f"dsa_l{layer}",
    )(proj, proj, proj, proj, kv_norm, wuk_pad, wuv_pad)


def _ret_kernel(q_ref, k_ref, v_ref, g_ref, gn_ref, o_ref, state_scr):
    cs = BLOCK

    @pl.when(pl.program_id(1) == 0)
    def _():
        state_scr[...] = jnp.zeros_like(state_scr)

    row = lax.broadcasted_iota(jnp.int32, (cs, LANES), 0)
    lane = lax.broadcasted_iota(jnp.int32, (cs, LANES), 1)
    first = lane < HEAD_DIM
    rowf = row.astype(F32)
    diff = (row - lane).astype(F32)
    same_head = (row < HEAD_DIM) == first
    seg_mean = jnp.where(same_head, 1.0 / HEAD_DIM, 0.0).astype(BF16)

    def seg_mean_dot(x):
        x_hi = x.astype(BF16)
        x_lo = (x - x_hi.astype(F32)).astype(BF16)
        return _dot(x_hi, seg_mean) + _dot(x_lo, seg_mean)

    tiles = RET_HEADS // 2
    outs = []
    for bi in range(q_ref.shape[0]):
        for tile in range(tiles):
            cols = slice(tile * LANES, (tile + 1) * LANES)
            lg = [math.log(1.0 - 2.0 ** (-5.0 - (2 * tile + par))) for par in range(2)]
            lg_lane = jnp.where(first, lg[0], lg[1])
            q = q_ref[bi, :, cols].astype(BF16)
            k = k_ref[bi, :, cols] * HEAD_DIM ** -0.5
            v = v_ref[bi, :, cols]
            state = state_scr[bi, tile]
            out = _dot(q, state.astype(BF16)) * jnp.exp(lg_lane * (rowf + 1.0))
            for par in range(2):
                keep = first if par == 0 else ~first
                decay = jnp.where(diff >= 0, jnp.exp(lg[par] * jnp.maximum(diff, 0.0)), 0.0)
                inner = _dot_nt(q, jnp.where(keep, k, 0.0).astype(BF16)) * decay
                out = out + _dot(inner.astype(BF16), jnp.where(keep, v, 0.0).astype(BF16))
            k_dec = (k * jnp.exp(lg_lane * (cs - 1.0 - rowf))).astype(BF16)
            kv = _dot_tn(k_dec, v.astype(BF16))
            state_scr[bi, tile] = state * jnp.exp(lg_lane * cs) + jnp.where(same_head, kv, 0.0)
            outs.append(out)

    out = jnp.concatenate(outs, axis=0)
    mu = seg_mean_dot(out)
    cen = out - mu
    var = seg_mean_dot(cen * cen)
    yn = cen * lax.rsqrt(var + EPS)
    for bi in range(q_ref.shape[0]):
        for tile in range(tiles):
            cols = slice(tile * LANES, (tile + 1) * LANES)
            y = yn[(bi * tiles + tile) * cs:(bi * tiles + tile + 1) * cs] * gn_ref[:, cols]
            o_ref[bi, :, cols] = (y * _silu(g_ref[bi, :, cols])).astype(o_ref.dtype)


RET_BATCH = 4


def _retention(proj, ret_norm, layer):
    b, s, _ = proj.shape
    nc = s // BLOCK
    rb = math.gcd(b, RET_BATCH)
    spec = lambda col: pl.BlockSpec((rb, BLOCK, C_WIDTH), lambda b, c: (b, c, col // C_WIDTH))
    return pl.pallas_call(
        _ret_kernel,
        grid=(b // rb, nc),
        in_specs=[spec(COL_CQ), spec(COL_CK), spec(COL_CV), spec(COL_CG),
                  pl.BlockSpec((1, C_WIDTH), lambda b, c: (0, 0))],
        out_specs=pl.BlockSpec((rb, BLOCK, C_WIDTH), lambda b, c: (b, c, 0)),
        out_shape=jax.ShapeDtypeStruct((b, s, C_WIDTH), BF16),
        scratch_shapes=[pltpu.VMEM((rb, RET_HEADS // 2, LANES, LANES), F32)],
        compiler_params=pltpu.CompilerParams(
            dimension_semantics=("parallel", "arbitrary"), vmem_limit_bytes=VMEM_LIMIT),
        name=f"ret_l{layer}",
    )(proj, proj, proj, proj, ret_norm)


def _mix_out_kernel(x_ref, gt_ref, gpost_ref, oa_ref, ob_ref, oc_ref, w_ref, o_ref, gain_scr, inv_scr):
    y = _dot(oa_ref[...], w_ref[:A_WIDTH, :])
    y = y + _dot(ob_ref[...], w_ref[A_WIDTH:A_WIDTH + B_WIDTH, :])
    o_ref[...] = y + _dot(oc_ref[...], w_ref[A_WIDTH + B_WIDTH:, :])
    gain_scr[...] = gt_ref[...] * gpost_ref[...]
    _postnorm_residual_rows(o_ref, x_ref, gain_scr, inv_scr, o_ref)


def _mix_out(x, mod, norm_g, layer, oa, ob, oc, w, tm):
    b, s, d = x.shape
    row = lambda width: pl.BlockSpec((None, tm, width), lambda b, i: (b, i, 0))
    return pl.pallas_call(
        _mix_out_kernel,
        grid=(b, s // tm),
        in_specs=[
            row(d), _mod_spec(layer, 5, d, 2), _gain_spec(layer, 3, d, 2),
            row(A_WIDTH), row(B_WIDTH), row(C_WIDTH),
            pl.BlockSpec(w.shape, lambda b, i: (0, 0)),
        ],
        out_specs=row(d),
        out_shape=jax.ShapeDtypeStruct((b, s, d), F32),
        scratch_shapes=[pltpu.VMEM((1, d), F32), pltpu.VMEM((tm, 1), F32)],
        compiler_params=pltpu.CompilerParams(
            dimension_semantics=("parallel", "parallel"), vmem_limit_bytes=VMEM_LIMIT),
        name=f"mix_out_l{layer}",
    )(x, mod, norm_g, oa, ob, oc, w)


def _prep_mix_in(w):
    sizes = [A_WIDTH, KV_WIDTH, KV_WIDTH, B_WIDTH, DSA_RANK, IDX_HEADS * IDX_DIM, IDX_DIM, IDX_HEADS,
             C_WIDTH, C_WIDTH, C_WIDTH, C_WIDTH]
    starts = [0]
    for sz in sizes:
        starts.append(starts[-1] + sz)
    aq, ak, av, bq, bkv, biq, bik, biw, cq, ck, cv, cg = [w[:, a:a + sz] for a, sz in zip(starts, sizes)]
    tail = jnp.zeros((w.shape[0], LANES - IDX_DIM - IDX_HEADS), w.dtype)
    return jnp.concatenate([aq, bq, biq, cq, ck, cv, cg, ak, av, bkv, bik, biw, tail], axis=1).astype(BF16)


def _prep_dsa_up(w_uk, w_uv):
    r, h, dh = w_uk.shape
    uk = jnp.transpose(w_uk, (1, 2, 0))
    uv = jnp.transpose(w_uv, (1, 0, 2))
    odd = (jnp.arange(h) % 2 == 1)[:, None, None]
    zk = jnp.zeros_like(uk)
    zv = jnp.zeros_like(uv)
    uk_pad = jnp.where(odd, jnp.concatenate([zk, uk], axis=1), jnp.concatenate([uk, zk], axis=1))
    uv_pad = jnp.where(odd, jnp.concatenate([zv, uv], axis=2), jnp.concatenate([uv, zv], axis=2))
    return uk_pad.astype(BF16), uv_pad.astype(BF16)


FFN_TM = 512
FFN_TF = 512
MIX_IN_TM = 1024
MIX_IN_TN = 1536
MIX_OUT_TM = 512


def kernel(x, c, ada_w, ada_b, norm_g, ffn1_w_in, ffn1_w_out, ffn2_w_in, ffn2_w_out,
           mix_w_in, mix_w_out, swa_sinks, dsa_kv_norm, dsa_w_uk, dsa_w_uv, ret_norm):
    depth = ada_w.shape[0]
    b, s, d = x.shape
    assert s % FFN_TM == 0 and s % DSA_KCHUNK == 0 and d % LANES == 0
    mod = _ada_mod(c, ada_w, ada_b).reshape(depth, b, N_MOD, 1, d)
    gains = norm_g.reshape(depth, norm_g.shape[1], 1, d)
    for l in range(depth):
        x = _ffn_block(x, mod, gains, l, 0, ffn1_w_in[l].astype(BF16), ffn1_w_out[l].astype(BF16),
                       0.5, FFN_TM, FFN_TF)

        proj = _mix_in(x, mod, gains, l, _prep_mix_in(mix_w_in[l]), MIX_IN_TM, MIX_IN_TN)
        oa = _swa(proj, swa_sinks[l], l)
        uk_pad, uv_pad = _prep_dsa_up(dsa_w_uk[l], dsa_w_uv[l])
        ob = _dsa(proj, dsa_kv_norm[l].reshape(1, -1), uk_pad, uv_pad, l)
        oc = _retention(proj, ret_norm[l].reshape(1, -1), l)
        x = _mix_out(x, mod, gains, l, oa, ob, oc, mix_w_out[l].astype(BF16), MIX_OUT_TM)

        x = _ffn_block(x, mod, gains, l, 2, ffn2_w_in[l].astype(BF16), ffn2_w_out[l].astype(BF16),
                       0.5, FFN_TM, FFN_TF)
    return x
```

```python
import functools
import math

import jax
import jax.numpy as jnp
from jax import lax
from jax.experimental import pallas as pl
from jax.experimental.pallas import tpu as pltpu

F32 = jnp.float32
BF16 = jnp.bfloat16

LANES = 128
HEAD_DIM = 64
SWA_HEADS = 16
SWA_KV_HEADS = 2
BLOCK = 128
DSA_HEADS = 8
DSA_RANK = 128
IDX_HEADS = 8
IDX_DIM = 64
DSA_TOPK_MAX = 256
RET_HEADS = 8
N_MOD = 9
EPS = 1e-6
NEG = -1e30

A_WIDTH = SWA_HEADS * HEAD_DIM
B_WIDTH = DSA_HEADS * HEAD_DIM
C_WIDTH = RET_HEADS * HEAD_DIM
KV_WIDTH = SWA_KV_HEADS * HEAD_DIM

COL_AQ = 0
COL_BQ = COL_AQ + A_WIDTH
COL_BIQ = COL_BQ + B_WIDTH
COL_CQ = COL_BIQ + IDX_HEADS * IDX_DIM
COL_CK = COL_CQ + C_WIDTH
COL_CV = COL_CK + C_WIDTH
COL_CG = COL_CV + C_WIDTH
COL_AKV = COL_CG + C_WIDTH
COL_BKV = COL_AKV + 2 * KV_WIDTH
COL_BIK = COL_BKV + DSA_RANK
MIX_PAD_WIDTH = COL_BIK + LANES

DSA_QBLOCK = 256
DSA_KCHUNK = 256
BISECT_ITERS = 24
VMEM_LIMIT = 56 * 1024 * 1024


def _dot(a, b):
    return jnp.dot(a, b, preferred_element_type=F32)


def _dot_nt(a, b):
    return lax.dot_general(a, b, (((1,), (1,)), ((), ())), preferred_element_type=F32)


def _dot_tn(a, b):
    return lax.dot_general(a, b, (((0,), (0,)), ((), ())), preferred_element_type=F32)


def _silu(x):
    return x / (1.0 + jnp.exp(-x))


def _rms(x, g):
    return x * lax.rsqrt(jnp.mean(x * x, axis=-1, keepdims=True) + EPS) * g


def _alibi_slope(h, n):
    return 2.0 ** (-8.0 * (h + 1) / n)


def _ada_kernel(c_ref, w_ref, b_ref, o_ref):
    cond = _silu(c_ref[...]).astype(BF16)
    o_ref[...] = _dot(cond, w_ref[...].astype(BF16)) + b_ref[...]


def _ada_mod(c, ada_w, ada_b):
    depth, d, n = ada_w.shape
    b = c.shape[0]
    tn = 1024
    return pl.pallas_call(
        _ada_kernel,
        grid=(depth, n // tn),
        in_specs=[
            pl.BlockSpec((b, d), lambda l, j: (0, 0)),
            pl.BlockSpec((None, d, tn), lambda l, j: (l, 0, j)),
            pl.BlockSpec((None, 1, tn), lambda l, j: (l, 0, j)),
        ],
        out_specs=pl.BlockSpec((None, b, tn), lambda l, j: (l, 0, j)),
        out_shape=jax.ShapeDtypeStruct((depth, b, n), F32),
        compiler_params=pltpu.CompilerParams(
            dimension_semantics=("arbitrary", "arbitrary"), vmem_limit_bytes=VMEM_LIMIT),
        name="ada_mod",
    )(c, ada_w, ada_b.reshape(depth, 1, n))


def _mod_spec(layer, k, d, grid_rank):
    if grid_rank == 2:
        return pl.BlockSpec((None, None, None, 1, d), lambda b, i: (layer, b, k, 0, 0))
    return pl.BlockSpec((None, None, None, 1, d), lambda b, i, j: (layer, b, k, 0, 0))


def _gain_spec(layer, k, d, grid_rank):
    if grid_rank == 2:
        return pl.BlockSpec((None, None, 1, d), lambda b, i: (layer, k, 0, 0))
    return pl.BlockSpec((None, None, 1, d), lambda b, i, j: (layer, k, 0, 0))


ROW_CHUNK = 16


def _row_chunks(n_rows, body):
    def step(i, carry):
        body(pl.ds(pl.multiple_of(i * ROW_CHUNK, ROW_CHUNK), ROW_CHUNK))
        return carry
    lax.fori_loop(0, n_rows // ROW_CHUNK, step, 0, unroll=True)


def _inv_rms_rows(x_ref, inv_ref):
    def body(r):
        x = x_ref[r, :]
        inv_ref[r, :] = lax.rsqrt(jnp.mean(x * x, axis=-1, keepdims=True) + EPS)
    _row_chunks(x_ref.shape[0], body)


def _prenorm_mod_rows(x_ref, gain_ref, shift_ref, inv_ref, h_ref):
    _inv_rms_rows(x_ref, inv_ref)

    def body(r):
        h_ref[r, :] = (x_ref[r, :] * inv_ref[r, :] * gain_ref[...] + shift_ref[...]).astype(h_ref.dtype)
    _row_chunks(x_ref.shape[0], body)


def _postnorm_residual_rows(y_ref, x_ref, gain_ref, inv_ref, o_ref):
    _inv_rms_rows(y_ref, inv_ref)

    def body(r):
        o_ref[r, :] = x_ref[r, :] + y_ref[r, :] * inv_ref[r, :] * gain_ref[...]
    _row_chunks(y_ref.shape[0], body)


def _ffn_kernel(res_w, tail, x_ref, sh_ref, sc_ref, gt_ref, gpre_ref, gpost_ref, wg_ref, wu_ref, wo_ref,
                o_ref, h_scr, acc_scr, gain_scr, inv_scr):
    f = pl.program_id(2)
    last = pl.num_programs(2) - 1
    tf = wo_ref.shape[1]

    @pl.when(f == 0)
    def _():
        gain_scr[...] = gpre_ref[...] * (1.0 + sc_ref[...])
        _prenorm_mod_rows(x_ref, gain_scr, sh_ref, inv_scr, h_scr)
        acc_scr[...] = jnp.zeros_like(acc_scr)

    def step(lo):
        h = h_scr[...]
        act = _silu(_dot(h, wg_ref[0, :, lo:])) * _dot(h, wu_ref[0, :, lo:])
        acc_scr[...] += _dot(act.astype(BF16), wo_ref[0, lo:, :])

    @pl.when(f < last)
    def _():
        step(0)

    @pl.when(f == last)
    def _():
        step(tf - tail)
        gain_scr[...] = res_w * gt_ref[...] * gpost_ref[...]
        _postnorm_residual_rows(acc_scr, x_ref, gain_scr, inv_scr, o_ref)


def _ffn_block(x, mod, norm_g, layer, sub, w_in, w_out, res_w, tm, tf):
    b, s, d = x.shape
    d_ff = w_out.shape[1]
    nf = pl.cdiv(d_ff, tf)
    tail = d_ff - (nf - 1) * tf
    assert tail % LANES == 0 and d_ff >= tf
    mk = 3 * sub
    start = lambda j, base: pl.multiple_of(base + jnp.minimum(j * tf, d_ff - tf), LANES)
    col = lambda base: (lambda b, i, j: (layer, 0, start(j, base)))
    one = pl.Element(1)
    return pl.pallas_call(
        functools.partial(_ffn_kernel, res_w, tail),
        grid=(b, s // tm, nf),
        in_specs=[
            pl.BlockSpec((None, tm, d), lambda b, i, j: (b, i, 0)),
            _mod_spec(layer, mk, d, 3), _mod_spec(layer, mk + 1, d, 3), _mod_spec(layer, mk + 2, d, 3),
            _gain_spec(layer, 2 * sub, d, 3), _gain_spec(layer, 2 * sub + 1, d, 3),
            pl.BlockSpec((one, pl.Element(d), pl.Element(tf)), col(0)),
            pl.BlockSpec((one, pl.Element(d), pl.Element(tf)), col(d_ff)),
            pl.BlockSpec((one, pl.Element(tf), pl.Element(d)), lambda b, i, j: (layer, start(j, 0), 0)),
        ],
        out_specs=pl.BlockSpec((None, tm, d), lambda b, i, j: (b, i, 0)),
        out_shape=jax.ShapeDtypeStruct((b, s, d), F32),
        scratch_shapes=[pltpu.VMEM((tm, d), BF16), pltpu.VMEM((tm, d), F32), pltpu.VMEM((1, d), F32),
                        pltpu.VMEM((tm, 1), F32)],
        compiler_params=pltpu.CompilerParams(
            dimension_semantics=("parallel", "parallel", "arbitrary"), vmem_limit_bytes=VMEM_LIMIT),
        name=f"ffn_l{layer}_s{sub}",
    )(x, mod, mod, mod, norm_g, norm_g, w_in, w_in, w_out)


def _mix_in_kernel(x_ref, sh_ref, sc_ref, gpre_ref, w_ref, o_ref, h_scr, gain_scr, inv_scr):
    @pl.when(pl.program_id(2) == 0)
    def _():
        gain_scr[...] = gpre_ref[...] * (1.0 + sc_ref[...])
        _prenorm_mod_rows(x_ref, gain_scr, sh_ref, inv_scr, h_scr)

    o_ref[...] = _dot(h_scr[...], w_ref[...])


def _mix_in(x, mod, norm_g, layer, w, tm, tn):
    b, s, d = x.shape
    n = w.shape[1]
    return pl.pallas_call(
        _mix_in_kernel,
        grid=(b, s // tm, n // tn),
        in_specs=[
            pl.BlockSpec((None, tm, d), lambda b, i, j: (b, i, 0)),
            _mod_spec(layer, 3, d, 3), _mod_spec(layer, 4, d, 3),
            _gain_spec(layer, 2, d, 3),
            pl.BlockSpec((d, tn), lambda b, i, j: (0, j)),
        ],
        out_specs=pl.BlockSpec((None, tm, tn), lambda b, i, j: (b, i, j)),
        out_shape=jax.ShapeDtypeStruct((b, s, n), F32),
        scratch_shapes=[pltpu.VMEM((tm, d), BF16), pltpu.VMEM((1, d), F32), pltpu.VMEM((tm, 1), F32)],
        compiler_params=pltpu.CompilerParams(
            dimension_semantics=("parallel", "parallel", "arbitrary"), vmem_limit_bytes=VMEM_LIMIT),
        name=f"mix_in_l{layer}",
    )(x, mod, mod, norm_g, w)


def _lane_halves(x, first_half_holds_data):
    lane = lax.broadcasted_iota(jnp.int32, x.shape, 1)
    if first_half_holds_data:
        lo = jnp.where(lane < HEAD_DIM, x, 0.0)
        return lo, pltpu.roll(lo, HEAD_DIM, 1)
    hi = jnp.where(lane >= HEAD_DIM, x, 0.0)
    return pltpu.roll(hi, HEAD_DIM, 1), hi


def _swa_kernel(sinks_ref, q_ref, kvc_ref, kvp_ref, o_ref):
    n = pl.program_id(1)
    w = BLOCK
    kcat = jnp.concatenate([kvp_ref[:, :KV_WIDTH], kvc_ref[:, :KV_WIDTH]], axis=0)
    vcat = jnp.concatenate([kvp_ref[:, KV_WIDTH:], kvc_ref[:, KV_WIDTH:]], axis=0)
    k_halves = [tuple(t.astype(BF16) for t in _lane_halves(kcat, hk == 0)) for hk in range(SWA_KV_HEADS)]
    v_halves = [tuple(t.astype(BF16) for t in _lane_halves(vcat, hk == 0)) for hk in range(SWA_KV_HEADS)]

    bands = SWA_HEADS // SWA_KV_HEADS // 2
    rows = bands * w
    r = lax.broadcasted_iota(jnp.int32, (rows, 2 * w), 0)
    j = lax.broadcasted_iota(jnp.int32, (rows, 2 * w), 1)
    dist = (r & (w - 1)) - j + w
    valid = (dist >= 0) & (dist < w) & ((n - 1) * w + j >= 0)
    band_step = _alibi_slope(2, SWA_HEADS) / _alibi_slope(0, SWA_HEADS)
    band_scale = jnp.ones((rows, 2 * w), F32)
    for band in range(1, bands):
        band_scale = jnp.where(r >= band * w, band_step ** band, band_scale)
    dist_scaled = dist.astype(F32) * band_scale
    band_col = lax.broadcasted_iota(jnp.int32, (rows, 1), 0)
    for hk in range(SWA_KV_HEADS):
        q = jnp.concatenate([q_ref[:, (bands * hk + band) * LANES:(bands * hk + band + 1) * LANES]
                             for band in range(bands)], axis=0).astype(BF16)
        out = None
        for par in range(2):
            head0 = 2 * bands * hk + par
            sink = jnp.full((rows, 1), sinks_ref[head0], F32)
            for band in range(1, bands):
                sink = jnp.where(band_col >= band * w, sinks_ref[head0 + 2 * band], sink)
            sc = _dot_nt(q, k_halves[hk][par]) * HEAD_DIM ** -0.5
            sc = jnp.where(valid, sc - _alibi_slope(head0, SWA_HEADS) * dist_scaled, -jnp.inf)
            m = jnp.maximum(jnp.max(sc, axis=-1, keepdims=True), sink)
            e = jnp.exp(sc - m)
            p = e / (jnp.sum(e, axis=-1, keepdims=True) + jnp.exp(sink - m))
            pv = _dot(p.astype(BF16), v_halves[hk][par])
            out = pv if out is None else out + pv
        for band in range(bands):
            tile = bands * hk + band
            o_ref[:, tile * LANES:(tile + 1) * LANES] = out[band * w:(band + 1) * w].astype(o_ref.dtype)


def _swa(proj, sinks, layer):
    b, s, _ = proj.shape
    nb = s // BLOCK
    kv_blk = COL_AKV // (2 * KV_WIDTH)
    return pl.pallas_call(
        _swa_kernel,
        grid=(b, nb),
        in_specs=[
            pl.BlockSpec(memory_space=pltpu.SMEM),
            pl.BlockSpec((None, BLOCK, A_WIDTH), lambda b, n: (b, n, COL_AQ // A_WIDTH)),
            pl.BlockSpec((None, BLOCK, 2 * KV_WIDTH), lambda b, n: (b, n, kv_blk)),
            pl.BlockSpec((None, BLOCK, 2 * KV_WIDTH), lambda b, n: (b, jnp.maximum(n - 1, 0), kv_blk)),
        ],
        out_specs=pl.BlockSpec((None, BLOCK, A_WIDTH), lambda b, n: (b, n, 0)),
        out_shape=jax.ShapeDtypeStruct((b, s, A_WIDTH), BF16),
        compiler_params=pltpu.CompilerParams(
            dimension_semantics=("parallel", "arbitrary"), vmem_limit_bytes=VMEM_LIMIT),
        name=f"swa_l{layer}",
    )(sinks, proj, proj, proj)


def _dsa_kernel(topk, q_ref, iq_ref, kv_ref, ikw_ref, kvn_ref, wuk_ref, wuv_ref, o_ref,
                ckv_scr, ik_scr, iqh_scr, score_scr, bias_scr, qlat_scr, s_scr, mrun_scr, m_scr,
                lpart_scr, acc_scr):
    n = pl.program_id(1)
    qb = DSA_QBLOCK
    kc = DSA_KCHUNK
    nchunks = lax.div(n * qb + qb - 1, kc) + 1

    @pl.when(n == 0)
    def _():
        ckv_scr[...] = _rms(kv_ref[...], kvn_ref[...]).astype(BF16)
        ik_scr[...] = ikw_ref[...].astype(BF16)

    row0 = pl.multiple_of(n * qb, qb)

    iw_t = ikw_ref[pl.ds(row0, qb), :].T
    iw_scale = IDX_HEADS ** -0.5 * IDX_DIM ** -0.5
    w_rows = [iw_t[IDX_DIM + h:IDX_DIM + h + 1, :] * iw_scale for h in range(IDX_HEADS)]
    lane = lax.broadcasted_iota(jnp.int32, (qb, LANES), 1)
    for h in range(IDX_HEADS):
        iq = iq_ref[:, (h // 2) * LANES:(h // 2 + 1) * LANES]
        if h % 2 == 0:
            iq = jnp.where(lane < IDX_DIM, iq, 0.0)
        else:
            iq = pltpu.roll(jnp.where(lane >= IDX_DIM, iq, 0.0), IDX_DIM, 1)
        iqh_scr[h] = iq.astype(BF16)

    key_row = lax.broadcasted_iota(jnp.int32, (kc, qb), 0)
    t_pos = n * qb + lax.broadcasted_iota(jnp.int32, (kc, qb), 1)

    def score_chunk(c, carry):
        off = pl.multiple_of(c * kc, kc)
        ik = ik_scr[pl.ds(off, kc), :]
        sc = jnp.zeros((kc, qb), F32)
        for h in range(IDX_HEADS):
            sc = sc + w_rows[h] * jnp.maximum(_dot_nt(ik, iqh_scr[h]), 0.0)
        score_scr[c] = jnp.where(off + key_row <= t_pos, sc, -jnp.inf)
        return carry

    lax.fori_loop(0, nchunks, score_chunk, 0)

    sub = 8

    def fold_rows(x, op):
        out = x[:sub]
        for g in range(1, x.shape[0] // sub):
            out = op(out, x[g * sub:(g + 1) * sub])
        return out

    def fold_lanes(x, op):
        out = x[:, :LANES]
        for t in range(1, x.shape[1] // LANES):
            out = op(out, x[:, t * LANES:(t + 1) * LANES])
        return out

    def count_ge(thr, strict=False):
        def body(c, cnt):
            x = score_scr[c]
            hit = (x > thr) if strict else (x >= thr)
            return cnt + fold_rows(jnp.where(hit, 1.0, 0.0), jnp.add)
        part = lax.fori_loop(0, nchunks, body, jnp.zeros((sub, qb), F32))
        return jnp.sum(part, axis=0, keepdims=True)

    def max_below(bound, strict):
        def body(c, mx):
            x = score_scr[c]
            keep = (x < bound) if strict else (x <= bound)
            return jnp.maximum(mx, fold_rows(jnp.where(keep, x, -jnp.inf), jnp.maximum))
        part = lax.fori_loop(0, nchunks, body, jnp.full((sub, qb), -jnp.inf, F32))
        return jnp.max(part, axis=0, keepdims=True)

    def store_bias(c, t, bias_t):
        for g in range(qb // LANES):
            bias_scr[c, g * LANES:(g + 1) * LANES, t * LANES:(t + 1) * LANES] = (
                bias_t[:, g * LANES:(g + 1) * LANES].T)

    @pl.when(n * qb + qb <= topk)
    def _():
        def body(c, carry):
            x = score_scr[c]
            for t in range(kc // LANES):
                store_bias(c, t, jnp.where(x[t * LANES:(t + 1) * LANES] == -jnp.inf, NEG, 0.0))
            return carry
        lax.fori_loop(0, nchunks, body, 0)

    @pl.when(n * qb + qb > topk)
    def _():
        kf = float(topk)

        def minmax(c, carry):
            mn, mx = carry
            x = score_scr[c]
            mn = jnp.minimum(mn, fold_rows(jnp.where(x == -jnp.inf, jnp.inf, x), jnp.minimum))
            return mn, jnp.maximum(mx, fold_rows(x, jnp.maximum))

        lo, hi = lax.fori_loop(0, nchunks, minmax,
                               (jnp.full((sub, qb), jnp.inf, F32), jnp.full((sub, qb), -jnp.inf, F32)))
        lo = jnp.min(lo, axis=0, keepdims=True)
        hi = jnp.max(hi, axis=0, keepdims=True)

        def bisect(_, carry):
            lo, hi = carry
            mid = 0.5 * (lo + hi)
            ge = count_ge(mid) >= kf
            return jnp.where(ge, mid, lo), jnp.where(ge, hi, mid)

        lo, hi = lax.fori_loop(0, BISECT_ITERS, bisect, (lo, hi))

        def walk_cond(carry):
            return carry[1] > 0.0

        def walk(carry):
            thr, _ = carry
            short = count_ge(thr) < kf
            thr = jnp.where(short, max_below(thr, True), thr)
            return thr, jnp.max(jnp.where(short, 1.0, 0.0))

        thr, _ = lax.while_loop(walk_cond, walk, (max_below(hi, False), jnp.float32(1.0)))

        need = kf - count_ge(thr, strict=True)

        r = lax.broadcasted_iota(jnp.int32, (LANES, LANES), 0)
        col = lax.broadcasted_iota(jnp.int32, (LANES, LANES), 1)
        tri = jnp.where(col <= r, 1.0, 0.0).astype(BF16)

        def select(c, seen):
            x = score_scr[c]
            for t in range(kc // LANES):
                xt = x[t * LANES:(t + 1) * LANES]
                eq = xt == thr
                eqf = jnp.where(eq, 1.0, 0.0)
                rank = seen + _dot(tri, eqf.astype(BF16))
                seen = seen + jnp.sum(eqf, axis=0, keepdims=True)
                sel = (xt > thr) | (eq & (rank <= need))
                store_bias(c, t, jnp.where(sel, 0.0, NEG))
            return seen

        lax.fori_loop(0, nchunks, select, jnp.zeros((1, qb), F32))

    for h in range(DSA_HEADS):
        tile = h // 2
        rows = slice(h * qb, (h + 1) * qb)
        q = q_ref[:, tile * LANES:(tile + 1) * LANES].astype(BF16)
        qlat_scr[rows, :] = (_dot(q, wuk_ref[h]) * HEAD_DIM ** -0.5).astype(BF16)
    mrun_scr[...] = jnp.full(mrun_scr.shape, NEG, F32)
    key_lane = lax.broadcasted_iota(jnp.int32, (1, kc), 1)

    def logits(c, carry):
        off = pl.multiple_of(c * kc, kc)
        s = _dot_nt(qlat_scr[...], ckv_scr[pl.ds(off, kc), :])
        bias = bias_scr[c]
        key_pos = (off + key_lane).astype(F32)
        for h in range(DSA_HEADS):
            rows = slice(h * qb, (h + 1) * qb)
            sh = s[rows] + (bias + _alibi_slope(h, DSA_HEADS) * key_pos)
            s_scr[c, rows, :] = sh
            mrun_scr[rows, :] = jnp.maximum(mrun_scr[rows, :], fold_lanes(sh, jnp.maximum))
        return carry

    lax.fori_loop(0, nchunks, logits, 0)
    m_scr[...] = jnp.broadcast_to(jnp.max(mrun_scr[...], axis=-1, keepdims=True), m_scr.shape)
    lpart_scr[...] = jnp.zeros_like(lpart_scr)
    acc_scr[...] = jnp.zeros_like(acc_scr)

    def attend(c, carry):
        off = pl.multiple_of(c * kc, kc)
        m = m_scr[...]
        p = [jnp.exp(s_scr[c, :, t * LANES:(t + 1) * LANES] - m) for t in range(kc // LANES)]
        lpart_scr[...] += functools.reduce(jnp.add, p)
        acc_scr[...] += _dot(jnp.concatenate(p, axis=1).astype(BF16), ckv_scr[pl.ds(off, kc), :])
        return carry

    lax.fori_loop(0, nchunks, attend, 0)

    o_all = (acc_scr[...] / jnp.sum(lpart_scr[...], axis=-1, keepdims=True)).astype(BF16)
    for tile in range(DSA_HEADS // 2):
        h = 2 * tile
        out = _dot(o_all[h * qb:(h + 1) * qb], wuv_ref[h]) + _dot(o_all[(h + 1) * qb:(h + 2) * qb], wuv_ref[h + 1])
        o_ref[:, tile * LANES:(tile + 1) * LANES] = out.astype(o_ref.dtype)


def _dsa(proj, kv_norm, wuk_pad, wuv_pad, layer):
    b, s, _ = proj.shape
    qb = DSA_QBLOCK
    nb = s // qb
    topk = min(DSA_TOPK_MAX, s // 4)
    assert topk % qb == 0
    nck = s // DSA_KCHUNK
    rows = DSA_HEADS * qb
    return pl.pallas_call(
        functools.partial(_dsa_kernel, topk),
        grid=(b, nb),
        in_specs=[
            pl.BlockSpec((None, qb, B_WIDTH), lambda b, n: (b, n, COL_BQ // B_WIDTH)),
            pl.BlockSpec((None, qb, B_WIDTH), lambda b, n: (b, n, COL_BIQ // B_WIDTH)),
            pl.BlockSpec((None, s, DSA_RANK), lambda b, n: (b, 0, COL_BKV // DSA_RANK)),
            pl.BlockSpec((None, s, LANES), lambda b, n: (b, 0, COL_BIK // LANES)),
            pl.BlockSpec((1, DSA_RANK), lambda b, n: (0, 0)),
            pl.BlockSpec((DSA_HEADS, LANES, DSA_RANK), lambda b, n: (0, 0, 0)),
            pl.BlockSpec((DSA_HEADS, DSA_RANK, LANES), lambda b, n: (0, 0, 0)),
        ],
        out_specs=pl.BlockSpec((None, qb, B_WIDTH), lambda b, n: (b, n, 0)),
        out_shape=jax.ShapeDtypeStruct((b, s, B_WIDTH), BF16),
        scratch_shapes=[
            pltpu.VMEM((s, DSA_RANK), BF16),
            pltpu.VMEM((s, LANES), BF16),
            pltpu.VMEM((IDX_HEADS, qb, LANES), BF16),
            pltpu.VMEM((nck, DSA_KCHUNK, qb), F32),
            pltpu.VMEM((nck, qb, DSA_KCHUNK), F32),
            pltpu.VMEM((rows, DSA_RANK), BF16),
            pltpu.VMEM((nck, rows, DSA_KCHUNK), F32),
            pltpu.VMEM((rows, LANES), F32),
            pltpu.VMEM((rows, LANES), F32),
            pltpu.VMEM((rows, LANES), F32),
            pltpu.VMEM((rows, DSA_RANK), F32),
        ],
        compiler_params=pltpu.CompilerParams(
            dimension_semantics=("parallel", "arbitrary"), vmem_limit_bytes=VMEM_LIMIT),
        name=f"dsa_l{layer}",
    )(proj, proj, proj, proj, kv_norm, wuk_pad, wuv_pad)


def _ret_kernel(q_ref, k_ref, v_ref, g_ref, gn_ref, o_ref, state_scr):
    cs = BLOCK

    @pl.when(pl.program_id(1) == 0)
    def _():
        state_scr[...] = jnp.zeros_like(state_scr)

    row = lax.broadcasted_iota(jnp.int32, (cs, LANES), 0)
    lane = lax.broadcasted_iota(jnp.int32, (cs, LANES), 1)
    first = lane < HEAD_DIM
    rowf = row.astype(F32)
    diff = (row - lane).astype(F32)
    same_head = (row < HEAD_DIM) == first
    seg_mean = jnp.where(same_head, 1.0 / HEAD_DIM, 0.0).astype(BF16)

    def seg_mean_dot(x):
        x_hi = x.astype(BF16)
        x_lo = (x - x_hi.astype(F32)).astype(BF16)
        return _dot(x_hi, seg_mean) + _dot(x_lo, seg_mean)

    tiles = RET_HEADS // 2
    outs = []
    for bi in range(q_ref.shape[0]):
        for tile in range(tiles):
            cols = slice(tile * LANES, (tile + 1) * LANES)
            lg = [math.log(1.0 - 2.0 ** (-5.0 - (2 * tile + par))) for par in range(2)]
            lg_lane = jnp.where(first, lg[0], lg[1])
            q = q_ref[bi, :, cols].astype(BF16)
            k = k_ref[bi, :, cols] * HEAD_DIM ** -0.5
            v = v_ref[bi, :, cols]
            state = state_scr[bi, tile]
            out = _dot(q, state.astype(BF16)) * jnp.exp(lg_lane * (rowf + 1.0))
            for par in range(2):
                keep = first if par == 0 else ~first
                decay = jnp.where(diff >= 0, jnp.exp(lg[par] * jnp.maximum(diff, 0.0)), 0.0)
                inner = _dot_nt(q, jnp.where(keep, k, 0.0).astype(BF16)) * decay
                out = out + _dot(inner.astype(BF16), jnp.where(keep, v, 0.0).astype(BF16))
            k_dec = (k * jnp.exp(lg_lane * (cs - 1.0 - rowf))).astype(BF16)
            kv = _dot_tn(k_dec, v.astype(BF16))
            state_scr[bi, tile] = state * jnp.exp(lg_lane * cs) + jnp.where(same_head, kv, 0.0)
            outs.append(out)

    out = jnp.concatenate(outs, axis=0)
    mu = seg_mean_dot(out)
    cen = out - mu
    var = seg_mean_dot(cen * cen)
    yn = cen * lax.rsqrt(var + EPS)
    for bi in range(q_ref.shape[0]):
        for tile in range(tiles):
            cols = slice(tile * LANES, (tile + 1) * LANES)
            y = yn[(bi * tiles + tile) * cs:(bi * tiles + tile + 1) * cs] * gn_ref[:, cols]
            o_ref[bi, :, cols] = (y * _silu(g_ref[bi, :, cols])).astype(o_ref.dtype)


RET_BATCH = 4


def _retention(proj, ret_norm, layer):
    b, s, _ = proj.shape
    nc = s // BLOCK
    rb = math.gcd(b, RET_BATCH)
    spec = lambda col: pl.BlockSpec((rb, BLOCK, C_WIDTH), lambda b, c: (b, c, col // C_WIDTH))
    return pl.pallas_call(
        _ret_kernel,
        grid=(b // rb, nc),
        in_specs=[spec(COL_CQ), spec(COL_CK), spec(COL_CV), spec(COL_CG),
                  pl.BlockSpec((1, C_WIDTH), lambda b, c: (0, 0))],
        out_specs=pl.BlockSpec((rb, BLOCK, C_WIDTH), lambda b, c: (b, c, 0)),
        out_shape=jax.ShapeDtypeStruct((b, s, C_WIDTH), BF16),
        scratch_shapes=[pltpu.VMEM((rb, RET_HEADS // 2, LANES, LANES), F32)],
        compiler_params=pltpu.CompilerParams(
            dimension_semantics=("parallel", "arbitrary"), vmem_limit_bytes=VMEM_LIMIT),
        name=f"ret_l{layer}",
    )(proj, proj, proj, proj, ret_norm)


def _mix_out_kernel(x_ref, gt_ref, gpost_ref, oa_ref, ob_ref, oc_ref, w_ref, o_ref, gain_scr, inv_scr):
    y = _dot(oa_ref[...], w_ref[:A_WIDTH, :])
    y = y + _dot(ob_ref[...], w_ref[A_WIDTH:A_WIDTH + B_WIDTH, :])
    o_ref[...] = y + _dot(oc_ref[...], w_ref[A_WIDTH + B_WIDTH:, :])
    gain_scr[...] = gt_ref[...] * gpost_ref[...]
    _postnorm_residual_rows(o_ref, x_ref, gain_scr, inv_scr, o_ref)


def _mix_out(x, mod, norm_g, layer, oa, ob, oc, w, tm):
    b, s, d = x.shape
    row = lambda width: pl.BlockSpec((None, tm, width), lambda b, i: (b, i, 0))
    return pl.pallas_call(
        _mix_out_kernel,
        grid=(b, s // tm),
        in_specs=[
            row(d), _mod_spec(layer, 5, d, 2), _gain_spec(layer, 3, d, 2),
            row(A_WIDTH), row(B_WIDTH), row(C_WIDTH),
            pl.BlockSpec((None,) + w.shape[1:], lambda b, i: (layer, 0, 0)),
        ],
        out_specs=row(d),
        out_shape=jax.ShapeDtypeStruct((b, s, d), F32),
        scratch_shapes=[pltpu.VMEM((1, d), F32), pltpu.VMEM((tm, 1), F32)],
        compiler_params=pltpu.CompilerParams(
            dimension_semantics=("parallel", "parallel"), vmem_limit_bytes=VMEM_LIMIT),
        name=f"mix_out_l{layer}",
    )(x, mod, norm_g, oa, ob, oc, w)


def _prep_mix_in(w):
    sizes = [A_WIDTH, KV_WIDTH, KV_WIDTH, B_WIDTH, DSA_RANK, IDX_HEADS * IDX_DIM, IDX_DIM, IDX_HEADS,
             C_WIDTH, C_WIDTH, C_WIDTH, C_WIDTH]
    starts = [0]
    for sz in sizes:
        starts.append(starts[-1] + sz)
    aq, ak, av, bq, bkv, biq, bik, biw, cq, ck, cv, cg = [w[:, a:a + sz] for a, sz in zip(starts, sizes)]
    tail = jnp.zeros((w.shape[0], LANES - IDX_DIM - IDX_HEADS), w.dtype)
    return jnp.concatenate([aq, bq, biq, cq, ck, cv, cg, ak, av, bkv, bik, biw, tail], axis=1).astype(BF16)


def _prep_dsa_up(w_uk, w_uv):
    r, h, dh = w_uk.shape
    uk = jnp.transpose(w_uk, (1, 2, 0))
    uv = jnp.transpose(w_uv, (1, 0, 2))
    odd = (jnp.arange(h) % 2 == 1)[:, None, None]
    zk = jnp.zeros_like(uk)
    zv = jnp.zeros_like(uv)
    uk_pad = jnp.where(odd, jnp.concatenate([zk, uk], axis=1), jnp.concatenate([uk, zk], axis=1))
    uv_pad = jnp.where(odd, jnp.concatenate([zv, uv], axis=2), jnp.concatenate([uv, zv], axis=2))
    return uk_pad.astype(BF16), uv_pad.astype(BF16)


FFN_TM = 512
FFN_TF = 1024
MIX_IN_TM = 1024
MIX_IN_TN = 1536
MIX_OUT_TM = 512


def kernel(x, c, ada_w, ada_b, norm_g, ffn1_w_in, ffn1_w_out, ffn2_w_in, ffn2_w_out,
           mix_w_in, mix_w_out, swa_sinks, dsa_kv_norm, dsa_w_uk, dsa_w_uv, ret_norm):
    depth = ada_w.shape[0]
    b, s, d = x.shape
    assert s % FFN_TM == 0 and s % DSA_KCHUNK == 0 and d % LANES == 0
    mod = _ada_mod(c, ada_w, ada_b).reshape(depth, b, N_MOD, 1, d)
    gains = norm_g.reshape(depth, norm_g.shape[1], 1, d)
    ffn1_in, ffn1_out = ffn1_w_in.astype(BF16), ffn1_w_out.astype(BF16)
    ffn2_in, ffn2_out = ffn2_w_in.astype(BF16), ffn2_w_out.astype(BF16)
    mix_out_w = mix_w_out.astype(BF16)
    for l in range(depth):
        x = _ffn_block(x, mod, gains, l, 0, ffn1_in, ffn1_out, 0.5, FFN_TM, FFN_TF)

        proj = _mix_in(x, mod, gains, l, _prep_mix_in(mix_w_in[l]), MIX_IN_TM, MIX_IN_TN)
        oa = _swa(proj, swa_sinks[l], l)
        uk_pad, uv_pad = _prep_dsa_up(dsa_w_uk[l], dsa_w_uv[l])
        ob = _dsa(proj, dsa_kv_norm[l].reshape(1, -1), uk_pad, uv_pad, l)
        oc = _retention(proj, ret_norm[l].reshape(1, -1), l)
        x = _mix_out(x, mod, gains, l, oa, ob, oc, mix_out_w, MIX_OUT_TM)

        x = _ffn_block(x, mod, gains, l, 2, ffn2_in, ffn2_out, 0.5, FFN_TM, FFN_TF)
    return x
```

```python
import functools
import math

import jax
import jax.numpy as jnp
from jax import lax
from jax.experimental import pallas as pl
from jax.experimental.pallas import tpu as pltpu

F32 = jnp.float32
BF16 = jnp.bfloat16

LANES = 128
HEAD_DIM = 64
SWA_HEADS = 16
SWA_KV_HEADS = 2
BLOCK = 128
DSA_HEADS = 8
DSA_RANK = 128
IDX_HEADS = 8
IDX_DIM = 64
DSA_TOPK_MAX = 256
RET_HEADS = 8
N_MOD = 9
EPS = 1e-6
NEG = -1e30

A_WIDTH = SWA_HEADS * HEAD_DIM
B_WIDTH = DSA_HEADS * HEAD_DIM
C_WIDTH = RET_HEADS * HEAD_DIM
KV_WIDTH = SWA_KV_HEADS * HEAD_DIM

COL_AQ = 0
COL_BQ = COL_AQ + A_WIDTH
COL_BIQ = COL_BQ + B_WIDTH
COL_CQ = COL_BIQ + IDX_HEADS * IDX_DIM
COL_CK = COL_CQ + C_WIDTH
COL_CV = COL_CK + C_WIDTH
COL_CG = COL_CV + C_WIDTH
COL_AKV = COL_CG + C_WIDTH
COL_BKV = COL_AKV + 2 * KV_WIDTH
COL_BIK = COL_BKV + DSA_RANK
MIX_PAD_WIDTH = COL_BIK + LANES

DSA_QBLOCK = 256
DSA_KCHUNK = 256
BISECT_ITERS = 18
VMEM_LIMIT = 56 * 1024 * 1024


def _dot(a, b):
    return jnp.dot(a, b, preferred_element_type=F32)


def _dot_nt(a, b):
    return lax.dot_general(a, b, (((1,), (1,)), ((), ())), preferred_element_type=F32)


def _dot_tn(a, b):
    return lax.dot_general(a, b, (((0,), (0,)), ((), ())), preferred_element_type=F32)


def _silu(x):
    return x / (1.0 + jnp.exp(-x))


def _rms(x, g):
    return x * lax.rsqrt(jnp.mean(x * x, axis=-1, keepdims=True) + EPS) * g


def _alibi_slope(h, n):
    return 2.0 ** (-8.0 * (h + 1) / n)


def _ada_kernel(c_ref, w_ref, b_ref, o_ref):
    cond = _silu(c_ref[...]).astype(BF16)
    o_ref[...] = _dot(cond, w_ref[...].astype(BF16)) + b_ref[...]


def _ada_mod(c, ada_w, ada_b):
    depth, d, n = ada_w.shape
    b = c.shape[0]
    tn = 1024
    return pl.pallas_call(
        _ada_kernel,
        grid=(depth, n // tn),
        in_specs=[
            pl.BlockSpec((b, d), lambda l, j: (0, 0)),
            pl.BlockSpec((None, d, tn), lambda l, j: (l, 0, j)),
            pl.BlockSpec((None, 1, tn), lambda l, j: (l, 0, j)),
        ],
        out_specs=pl.BlockSpec((None, b, tn), lambda l, j: (l, 0, j)),
        out_shape=jax.ShapeDtypeStruct((depth, b, n), F32),
        compiler_params=pltpu.CompilerParams(
            dimension_semantics=("arbitrary", "arbitrary"), vmem_limit_bytes=VMEM_LIMIT),
        name="ada_mod",
    )(c, ada_w, ada_b.reshape(depth, 1, n))


def _mod_spec(layer, k, d, grid_rank):
    if grid_rank == 2:
        return pl.BlockSpec((None, None, None, 1, d), lambda b, i: (layer, b, k, 0, 0))
    return pl.BlockSpec((None, None, None, 1, d), lambda b, i, j: (layer, b, k, 0, 0))


def _gain_spec(layer, k, d, grid_rank):
    if grid_rank == 2:
        return pl.BlockSpec((None, None, 1, d), lambda b, i: (layer, k, 0, 0))
    return pl.BlockSpec((None, None, 1, d), lambda b, i, j: (layer, k, 0, 0))


ROW_CHUNK = 16


def _row_chunks(n_rows, body):
    def step(i, carry):
        body(pl.ds(pl.multiple_of(i * ROW_CHUNK, ROW_CHUNK), ROW_CHUNK))
        return carry
    lax.fori_loop(0, n_rows // ROW_CHUNK, step, 0, unroll=True)


def _inv_rms_rows(x_ref, inv_ref):
    def body(r):
        x = x_ref[r, :]
        inv_ref[r, :] = lax.rsqrt(jnp.mean(x * x, axis=-1, keepdims=True) + EPS)
    _row_chunks(x_ref.shape[0], body)


def _prenorm_mod_rows(x_ref, gain_ref, shift_ref, inv_ref, h_ref):
    _inv_rms_rows(x_ref, inv_ref)

    def body(r):
        h_ref[r, :] = (x_ref[r, :] * inv_ref[r, :] * gain_ref[...] + shift_ref[...]).astype(h_ref.dtype)
    _row_chunks(x_ref.shape[0], body)


def _postnorm_residual_rows(y_ref, x_ref, gain_ref, inv_ref, o_ref):
    _inv_rms_rows(y_ref, inv_ref)

    def body(r):
        o_ref[r, :] = x_ref[r, :] + y_ref[r, :] * inv_ref[r, :] * gain_ref[...]
    _row_chunks(y_ref.shape[0], body)


def _ffn_kernel(res_w, tail, x_ref, xn_ref, sh_ref, sc_ref, shn_ref, scn_ref, gt_ref, gpre_ref, gpost_ref,
                wg_ref, wu_ref, wo_ref, o_ref, h_scr, acc_scr, gain_pre_scr, inv_pre_scr, gain_post_scr,
                inv_post_scr):
    f = pl.program_id(2)
    last = pl.num_programs(2) - 1
    tile = pl.program_id(0) * pl.num_programs(1) + pl.program_id(1)
    slot = lax.rem(tile, 2)
    tf = wo_ref.shape[1]

    def prenorm(src_ref, shift_ref, scale_ref, dst_slot):
        gain_pre_scr[...] = gpre_ref[...] * (1.0 + scale_ref[...])
        _prenorm_mod_rows(src_ref, gain_pre_scr, shift_ref, inv_pre_scr, h_scr.at[dst_slot])

    @pl.when((tile == 0) & (f == 0))
    def _():
        prenorm(x_ref, sh_ref, sc_ref, 0)

    def step(lo, first):
        h = h_scr[slot]
        act = _silu(_dot(h, wg_ref[0, :, lo:])) * _dot(h, wu_ref[0, :, lo:])
        out = _dot(act.astype(BF16), wo_ref[0, lo:, :])
        if first:
            acc_scr[...] = out
        else:
            acc_scr[...] += out

    @pl.when(f == 0)
    def _():
        step(0, True)

    @pl.when((f > 0) & (f < last))
    def _():
        step(0, False)

    @pl.when(f == last)
    def _():
        prenorm(xn_ref, shn_ref, scn_ref, 1 - slot)
        step(tf - tail, False)
        gain_post_scr[...] = res_w * gt_ref[...] * gpost_ref[...]
        _postnorm_residual_rows(acc_scr, x_ref, gain_post_scr, inv_post_scr, o_ref)


def _ffn_block(x, mod, norm_g, layer, sub, w_in, w_out, res_w, tm, tf):
    b, s, d = x.shape
    nt = s // tm
    d_ff = w_out.shape[1]
    nf = pl.cdiv(d_ff, tf)
    tail = d_ff - (nf - 1) * tf
    assert tail % LANES == 0 and d_ff >= tf and nf >= 2
    mk = 3 * sub
    next_b = lambda b, i: jnp.minimum(b + (i + 1) // nt, x.shape[0] - 1)
    next_i = lambda i: (i + 1) % nt
    mod_next = lambda k: pl.BlockSpec((None, None, None, 1, d),
                                      lambda b, i, j: (layer, next_b(b, i), k, 0, 0))
    start = lambda j, base: pl.multiple_of(base + jnp.minimum(j * tf, d_ff - tf), LANES)
    col = lambda base: (lambda b, i, j: (layer, 0, start(j, base)))
    one = pl.Element(1)
    return pl.pallas_call(
        functools.partial(_ffn_kernel, res_w, tail),
        grid=(b, s // tm, nf),
        in_specs=[
            pl.BlockSpec((None, tm, d), lambda b, i, j: (b, i, 0)),
            pl.BlockSpec((None, tm, d), lambda b, i, j: (next_b(b, i), next_i(i), 0)),
            _mod_spec(layer, mk, d, 3), _mod_spec(layer, mk + 1, d, 3),
            mod_next(mk), mod_next(mk + 1),
            _mod_spec(layer, mk + 2, d, 3),
            _gain_spec(layer, 2 * sub, d, 3), _gain_spec(layer, 2 * sub + 1, d, 3),
            pl.BlockSpec((one, pl.Element(d), pl.Element(tf)), col(0)),
            pl.BlockSpec((one, pl.Element(d), pl.Element(tf)), col(d_ff)),
            pl.BlockSpec((one, pl.Element(tf), pl.Element(d)), lambda b, i, j: (layer, start(j, 0), 0)),
        ],
        out_specs=pl.BlockSpec((None, tm, d), lambda b, i, j: (b, i, 0)),
        out_shape=jax.ShapeDtypeStruct((b, s, d), F32),
        scratch_shapes=[pltpu.VMEM((2, tm, d), BF16), pltpu.VMEM((tm, d), F32),
                        pltpu.VMEM((1, d), F32), pltpu.VMEM((tm, 1), F32),
                        pltpu.VMEM((1, d), F32), pltpu.VMEM((tm, 1), F32)],
        compiler_params=pltpu.CompilerParams(
            dimension_semantics=("arbitrary", "arbitrary", "arbitrary"), vmem_limit_bytes=VMEM_LIMIT),
        name=f"ffn_l{layer}_s{sub}",
    )(x, x, mod, mod, mod, mod, mod, norm_g, norm_g, w_in, w_in, w_out)


def _mix_in_kernel(x_ref, sh_ref, sc_ref, gpre_ref, w_ref, o_ref, h_scr, gain_scr, inv_scr):
    @pl.when(pl.program_id(2) == 0)
    def _():
        gain_scr[...] = gpre_ref[...] * (1.0 + sc_ref[...])
        _prenorm_mod_rows(x_ref, gain_scr, sh_ref, inv_scr, h_scr)

    o_ref[...] = _dot(h_scr[...], w_ref[...])


def _mix_in(x, mod, norm_g, layer, w, tm, tn):
    b, s, d = x.shape
    n = w.shape[1]
    return pl.pallas_call(
        _mix_in_kernel,
        grid=(b, s // tm, n // tn),
        in_specs=[
            pl.BlockSpec((None, tm, d), lambda b, i, j: (b, i, 0)),
            _mod_spec(layer, 3, d, 3), _mod_spec(layer, 4, d, 3),
            _gain_spec(layer, 2, d, 3),
            pl.BlockSpec((d, tn), lambda b, i, j: (0, j)),
        ],
        out_specs=pl.BlockSpec((None, tm, tn), lambda b, i, j: (b, i, j)),
        out_shape=jax.ShapeDtypeStruct((b, s, n), F32),
        scratch_shapes=[pltpu.VMEM((tm, d), BF16), pltpu.VMEM((1, d), F32), pltpu.VMEM((tm, 1), F32)],
        compiler_params=pltpu.CompilerParams(
            dimension_semantics=("parallel", "parallel", "arbitrary"), vmem_limit_bytes=VMEM_LIMIT),
        name=f"mix_in_l{layer}",
    )(x, mod, mod, norm_g, w)


def _lane_halves(x, first_half_holds_data):
    lane = lax.broadcasted_iota(jnp.int32, x.shape, 1)
    if first_half_holds_data:
        lo = jnp.where(lane < HEAD_DIM, x, 0.0)
        return lo, pltpu.roll(lo, HEAD_DIM, 1)
    hi = jnp.where(lane >= HEAD_DIM, x, 0.0)
    return pltpu.roll(hi, HEAD_DIM, 1), hi


def _swa_kernel(sinks_ref, q_ref, kvc_ref, kvp_ref, o_ref):
    n = pl.program_id(1)
    w = BLOCK
    kcat = jnp.concatenate([kvp_ref[:, :KV_WIDTH], kvc_ref[:, :KV_WIDTH]], axis=0)
    vcat = jnp.concatenate([kvp_ref[:, KV_WIDTH:], kvc_ref[:, KV_WIDTH:]], axis=0)
    k_halves = [tuple(t.astype(BF16) for t in _lane_halves(kcat, hk == 0)) for hk in range(SWA_KV_HEADS)]
    v_halves = [tuple(t.astype(BF16) for t in _lane_halves(vcat, hk == 0)) for hk in range(SWA_KV_HEADS)]

    bands = SWA_HEADS // SWA_KV_HEADS // 2
    rows = bands * w
    r = lax.broadcasted_iota(jnp.int32, (rows, 2 * w), 0)
    j = lax.broadcasted_iota(jnp.int32, (rows, 2 * w), 1)
    dist = (r & (w - 1)) - j + w
    valid = (dist >= 0) & (dist < w) & ((n - 1) * w + j >= 0)
    band_step = _alibi_slope(2, SWA_HEADS) / _alibi_slope(0, SWA_HEADS)
    band_scale = jnp.ones((rows, 2 * w), F32)
    for band in range(1, bands):
        band_scale = jnp.where(r >= band * w, band_step ** band, band_scale)
    dist_scaled = dist.astype(F32) * band_scale
    band_col = lax.broadcasted_iota(jnp.int32, (rows, 1), 0)
    for hk in range(SWA_KV_HEADS):
        q = jnp.concatenate([q_ref[:, (bands * hk + band) * LANES:(bands * hk + band + 1) * LANES]
                             for band in range(bands)], axis=0).astype(BF16)
        out = None
        for par in range(2):
            head0 = 2 * bands * hk + par
            sink = jnp.full((rows, 1), sinks_ref[head0], F32)
            for band in range(1, bands):
                sink = jnp.where(band_col >= band * w, sinks_ref[head0 + 2 * band], sink)
            sc = _dot_nt(q, k_halves[hk][par]) * HEAD_DIM ** -0.5
            sc = jnp.where(valid, sc - _alibi_slope(head0, SWA_HEADS) * dist_scaled, -jnp.inf)
            m = jnp.maximum(jnp.max(sc, axis=-1, keepdims=True), sink)
            e = jnp.exp(sc - m)
            p = e / (jnp.sum(e, axis=-1, keepdims=True) + jnp.exp(sink - m))
            pv = _dot(p.astype(BF16), v_halves[hk][par])
            out = pv if out is None else out + pv
        for band in range(bands):
            tile = bands * hk + band
            o_ref[:, tile * LANES:(tile + 1) * LANES] = out[band * w:(band + 1) * w].astype(o_ref.dtype)


def _swa(proj, sinks, layer):
    b, s, _ = proj.shape
    nb = s // BLOCK
    kv_blk = COL_AKV // (2 * KV_WIDTH)
    return pl.pallas_call(
        _swa_kernel,
        grid=(b, nb),
        in_specs=[
            pl.BlockSpec(memory_space=pltpu.SMEM),
            pl.BlockSpec((None, BLOCK, A_WIDTH), lambda b, n: (b, n, COL_AQ // A_WIDTH)),
            pl.BlockSpec((None, BLOCK, 2 * KV_WIDTH), lambda b, n: (b, n, kv_blk)),
            pl.BlockSpec((None, BLOCK, 2 * KV_WIDTH), lambda b, n: (b, jnp.maximum(n - 1, 0), kv_blk)),
        ],
        out_specs=pl.BlockSpec((None, BLOCK, A_WIDTH), lambda b, n: (b, n, 0)),
        out_shape=jax.ShapeDtypeStruct((b, s, A_WIDTH), BF16),
        compiler_params=pltpu.CompilerParams(
            dimension_semantics=("parallel", "arbitrary"), vmem_limit_bytes=VMEM_LIMIT),
        name=f"swa_l{layer}",
    )(sinks, proj, proj, proj)


def _dsa_kernel(topk, q_ref, iq_ref, kv_ref, ikw_ref, kvn_ref, wuk_ref, wuv_ref, o_ref,
                ckv_scr, ik_scr, iqh_scr, score_scr, bias_scr, qlat_scr, s_scr, mrun_scr, m_scr,
                lpart_scr, acc_scr):
    n = pl.program_id(1)
    qb = DSA_QBLOCK
    kc = DSA_KCHUNK
    nchunks = lax.div(n * qb + qb - 1, kc) + 1

    @pl.when(n == 0)
    def _():
        ckv_scr[...] = _rms(kv_ref[...], kvn_ref[...]).astype(BF16)
        ik_scr[...] = ikw_ref[...].astype(BF16)

    row0 = pl.multiple_of(n * qb, qb)

    iw_t = ikw_ref[pl.ds(row0, qb), :].T
    iw_scale = IDX_HEADS ** -0.5 * IDX_DIM ** -0.5
    w_rows = [iw_t[IDX_DIM + h:IDX_DIM + h + 1, :] * iw_scale for h in range(IDX_HEADS)]
    lane = lax.broadcasted_iota(jnp.int32, (qb, LANES), 1)
    for h in range(IDX_HEADS):
        iq = iq_ref[:, (h // 2) * LANES:(h // 2 + 1) * LANES]
        if h % 2 == 0:
            iq = jnp.where(lane < IDX_DIM, iq, 0.0)
        else:
            iq = pltpu.roll(jnp.where(lane >= IDX_DIM, iq, 0.0), IDX_DIM, 1)
        iqh_scr[h] = iq.astype(BF16)

    key_row = lax.broadcasted_iota(jnp.int32, (kc, qb), 0)
    t_pos = n * qb + lax.broadcasted_iota(jnp.int32, (kc, qb), 1)

    def score_chunk(c, carry):
        off = pl.multiple_of(c * kc, kc)
        ik = ik_scr[pl.ds(off, kc), :]
        sc = jnp.zeros((kc, qb), F32)
        for h in range(IDX_HEADS):
            sc = sc + w_rows[h] * jnp.maximum(_dot_nt(ik, iqh_scr[h]), 0.0)
        score_scr[c] = jnp.where(off + key_row <= t_pos, sc, -jnp.inf)
        return carry

    lax.fori_loop(0, nchunks, score_chunk, 0)

    sub = 8

    def fold_rows(x, op):
        out = x[:sub]
        for g in range(1, x.shape[0] // sub):
            out = op(out, x[g * sub:(g + 1) * sub])
        return out

    def fold_lanes(x, op):
        out = x[:, :LANES]
        for t in range(1, x.shape[1] // LANES):
            out = op(out, x[:, t * LANES:(t + 1) * LANES])
        return out

    def count_ge(thr, strict=False):
        def body(c, cnt):
            x = score_scr[c]
            hit = (x > thr) if strict else (x >= thr)
            return cnt + fold_rows(jnp.where(hit, 1.0, 0.0), jnp.add)
        part = lax.fori_loop(0, nchunks, body, jnp.zeros((sub, qb), F32))
        return jnp.sum(part, axis=0, keepdims=True)

    def max_at_most(bound):
        def body(c, mx):
            x = score_scr[c]
            return jnp.maximum(mx, fold_rows(jnp.where(x <= bound, x, -jnp.inf), jnp.maximum))
        part = lax.fori_loop(0, nchunks, body, jnp.full((sub, qb), -jnp.inf, F32))
        return jnp.max(part, axis=0, keepdims=True)

    def store_bias(c, t, bias_t):
        for g in range(qb // LANES):
            bias_scr[c, g * LANES:(g + 1) * LANES, t * LANES:(t + 1) * LANES] = (
                bias_t[:, g * LANES:(g + 1) * LANES].T)

    @pl.when(n * qb + qb <= topk)
    def _():
        def body(c, carry):
            x = score_scr[c]
            for t in range(kc // LANES):
                store_bias(c, t, jnp.where(x[t * LANES:(t + 1) * LANES] == -jnp.inf, NEG, 0.0))
            return carry
        lax.fori_loop(0, nchunks, body, 0)

    @pl.when(n * qb + qb > topk)
    def _():
        kf = float(topk)

        def minmax(c, carry):
            mn, mx = carry
            x = score_scr[c]
            mn = jnp.minimum(mn, fold_rows(jnp.where(x == -jnp.inf, jnp.inf, x), jnp.minimum))
            return mn, jnp.maximum(mx, fold_rows(x, jnp.maximum))

        lo, hi = lax.fori_loop(0, nchunks, minmax,
                               (jnp.full((sub, qb), jnp.inf, F32), jnp.full((sub, qb), -jnp.inf, F32)))
        lo = jnp.min(lo, axis=0, keepdims=True)
        hi = jnp.max(hi, axis=0, keepdims=True)

        def bisect(_, carry):
            lo, hi = carry
            mid = 0.5 * (lo + hi)
            ge = count_ge(mid) >= kf
            return jnp.where(ge, mid, lo), jnp.where(ge, hi, mid)

        lo, hi = lax.fori_loop(0, BISECT_ITERS, bisect, (lo, hi))

        def walk_cond(carry):
            return carry[1] > 0.0

        def count_and_next(thr):
            def body(c, carry):
                cnt, mx = carry
                x = score_scr[c]
                ge = x >= thr
                cnt = cnt + fold_rows(jnp.where(ge, 1.0, 0.0), jnp.add)
                return cnt, jnp.maximum(mx, fold_rows(jnp.where(ge, -jnp.inf, x), jnp.maximum))
            cnt, mx = lax.fori_loop(0, nchunks, body, (jnp.zeros((sub, qb), F32),
                                                       jnp.full((sub, qb), -jnp.inf, F32)))
            return jnp.sum(cnt, axis=0, keepdims=True), jnp.max(mx, axis=0, keepdims=True)

        def walk(carry):
            thr, _ = carry
            cnt, below = count_and_next(thr)
            short = cnt < kf
            return jnp.where(short, below, thr), jnp.max(jnp.where(short, 1.0, 0.0))

        thr, _ = lax.while_loop(walk_cond, walk, (max_at_most(hi), jnp.float32(1.0)))

        need = kf - count_ge(thr, strict=True)

        r = lax.broadcasted_iota(jnp.int32, (LANES, LANES), 0)
        col = lax.broadcasted_iota(jnp.int32, (LANES, LANES), 1)
        tri = jnp.where(col <= r, 1.0, 0.0).astype(BF16)

        def select(c, seen):
            x = score_scr[c]
            for t in range(kc // LANES):
                xt = x[t * LANES:(t + 1) * LANES]
                eq = xt == thr
                eqf = jnp.where(eq, 1.0, 0.0)
                rank = seen + _dot(tri, eqf.astype(BF16))
                seen = seen + jnp.sum(eqf, axis=0, keepdims=True)
                sel = (xt > thr) | (eq & (rank <= need))
                store_bias(c, t, jnp.where(sel, 0.0, NEG))
            return seen

        lax.fori_loop(0, nchunks, select, jnp.zeros((1, qb), F32))

    for h in range(DSA_HEADS):
        tile = h // 2
        rows = slice(h * qb, (h + 1) * qb)
        q = q_ref[:, tile * LANES:(tile + 1) * LANES].astype(BF16)
        qlat_scr[rows, :] = (_dot(q, wuk_ref[h]) * HEAD_DIM ** -0.5).astype(BF16)
    mrun_scr[...] = jnp.full(mrun_scr.shape, NEG, F32)
    key_lane = lax.broadcasted_iota(jnp.int32, (1, kc), 1)

    def logits(c, carry):
        off = pl.multiple_of(c * kc, kc)
        s = _dot_nt(qlat_scr[...], ckv_scr[pl.ds(off, kc), :])
        bias = bias_scr[c]
        key_pos = (off + key_lane).astype(F32)
        for h in range(DSA_HEADS):
            rows = slice(h * qb, (h + 1) * qb)
            sh = s[rows] + (bias + _alibi_slope(h, DSA_HEADS) * key_pos)
            s_scr[c, rows, :] = sh
            mrun_scr[rows, :] = jnp.maximum(mrun_scr[rows, :], fold_lanes(sh, jnp.maximum))
        return carry

    lax.fori_loop(0, nchunks, logits, 0)
    m_scr[...] = jnp.broadcast_to(jnp.max(mrun_scr[...], axis=-1, keepdims=True), m_scr.shape)
    lpart_scr[...] = jnp.zeros_like(lpart_scr)
    acc_scr[...] = jnp.zeros_like(acc_scr)

    def attend(c, carry):
        off = pl.multiple_of(c * kc, kc)
        m = m_scr[...]
        p = [jnp.exp(s_scr[c, :, t * LANES:(t + 1) * LANES] - m) for t in range(kc // LANES)]
        lpart_scr[...] += functools.reduce(jnp.add, p)
        acc_scr[...] += _dot(jnp.concatenate(p, axis=1).astype(BF16), ckv_scr[pl.ds(off, kc), :])
        return carry

    lax.fori_loop(0, nchunks, attend, 0)

    o_all = (acc_scr[...] / jnp.sum(lpart_scr[...], axis=-1, keepdims=True)).astype(BF16)
    for tile in range(DSA_HEADS // 2):
        h = 2 * tile
        out = _dot(o_all[h * qb:(h + 1) * qb], wuv_ref[h]) + _dot(o_all[(h + 1) * qb:(h + 2) * qb], wuv_ref[h + 1])
        o_ref[:, tile * LANES:(tile + 1) * LANES] = out.astype(o_ref.dtype)


def _dsa(proj, kv_norm, wuk_pad, wuv_pad, layer):
    b, s, _ = proj.shape
    qb = DSA_QBLOCK
    nb = s // qb
    topk = min(DSA_TOPK_MAX, s // 4)
    assert topk % qb == 0
    nck = s // DSA_KCHUNK
    rows = DSA_HEADS * qb
    return pl.pallas_call(
        functools.partial(_dsa_kernel, topk),
        grid=(b, nb),
        in_specs=[
            pl.BlockSpec((None, qb, B_WIDTH), lambda b, n: (b, n, COL_BQ // B_WIDTH)),
            pl.BlockSpec((None, qb, B_WIDTH), lambda b, n: (b, n, COL_BIQ // B_WIDTH)),
            pl.BlockSpec((None, s, DSA_RANK), lambda b, n: (b, 0, COL_BKV // DSA_RANK)),
            pl.BlockSpec((None, s, LANES), lambda b, n: (b, 0, COL_BIK // LANES)),
            pl.BlockSpec((1, DSA_RANK), lambda b, n: (0, 0)),
            pl.BlockSpec((DSA_HEADS, LANES, DSA_RANK), lambda b, n: (0, 0, 0)),
            pl.BlockSpec((DSA_HEADS, DSA_RANK, LANES), lambda b, n: (0, 0, 0)),
        ],
        out_specs=pl.BlockSpec((None, qb, B_WIDTH), lambda b, n: (b, n, 0)),
        out_shape=jax.ShapeDtypeStruct((b, s, B_WIDTH), BF16),
        scratch_shapes=[
            pltpu.VMEM((s, DSA_RANK), BF16),
            pltpu.VMEM((s, LANES), BF16),
            pltpu.VMEM((IDX_HEADS, qb, LANES), BF16),
            pltpu.VMEM((nck, DSA_KCHUNK, qb), F32),
            pltpu.VMEM((nck, qb, DSA_KCHUNK), F32),
            pltpu.VMEM((rows, DSA_RANK), BF16),
            pltpu.VMEM((nck, rows, DSA_KCHUNK), F32),
            pltpu.VMEM((rows, LANES), F32),
            pltpu.VMEM((rows, LANES), F32),
            pltpu.VMEM((rows, LANES), F32),
            pltpu.VMEM((rows, DSA_RANK), F32),
        ],
        compiler_params=pltpu.CompilerParams(
            dimension_semantics=("parallel", "arbitrary"), vmem_limit_bytes=VMEM_LIMIT),
        name=f"dsa_l{layer}",
    )(proj, proj, proj, proj, kv_norm, wuk_pad, wuv_pad)


def _ret_kernel(q_ref, k_ref, v_ref, g_ref, gn_ref, o_ref, state_scr):
    cs = BLOCK

    @pl.when(pl.program_id(1) == 0)
    def _():
        state_scr[...] = jnp.zeros_like(state_scr)

    row = lax.broadcasted_iota(jnp.int32, (cs, LANES), 0)
    lane = lax.broadcasted_iota(jnp.int32, (cs, LANES), 1)
    first = lane < HEAD_DIM
    rowf = row.astype(F32)
    diff = (row - lane).astype(F32)
    same_head = (row < HEAD_DIM) == first
    seg_mean = jnp.where(same_head, 1.0 / HEAD_DIM, 0.0).astype(BF16)

    def seg_mean_dot(x):
        x_hi = x.astype(BF16)
        x_lo = (x - x_hi.astype(F32)).astype(BF16)
        return _dot(x_hi, seg_mean) + _dot(x_lo, seg_mean)

    tiles = RET_HEADS // 2
    outs = []
    for bi in range(q_ref.shape[0]):
        for tile in range(tiles):
            cols = slice(tile * LANES, (tile + 1) * LANES)
            lg = [math.log(1.0 - 2.0 ** (-5.0 - (2 * tile + par))) for par in range(2)]
            lg_lane = jnp.where(first, lg[0], lg[1])
            q = q_ref[bi, :, cols].astype(BF16)
            k = k_ref[bi, :, cols] * HEAD_DIM ** -0.5
            v = v_ref[bi, :, cols]
            state = state_scr[bi, tile]
            out = _dot(q, state.astype(BF16)) * jnp.exp(lg_lane * (rowf + 1.0))
            for par in range(2):
                keep = first if par == 0 else ~first
                decay = jnp.where(diff >= 0, jnp.exp(lg[par] * jnp.maximum(diff, 0.0)), 0.0)
                inner = _dot_nt(q, jnp.where(keep, k, 0.0).astype(BF16)) * decay
                out = out + _dot(inner.astype(BF16), jnp.where(keep, v, 0.0).astype(BF16))
            k_dec = (k * jnp.exp(lg_lane * (cs - 1.0 - rowf))).astype(BF16)
            kv = _dot_tn(k_dec, v.astype(BF16))
            state_scr[bi, tile] = state * jnp.exp(lg_lane * cs) + jnp.where(same_head, kv, 0.0)
            outs.append(out)

    out = jnp.concatenate(outs, axis=0)
    mu = seg_mean_dot(out)
    cen = out - mu
    var = seg_mean_dot(cen * cen)
    yn = cen * lax.rsqrt(var + EPS)
    for bi in range(q_ref.shape[0]):
        for tile in range(tiles):
            cols = slice(tile * LANES, (tile + 1) * LANES)
            y = yn[(bi * tiles + tile) * cs:(bi * tiles + tile + 1) * cs] * gn_ref[:, cols]
            o_ref[bi, :, cols] = (y * _silu(g_ref[bi, :, cols])).astype(o_ref.dtype)


RET_BATCH = 4


def _retention(proj, ret_norm, layer):
    b, s, _ = proj.shape
    nc = s // BLOCK
    rb = math.gcd(b, RET_BATCH)
    spec = lambda col: pl.BlockSpec((rb, BLOCK, C_WIDTH), lambda b, c: (b, c, col // C_WIDTH))
    return pl.pallas_call(
        _ret_kernel,
        grid=(b // rb, nc),
        in_specs=[spec(COL_CQ), spec(COL_CK), spec(COL_CV), spec(COL_CG),
                  pl.BlockSpec((1, C_WIDTH), lambda b, c: (0, 0))],
        out_specs=pl.BlockSpec((rb, BLOCK, C_WIDTH), lambda b, c: (b, c, 0)),
        out_shape=jax.ShapeDtypeStruct((b, s, C_WIDTH), BF16),
        scratch_shapes=[pltpu.VMEM((rb, RET_HEADS // 2, LANES, LANES), F32)],
        compiler_params=pltpu.CompilerParams(
            dimension_semantics=("parallel", "arbitrary"), vmem_limit_bytes=VMEM_LIMIT),
        name=f"ret_l{layer}",
    )(proj, proj, proj, proj, ret_norm)


def _mix_out_kernel(x_ref, gt_ref, gpost_ref, oa_ref, ob_ref, oc_ref, w_ref, o_ref, gain_scr, inv_scr):
    y = _dot(oa_ref[...], w_ref[:A_WIDTH, :])
    y = y + _dot(ob_ref[...], w_ref[A_WIDTH:A_WIDTH + B_WIDTH, :])
    o_ref[...] = y + _dot(oc_ref[...], w_ref[A_WIDTH + B_WIDTH:, :])
    gain_scr[...] = gt_ref[...] * gpost_ref[...]
    _postnorm_residual_rows(o_ref, x_ref, gain_scr, inv_scr, o_ref)


def _mix_out(x, mod, norm_g, layer, oa, ob, oc, w, tm):
    b, s, d = x.shape
    row = lambda width: pl.BlockSpec((None, tm, width), lambda b, i: (b, i, 0))
    return pl.pallas_call(
        _mix_out_kernel,
        grid=(b, s // tm),
        in_specs=[
            row(d), _mod_spec(layer, 5, d, 2), _gain_spec(layer, 3, d, 2),
            row(A_WIDTH), row(B_WIDTH), row(C_WIDTH),
            pl.BlockSpec((None,) + w.shape[1:], lambda b, i: (layer, 0, 0)),
        ],
        out_specs=row(d),
        out_shape=jax.ShapeDtypeStruct((b, s, d), F32),
        scratch_shapes=[pltpu.VMEM((1, d), F32), pltpu.VMEM((tm, 1), F32)],
        compiler_params=pltpu.CompilerParams(
            dimension_semantics=("parallel", "parallel"), vmem_limit_bytes=VMEM_LIMIT),
        name=f"mix_out_l{layer}",
    )(x, mod, norm_g, oa, ob, oc, w)


def _prep_mix_in(w):
    sizes = [A_WIDTH, KV_WIDTH, KV_WIDTH, B_WIDTH, DSA_RANK, IDX_HEADS * IDX_DIM, IDX_DIM, IDX_HEADS,
             C_WIDTH, C_WIDTH, C_WIDTH, C_WIDTH]
    starts = [0]
    for sz in sizes:
        starts.append(starts[-1] + sz)
    aq, ak, av, bq, bkv, biq, bik, biw, cq, ck, cv, cg = [w[:, a:a + sz] for a, sz in zip(starts, sizes)]
    tail = jnp.zeros((w.shape[0], LANES - IDX_DIM - IDX_HEADS), w.dtype)
    return jnp.concatenate([aq, bq, biq, cq, ck, cv, cg, ak, av, bkv, bik, biw, tail], axis=1).astype(BF16)


def _prep_dsa_up(w_uk, w_uv):
    r, h, dh = w_uk.shape
    uk = jnp.transpose(w_uk, (1, 2, 0))
    uv = jnp.transpose(w_uv, (1, 0, 2))
    odd = (jnp.arange(h) % 2 == 1)[:, None, None]
    zk = jnp.zeros_like(uk)
    zv = jnp.zeros_like(uv)
    uk_pad = jnp.where(odd, jnp.concatenate([zk, uk], axis=1), jnp.concatenate([uk, zk], axis=1))
    uv_pad = jnp.where(odd, jnp.concatenate([zv, uv], axis=2), jnp.concatenate([uv, zv], axis=2))
    return uk_pad.astype(BF16), uv_pad.astype(BF16)


FFN_TM = 512
FFN_TF = 512
MIX_IN_TM = 1024
MIX_IN_TN = 1536
MIX_OUT_TM = 512


def kernel(x, c, ada_w, ada_b, norm_g, ffn1_w_in, ffn1_w_out, ffn2_w_in, ffn2_w_out,
           mix_w_in, mix_w_out, swa_sinks, dsa_kv_norm, dsa_w_uk, dsa_w_uv, ret_norm):
    depth = ada_w.shape[0]
    b, s, d = x.shape
    assert s % FFN_TM == 0 and s % DSA_KCHUNK == 0 and d % LANES == 0
    mod = _ada_mod(c, ada_w, ada_b).reshape(depth, b, N_MOD, 1, d)
    gains = norm_g.reshape(depth, norm_g.shape[1], 1, d)
    ffn1_in, ffn1_out = ffn1_w_in.astype(BF16), ffn1_w_out.astype(BF16)
    ffn2_in, ffn2_out = ffn2_w_in.astype(BF16), ffn2_w_out.astype(BF16)
    mix_out_w = mix_w_out.astype(BF16)
    for l in range(depth):
        x = _ffn_block(x, mod, gains, l, 0, ffn1_in, ffn1_out, 0.5, FFN_TM, FFN_TF)

        proj = _mix_in(x, mod, gains, l, _prep_mix_in(mix_w_in[l]), MIX_IN_TM, MIX_IN_TN)
        oa = _swa(proj, swa_sinks[l], l)
        uk_pad, uv_pad = _prep_dsa_up(dsa_w_uk[l], dsa_w_uv[l])
        ob = _dsa(proj, dsa_kv_norm[l].reshape(1, -1), uk_pad, uv_pad, l)
        oc = _retention(proj, ret_norm[l].reshape(1, -1), l)
        x = _mix_out(x, mod, gains, l, oa, ob, oc, mix_out_w, MIX_OUT_TM)

        x = _ffn_block(x, mod, gains, l, 2, ffn2_in, ffn2_out, 0.5, FFN_TM, FFN_TF)
    return x
```

```python
import functools
import math

import jax
import jax.numpy as jnp
from jax import lax
from jax.experimental import pallas as pl
from jax.experimental.pallas import tpu as pltpu

F32 = jnp.float32
BF16 = jnp.bfloat16

LANES = 128
HEAD_DIM = 64
SWA_HEADS = 16
SWA_KV_HEADS = 2
BLOCK = 128
DSA_HEADS = 8
DSA_RANK = 128
IDX_HEADS = 8
IDX_DIM = 64
DSA_TOPK_MAX = 256
RET_HEADS = 8
N_MOD = 9
EPS = 1e-6
NEG = -1e30

A_WIDTH = SWA_HEADS * HEAD_DIM
B_WIDTH = DSA_HEADS * HEAD_DIM
C_WIDTH = RET_HEADS * HEAD_DIM
KV_WIDTH = SWA_KV_HEADS * HEAD_DIM

COL_AQ = 0
COL_BQ = COL_AQ + A_WIDTH
COL_BIQ = COL_BQ + B_WIDTH
COL_CQ = COL_BIQ + IDX_HEADS * IDX_DIM
COL_CK = COL_CQ + C_WIDTH
COL_CV = COL_CK + C_WIDTH
COL_CG = COL_CV + C_WIDTH
COL_AKV = COL_CG + C_WIDTH
COL_BKV = COL_AKV + 2 * KV_WIDTH
COL_BIK = COL_BKV + DSA_RANK
MIX_PAD_WIDTH = COL_BIK + LANES

DSA_QBLOCK = 256
DSA_KCHUNK = 256
BISECT_ITERS = 18
VMEM_LIMIT = 56 * 1024 * 1024


def _dot(a, b):
    return jnp.dot(a, b, preferred_element_type=F32)


def _dot_nt(a, b):
    return lax.dot_general(a, b, (((1,), (1,)), ((), ())), preferred_element_type=F32)


def _dot_tn(a, b):
    return lax.dot_general(a, b, (((0,), (0,)), ((), ())), preferred_element_type=F32)


def _silu(x):
    return x / (1.0 + jnp.exp(-x))


def _rms(x, g):
    return x * lax.rsqrt(jnp.mean(x * x, axis=-1, keepdims=True) + EPS) * g


def _alibi_slope(h, n):
    return 2.0 ** (-8.0 * (h + 1) / n)


def _ada_kernel(c_ref, w_ref, b_ref, o_ref):
    cond = _silu(c_ref[...]).astype(BF16)
    o_ref[...] = _dot(cond, w_ref[...].astype(BF16)) + b_ref[...]


def _ada_mod(c, ada_w, ada_b):
    depth, d, n = ada_w.shape
    b = c.shape[0]
    tn = 1024
    return pl.pallas_call(
        _ada_kernel,
        grid=(depth, n // tn),
        in_specs=[
            pl.BlockSpec((b, d), lambda l, j: (0, 0)),
            pl.BlockSpec((None, d, tn), lambda l, j: (l, 0, j)),
            pl.BlockSpec((None, 1, tn), lambda l, j: (l, 0, j)),
        ],
        out_specs=pl.BlockSpec((None, b, tn), lambda l, j: (l, 0, j)),
        out_shape=jax.ShapeDtypeStruct((depth, b, n), F32),
        compiler_params=pltpu.CompilerParams(
            dimension_semantics=("arbitrary", "arbitrary"), vmem_limit_bytes=VMEM_LIMIT),
        name="ada_mod",
    )(c, ada_w, ada_b.reshape(depth, 1, n))


def _mod_spec(layer, k, d, grid_rank):
    if grid_rank == 2:
        return pl.BlockSpec((None, None, None, 1, d), lambda b, i: (layer, b, k, 0, 0))
    return pl.BlockSpec((None, None, None, 1, d), lambda b, i, j: (layer, b, k, 0, 0))


def _gain_spec(layer, k, d, grid_rank):
    if grid_rank == 2:
        return pl.BlockSpec((None, None, 1, d), lambda b, i: (layer, k, 0, 0))
    return pl.BlockSpec((None, None, 1, d), lambda b, i, j: (layer, k, 0, 0))


ROW_CHUNK = 16


def _row_chunks(n_rows, body):
    def step(i, carry):
        body(pl.ds(pl.multiple_of(i * ROW_CHUNK, ROW_CHUNK), ROW_CHUNK))
        return carry
    lax.fori_loop(0, n_rows // ROW_CHUNK, step, 0, unroll=True)


def _inv_rms_rows(x_ref, inv_ref):
    def body(r):
        x = x_ref[r, :]
        inv_ref[r, :] = lax.rsqrt(jnp.mean(x * x, axis=-1, keepdims=True) + EPS)
    _row_chunks(x_ref.shape[0], body)


def _prenorm_mod_rows(x_ref, gain_ref, shift_ref, inv_ref, h_ref):
    _inv_rms_rows(x_ref, inv_ref)

    def body(r):
        h_ref[r, :] = (x_ref[r, :] * inv_ref[r, :] * gain_ref[...] + shift_ref[...]).astype(h_ref.dtype)
    _row_chunks(x_ref.shape[0], body)


def _postnorm_residual_rows(y_ref, x_ref, gain_ref, inv_ref, o_ref):
    _inv_rms_rows(y_ref, inv_ref)

    def body(r):
        o_ref[r, :] = x_ref[r, :] + y_ref[r, :] * inv_ref[r, :] * gain_ref[...]
    _row_chunks(y_ref.shape[0], body)


def _ffn_kernel(res_w, tail, x_ref, xn_ref, sh_ref, sc_ref, shn_ref, scn_ref, gt_ref, gpre_ref, gpost_ref,
                wg_ref, wu_ref, wo_ref, o_ref, h_scr, acc_scr, gain_pre_scr, inv_pre_scr, gain_post_scr,
                inv_post_scr):
    f = pl.program_id(2)
    last = pl.num_programs(2) - 1
    tile = pl.program_id(0) * pl.num_programs(1) + pl.program_id(1)
    slot = lax.rem(tile, 2)
    tf = wo_ref.shape[1]

    def prenorm(src_ref, shift_ref, scale_ref, dst_slot):
        gain_pre_scr[...] = gpre_ref[...] * (1.0 + scale_ref[...])
        _prenorm_mod_rows(src_ref, gain_pre_scr, shift_ref, inv_pre_scr, h_scr.at[dst_slot])

    @pl.when((tile == 0) & (f == 0))
    def _():
        prenorm(x_ref, sh_ref, sc_ref, 0)

    def step(lo, first):
        h = h_scr[slot]
        act = _silu(_dot(h, wg_ref[:, lo:])) * _dot(h, wu_ref[:, lo:])
        out = _dot(act.astype(BF16), wo_ref[0, lo:, :])
        if first:
            acc_scr[...] = out
        else:
            acc_scr[...] += out

    @pl.when(f == 0)
    def _():
        step(0, True)

    @pl.when((f > 0) & (f < last))
    def _():
        step(0, False)

    @pl.when(f == last)
    def _():
        prenorm(xn_ref, shn_ref, scn_ref, 1 - slot)
        step(tf - tail, False)
        gain_post_scr[...] = res_w * gt_ref[...] * gpost_ref[...]
        _postnorm_residual_rows(acc_scr, x_ref, gain_post_scr, inv_post_scr, o_ref)


def _ffn_block(x, mod, norm_g, layer, sub, w_in, w_out, res_w, tm, tf):
    b, s, d = x.shape
    nt = s // tm
    d_ff = w_out.shape[1]
    nf = pl.cdiv(d_ff, tf)
    tail = d_ff - (nf - 1) * tf
    assert tail % LANES == 0 and d_ff >= tf and nf >= 2
    mk = 3 * sub
    next_b = lambda b, i: jnp.minimum(b + (i + 1) // nt, x.shape[0] - 1)
    next_i = lambda i: (i + 1) % nt
    mod_next = lambda k: pl.BlockSpec((None, None, None, 1, d),
                                      lambda b, i, j: (layer, next_b(b, i), k, 0, 0))
    start = lambda j: pl.multiple_of(jnp.minimum(j * tf, d_ff - tf), LANES)
    w_in_block = lambda half: pl.BlockSpec((None, None, None, d, tf), lambda b, i, j: (layer, half, j, 0, 0))
    return pl.pallas_call(
        functools.partial(_ffn_kernel, res_w, tail),
        grid=(b, s // tm, nf),
        in_specs=[
            pl.BlockSpec((None, tm, d), lambda b, i, j: (b, i, 0)),
            pl.BlockSpec((None, tm, d), lambda b, i, j: (next_b(b, i), next_i(i), 0)),
            _mod_spec(layer, mk, d, 3), _mod_spec(layer, mk + 1, d, 3),
            mod_next(mk), mod_next(mk + 1),
            _mod_spec(layer, mk + 2, d, 3),
            _gain_spec(layer, 2 * sub, d, 3), _gain_spec(layer, 2 * sub + 1, d, 3),
            w_in_block(0), w_in_block(1),
            pl.BlockSpec((pl.Element(1), pl.Element(tf), pl.Element(d)), lambda b, i, j: (layer, start(j), 0)),
        ],
        out_specs=pl.BlockSpec((None, tm, d), lambda b, i, j: (b, i, 0)),
        out_shape=jax.ShapeDtypeStruct((b, s, d), F32),
        scratch_shapes=[pltpu.VMEM((2, tm, d), BF16), pltpu.VMEM((tm, d), F32),
                        pltpu.VMEM((1, d), F32), pltpu.VMEM((tm, 1), F32),
                        pltpu.VMEM((1, d), F32), pltpu.VMEM((tm, 1), F32)],
        compiler_params=pltpu.CompilerParams(
            dimension_semantics=("arbitrary", "arbitrary", "arbitrary"), vmem_limit_bytes=VMEM_LIMIT),
        name=f"ffn_l{layer}_s{sub}",
    )(x, x, mod, mod, mod, mod, mod, norm_g, norm_g, w_in, w_in, w_out)


def _mix_in_kernel(x_ref, sh_ref, sc_ref, gpre_ref, w_ref, o_ref, h_scr, gain_scr, inv_scr):
    @pl.when(pl.program_id(2) == 0)
    def _():
        gain_scr[...] = gpre_ref[...] * (1.0 + sc_ref[...])
        _prenorm_mod_rows(x_ref, gain_scr, sh_ref, inv_scr, h_scr)

    o_ref[...] = _dot(h_scr[...], w_ref[...])


def _mix_in(x, mod, norm_g, layer, w, tm, tn):
    b, s, d = x.shape
    n = w.shape[1]
    return pl.pallas_call(
        _mix_in_kernel,
        grid=(b, s // tm, n // tn),
        in_specs=[
            pl.BlockSpec((None, tm, d), lambda b, i, j: (b, i, 0)),
            _mod_spec(layer, 3, d, 3), _mod_spec(layer, 4, d, 3),
            _gain_spec(layer, 2, d, 3),
            pl.BlockSpec((d, tn), lambda b, i, j: (0, j)),
        ],
        out_specs=pl.BlockSpec((None, tm, tn), lambda b, i, j: (b, i, j)),
        out_shape=jax.ShapeDtypeStruct((b, s, n), F32),
        scratch_shapes=[pltpu.VMEM((tm, d), BF16), pltpu.VMEM((1, d), F32), pltpu.VMEM((tm, 1), F32)],
        compiler_params=pltpu.CompilerParams(
            dimension_semantics=("parallel", "parallel", "arbitrary"), vmem_limit_bytes=VMEM_LIMIT),
        name=f"mix_in_l{layer}",
    )(x, mod, mod, norm_g, w)


def _lane_halves(x, first_half_holds_data):
    lane = lax.broadcasted_iota(jnp.int32, x.shape, 1)
    if first_half_holds_data:
        lo = jnp.where(lane < HEAD_DIM, x, 0.0)
        return lo, pltpu.roll(lo, HEAD_DIM, 1)
    hi = jnp.where(lane >= HEAD_DIM, x, 0.0)
    return pltpu.roll(hi, HEAD_DIM, 1), hi


def _swa_kernel(sinks_ref, q_ref, kvc_ref, kvp_ref, o_ref):
    n = pl.program_id(1)
    w = BLOCK
    kcat = jnp.concatenate([kvp_ref[:, :KV_WIDTH], kvc_ref[:, :KV_WIDTH]], axis=0)
    vcat = jnp.concatenate([kvp_ref[:, KV_WIDTH:], kvc_ref[:, KV_WIDTH:]], axis=0)
    k_halves = [tuple(t.astype(BF16) for t in _lane_halves(kcat, hk == 0)) for hk in range(SWA_KV_HEADS)]
    v_halves = [tuple(t.astype(BF16) for t in _lane_halves(vcat, hk == 0)) for hk in range(SWA_KV_HEADS)]

    bands = SWA_HEADS // SWA_KV_HEADS // 2
    rows = bands * w
    r = lax.broadcasted_iota(jnp.int32, (rows, 2 * w), 0)
    j = lax.broadcasted_iota(jnp.int32, (rows, 2 * w), 1)
    dist = (r & (w - 1)) - j + w
    valid = (dist >= 0) & (dist < w) & ((n - 1) * w + j >= 0)
    band_step = _alibi_slope(2, SWA_HEADS) / _alibi_slope(0, SWA_HEADS)
    band_scale = jnp.ones((rows, 2 * w), F32)
    for band in range(1, bands):
        band_scale = jnp.where(r >= band * w, band_step ** band, band_scale)
    dist_scaled = dist.astype(F32) * band_scale
    band_col = lax.broadcasted_iota(jnp.int32, (rows, 1), 0)
    for hk in range(SWA_KV_HEADS):
        q = jnp.concatenate([q_ref[:, (bands * hk + band) * LANES:(bands * hk + band + 1) * LANES]
                             for band in range(bands)], axis=0).astype(BF16)
        out = None
        for par in range(2):
            head0 = 2 * bands * hk + par
            sink = jnp.full((rows, 1), sinks_ref[head0], F32)
            for band in range(1, bands):
                sink = jnp.where(band_col >= band * w, sinks_ref[head0 + 2 * band], sink)
            sc = _dot_nt(q, k_halves[hk][par]) * HEAD_DIM ** -0.5
            sc = jnp.where(valid, sc - _alibi_slope(head0, SWA_HEADS) * dist_scaled, -jnp.inf)
            m = jnp.maximum(jnp.max(sc, axis=-1, keepdims=True), sink)
            e = jnp.exp(sc - m)
            p = e / (jnp.sum(e, axis=-1, keepdims=True) + jnp.exp(sink - m))
            pv = _dot(p.astype(BF16), v_halves[hk][par])
            out = pv if out is None else out + pv
        for band in range(bands):
            tile = bands * hk + band
            o_ref[:, tile * LANES:(tile + 1) * LANES] = out[band * w:(band + 1) * w].astype(o_ref.dtype)


def _swa(proj, sinks, layer):
    b, s, _ = proj.shape
    nb = s // BLOCK
    kv_blk = COL_AKV // (2 * KV_WIDTH)
    return pl.pallas_call(
        _swa_kernel,
        grid=(b, nb),
        in_specs=[
            pl.BlockSpec(memory_space=pltpu.SMEM),
            pl.BlockSpec((None, BLOCK, A_WIDTH), lambda b, n: (b, n, COL_AQ // A_WIDTH)),
            pl.BlockSpec((None, BLOCK, 2 * KV_WIDTH), lambda b, n: (b, n, kv_blk)),
            pl.BlockSpec((None, BLOCK, 2 * KV_WIDTH), lambda b, n: (b, jnp.maximum(n - 1, 0), kv_blk)),
        ],
        out_specs=pl.BlockSpec((None, BLOCK, A_WIDTH), lambda b, n: (b, n, 0)),
        out_shape=jax.ShapeDtypeStruct((b, s, A_WIDTH), BF16),
        compiler_params=pltpu.CompilerParams(
            dimension_semantics=("parallel", "arbitrary"), vmem_limit_bytes=VMEM_LIMIT),
        name=f"swa_l{layer}",
    )(sinks, proj, proj, proj)


def _dsa_kernel(topk, q_ref, iq_ref, kv_ref, ikw_ref, kvn_ref, wuk_ref, wuv_ref, o_ref,
                ckv_scr, ik_scr, iqh_scr, score_scr, bias_scr, qlat_scr, s_scr, mrun_scr, m_scr,
                lpart_scr, acc_scr):
    n = pl.program_id(1)
    qb = DSA_QBLOCK
    kc = DSA_KCHUNK
    nchunks = lax.div(n * qb + qb - 1, kc) + 1

    @pl.when(n == 0)
    def _():
        ckv_scr[...] = _rms(kv_ref[...], kvn_ref[...]).astype(BF16)
        ik_scr[...] = ikw_ref[...].astype(BF16)

    row0 = pl.multiple_of(n * qb, qb)

    iw_t = ikw_ref[pl.ds(row0, qb), :].T
    iw_scale = IDX_HEADS ** -0.5 * IDX_DIM ** -0.5
    w_rows = [iw_t[IDX_DIM + h:IDX_DIM + h + 1, :] * iw_scale for h in range(IDX_HEADS)]
    lane = lax.broadcasted_iota(jnp.int32, (qb, LANES), 1)
    for h in range(IDX_HEADS):
        iq = iq_ref[:, (h // 2) * LANES:(h // 2 + 1) * LANES]
        if h % 2 == 0:
            iq = jnp.where(lane < IDX_DIM, iq, 0.0)
        else:
            iq = pltpu.roll(jnp.where(lane >= IDX_DIM, iq, 0.0), IDX_DIM, 1)
        iqh_scr[h] = iq.astype(BF16)

    key_row = lax.broadcasted_iota(jnp.int32, (kc, qb), 0)
    t_pos = n * qb + lax.broadcasted_iota(jnp.int32, (kc, qb), 1)

    def score_chunk(c, carry):
        off = pl.multiple_of(c * kc, kc)
        ik = ik_scr[pl.ds(off, kc), :]
        sc = jnp.zeros((kc, qb), F32)
        for h in range(IDX_HEADS):
            sc = sc + w_rows[h] * jnp.maximum(_dot_nt(ik, iqh_scr[h]), 0.0)
        score_scr[c] = jnp.where(off + key_row <= t_pos, sc, -jnp.inf)
        return carry

    lax.fori_loop(0, nchunks, score_chunk, 0)

    sub = 8

    def fold_rows(x, op):
        out = x[:sub]
        for g in range(1, x.shape[0] // sub):
            out = op(out, x[g * sub:(g + 1) * sub])
        return out

    def fold_lanes(x, op):
        out = x[:, :LANES]
        for t in range(1, x.shape[1] // LANES):
            out = op(out, x[:, t * LANES:(t + 1) * LANES])
        return out

    def count_ge(thr, strict=False):
        def body(c, cnt):
            x = score_scr[c]
            hit = (x > thr) if strict else (x >= thr)
            return cnt + fold_rows(jnp.where(hit, 1.0, 0.0), jnp.add)
        part = lax.fori_loop(0, nchunks, body, jnp.zeros((sub, qb), F32))
        return jnp.sum(part, axis=0, keepdims=True)

    def max_at_most(bound):
        def body(c, mx):
            x = score_scr[c]
            return jnp.maximum(mx, fold_rows(jnp.where(x <= bound, x, -jnp.inf), jnp.maximum))
        part = lax.fori_loop(0, nchunks, body, jnp.full((sub, qb), -jnp.inf, F32))
        return jnp.max(part, axis=0, keepdims=True)

    def store_bias(c, t, bias_t):
        for g in range(qb // LANES):
            bias_scr[c, g * LANES:(g + 1) * LANES, t * LANES:(t + 1) * LANES] = (
                bias_t[:, g * LANES:(g + 1) * LANES].T)

    @pl.when(n * qb + qb <= topk)
    def _():
        def body(c, carry):
            x = score_scr[c]
            for t in range(kc // LANES):
                store_bias(c, t, jnp.where(x[t * LANES:(t + 1) * LANES] == -jnp.inf, NEG, 0.0))
            return carry
        lax.fori_loop(0, nchunks, body, 0)

    @pl.when(n * qb + qb > topk)
    def _():
        kf = float(topk)

        def minmax(c, carry):
            mn, mx = carry
            x = score_scr[c]
            mn = jnp.minimum(mn, fold_rows(jnp.where(x == -jnp.inf, jnp.inf, x), jnp.minimum))
            return mn, jnp.maximum(mx, fold_rows(x, jnp.maximum))

        lo, hi = lax.fori_loop(0, nchunks, minmax,
                               (jnp.full((sub, qb), jnp.inf, F32), jnp.full((sub, qb), -jnp.inf, F32)))
        lo = jnp.min(lo, axis=0, keepdims=True)
        hi = jnp.max(hi, axis=0, keepdims=True)

        def bisect(_, carry):
            lo, hi = carry
            mid = 0.5 * (lo + hi)
            ge = count_ge(mid) >= kf
            return jnp.where(ge, mid, lo), jnp.where(ge, hi, mid)

        lo, hi = lax.fori_loop(0, BISECT_ITERS, bisect, (lo, hi))

        def walk_cond(carry):
            return carry[1] > 0.0

        def count_and_next(thr):
            def body(c, carry):
                cnt, mx = carry
                x = score_scr[c]
                ge = x >= thr
                cnt = cnt + fold_rows(jnp.where(ge, 1.0, 0.0), jnp.add)
                return cnt, jnp.maximum(mx, fold_rows(jnp.where(ge, -jnp.inf, x), jnp.maximum))
            cnt, mx = lax.fori_loop(0, nchunks, body, (jnp.zeros((sub, qb), F32),
                                                       jnp.full((sub, qb), -jnp.inf, F32)))
            return jnp.sum(cnt, axis=0, keepdims=True), jnp.max(mx, axis=0, keepdims=True)

        def walk(carry):
            thr, _ = carry
            cnt, below = count_and_next(thr)
            short = cnt < kf
            return jnp.where(short, below, thr), jnp.max(jnp.where(short, 1.0, 0.0))

        thr, _ = lax.while_loop(walk_cond, walk, (max_at_most(hi), jnp.float32(1.0)))

        need = kf - count_ge(thr, strict=True)

        r = lax.broadcasted_iota(jnp.int32, (LANES, LANES), 0)
        col = lax.broadcasted_iota(jnp.int32, (LANES, LANES), 1)
        tri = jnp.where(col <= r, 1.0, 0.0).astype(BF16)

        def select(c, seen):
            x = score_scr[c]
            for t in range(kc // LANES):
                xt = x[t * LANES:(t + 1) * LANES]
                eq = xt == thr
                eqf = jnp.where(eq, 1.0, 0.0)
                rank = seen + _dot(tri, eqf.astype(BF16))
                seen = seen + jnp.sum(eqf, axis=0, keepdims=True)
                sel = (xt > thr) | (eq & (rank <= need))
                store_bias(c, t, jnp.where(sel, 0.0, NEG))
            return seen

        lax.fori_loop(0, nchunks, select, jnp.zeros((1, qb), F32))

    for h in range(DSA_HEADS):
        tile = h // 2
        rows = slice(h * qb, (h + 1) * qb)
        q = q_ref[:, tile * LANES:(tile + 1) * LANES].astype(BF16)
        qlat_scr[rows, :] = (_dot(q, wuk_ref[h]) * HEAD_DIM ** -0.5).astype(BF16)
    mrun_scr[...] = jnp.full(mrun_scr.shape, NEG, F32)
    key_lane = lax.broadcasted_iota(jnp.int32, (1, kc), 1)

    def logits(c, carry):
        off = pl.multiple_of(c * kc, kc)
        s = _dot_nt(qlat_scr[...], ckv_scr[pl.ds(off, kc), :])
        bias = bias_scr[c]
        key_pos = (off + key_lane).astype(F32)
        for h in range(DSA_HEADS):
            rows = slice(h * qb, (h + 1) * qb)
            sh = s[rows] + (bias + _alibi_slope(h, DSA_HEADS) * key_pos)
            s_scr[c, rows, :] = sh
            mrun_scr[rows, :] = jnp.maximum(mrun_scr[rows, :], fold_lanes(sh, jnp.maximum))
        return carry

    lax.fori_loop(0, nchunks, logits, 0)
    m_scr[...] = jnp.broadcast_to(jnp.max(mrun_scr[...], axis=-1, keepdims=True), m_scr.shape)
    lpart_scr[...] = jnp.zeros_like(lpart_scr)
    acc_scr[...] = jnp.zeros_like(acc_scr)

    def attend(c, carry):
        off = pl.multiple_of(c * kc, kc)
        m = m_scr[...]
        p = [jnp.exp(s_scr[c, :, t * LANES:(t + 1) * LANES] - m) for t in range(kc // LANES)]
        lpart_scr[...] += functools.reduce(jnp.add, p)
        acc_scr[...] += _dot(jnp.concatenate(p, axis=1).astype(BF16), ckv_scr[pl.ds(off, kc), :])
        return carry

    lax.fori_loop(0, nchunks, attend, 0)

    o_all = (acc_scr[...] / jnp.sum(lpart_scr[...], axis=-1, keepdims=True)).astype(BF16)
    for tile in range(DSA_HEADS // 2):
        h = 2 * tile
        out = _dot(o_all[h * qb:(h + 1) * qb], wuv_ref[h]) + _dot(o_all[(h + 1) * qb:(h + 2) * qb], wuv_ref[h + 1])
        o_ref[:, tile * LANES:(tile + 1) * LANES] = out.astype(o_ref.dtype)


def _dsa(proj, kv_norm, wuk_pad, wuv_pad, layer):
    b, s, _ = proj.shape
    qb = DSA_QBLOCK
    nb = s // qb
    topk = min(DSA_TOPK_MAX, s // 4)
    assert topk % qb == 0
    nck = s // DSA_KCHUNK
    rows = DSA_HEADS * qb
    return pl.pallas_call(
        functools.partial(_dsa_kernel, topk),
        grid=(b, nb),
        in_specs=[
            pl.BlockSpec((None, qb, B_WIDTH), lambda b, n: (b, n, COL_BQ // B_WIDTH)),
            pl.BlockSpec((None, qb, B_WIDTH), lambda b, n: (b, n, COL_BIQ // B_WIDTH)),
            pl.BlockSpec((None, s, DSA_RANK), lambda b, n: (b, 0, COL_BKV // DSA_RANK)),
            pl.BlockSpec((None, s, LANES), lambda b, n: (b, 0, COL_BIK // LANES)),
            pl.BlockSpec((1, DSA_RANK), lambda b, n: (0, 0)),
            pl.BlockSpec((DSA_HEADS, LANES, DSA_RANK), lambda b, n: (0, 0, 0)),
            pl.BlockSpec((DSA_HEADS, DSA_RANK, LANES), lambda b, n: (0, 0, 0)),
        ],
        out_specs=pl.BlockSpec((None, qb, B_WIDTH), lambda b, n: (b, n, 0)),
        out_shape=jax.ShapeDtypeStruct((b, s, B_WIDTH), BF16),
        scratch_shapes=[
            pltpu.VMEM((s, DSA_RANK), BF16),
            pltpu.VMEM((s, LANES), BF16),
            pltpu.VMEM((IDX_HEADS, qb, LANES), BF16),
            pltpu.VMEM((nck, DSA_KCHUNK, qb), F32),
            pltpu.VMEM((nck, qb, DSA_KCHUNK), F32),
            pltpu.VMEM((rows, DSA_RANK), BF16),
            pltpu.VMEM((nck, rows, DSA_KCHUNK), F32),
            pltpu.VMEM((rows, LANES), F32),
            pltpu.VMEM((rows, LANES), F32),
            pltpu.VMEM((rows, LANES), F32),
            pltpu.VMEM((rows, DSA_RANK), F32),
        ],
        compiler_params=pltpu.CompilerParams(
            dimension_semantics=("parallel", "arbitrary"), vmem_limit_bytes=VMEM_LIMIT),
        name=f"dsa_l{layer}",
    )(proj, proj, proj, proj, kv_norm, wuk_pad, wuv_pad)


def _ret_kernel(q_ref, k_ref, v_ref, g_ref, gn_ref, o_ref, state_scr):
    cs = BLOCK

    @pl.when(pl.program_id(1) == 0)
    def _():
        state_scr[...] = jnp.zeros_like(state_scr)

    row = lax.broadcasted_iota(jnp.int32, (cs, LANES), 0)
    lane = lax.broadcasted_iota(jnp.int32, (cs, LANES), 1)
    first = lane < HEAD_DIM
    rowf = row.astype(F32)
    diff = (row - lane).astype(F32)
    same_head = (row < HEAD_DIM) == first
    seg_mean = jnp.where(same_head, 1.0 / HEAD_DIM, 0.0).astype(BF16)

    def seg_mean_dot(x):
        x_hi = x.astype(BF16)
        x_lo = (x - x_hi.astype(F32)).astype(BF16)
        return _dot(x_hi, seg_mean) + _dot(x_lo, seg_mean)

    tiles = RET_HEADS // 2
    outs = []
    for bi in range(q_ref.shape[0]):
        for tile in range(tiles):
            cols = slice(tile * LANES, (tile + 1) * LANES)
            lg = [math.log(1.0 - 2.0 ** (-5.0 - (2 * tile + par))) for par in range(2)]
            lg_lane = jnp.where(first, lg[0], lg[1])
            q = q_ref[bi, :, cols].astype(BF16)
            k = k_ref[bi, :, cols] * HEAD_DIM ** -0.5
            v = v_ref[bi, :, cols]
            state = state_scr[bi, tile]
            out = _dot(q, state.astype(BF16)) * jnp.exp(lg_lane * (rowf + 1.0))
            for par in range(2):
                keep = first if par == 0 else ~first
                decay = jnp.where(diff >= 0, jnp.exp(lg[par] * jnp.maximum(diff, 0.0)), 0.0)
                inner = _dot_nt(q, jnp.where(keep, k, 0.0).astype(BF16)) * decay
                out = out + _dot(inner.astype(BF16), jnp.where(keep, v, 0.0).astype(BF16))
            k_dec = (k * jnp.exp(lg_lane * (cs - 1.0 - rowf))).astype(BF16)
            kv = _dot_tn(k_dec, v.astype(BF16))
            state_scr[bi, tile] = state * jnp.exp(lg_lane * cs) + jnp.where(same_head, kv, 0.0)
            outs.append(out)

    out = jnp.concatenate(outs, axis=0)
    mu = seg_mean_dot(out)
    cen = out - mu
    var = seg_mean_dot(cen * cen)
    yn = cen * lax.rsqrt(var + EPS)
    for bi in range(q_ref.shape[0]):
        for tile in range(tiles):
            cols = slice(tile * LANES, (tile + 1) * LANES)
            y = yn[(bi * tiles + tile) * cs:(bi * tiles + tile + 1) * cs] * gn_ref[:, cols]
            o_ref[bi, :, cols] = (y * _silu(g_ref[bi, :, cols])).astype(o_ref.dtype)


RET_BATCH = 4


def _retention(proj, ret_norm, layer):
    b, s, _ = proj.shape
    nc = s // BLOCK
    rb = math.gcd(b, RET_BATCH)
    spec = lambda col: pl.BlockSpec((rb, BLOCK, C_WIDTH), lambda b, c: (b, c, col // C_WIDTH))
    return pl.pallas_call(
        _ret_kernel,
        grid=(b // rb, nc),
        in_specs=[spec(COL_CQ), spec(COL_CK), spec(COL_CV), spec(COL_CG),
                  pl.BlockSpec((1, C_WIDTH), lambda b, c: (0, 0))],
        out_specs=pl.BlockSpec((rb, BLOCK, C_WIDTH), lambda b, c: (b, c, 0)),
        out_shape=jax.ShapeDtypeStruct((b, s, C_WIDTH), BF16),
        scratch_shapes=[pltpu.VMEM((rb, RET_HEADS // 2, LANES, LANES), F32)],
        compiler_params=pltpu.CompilerParams(
            dimension_semantics=("parallel", "arbitrary"), vmem_limit_bytes=VMEM_LIMIT),
        name=f"ret_l{layer}",
    )(proj, proj, proj, proj, ret_norm)


def _mix_out_kernel(x_ref, gt_ref, gpost_ref, oa_ref, ob_ref, oc_ref, w_ref, o_ref, gain_scr, inv_scr):
    y = _dot(oa_ref[...], w_ref[:A_WIDTH, :])
    y = y + _dot(ob_ref[...], w_ref[A_WIDTH:A_WIDTH + B_WIDTH, :])
    o_ref[...] = y + _dot(oc_ref[...], w_ref[A_WIDTH + B_WIDTH:, :])
    gain_scr[...] = gt_ref[...] * gpost_ref[...]
    _postnorm_residual_rows(o_ref, x_ref, gain_scr, inv_scr, o_ref)


def _mix_out(x, mod, norm_g, layer, oa, ob, oc, w, tm):
    b, s, d = x.shape
    row = lambda width: pl.BlockSpec((None, tm, width), lambda b, i: (b, i, 0))
    return pl.pallas_call(
        _mix_out_kernel,
        grid=(b, s // tm),
        in_specs=[
            row(d), _mod_spec(layer, 5, d, 2), _gain_spec(layer, 3, d, 2),
            row(A_WIDTH), row(B_WIDTH), row(C_WIDTH),
            pl.BlockSpec((None,) + w.shape[1:], lambda b, i: (layer, 0, 0)),
        ],
        out_specs=row(d),
        out_shape=jax.ShapeDtypeStruct((b, s, d), F32),
        scratch_shapes=[pltpu.VMEM((1, d), F32), pltpu.VMEM((tm, 1), F32)],
        compiler_params=pltpu.CompilerParams(
            dimension_semantics=("parallel", "parallel"), vmem_limit_bytes=VMEM_LIMIT),
        name=f"mix_out_l{layer}",
    )(x, mod, norm_g, oa, ob, oc, w)


def _prep_ffn_in(w_in, tf):
    depth, d, two_f = w_in.shape
    d_ff = two_f // 2
    nf = pl.cdiv(d_ff, tf)
    w = w_in.reshape(depth, d, 2, d_ff)
    full = w[..., :(nf - 1) * tf].reshape(depth, d, 2, nf - 1, tf)
    last = w[..., d_ff - tf:].reshape(depth, d, 2, 1, tf)
    blocks = jnp.concatenate([full, last], axis=3)
    return jnp.transpose(blocks, (0, 2, 3, 1, 4)).astype(BF16)


def _prep_mix_in(w):
    sizes = [A_WIDTH, KV_WIDTH, KV_WIDTH, B_WIDTH, DSA_RANK, IDX_HEADS * IDX_DIM, IDX_DIM, IDX_HEADS,
             C_WIDTH, C_WIDTH, C_WIDTH, C_WIDTH]
    starts = [0]
    for sz in sizes:
        starts.append(starts[-1] + sz)
    aq, ak, av, bq, bkv, biq, bik, biw, cq, ck, cv, cg = [w[:, a:a + sz] for a, sz in zip(starts, sizes)]
    tail = jnp.zeros((w.shape[0], LANES - IDX_DIM - IDX_HEADS), w.dtype)
    return jnp.concatenate([aq, bq, biq, cq, ck, cv, cg, ak, av, bkv, bik, biw, tail], axis=1).astype(BF16)


def _prep_dsa_up(w_uk, w_uv):
    r, h, dh = w_uk.shape
    uk = jnp.transpose(w_uk, (1, 2, 0))
    uv = jnp.transpose(w_uv, (1, 0, 2))
    odd = (jnp.arange(h) % 2 == 1)[:, None, None]
    zk = jnp.zeros_like(uk)
    zv = jnp.zeros_like(uv)
    uk_pad = jnp.where(odd, jnp.concatenate([zk, uk], axis=1), jnp.concatenate([uk, zk], axis=1))
    uv_pad = jnp.where(odd, jnp.concatenate([zv, uv], axis=2), jnp.concatenate([uv, zv], axis=2))
    return uk_pad.astype(BF16), uv_pad.astype(BF16)


FFN_TM = 512
FFN_TF = 512
MIX_IN_TM = 1024
MIX_IN_TN = 1536
MIX_OUT_TM = 512


def kernel(x, c, ada_w, ada_b, norm_g, ffn1_w_in, ffn1_w_out, ffn2_w_in, ffn2_w_out,
           mix_w_in, mix_w_out, swa_sinks, dsa_kv_norm, dsa_w_uk, dsa_w_uv, ret_norm):
    depth = ada_w.shape[0]
    b, s, d = x.shape
    assert s % FFN_TM == 0 and s % DSA_KCHUNK == 0 and d % LANES == 0
    mod = _ada_mod(c, ada_w, ada_b).reshape(depth, b, N_MOD, 1, d)
    gains = norm_g.reshape(depth, norm_g.shape[1], 1, d)
    ffn1_in, ffn1_out = _prep_ffn_in(ffn1_w_in, FFN_TF), ffn1_w_out.astype(BF16)
    ffn2_in, ffn2_out = _prep_ffn_in(ffn2_w_in, FFN_TF), ffn2_w_out.astype(BF16)
    mix_out_w = mix_w_out.astype(BF16)
    for l in range(depth):
        x = _ffn_block(x, mod, gains, l, 0, ffn1_in, ffn1_out, 0.5, FFN_TM, FFN_TF)

        proj = _mix_in(x, mod, gains, l, _prep_mix_in(mix_w_in[l]), MIX_IN_TM, MIX_IN_TN)
        oa = _swa(proj, swa_sinks[l], l)
        uk_pad, uv_pad = _prep_dsa_up(dsa_w_uk[l], dsa_w_uv[l])
        ob = _dsa(proj, dsa_kv_norm[l].reshape(1, -1), uk_pad, uv_pad, l)
        oc = _retention(proj, ret_norm[l].reshape(1, -1), l)
        x = _mix_out(x, mod, gains, l, oa, ob, oc, mix_out_w, MIX_OUT_TM)

        x = _ffn_block(x, mod, gains, l, 2, ffn2_in, ffn2_out, 0.5, FFN_TM, FFN_TF)
    return x
```

```python
import functools
import math

import jax
import jax.numpy as jnp
from jax import lax
from jax.experimental import pallas as pl
from jax.experimental.pallas import tpu as pltpu

F32 = jnp.float32
BF16 = jnp.bfloat16

LANES = 128
HEAD_DIM = 64
SWA_HEADS = 16
SWA_KV_HEADS = 2
BLOCK = 128
DSA_HEADS = 8
DSA_RANK = 128
IDX_HEADS = 8
IDX_DIM = 64
DSA_TOPK_MAX = 256
RET_HEADS = 8
N_MOD = 9
EPS = 1e-6
NEG = -1e30

A_WIDTH = SWA_HEADS * HEAD_DIM
B_WIDTH = DSA_HEADS * HEAD_DIM
C_WIDTH = RET_HEADS * HEAD_DIM
KV_WIDTH = SWA_KV_HEADS * HEAD_DIM

COL_AQ = 0
COL_BQ = COL_AQ + A_WIDTH
COL_BIQ = COL_BQ + B_WIDTH
COL_CQ = COL_BIQ + IDX_HEADS * IDX_DIM
COL_CK = COL_CQ + C_WIDTH
COL_CV = COL_CK + C_WIDTH
COL_CG = COL_CV + C_WIDTH
COL_AKV = COL_CG + C_WIDTH
COL_BKV = COL_AKV + 2 * KV_WIDTH
COL_BIK = COL_BKV + DSA_RANK
MIX_PAD_WIDTH = COL_BIK + LANES

DSA_QBLOCK = 256
DSA_KCHUNK = 256
BISECT_ITERS = 18
VMEM_LIMIT = 56 * 1024 * 1024


def _dot(a, b):
    return jnp.dot(a, b, preferred_element_type=F32)


def _dot_nt(a, b):
    return lax.dot_general(a, b, (((1,), (1,)), ((), ())), preferred_element_type=F32)


def _dot_tn(a, b):
    return lax.dot_general(a, b, (((0,), (0,)), ((), ())), preferred_element_type=F32)


def _silu(x):
    return x / (1.0 + jnp.exp(-x))


def _rms(x, g):
    return x * lax.rsqrt(jnp.mean(x * x, axis=-1, keepdims=True) + EPS) * g


def _alibi_slope(h, n):
    return 2.0 ** (-8.0 * (h + 1) / n)


def _ada_kernel(c_ref, w_ref, b_ref, o_ref):
    cond = _silu(c_ref[...]).astype(BF16)
    o_ref[...] = _dot(cond, w_ref[...].astype(BF16)) + b_ref[...]


def _ada_mod(c, ada_w, ada_b):
    depth, d, n = ada_w.shape
    b = c.shape[0]
    tn = 1024
    return pl.pallas_call(
        _ada_kernel,
        grid=(depth, n // tn),
        in_specs=[
            pl.BlockSpec((b, d), lambda l, j: (0, 0)),
            pl.BlockSpec((None, d, tn), lambda l, j: (l, 0, j)),
            pl.BlockSpec((None, 1, tn), lambda l, j: (l, 0, j)),
        ],
        out_specs=pl.BlockSpec((None, b, tn), lambda l, j: (l, 0, j)),
        out_shape=jax.ShapeDtypeStruct((depth, b, n), F32),
        compiler_params=pltpu.CompilerParams(
            dimension_semantics=("arbitrary", "arbitrary"), vmem_limit_bytes=VMEM_LIMIT),
        name="ada_mod",
    )(c, ada_w, ada_b.reshape(depth, 1, n))


def _mod_spec(layer, k, d, grid_rank):
    if grid_rank == 2:
        return pl.BlockSpec((None, None, None, 1, d), lambda b, i: (layer, b, k, 0, 0))
    return pl.BlockSpec((None, None, None, 1, d), lambda b, i, j: (layer, b, k, 0, 0))


def _gain_spec(layer, k, d, grid_rank):
    if grid_rank == 2:
        return pl.BlockSpec((None, None, 1, d), lambda b, i: (layer, k, 0, 0))
    return pl.BlockSpec((None, None, 1, d), lambda b, i, j: (layer, k, 0, 0))


ROW_CHUNK = 16


def _row_chunks(n_rows, body):
    def step(i, carry):
        body(pl.ds(pl.multiple_of(i * ROW_CHUNK, ROW_CHUNK), ROW_CHUNK))
        return carry
    lax.fori_loop(0, n_rows // ROW_CHUNK, step, 0, unroll=True)


def _inv_rms_rows(x_ref, inv_ref):
    def body(r):
        x = x_ref[r, :]
        inv_ref[r, :] = lax.rsqrt(jnp.mean(x * x, axis=-1, keepdims=True) + EPS)
    _row_chunks(x_ref.shape[0], body)


def _prenorm_mod_rows(x_ref, gain_ref, shift_ref, inv_ref, h_ref):
    _inv_rms_rows(x_ref, inv_ref)

    def body(r):
        h_ref[r, :] = (x_ref[r, :] * inv_ref[r, :] * gain_ref[...] + shift_ref[...]).astype(h_ref.dtype)
    _row_chunks(x_ref.shape[0], body)


def _postnorm_residual_rows(y_ref, x_ref, gain_ref, inv_ref, o_ref):
    _inv_rms_rows(y_ref, inv_ref)

    def body(r):
        o_ref[r, :] = x_ref[r, :] + y_ref[r, :] * inv_ref[r, :] * gain_ref[...]
    _row_chunks(y_ref.shape[0], body)


def _ffn_kernel(res_w, tail, x_ref, sh_ref, sc_ref, gt_ref, gpre_ref, gpost_ref, wg_ref, wu_ref, wo_ref,
                o_ref, h_scr, gain_scr, inv_scr):
    f = pl.program_id(2)
    last = pl.num_programs(2) - 1
    tf = wo_ref.shape[1]

    def step(lo, first):
        h = h_scr[...]
        act = _silu(_dot(h, wg_ref[0, :, lo:])) * _dot(h, wu_ref[0, :, lo:])
        out = _dot(act.astype(BF16), wo_ref[0, lo:, :])
        if first:
            o_ref[...] = out
        else:
            o_ref[...] += out

    @pl.when(f == 0)
    def _():
        gain_scr[...] = gpre_ref[...] * (1.0 + sc_ref[...])
        _prenorm_mod_rows(x_ref, gain_scr, sh_ref, inv_scr, h_scr)
        step(0, True)

    @pl.when((f > 0) & (f < last))
    def _():
        step(0, False)

    @pl.when(f == last)
    def _():
        step(tf - tail, False)
        gain_scr[...] = res_w * gt_ref[...] * gpost_ref[...]
        _postnorm_residual_rows(o_ref, x_ref, gain_scr, inv_scr, o_ref)


def _ffn_block(x, mod, norm_g, layer, sub, w_in, w_out, res_w, tm, tf):
    b, s, d = x.shape
    d_ff = w_out.shape[1]
    nf = pl.cdiv(d_ff, tf)
    tail = d_ff - (nf - 1) * tf
    assert tail % LANES == 0 and d_ff >= tf and nf >= 2
    mk = 3 * sub
    start = lambda j, base: pl.multiple_of(base + jnp.minimum(j * tf, d_ff - tf), LANES)
    col = lambda base: (lambda b, i, j: (layer, 0, start(j, base)))
    one = pl.Element(1)
    return pl.pallas_call(
        functools.partial(_ffn_kernel, res_w, tail),
        grid=(b, s // tm, nf),
        in_specs=[
            pl.BlockSpec((None, tm, d), lambda b, i, j: (b, i, 0)),
            _mod_spec(layer, mk, d, 3), _mod_spec(layer, mk + 1, d, 3), _mod_spec(layer, mk + 2, d, 3),
            _gain_spec(layer, 2 * sub, d, 3), _gain_spec(layer, 2 * sub + 1, d, 3),
            pl.BlockSpec((one, pl.Element(d), pl.Element(tf)), col(0)),
            pl.BlockSpec((one, pl.Element(d), pl.Element(tf)), col(d_ff)),
            pl.BlockSpec((one, pl.Element(tf), pl.Element(d)), lambda b, i, j: (layer, start(j, 0), 0)),
        ],
        out_specs=pl.BlockSpec((None, tm, d), lambda b, i, j: (b, i, 0)),
        out_shape=jax.ShapeDtypeStruct((b, s, d), F32),
        scratch_shapes=[pltpu.VMEM((tm, d), BF16), pltpu.VMEM((1, d), F32), pltpu.VMEM((tm, 1), F32)],
        compiler_params=pltpu.CompilerParams(
            dimension_semantics=("parallel", "parallel", "arbitrary"), vmem_limit_bytes=FFN_VMEM_LIMIT),
        name=f"ffn_l{layer}_s{sub}",
    )(x, mod, mod, mod, norm_g, norm_g, w_in, w_in, w_out)


def _mix_in_kernel(x_ref, sh_ref, sc_ref, gpre_ref, w_ref, o_ref, h_scr, gain_scr, inv_scr):
    @pl.when(pl.program_id(2) == 0)
    def _():
        gain_scr[...] = gpre_ref[...] * (1.0 + sc_ref[...])
        _prenorm_mod_rows(x_ref, gain_scr, sh_ref, inv_scr, h_scr)

    o_ref[...] = _dot(h_scr[...], w_ref[...])


def _mix_in(x, mod, norm_g, layer, w, tm, tn):
    b, s, d = x.shape
    n = w.shape[1]
    return pl.pallas_call(
        _mix_in_kernel,
        grid=(b, s // tm, n // tn),
        in_specs=[
            pl.BlockSpec((None, tm, d), lambda b, i, j: (b, i, 0)),
            _mod_spec(layer, 3, d, 3), _mod_spec(layer, 4, d, 3),
            _gain_spec(layer, 2, d, 3),
            pl.BlockSpec((d, tn), lambda b, i, j: (0, j)),
        ],
        out_specs=pl.BlockSpec((None, tm, tn), lambda b, i, j: (b, i, j)),
        out_shape=jax.ShapeDtypeStruct((b, s, n), F32),
        scratch_shapes=[pltpu.VMEM((tm, d), BF16), pltpu.VMEM((1, d), F32), pltpu.VMEM((tm, 1), F32)],
        compiler_params=pltpu.CompilerParams(
            dimension_semantics=("parallel", "parallel", "arbitrary"), vmem_limit_bytes=VMEM_LIMIT),
        name=f"mix_in_l{layer}",
    )(x, mod, mod, norm_g, w)


def _lane_halves(x, first_half_holds_data):
    lane = lax.broadcasted_iota(jnp.int32, x.shape, 1)
    if first_half_holds_data:
        lo = jnp.where(lane < HEAD_DIM, x, 0.0)
        return lo, pltpu.roll(lo, HEAD_DIM, 1)
    hi = jnp.where(lane >= HEAD_DIM, x, 0.0)
    return pltpu.roll(hi, HEAD_DIM, 1), hi


def _swa_kernel(sinks_ref, q_ref, kvc_ref, kvp_ref, o_ref):
    n = pl.program_id(1)
    w = BLOCK
    kcat = jnp.concatenate([kvp_ref[:, :KV_WIDTH], kvc_ref[:, :KV_WIDTH]], axis=0)
    vcat = jnp.concatenate([kvp_ref[:, KV_WIDTH:], kvc_ref[:, KV_WIDTH:]], axis=0)
    k_halves = [tuple(t.astype(BF16) for t in _lane_halves(kcat, hk == 0)) for hk in range(SWA_KV_HEADS)]
    v_halves = [tuple(t.astype(BF16) for t in _lane_halves(vcat, hk == 0)) for hk in range(SWA_KV_HEADS)]

    bands = SWA_HEADS // SWA_KV_HEADS // 2
    rows = bands * w
    r = lax.broadcasted_iota(jnp.int32, (rows, 2 * w), 0)
    j = lax.broadcasted_iota(jnp.int32, (rows, 2 * w), 1)
    dist = (r & (w - 1)) - j + w
    valid = (dist >= 0) & (dist < w) & ((n - 1) * w + j >= 0)
    band_step = _alibi_slope(2, SWA_HEADS) / _alibi_slope(0, SWA_HEADS)
    band_scale = jnp.ones((rows, 2 * w), F32)
    for band in range(1, bands):
        band_scale = jnp.where(r >= band * w, band_step ** band, band_scale)
    dist_scaled = dist.astype(F32) * band_scale
    band_col = lax.broadcasted_iota(jnp.int32, (rows, 1), 0)
    for hk in range(SWA_KV_HEADS):
        q = jnp.concatenate([q_ref[:, (bands * hk + band) * LANES:(bands * hk + band + 1) * LANES]
                             for band in range(bands)], axis=0).astype(BF16)
        out = None
        for par in range(2):
            head0 = 2 * bands * hk + par
            sink = jnp.full((rows, 1), sinks_ref[head0], F32)
            for band in range(1, bands):
                sink = jnp.where(band_col >= band * w, sinks_ref[head0 + 2 * band], sink)
            sc = _dot_nt(q, k_halves[hk][par]) * HEAD_DIM ** -0.5
            sc = jnp.where(valid, sc - _alibi_slope(head0, SWA_HEADS) * dist_scaled, -jnp.inf)
            m = jnp.maximum(jnp.max(sc, axis=-1, keepdims=True), sink)
            e = jnp.exp(sc - m)
            p = e / (jnp.sum(e, axis=-1, keepdims=True) + jnp.exp(sink - m))
            pv = _dot(p.astype(BF16), v_halves[hk][par])
            out = pv if out is None else out + pv
        for band in range(bands):
            tile = bands * hk + band
            o_ref[:, tile * LANES:(tile + 1) * LANES] = out[band * w:(band + 1) * w].astype(o_ref.dtype)


def _swa(proj, sinks, layer):
    b, s, _ = proj.shape
    nb = s // BLOCK
    kv_blk = COL_AKV // (2 * KV_WIDTH)
    return pl.pallas_call(
        _swa_kernel,
        grid=(b, nb),
        in_specs=[
            pl.BlockSpec(memory_space=pltpu.SMEM),
            pl.BlockSpec((None, BLOCK, A_WIDTH), lambda b, n: (b, n, COL_AQ // A_WIDTH)),
            pl.BlockSpec((None, BLOCK, 2 * KV_WIDTH), lambda b, n: (b, n, kv_blk)),
            pl.BlockSpec((None, BLOCK, 2 * KV_WIDTH), lambda b, n: (b, jnp.maximum(n - 1, 0), kv_blk)),
        ],
        out_specs=pl.BlockSpec((None, BLOCK, A_WIDTH), lambda b, n: (b, n, 0)),
        out_shape=jax.ShapeDtypeStruct((b, s, A_WIDTH), BF16),
        compiler_params=pltpu.CompilerParams(
            dimension_semantics=("parallel", "arbitrary"), vmem_limit_bytes=VMEM_LIMIT),
        name=f"swa_l{layer}",
    )(sinks, proj, proj, proj)


def _dsa_kernel(topk, q_ref, iq_ref, kv_ref, ikw_ref, kvn_ref, wuk_ref, wuv_ref, o_ref,
                ckv_scr, ik_scr, iqh_scr, score_scr, bias_scr, qlat_scr, s_scr, mrun_scr, m_scr,
                lpart_scr, acc_scr):
    n = pl.program_id(1)
    qb = DSA_QBLOCK
    kc = DSA_KCHUNK
    nchunks = lax.div(n * qb + qb - 1, kc) + 1

    @pl.when(n == 0)
    def _():
        ckv_scr[...] = _rms(kv_ref[...], kvn_ref[...]).astype(BF16)
        ik_scr[...] = ikw_ref[...].astype(BF16)

    row0 = pl.multiple_of(n * qb, qb)

    iw_t = ikw_ref[pl.ds(row0, qb), :].T
    iw_scale = IDX_HEADS ** -0.5 * IDX_DIM ** -0.5
    w_rows = [iw_t[IDX_DIM + h:IDX_DIM + h + 1, :] * iw_scale for h in range(IDX_HEADS)]
    lane = lax.broadcasted_iota(jnp.int32, (qb, LANES), 1)
    for h in range(IDX_HEADS):
        iq = iq_ref[:, (h // 2) * LANES:(h // 2 + 1) * LANES]
        if h % 2 == 0:
            iq = jnp.where(lane < IDX_DIM, iq, 0.0)
        else:
            iq = pltpu.roll(jnp.where(lane >= IDX_DIM, iq, 0.0), IDX_DIM, 1)
        iqh_scr[h] = iq.astype(BF16)

    key_row = lax.broadcasted_iota(jnp.int32, (kc, qb), 0)
    t_pos = n * qb + lax.broadcasted_iota(jnp.int32, (kc, qb), 1)

    def score_chunk(c, carry):
        off = pl.multiple_of(c * kc, kc)
        ik = ik_scr[pl.ds(off, kc), :]
        sc = jnp.zeros((kc, qb), F32)
        for h in range(IDX_HEADS):
            sc = sc + w_rows[h] * jnp.maximum(_dot_nt(ik, iqh_scr[h]), 0.0)
        score_scr[c] = jnp.where(off + key_row <= t_pos, sc, -jnp.inf)
        return carry

    lax.fori_loop(0, nchunks, score_chunk, 0)

    sub = 8

    def fold_rows(x, op):
        out = x[:sub]
        for g in range(1, x.shape[0] // sub):
            out = op(out, x[g * sub:(g + 1) * sub])
        return out

    def fold_lanes(x, op):
        out = x[:, :LANES]
        for t in range(1, x.shape[1] // LANES):
            out = op(out, x[:, t * LANES:(t + 1) * LANES])
        return out

    def count_ge(thr, strict=False):
        def body(c, cnt):
            x = score_scr[c]
            hit = (x > thr) if strict else (x >= thr)
            return cnt + fold_rows(jnp.where(hit, 1.0, 0.0), jnp.add)
        part = lax.fori_loop(0, nchunks, body, jnp.zeros((sub, qb), F32))
        return jnp.sum(part, axis=0, keepdims=True)

    def max_at_most(bound):
        def body(c, mx):
            x = score_scr[c]
            return jnp.maximum(mx, fold_rows(jnp.where(x <= bound, x, -jnp.inf), jnp.maximum))
        part = lax.fori_loop(0, nchunks, body, jnp.full((sub, qb), -jnp.inf, F32))
        return jnp.max(part, axis=0, keepdims=True)

    def store_bias(c, t, bias_t):
        for g in range(qb // LANES):
            bias_scr[c, g * LANES:(g + 1) * LANES, t * LANES:(t + 1) * LANES] = (
                bias_t[:, g * LANES:(g + 1) * LANES].T)

    @pl.when(n * qb + qb <= topk)
    def _():
        def body(c, carry):
            x = score_scr[c]
            for t in range(kc // LANES):
                store_bias(c, t, jnp.where(x[t * LANES:(t + 1) * LANES] == -jnp.inf, NEG, 0.0))
            return carry
        lax.fori_loop(0, nchunks, body, 0)

    @pl.when(n * qb + qb > topk)
    def _():
        kf = float(topk)

        def minmax(c, carry):
            mn, mx = carry
            x = score_scr[c]
            mn = jnp.minimum(mn, fold_rows(jnp.where(x == -jnp.inf, jnp.inf, x), jnp.minimum))
            return mn, jnp.maximum(mx, fold_rows(x, jnp.maximum))

        lo, hi = lax.fori_loop(0, nchunks, minmax,
                               (jnp.full((sub, qb), jnp.inf, F32), jnp.full((sub, qb), -jnp.inf, F32)))
        lo = jnp.min(lo, axis=0, keepdims=True)
        hi = jnp.max(hi, axis=0, keepdims=True)

        def bisect(_, carry):
            lo, hi = carry
            mid = 0.5 * (lo + hi)
            ge = count_ge(mid) >= kf
            return jnp.where(ge, mid, lo), jnp.where(ge, hi, mid)

        lo, hi = lax.fori_loop(0, BISECT_ITERS, bisect, (lo, hi))

        def walk_cond(carry):
            return carry[1] > 0.0

        def count_and_next(thr):
            def body(c, carry):
                cnt, mx = carry
                x = score_scr[c]
                ge = x >= thr
                cnt = cnt + fold_rows(jnp.where(ge, 1.0, 0.0), jnp.add)
                return cnt, jnp.maximum(mx, fold_rows(jnp.where(ge, -jnp.inf, x), jnp.maximum))
            cnt, mx = lax.fori_loop(0, nchunks, body, (jnp.zeros((sub, qb), F32),
                                                       jnp.full((sub, qb), -jnp.inf, F32)))
            return jnp.sum(cnt, axis=0, keepdims=True), jnp.max(mx, axis=0, keepdims=True)

        def walk(carry):
            thr, _ = carry
            cnt, below = count_and_next(thr)
            short = cnt < kf
            return jnp.where(short, below, thr), jnp.max(jnp.where(short, 1.0, 0.0))

        thr, _ = lax.while_loop(walk_cond, walk, (max_at_most(hi), jnp.float32(1.0)))

        need = kf - count_ge(thr, strict=True)

        r = lax.broadcasted_iota(jnp.int32, (LANES, LANES), 0)
        col = lax.broadcasted_iota(jnp.int32, (LANES, LANES), 1)
        tri = jnp.where(col <= r, 1.0, 0.0).astype(BF16)

        def select(c, seen):
            x = score_scr[c]
            for t in range(kc // LANES):
                xt = x[t * LANES:(t + 1) * LANES]
                eq = xt == thr
                eqf = jnp.where(eq, 1.0, 0.0)
                rank = seen + _dot(tri, eqf.astype(BF16))
                seen = seen + jnp.sum(eqf, axis=0, keepdims=True)
                sel = (xt > thr) | (eq & (rank <= need))
                store_bias(c, t, jnp.where(sel, 0.0, NEG))
            return seen

        lax.fori_loop(0, nchunks, select, jnp.zeros((1, qb), F32))

    for h in range(DSA_HEADS):
        tile = h // 2
        rows = slice(h * qb, (h + 1) * qb)
        q = q_ref[:, tile * LANES:(tile + 1) * LANES].astype(BF16)
        qlat_scr[rows, :] = (_dot(q, wuk_ref[h]) * HEAD_DIM ** -0.5).astype(BF16)
    mrun_scr[...] = jnp.full(mrun_scr.shape, NEG, F32)
    key_lane = lax.broadcasted_iota(jnp.int32, (1, kc), 1)

    def logits(c, carry):
        off = pl.multiple_of(c * kc, kc)
        s = _dot_nt(qlat_scr[...], ckv_scr[pl.ds(off, kc), :])
        bias = bias_scr[c]
        key_pos = (off + key_lane).astype(F32)
        for h in range(DSA_HEADS):
            rows = slice(h * qb, (h + 1) * qb)
            sh = s[rows] + (bias + _alibi_slope(h, DSA_HEADS) * key_pos)
            s_scr[c, rows, :] = sh
            mrun_scr[rows, :] = jnp.maximum(mrun_scr[rows, :], fold_lanes(sh, jnp.maximum))
        return carry

    lax.fori_loop(0, nchunks, logits, 0)
    m_scr[...] = jnp.broadcast_to(jnp.max(mrun_scr[...], axis=-1, keepdims=True), m_scr.shape)
    lpart_scr[...] = jnp.zeros_like(lpart_scr)
    acc_scr[...] = jnp.zeros_like(acc_scr)

    def attend(c, carry):
        off = pl.multiple_of(c * kc, kc)
        m = m_scr[...]
        p = [jnp.exp(s_scr[c, :, t * LANES:(t + 1) * LANES] - m) for t in range(kc // LANES)]
        lpart_scr[...] += functools.reduce(jnp.add, p)
        acc_scr[...] += _dot(jnp.concatenate(p, axis=1).astype(BF16), ckv_scr[pl.ds(off, kc), :])
        return carry

    lax.fori_loop(0, nchunks, attend, 0)

    o_all = (acc_scr[...] / jnp.sum(lpart_scr[...], axis=-1, keepdims=True)).astype(BF16)
    for tile in range(DSA_HEADS // 2):
        h = 2 * tile
        out = _dot(o_all[h * qb:(h + 1) * qb], wuv_ref[h]) + _dot(o_all[(h + 1) * qb:(h + 2) * qb], wuv_ref[h + 1])
        o_ref[:, tile * LANES:(tile + 1) * LANES] = out.astype(o_ref.dtype)


def _dsa(proj, kv_norm, wuk_pad, wuv_pad, layer):
    b, s, _ = proj.shape
    qb = DSA_QBLOCK
    nb = s // qb
    topk = min(DSA_TOPK_MAX, s // 4)
    assert topk % qb == 0
    nck = s // DSA_KCHUNK
    rows = DSA_HEADS * qb
    return pl.pallas_call(
        functools.partial(_dsa_kernel, topk),
        grid=(b, nb),
        in_specs=[
            pl.BlockSpec((None, qb, B_WIDTH), lambda b, n: (b, n, COL_BQ // B_WIDTH)),
            pl.BlockSpec((None, qb, B_WIDTH), lambda b, n: (b, n, COL_BIQ // B_WIDTH)),
            pl.BlockSpec((None, s, DSA_RANK), lambda b, n: (b, 0, COL_BKV // DSA_RANK)),
            pl.BlockSpec((None, s, LANES), lambda b, n: (b, 0, COL_BIK // LANES)),
            pl.BlockSpec((1, DSA_RANK), lambda b, n: (0, 0)),
            pl.BlockSpec((DSA_HEADS, LANES, DSA_RANK), lambda b, n: (0, 0, 0)),
            pl.BlockSpec((DSA_HEADS, DSA_RANK, LANES), lambda b, n: (0, 0, 0)),
        ],
        out_specs=pl.BlockSpec((None, qb, B_WIDTH), lambda b, n: (b, n, 0)),
        out_shape=jax.ShapeDtypeStruct((b, s, B_WIDTH), BF16),
        scratch_shapes=[
            pltpu.VMEM((s, DSA_RANK), BF16),
            pltpu.VMEM((s, LANES), BF16),
            pltpu.VMEM((IDX_HEADS, qb, LANES), BF16),
            pltpu.VMEM((nck, DSA_KCHUNK, qb), F32),
            pltpu.VMEM((nck, qb, DSA_KCHUNK), F32),
            pltpu.VMEM((rows, DSA_RANK), BF16),
            pltpu.VMEM((nck, rows, DSA_KCHUNK), F32),
            pltpu.VMEM((rows, LANES), F32),
            pltpu.VMEM((rows, LANES), F32),
            pltpu.VMEM((rows, LANES), F32),
            pltpu.VMEM((rows, DSA_RANK), F32),
        ],
        compiler_params=pltpu.CompilerParams(
            dimension_semantics=("parallel", "arbitrary"), vmem_limit_bytes=VMEM_LIMIT),
        name=f"dsa_l{layer}",
    )(proj, proj, proj, proj, kv_norm, wuk_pad, wuv_pad)


def _ret_kernel(q_ref, k_ref, v_ref, g_ref, gn_ref, o_ref, state_scr):
    cs = BLOCK

    @pl.when(pl.program_id(1) == 0)
    def _():
        state_scr[...] = jnp.zeros_like(state_scr)

    row = lax.broadcasted_iota(jnp.int32, (cs, LANES), 0)
    lane = lax.broadcasted_iota(jnp.int32, (cs, LANES), 1)
    first = lane < HEAD_DIM
    rowf = row.astype(F32)
    diff = (row - lane).astype(F32)
    same_head = (row < HEAD_DIM) == first
    seg_mean = jnp.where(same_head, 1.0 / HEAD_DIM, 0.0).astype(BF16)

    def seg_mean_dot(x):
        x_hi = x.astype(BF16)
        x_lo = (x - x_hi.astype(F32)).astype(BF16)
        return _dot(x_hi, seg_mean) + _dot(x_lo, seg_mean)

    tiles = RET_HEADS // 2
    outs = []
    for bi in range(q_ref.shape[0]):
        for tile in range(tiles):
            cols = slice(tile * LANES, (tile + 1) * LANES)
            lg = [math.log(1.0 - 2.0 ** (-5.0 - (2 * tile + par))) for par in range(2)]
            lg_lane = jnp.where(first, lg[0], lg[1])
            q = q_ref[bi, :, cols].astype(BF16)
            k = k_ref[bi, :, cols] * HEAD_DIM ** -0.5
            v = v_ref[bi, :, cols]
            state = state_scr[bi, tile]
            out = _dot(q, state.astype(BF16)) * jnp.exp(lg_lane * (rowf + 1.0))
            for par in range(2):
                keep = first if par == 0 else ~first
                decay = jnp.where(diff >= 0, jnp.exp(lg[par] * jnp.maximum(diff, 0.0)), 0.0)
                inner = _dot_nt(q, jnp.where(keep, k, 0.0).astype(BF16)) * decay
                out = out + _dot(inner.astype(BF16), jnp.where(keep, v, 0.0).astype(BF16))
            k_dec = (k * jnp.exp(lg_lane * (cs - 1.0 - rowf))).astype(BF16)
            kv = _dot_tn(k_dec, v.astype(BF16))
            state_scr[bi, tile] = state * jnp.exp(lg_lane * cs) + jnp.where(same_head, kv, 0.0)
            outs.append(out)

    out = jnp.concatenate(outs, axis=0)
    mu = seg_mean_dot(out)
    cen = out - mu
    var = seg_mean_dot(cen * cen)
    yn = cen * lax.rsqrt(var + EPS)
    for bi in range(q_ref.shape[0]):
        for tile in range(tiles):
            cols = slice(tile * LANES, (tile + 1) * LANES)
            y = yn[(bi * tiles + tile) * cs:(bi * tiles + tile + 1) * cs] * gn_ref[:, cols]
            o_ref[bi, :, cols] = (y * _silu(g_ref[bi, :, cols])).astype(o_ref.dtype)


RET_BATCH = 4


def _retention(proj, ret_norm, layer):
    b, s, _ = proj.shape
    nc = s // BLOCK
    rb = math.gcd(b, RET_BATCH)
    spec = lambda col: pl.BlockSpec((rb, BLOCK, C_WIDTH), lambda b, c: (b, c, col // C_WIDTH))
    return pl.pallas_call(
        _ret_kernel,
        grid=(b // rb, nc),
        in_specs=[spec(COL_CQ), spec(COL_CK), spec(COL_CV), spec(COL_CG),
                  pl.BlockSpec((1, C_WIDTH), lambda b, c: (0, 0))],
        out_specs=pl.BlockSpec((rb, BLOCK, C_WIDTH), lambda b, c: (b, c, 0)),
        out_shape=jax.ShapeDtypeStruct((b, s, C_WIDTH), BF16),
        scratch_shapes=[pltpu.VMEM((rb, RET_HEADS // 2, LANES, LANES), F32)],
        compiler_params=pltpu.CompilerParams(
            dimension_semantics=("parallel", "arbitrary"), vmem_limit_bytes=VMEM_LIMIT),
        name=f"ret_l{layer}",
    )(proj, proj, proj, proj, ret_norm)


def _mix_out_kernel(x_ref, gt_ref, gpost_ref, oa_ref, ob_ref, oc_ref, w_ref, o_ref, gain_scr, inv_scr):
    y = _dot(oa_ref[...], w_ref[:A_WIDTH, :])
    y = y + _dot(ob_ref[...], w_ref[A_WIDTH:A_WIDTH + B_WIDTH, :])
    o_ref[...] = y + _dot(oc_ref[...], w_ref[A_WIDTH + B_WIDTH:, :])
    gain_scr[...] = gt_ref[...] * gpost_ref[...]
    _postnorm_residual_rows(o_ref, x_ref, gain_scr, inv_scr, o_ref)


def _mix_out(x, mod, norm_g, layer, oa, ob, oc, w, tm):
    b, s, d = x.shape
    row = lambda width: pl.BlockSpec((None, tm, width), lambda b, i: (b, i, 0))
    return pl.pallas_call(
        _mix_out_kernel,
        grid=(b, s // tm),
        in_specs=[
            row(d), _mod_spec(layer, 5, d, 2), _gain_spec(layer, 3, d, 2),
            row(A_WIDTH), row(B_WIDTH), row(C_WIDTH),
            pl.BlockSpec((None,) + w.shape[1:], lambda b, i: (layer, 0, 0)),
        ],
        out_specs=row(d),
        out_shape=jax.ShapeDtypeStruct((b, s, d), F32),
        scratch_shapes=[pltpu.VMEM((1, d), F32), pltpu.VMEM((tm, 1), F32)],
        compiler_params=pltpu.CompilerParams(
            dimension_semantics=("parallel", "parallel"), vmem_limit_bytes=VMEM_LIMIT),
        name=f"mix_out_l{layer}",
    )(x, mod, norm_g, oa, ob, oc, w)


def _prep_mix_in(w):
    sizes = [A_WIDTH, KV_WIDTH, KV_WIDTH, B_WIDTH, DSA_RANK, IDX_HEADS * IDX_DIM, IDX_DIM, IDX_HEADS,
             C_WIDTH, C_WIDTH, C_WIDTH, C_WIDTH]
    starts = [0]
    for sz in sizes:
        starts.append(starts[-1] + sz)
    aq, ak, av, bq, bkv, biq, bik, biw, cq, ck, cv, cg = [w[:, a:a + sz] for a, sz in zip(starts, sizes)]
    tail = jnp.zeros((w.shape[0], LANES - IDX_DIM - IDX_HEADS), w.dtype)
    return jnp.concatenate([aq, bq, biq, cq, ck, cv, cg, ak, av, bkv, bik, biw, tail], axis=1).astype(BF16)


def _prep_dsa_up(w_uk, w_uv):
    r, h, dh = w_uk.shape
    uk = jnp.transpose(w_uk, (1, 2, 0))
    uv = jnp.transpose(w_uv, (1, 0, 2))
    odd = (jnp.arange(h) % 2 == 1)[:, None, None]
    zk = jnp.zeros_like(uk)
    zv = jnp.zeros_like(uv)
    uk_pad = jnp.where(odd, jnp.concatenate([zk, uk], axis=1), jnp.concatenate([uk, zk], axis=1))
    uv_pad = jnp.where(odd, jnp.concatenate([zv, uv], axis=2), jnp.concatenate([uv, zv], axis=2))
    return uk_pad.astype(BF16), uv_pad.astype(BF16)


FFN_TM = 1024
FFN_TF = 512
FFN_VMEM_LIMIT = 60 * 1024 * 1024
MIX_IN_TM = 1024
MIX_IN_TN = 1536
MIX_OUT_TM = 512


def kernel(x, c, ada_w, ada_b, norm_g, ffn1_w_in, ffn1_w_out, ffn2_w_in, ffn2_w_out,
           mix_w_in, mix_w_out, swa_sinks, dsa_kv_norm, dsa_w_uk, dsa_w_uv, ret_norm):
    depth = ada_w.shape[0]
    b, s, d = x.shape
    assert s % FFN_TM == 0 and s % DSA_KCHUNK == 0 and d % LANES == 0
    mod = _ada_mod(c, ada_w, ada_b).reshape(depth, b, N_MOD, 1, d)
    gains = norm_g.reshape(depth, norm_g.shape[1], 1, d)
    ffn1_in, ffn1_out = ffn1_w_in.astype(BF16), ffn1_w_out.astype(BF16)
    ffn2_in, ffn2_out = ffn2_w_in.astype(BF16), ffn2_w_out.astype(BF16)
    mix_out_w = mix_w_out.astype(BF16)
    for l in range(depth):
        x = _ffn_block(x, mod, gains, l, 0, ffn1_in, ffn1_out, 0.5, FFN_TM, FFN_TF)

        proj = _mix_in(x, mod, gains, l, _prep_mix_in(mix_w_in[l]), MIX_IN_TM, MIX_IN_TN)
        oa = _swa(proj, swa_sinks[l], l)
        uk_pad, uv_pad = _prep_dsa_up(dsa_w_uk[l], dsa_w_uv[l])
        ob = _dsa(proj, dsa_kv_norm[l].reshape(1, -1), uk_pad, uv_pad, l)
        oc = _retention(proj, ret_norm[l].reshape(1, -1), l)
        x = _mix_out(x, mod, gains, l, oa, ob, oc, mix_out_w, MIX_OUT_TM)

        x = _ffn_block(x, mod, gains, l, 2, ffn2_in, ffn2_out, 0.5, FFN_TM, FFN_TF)
    return x
```

```python
import functools
import math

import jax
import jax.numpy as jnp
from jax import lax
from jax.experimental import pallas as pl
from jax.experimental.pallas import tpu as pltpu

F32 = jnp.float32
BF16 = jnp.bfloat16

LANES = 128
HEAD_DIM = 64
SWA_HEADS = 16
SWA_KV_HEADS = 2
BLOCK = 128
DSA_HEADS = 8
DSA_RANK = 128
IDX_HEADS = 8
IDX_DIM = 64
DSA_TOPK_MAX = 256
RET_HEADS = 8
N_MOD = 9
EPS = 1e-6
NEG = -1e30

A_WIDTH = SWA_HEADS * HEAD_DIM
B_WIDTH = DSA_HEADS * HEAD_DIM
C_WIDTH = RET_HEADS * HEAD_DIM
KV_WIDTH = SWA_KV_HEADS * HEAD_DIM

COL_AQ = 0
COL_BQ = COL_AQ + A_WIDTH
COL_BIQ = COL_BQ + B_WIDTH
COL_CQ = COL_BIQ + IDX_HEADS * IDX_DIM
COL_CK = COL_CQ + C_WIDTH
COL_CV = COL_CK + C_WIDTH
COL_AKV = COL_CV + C_WIDTH
MIX_BF16_WIDTH = COL_AKV + 2 * KV_WIDTH
COL32_CG = 0
COL32_BKV = COL32_CG + C_WIDTH
COL32_BIK = COL32_BKV + DSA_RANK
MIX_F32_WIDTH = COL32_BIK + LANES

DSA_QBLOCK = 256
DSA_KCHUNK = 256
BISECT_ITERS = 18
VMEM_LIMIT = 56 * 1024 * 1024


def _dot(a, b):
    return jnp.dot(a, b, preferred_element_type=F32)


def _dot_nt(a, b):
    return lax.dot_general(a, b, (((1,), (1,)), ((), ())), preferred_element_type=F32)


def _dot_tn(a, b):
    return lax.dot_general(a, b, (((0,), (0,)), ((), ())), preferred_element_type=F32)


def _silu(x):
    return x / (1.0 + jnp.exp(-x))


def _rms(x, g):
    return x * lax.rsqrt(jnp.mean(x * x, axis=-1, keepdims=True) + EPS) * g


def _alibi_slope(h, n):
    return 2.0 ** (-8.0 * (h + 1) / n)


def _ada_kernel(c_ref, w_ref, b_ref, o_ref):
    cond = _silu(c_ref[...]).astype(BF16)
    o_ref[...] = _dot(cond, w_ref[...].astype(BF16)) + b_ref[...]


def _ada_mod(c, ada_w, ada_b):
    depth, d, n = ada_w.shape
    b = c.shape[0]
    tn = 1024
    return pl.pallas_call(
        _ada_kernel,
        grid=(depth, n // tn),
        in_specs=[
            pl.BlockSpec((b, d), lambda l, j: (0, 0)),
            pl.BlockSpec((None, d, tn), lambda l, j: (l, 0, j)),
            pl.BlockSpec((None, 1, tn), lambda l, j: (l, 0, j)),
        ],
        out_specs=pl.BlockSpec((None, b, tn), lambda l, j: (l, 0, j)),
        out_shape=jax.ShapeDtypeStruct((depth, b, n), F32),
        compiler_params=pltpu.CompilerParams(
            dimension_semantics=("arbitrary", "arbitrary"), vmem_limit_bytes=VMEM_LIMIT),
        name="ada_mod",
    )(c, ada_w, ada_b.reshape(depth, 1, n))


def _mod_spec(layer, k, d, grid_rank):
    if grid_rank == 2:
        return pl.BlockSpec((None, None, None, 1, d), lambda b, i: (layer, b, k, 0, 0))
    return pl.BlockSpec((None, None, None, 1, d), lambda b, i, j: (layer, b, k, 0, 0))


def _gain_spec(layer, k, d, grid_rank):
    if grid_rank == 2:
        return pl.BlockSpec((None, None, 1, d), lambda b, i: (layer, k, 0, 0))
    return pl.BlockSpec((None, None, 1, d), lambda b, i, j: (layer, k, 0, 0))


ROW_CHUNK = 16


def _row_chunks(n_rows, body):
    def step(i, carry):
        body(pl.ds(pl.multiple_of(i * ROW_CHUNK, ROW_CHUNK), ROW_CHUNK))
        return carry
    lax.fori_loop(0, n_rows // ROW_CHUNK, step, 0, unroll=True)


def _inv_rms_rows(x_ref, inv_ref):
    def body(r):
        x = x_ref[r, :]
        inv_ref[r, :] = lax.rsqrt(jnp.mean(x * x, axis=-1, keepdims=True) + EPS)
    _row_chunks(x_ref.shape[0], body)


def _prenorm_mod_rows(x_ref, gain_ref, shift_ref, inv_ref, h_ref):
    _inv_rms_rows(x_ref, inv_ref)

    def body(r):
        h_ref[r, :] = (x_ref[r, :] * inv_ref[r, :] * gain_ref[...] + shift_ref[...]).astype(h_ref.dtype)
    _row_chunks(x_ref.shape[0], body)


def _postnorm_residual_rows(y_ref, x_ref, gain_ref, inv_ref, o_ref):
    _inv_rms_rows(y_ref, inv_ref)

    def body(r):
        o_ref[r, :] = x_ref[r, :] + y_ref[r, :] * inv_ref[r, :] * gain_ref[...]
    _row_chunks(y_ref.shape[0], body)


def _ffn_kernel(res_w, tail, x_ref, sh_ref, sc_ref, gt_ref, gpre_ref, gpost_ref, wg_ref, wu_ref, wo_ref,
                o_ref, h_scr, gain_scr, inv_scr):
    f = pl.program_id(2)
    last = pl.num_programs(2) - 1
    tf = wo_ref.shape[1]

    def step(lo, first):
        h = h_scr[...]
        act = _silu(_dot(h, wg_ref[0, :, lo:])) * _dot(h, wu_ref[0, :, lo:])
        out = _dot(act.astype(BF16), wo_ref[0, lo:, :])
        if first:
            o_ref[...] = out
        else:
            o_ref[...] += out

    @pl.when(f == 0)
    def _():
        gain_scr[...] = gpre_ref[...] * (1.0 + sc_ref[...])
        _prenorm_mod_rows(x_ref, gain_scr, sh_ref, inv_scr, h_scr)
        step(0, True)

    @pl.when((f > 0) & (f < last))
    def _():
        step(0, False)

    @pl.when(f == last)
    def _():
        step(tf - tail, False)
        gain_scr[...] = res_w * gt_ref[...] * gpost_ref[...]
        _postnorm_residual_rows(o_ref, x_ref, gain_scr, inv_scr, o_ref)


def _ffn_block(x, mod, norm_g, layer, sub, w_in, w_out, res_w, tm, tf):
    b, s, d = x.shape
    d_ff = w_out.shape[1]
    nf = pl.cdiv(d_ff, tf)
    tail = d_ff - (nf - 1) * tf
    assert tail % LANES == 0 and d_ff >= tf and nf >= 2
    mk = 3 * sub
    start = lambda j, base: pl.multiple_of(base + jnp.minimum(j * tf, d_ff - tf), LANES)
    col = lambda base: (lambda b, i, j: (layer, 0, start(j, base)))
    one = pl.Element(1)
    return pl.pallas_call(
        functools.partial(_ffn_kernel, res_w, tail),
        grid=(b, s // tm, nf),
        in_specs=[
            pl.BlockSpec((None, tm, d), lambda b, i, j: (b, i, 0)),
            _mod_spec(layer, mk, d, 3), _mod_spec(layer, mk + 1, d, 3), _mod_spec(layer, mk + 2, d, 3),
            _gain_spec(layer, 2 * sub, d, 3), _gain_spec(layer, 2 * sub + 1, d, 3),
            pl.BlockSpec((one, pl.Element(d), pl.Element(tf)), col(0)),
            pl.BlockSpec((one, pl.Element(d), pl.Element(tf)), col(d_ff)),
            pl.BlockSpec((one, pl.Element(tf), pl.Element(d)), lambda b, i, j: (layer, start(j, 0), 0)),
        ],
        out_specs=pl.BlockSpec((None, tm, d), lambda b, i, j: (b, i, 0)),
        out_shape=jax.ShapeDtypeStruct((b, s, d), F32),
        scratch_shapes=[pltpu.VMEM((tm, d), BF16), pltpu.VMEM((1, d), F32), pltpu.VMEM((tm, 1), F32)],
        compiler_params=pltpu.CompilerParams(
            dimension_semantics=("parallel", "parallel", "arbitrary"), vmem_limit_bytes=FFN_VMEM_LIMIT),
        name=f"ffn_l{layer}_s{sub}",
    )(x, mod, mod, mod, norm_g, norm_g, w_in, w_in, w_out)


def _mix_in_kernel(n16, x_ref, sh_ref, sc_ref, gpre_ref, w_ref, o16_ref, o32_ref, h_scr, gain_scr, inv_scr):
    j = pl.program_id(2)

    @pl.when(j == 0)
    def _():
        gain_scr[...] = gpre_ref[...] * (1.0 + sc_ref[...])
        _prenorm_mod_rows(x_ref, gain_scr, sh_ref, inv_scr, h_scr)

    @pl.when(j < n16)
    def _():
        o16_ref[...] = _dot(h_scr[...], w_ref[...]).astype(o16_ref.dtype)

    @pl.when(j >= n16)
    def _():
        o32_ref[...] = _dot(h_scr[...], w_ref[...])


def _mix_in(x, mod, norm_g, layer, w, tm, tn):
    b, s, d = x.shape
    assert MIX_BF16_WIDTH % tn == 0 and MIX_F32_WIDTH % tn == 0
    n16, n32 = MIX_BF16_WIDTH // tn, MIX_F32_WIDTH // tn
    return pl.pallas_call(
        functools.partial(_mix_in_kernel, n16),
        grid=(b, s // tm, n16 + n32),
        in_specs=[
            pl.BlockSpec((None, tm, d), lambda b, i, j: (b, i, 0)),
            _mod_spec(layer, 3, d, 3), _mod_spec(layer, 4, d, 3),
            _gain_spec(layer, 2, d, 3),
            pl.BlockSpec((None, d, tn), lambda b, i, j: (layer, 0, j)),
        ],
        out_specs=[
            pl.BlockSpec((None, tm, tn), lambda b, i, j: (b, i, jnp.minimum(j, n16 - 1))),
            pl.BlockSpec((None, tm, tn), lambda b, i, j: (b, i, jnp.maximum(j - n16, 0))),
        ],
        out_shape=[jax.ShapeDtypeStruct((b, s, MIX_BF16_WIDTH), BF16),
                   jax.ShapeDtypeStruct((b, s, MIX_F32_WIDTH), F32)],
        scratch_shapes=[pltpu.VMEM((tm, d), BF16), pltpu.VMEM((1, d), F32), pltpu.VMEM((tm, 1), F32)],
        compiler_params=pltpu.CompilerParams(
            dimension_semantics=("parallel", "parallel", "arbitrary"), vmem_limit_bytes=VMEM_LIMIT),
        name=f"mix_in_l{layer}",
    )(x, mod, mod, norm_g, w)


def _lane_halves(x, first_half_holds_data):
    lane = lax.broadcasted_iota(jnp.int32, x.shape, 1)
    if first_half_holds_data:
        lo = jnp.where(lane < HEAD_DIM, x, 0.0)
        return lo, pltpu.roll(lo, HEAD_DIM, 1)
    hi = jnp.where(lane >= HEAD_DIM, x, 0.0)
    return pltpu.roll(hi, HEAD_DIM, 1), hi


def _swa_kernel(sinks_ref, q_ref, kvc_ref, kvp_ref, o_ref):
    n = pl.program_id(1)
    w = BLOCK
    kcat = jnp.concatenate([kvp_ref[:, :KV_WIDTH], kvc_ref[:, :KV_WIDTH]], axis=0).astype(F32)
    vcat = jnp.concatenate([kvp_ref[:, KV_WIDTH:], kvc_ref[:, KV_WIDTH:]], axis=0).astype(F32)
    k_halves = [tuple(t.astype(BF16) for t in _lane_halves(kcat, hk == 0)) for hk in range(SWA_KV_HEADS)]
    v_halves = [tuple(t.astype(BF16) for t in _lane_halves(vcat, hk == 0)) for hk in range(SWA_KV_HEADS)]

    bands = SWA_HEADS // SWA_KV_HEADS // 2
    rows = bands * w
    r = lax.broadcasted_iota(jnp.int32, (rows, 2 * w), 0)
    j = lax.broadcasted_iota(jnp.int32, (rows, 2 * w), 1)
    dist = (r & (w - 1)) - j + w
    valid = (dist >= 0) & (dist < w) & ((n - 1) * w + j >= 0)
    band_step = _alibi_slope(2, SWA_HEADS) / _alibi_slope(0, SWA_HEADS)
    band_scale = jnp.ones((rows, 2 * w), F32)
    for band in range(1, bands):
        band_scale = jnp.where(r >= band * w, band_step ** band, band_scale)
    dist_scaled = dist.astype(F32) * band_scale
    band_col = lax.broadcasted_iota(jnp.int32, (rows, 1), 0)
    for hk in range(SWA_KV_HEADS):
        q = jnp.concatenate([q_ref[:, (bands * hk + band) * LANES:(bands * hk + band + 1) * LANES]
                             for band in range(bands)], axis=0).astype(BF16)
        out = None
        for par in range(2):
            head0 = 2 * bands * hk + par
            sink = jnp.full((rows, 1), sinks_ref[head0], F32)
            for band in range(1, bands):
                sink = jnp.where(band_col >= band * w, sinks_ref[head0 + 2 * band], sink)
            sc = _dot_nt(q, k_halves[hk][par]) * HEAD_DIM ** -0.5
            sc = jnp.where(valid, sc - _alibi_slope(head0, SWA_HEADS) * dist_scaled, -jnp.inf)
            m = jnp.maximum(jnp.max(sc, axis=-1, keepdims=True), sink)
            e = jnp.exp(sc - m)
            p = e / (jnp.sum(e, axis=-1, keepdims=True) + jnp.exp(sink - m))
            pv = _dot(p.astype(BF16), v_halves[hk][par])
            out = pv if out is None else out + pv
        for band in range(bands):
            tile = bands * hk + band
            o_ref[:, tile * LANES:(tile + 1) * LANES] = out[band * w:(band + 1) * w].astype(o_ref.dtype)


def _swa(proj, sinks, layer):
    b, s, _ = proj.shape
    nb = s // BLOCK
    kv_blk = COL_AKV // (2 * KV_WIDTH)
    return pl.pallas_call(
        _swa_kernel,
        grid=(b, nb),
        in_specs=[
            pl.BlockSpec(memory_space=pltpu.SMEM),
            pl.BlockSpec((None, BLOCK, A_WIDTH), lambda b, n: (b, n, COL_AQ // A_WIDTH)),
            pl.BlockSpec((None, BLOCK, 2 * KV_WIDTH), lambda b, n: (b, n, kv_blk)),
            pl.BlockSpec((None, BLOCK, 2 * KV_WIDTH), lambda b, n: (b, jnp.maximum(n - 1, 0), kv_blk)),
        ],
        out_specs=pl.BlockSpec((None, BLOCK, A_WIDTH), lambda b, n: (b, n, 0)),
        out_shape=jax.ShapeDtypeStruct((b, s, A_WIDTH), BF16),
        compiler_params=pltpu.CompilerParams(
            dimension_semantics=("parallel", "arbitrary"), vmem_limit_bytes=VMEM_LIMIT),
        name=f"swa_l{layer}",
    )(sinks, proj, proj, proj)


def _dsa_kernel(topk, q_ref, iq_ref, kv_ref, ikw_ref, kvn_ref, wuk_ref, wuv_ref, o_ref,
                ckv_scr, ik_scr, iqh_scr, score_scr, bias_scr, qlat_scr, s_scr, mrun_scr, m_scr,
                lpart_scr, acc_scr):
    n = pl.program_id(1)
    qb = DSA_QBLOCK
    kc = DSA_KCHUNK
    nchunks = lax.div(n * qb + qb - 1, kc) + 1

    @pl.when(n == 0)
    def _():
        ckv_scr[...] = _rms(kv_ref[...], kvn_ref[...]).astype(BF16)
        ik_scr[...] = ikw_ref[...].astype(BF16)

    row0 = pl.multiple_of(n * qb, qb)

    iw_t = ikw_ref[pl.ds(row0, qb), :].T
    iw_scale = IDX_HEADS ** -0.5 * IDX_DIM ** -0.5
    w_rows = [iw_t[IDX_DIM + h:IDX_DIM + h + 1, :] * iw_scale for h in range(IDX_HEADS)]
    lane = lax.broadcasted_iota(jnp.int32, (qb, LANES), 1)
    for h in range(IDX_HEADS):
        iq = iq_ref[:, (h // 2) * LANES:(h // 2 + 1) * LANES].astype(F32)
        if h % 2 == 0:
            iq = jnp.where(lane < IDX_DIM, iq, 0.0)
        else:
            iq = pltpu.roll(jnp.where(lane >= IDX_DIM, iq, 0.0), IDX_DIM, 1)
        iqh_scr[h] = iq.astype(BF16)

    key_row = lax.broadcasted_iota(jnp.int32, (kc, qb), 0)
    t_pos = n * qb + lax.broadcasted_iota(jnp.int32, (kc, qb), 1)

    def score_chunk(c, carry):
        off = pl.multiple_of(c * kc, kc)
        ik = ik_scr[pl.ds(off, kc), :]
        sc = jnp.zeros((kc, qb), F32)
        for h in range(IDX_HEADS):
            sc = sc + w_rows[h] * jnp.maximum(_dot_nt(ik, iqh_scr[h]), 0.0)
        score_scr[c] = jnp.where(off + key_row <= t_pos, sc, -jnp.inf)
        return carry

    lax.fori_loop(0, nchunks, score_chunk, 0)

    sub = 8

    def fold_rows(x, op):
        out = x[:sub]
        for g in range(1, x.shape[0] // sub):
            out = op(out, x[g * sub:(g + 1) * sub])
        return out

    def fold_lanes(x, op):
        out = x[:, :LANES]
        for t in range(1, x.shape[1] // LANES):
            out = op(out, x[:, t * LANES:(t + 1) * LANES])
        return out

    def count_ge(thr, strict=False):
        def body(c, cnt):
            x = score_scr[c]
            hit = (x > thr) if strict else (x >= thr)
            return cnt + fold_rows(jnp.where(hit, 1.0, 0.0), jnp.add)
        part = lax.fori_loop(0, nchunks, body, jnp.zeros((sub, qb), F32))
        return jnp.sum(part, axis=0, keepdims=True)

    def max_at_most(bound):
        def body(c, mx):
            x = score_scr[c]
            return jnp.maximum(mx, fold_rows(jnp.where(x <= bound, x, -jnp.inf), jnp.maximum))
        part = lax.fori_loop(0, nchunks, body, jnp.full((sub, qb), -jnp.inf, F32))
        return jnp.max(part, axis=0, keepdims=True)

    def store_bias(c, t, bias_t):
        for g in range(qb // LANES):
            bias_scr[c, g * LANES:(g + 1) * LANES, t * LANES:(t + 1) * LANES] = (
                bias_t[:, g * LANES:(g + 1) * LANES].T)

    @pl.when(n * qb + qb <= topk)
    def _():
        def body(c, carry):
            x = score_scr[c]
            for t in range(kc // LANES):
                store_bias(c, t, jnp.where(x[t * LANES:(t + 1) * LANES] == -jnp.inf, NEG, 0.0))
            return carry
        lax.fori_loop(0, nchunks, body, 0)

    @pl.when(n * qb + qb > topk)
    def _():
        kf = float(topk)

        def minmax(c, carry):
            mn, mx = carry
            x = score_scr[c]
            mn = jnp.minimum(mn, fold_rows(jnp.where(x == -jnp.inf, jnp.inf, x), jnp.minimum))
            return mn, jnp.maximum(mx, fold_rows(x, jnp.maximum))

        lo, hi = lax.fori_loop(0, nchunks, minmax,
                               (jnp.full((sub, qb), jnp.inf, F32), jnp.full((sub, qb), -jnp.inf, F32)))
        lo = jnp.min(lo, axis=0, keepdims=True)
        hi = jnp.max(hi, axis=0, keepdims=True)

        def bisect(_, carry):
            lo, hi = carry
            mid = 0.5 * (lo + hi)
            ge = count_ge(mid) >= kf
            return jnp.where(ge, mid, lo), jnp.where(ge, hi, mid)

        lo, hi = lax.fori_loop(0, BISECT_ITERS, bisect, (lo, hi))

        def walk_cond(carry):
            return carry[1] > 0.0

        def count_and_next(thr):
            def body(c, carry):
                cnt, mx = carry
                x = score_scr[c]
                ge = x >= thr
                cnt = cnt + fold_rows(jnp.where(ge, 1.0, 0.0), jnp.add)
                return cnt, jnp.maximum(mx, fold_rows(jnp.where(ge, -jnp.inf, x), jnp.maximum))
            cnt, mx = lax.fori_loop(0, nchunks, body, (jnp.zeros((sub, qb), F32),
                                                       jnp.full((sub, qb), -jnp.inf, F32)))
            return jnp.sum(cnt, axis=0, keepdims=True), jnp.max(mx, axis=0, keepdims=True)

        def walk(carry):
            thr, _ = carry
            cnt, below = count_and_next(thr)
            short = cnt < kf
            return jnp.where(short, below, thr), jnp.max(jnp.where(short, 1.0, 0.0))

        thr, _ = lax.while_loop(walk_cond, walk, (max_at_most(hi), jnp.float32(1.0)))

        need = kf - count_ge(thr, strict=True)

        r = lax.broadcasted_iota(jnp.int32, (LANES, LANES), 0)
        col = lax.broadcasted_iota(jnp.int32, (LANES, LANES), 1)
        tri = jnp.where(col <= r, 1.0, 0.0).astype(BF16)

        def select(c, seen):
            x = score_scr[c]
            for t in range(kc // LANES):
                xt = x[t * LANES:(t + 1) * LANES]
                eq = xt == thr
                eqf = jnp.where(eq, 1.0, 0.0)
                rank = seen + _dot(tri, eqf.astype(BF16))
                seen = seen + jnp.sum(eqf, axis=0, keepdims=True)
                sel = (xt > thr) | (eq & (rank <= need))
                store_bias(c, t, jnp.where(sel, 0.0, NEG))
            return seen

        lax.fori_loop(0, nchunks, select, jnp.zeros((1, qb), F32))

    for h in range(DSA_HEADS):
        tile = h // 2
        rows = slice(h * qb, (h + 1) * qb)
        q = q_ref[:, tile * LANES:(tile + 1) * LANES].astype(BF16)
        qlat_scr[rows, :] = (_dot(q, wuk_ref[h]) * HEAD_DIM ** -0.5).astype(BF16)
    mrun_scr[...] = jnp.full(mrun_scr.shape, NEG, F32)
    key_lane = lax.broadcasted_iota(jnp.int32, (1, kc), 1)

    def logits(c, carry):
        off = pl.multiple_of(c * kc, kc)
        s = _dot_nt(qlat_scr[...], ckv_scr[pl.ds(off, kc), :])
        bias = bias_scr[c]
        key_pos = (off + key_lane).astype(F32)
        for h in range(DSA_HEADS):
            rows = slice(h * qb, (h + 1) * qb)
            sh = s[rows] + (bias + _alibi_slope(h, DSA_HEADS) * key_pos)
            s_scr[c, rows, :] = sh
            mrun_scr[rows, :] = jnp.maximum(mrun_scr[rows, :], fold_lanes(sh, jnp.maximum))
        return carry

    lax.fori_loop(0, nchunks, logits, 0)
    m_scr[...] = jnp.broadcast_to(jnp.max(mrun_scr[...], axis=-1, keepdims=True), m_scr.shape)
    lpart_scr[...] = jnp.zeros_like(lpart_scr)
    acc_scr[...] = jnp.zeros_like(acc_scr)

    def attend(c, carry):
        off = pl.multiple_of(c * kc, kc)
        m = m_scr[...]
        p = [jnp.exp(s_scr[c, :, t * LANES:(t + 1) * LANES] - m) for t in range(kc // LANES)]
        lpart_scr[...] += functools.reduce(jnp.add, p)
        acc_scr[...] += _dot(jnp.concatenate(p, axis=1).astype(BF16), ckv_scr[pl.ds(off, kc), :])
        return carry

    lax.fori_loop(0, nchunks, attend, 0)

    o_all = (acc_scr[...] / jnp.sum(lpart_scr[...], axis=-1, keepdims=True)).astype(BF16)
    for tile in range(DSA_HEADS // 2):
        h = 2 * tile
        out = _dot(o_all[h * qb:(h + 1) * qb], wuv_ref[h]) + _dot(o_all[(h + 1) * qb:(h + 2) * qb], wuv_ref[h + 1])
        o_ref[:, tile * LANES:(tile + 1) * LANES] = out.astype(o_ref.dtype)


def _dsa(proj16, proj32, kv_norm, wuk_pad, wuv_pad, layer):
    b, s, _ = proj16.shape
    qb = DSA_QBLOCK
    nb = s // qb
    topk = min(DSA_TOPK_MAX, s // 4)
    assert topk % qb == 0
    nck = s // DSA_KCHUNK
    rows = DSA_HEADS * qb
    return pl.pallas_call(
        functools.partial(_dsa_kernel, topk),
        grid=(b, nb),
        in_specs=[
            pl.BlockSpec((None, qb, B_WIDTH), lambda b, n: (b, n, COL_BQ // B_WIDTH)),
            pl.BlockSpec((None, qb, B_WIDTH), lambda b, n: (b, n, COL_BIQ // B_WIDTH)),
            pl.BlockSpec((None, s, DSA_RANK), lambda b, n: (b, 0, COL32_BKV // DSA_RANK)),
            pl.BlockSpec((None, s, LANES), lambda b, n: (b, 0, COL32_BIK // LANES)),
            pl.BlockSpec((1, DSA_RANK), lambda b, n: (0, 0)),
            pl.BlockSpec((DSA_HEADS, LANES, DSA_RANK), lambda b, n: (0, 0, 0)),
            pl.BlockSpec((DSA_HEADS, DSA_RANK, LANES), lambda b, n: (0, 0, 0)),
        ],
        out_specs=pl.BlockSpec((None, qb, B_WIDTH), lambda b, n: (b, n, 0)),
        out_shape=jax.ShapeDtypeStruct((b, s, B_WIDTH), BF16),
        scratch_shapes=[
            pltpu.VMEM((s, DSA_RANK), BF16),
            pltpu.VMEM((s, LANES), BF16),
            pltpu.VMEM((IDX_HEADS, qb, LANES), BF16),
            pltpu.VMEM((nck, DSA_KCHUNK, qb), F32),
            pltpu.VMEM((nck, qb, DSA_KCHUNK), F32),
            pltpu.VMEM((rows, DSA_RANK), BF16),
            pltpu.VMEM((nck, rows, DSA_KCHUNK), F32),
            pltpu.VMEM((rows, LANES), F32),
            pltpu.VMEM((rows, LANES), F32),
            pltpu.VMEM((rows, LANES), F32),
            pltpu.VMEM((rows, DSA_RANK), F32),
        ],
        compiler_params=pltpu.CompilerParams(
            dimension_semantics=("parallel", "arbitrary"), vmem_limit_bytes=VMEM_LIMIT),
        name=f"dsa_l{layer}",
    )(proj16, proj16, proj32, proj32, kv_norm, wuk_pad, wuv_pad)


def _ret_kernel(q_ref, k_ref, v_ref, g_ref, gn_ref, o_ref, state_scr):
    cs = BLOCK

    @pl.when(pl.program_id(1) == 0)
    def _():
        state_scr[...] = jnp.zeros_like(state_scr)

    row = lax.broadcasted_iota(jnp.int32, (cs, LANES), 0)
    lane = lax.broadcasted_iota(jnp.int32, (cs, LANES), 1)
    first = lane < HEAD_DIM
    rowf = row.astype(F32)
    diff = (row - lane).astype(F32)
    same_head = (row < HEAD_DIM) == first
    seg_mean = jnp.where(same_head, 1.0 / HEAD_DIM, 0.0).astype(BF16)

    def seg_mean_dot(x):
        x_hi = x.astype(BF16)
        x_lo = (x - x_hi.astype(F32)).astype(BF16)
        return _dot(x_hi, seg_mean) + _dot(x_lo, seg_mean)

    tiles = RET_HEADS // 2
    outs = []
    for bi in range(q_ref.shape[0]):
        for tile in range(tiles):
            cols = slice(tile * LANES, (tile + 1) * LANES)
            lg = [math.log(1.0 - 2.0 ** (-5.0 - (2 * tile + par))) for par in range(2)]
            lg_lane = jnp.where(first, lg[0], lg[1])
            q = q_ref[bi, :, cols].astype(BF16)
            k = k_ref[bi, :, cols].astype(F32) * HEAD_DIM ** -0.5
            v = v_ref[bi, :, cols].astype(F32)
            state = state_scr[bi, tile]
            out = _dot(q, state.astype(BF16)) * jnp.exp(lg_lane * (rowf + 1.0))
            for par in range(2):
                keep = first if par == 0 else ~first
                decay = jnp.where(diff >= 0, jnp.exp(lg[par] * jnp.maximum(diff, 0.0)), 0.0)
                inner = _dot_nt(q, jnp.where(keep, k, 0.0).astype(BF16)) * decay
                out = out + _dot(inner.astype(BF16), jnp.where(keep, v, 0.0).astype(BF16))
            k_dec = (k * jnp.exp(lg_lane * (cs - 1.0 - rowf))).astype(BF16)
            kv = _dot_tn(k_dec, v.astype(BF16))
            state_scr[bi, tile] = state * jnp.exp(lg_lane * cs) + jnp.where(same_head, kv, 0.0)
            outs.append(out)

    out = jnp.concatenate(outs, axis=0)
    mu = seg_mean_dot(out)
    cen = out - mu
    var = seg_mean_dot(cen * cen)
    yn = cen * lax.rsqrt(var + EPS)
    for bi in range(q_ref.shape[0]):
        for tile in range(tiles):
            cols = slice(tile * LANES, (tile + 1) * LANES)
            y = yn[(bi * tiles + tile) * cs:(bi * tiles + tile + 1) * cs] * gn_ref[:, cols]
            o_ref[bi, :, cols] = (y * _silu(g_ref[bi, :, cols])).astype(o_ref.dtype)


RET_BATCH = 4


def _retention(proj16, proj32, ret_norm, layer):
    b, s, _ = proj16.shape
    nc = s // BLOCK
    rb = math.gcd(b, RET_BATCH)
    spec = lambda col: pl.BlockSpec((rb, BLOCK, C_WIDTH), lambda b, c: (b, c, col // C_WIDTH))
    return pl.pallas_call(
        _ret_kernel,
        grid=(b // rb, nc),
        in_specs=[spec(COL_CQ), spec(COL_CK), spec(COL_CV), spec(COL32_CG),
                  pl.BlockSpec((1, C_WIDTH), lambda b, c: (0, 0))],
        out_specs=pl.BlockSpec((rb, BLOCK, C_WIDTH), lambda b, c: (b, c, 0)),
        out_shape=jax.ShapeDtypeStruct((b, s, C_WIDTH), BF16),
        scratch_shapes=[pltpu.VMEM((rb, RET_HEADS // 2, LANES, LANES), F32)],
        compiler_params=pltpu.CompilerParams(
            dimension_semantics=("parallel", "arbitrary"), vmem_limit_bytes=VMEM_LIMIT),
        name=f"ret_l{layer}",
    )(proj16, proj16, proj16, proj32, ret_norm)


def _mix_out_kernel(x_ref, gt_ref, gpost_ref, oa_ref, ob_ref, oc_ref, w_ref, o_ref, gain_scr, inv_scr):
    y = _dot(oa_ref[...], w_ref[:A_WIDTH, :])
    y = y + _dot(ob_ref[...], w_ref[A_WIDTH:A_WIDTH + B_WIDTH, :])
    o_ref[...] = y + _dot(oc_ref[...], w_ref[A_WIDTH + B_WIDTH:, :])
    gain_scr[...] = gt_ref[...] * gpost_ref[...]
    _postnorm_residual_rows(o_ref, x_ref, gain_scr, inv_scr, o_ref)


def _mix_out(x, mod, norm_g, layer, oa, ob, oc, w, tm):
    b, s, d = x.shape
    row = lambda width: pl.BlockSpec((None, tm, width), lambda b, i: (b, i, 0))
    return pl.pallas_call(
        _mix_out_kernel,
        grid=(b, s // tm),
        in_specs=[
            row(d), _mod_spec(layer, 5, d, 2), _gain_spec(layer, 3, d, 2),
            row(A_WIDTH), row(B_WIDTH), row(C_WIDTH),
            pl.BlockSpec((None,) + w.shape[1:], lambda b, i: (layer, 0, 0)),
        ],
        out_specs=row(d),
        out_shape=jax.ShapeDtypeStruct((b, s, d), F32),
        scratch_shapes=[pltpu.VMEM((1, d), F32), pltpu.VMEM((tm, 1), F32)],
        compiler_params=pltpu.CompilerParams(
            dimension_semantics=("parallel", "parallel"), vmem_limit_bytes=VMEM_LIMIT),
        name=f"mix_out_l{layer}",
    )(x, mod, norm_g, oa, ob, oc, w)


def _prep_mix_in(w):
    w = w.astype(BF16)
    sizes = [A_WIDTH, KV_WIDTH, KV_WIDTH, B_WIDTH, DSA_RANK, IDX_HEADS * IDX_DIM, IDX_DIM, IDX_HEADS,
             C_WIDTH, C_WIDTH, C_WIDTH, C_WIDTH]
    starts = [0]
    for sz in sizes:
        starts.append(starts[-1] + sz)
    aq, ak, av, bq, bkv, biq, bik, biw, cq, ck, cv, cg = [w[..., a:a + sz] for a, sz in zip(starts, sizes)]
    tail = jnp.zeros(w.shape[:-1] + (LANES - IDX_DIM - IDX_HEADS,), w.dtype)
    return jnp.concatenate([aq, bq, biq, cq, ck, cv, ak, av, cg, bkv, bik, biw, tail], axis=-1)


def _prep_dsa_up(w_uk, w_uv):
    r, h, dh = w_uk.shape
    uk = jnp.transpose(w_uk, (1, 2, 0))
    uv = jnp.transpose(w_uv, (1, 0, 2))
    odd = (jnp.arange(h) % 2 == 1)[:, None, None]
    zk = jnp.zeros_like(uk)
    zv = jnp.zeros_like(uv)
    uk_pad = jnp.where(odd, jnp.concatenate([zk, uk], axis=1), jnp.concatenate([uk, zk], axis=1))
    uv_pad = jnp.where(odd, jnp.concatenate([zv, uv], axis=2), jnp.concatenate([uv, zv], axis=2))
    return uk_pad.astype(BF16), uv_pad.astype(BF16)


FFN_TM = 1024
FFN_TF = 512
FFN_VMEM_LIMIT = 60 * 1024 * 1024
MIX_IN_TM = 1024
MIX_IN_TN = 768
MIX_OUT_TM = 512


def kernel(x, c, ada_w, ada_b, norm_g, ffn1_w_in, ffn1_w_out, ffn2_w_in, ffn2_w_out,
           mix_w_in, mix_w_out, swa_sinks, dsa_kv_norm, dsa_w_uk, dsa_w_uv, ret_norm):
    depth = ada_w.shape[0]
    b, s, d = x.shape
    assert s % FFN_TM == 0 and s % DSA_KCHUNK == 0 and d % LANES == 0
    mod = _ada_mod(c, ada_w, ada_b).reshape(depth, b, N_MOD, 1, d)
    gains = norm_g.reshape(depth, norm_g.shape[1], 1, d)
    ffn1_in, ffn1_out = ffn1_w_in.astype(BF16), ffn1_w_out.astype(BF16)
    ffn2_in, ffn2_out = ffn2_w_in.astype(BF16), ffn2_w_out.astype(BF16)
    mix_out_w = mix_w_out.astype(BF16)
    mix_in_w = _prep_mix_in(mix_w_in)
    for l in range(depth):
        x = _ffn_block(x, mod, gains, l, 0, ffn1_in, ffn1_out, 0.5, FFN_TM, FFN_TF)

        proj16, proj32 = _mix_in(x, mod, gains, l, mix_in_w, MIX_IN_TM, MIX_IN_TN)
        oa = _swa(proj16, swa_sinks[l], l)
        uk_pad, uv_pad = _prep_dsa_up(dsa_w_uk[l], dsa_w_uv[l])
        ob = _dsa(proj16, proj32, dsa_kv_norm[l].reshape(1, -1), uk_pad, uv_pad, l)
        oc = _retention(proj16, proj32, ret_norm[l].reshape(1, -1), l)
        x = _mix_out(x, mod, gains, l, oa, ob, oc, mix_out_w, MIX_OUT_TM)

        x = _ffn_block(x, mod, gains, l, 2, ffn2_in, ffn2_out, 0.5, FFN_TM, FFN_TF)
    return x
```

```python
import functools
import math

import jax
import jax.numpy as jnp
from jax import lax
from jax.experimental import pallas as pl
from jax.experimental.pallas import tpu as pltpu

F32 = jnp.float32
BF16 = jnp.bfloat16

LANES = 128
HEAD_DIM = 64
SWA_HEADS = 16
SWA_KV_HEADS = 2
BLOCK = 128
DSA_HEADS = 8
DSA_RANK = 128
IDX_HEADS = 8
IDX_DIM = 64
DSA_TOPK_MAX = 256
RET_HEADS = 8
N_MOD = 9
EPS = 1e-6
NEG = -1e30

A_WIDTH = SWA_HEADS * HEAD_DIM
B_WIDTH = DSA_HEADS * HEAD_DIM
C_WIDTH = RET_HEADS * HEAD_DIM
KV_WIDTH = SWA_KV_HEADS * HEAD_DIM

COL_AQ = 0
COL_BQ = COL_AQ + A_WIDTH
COL_BIQ = COL_BQ + B_WIDTH
COL_CQ = COL_BIQ + IDX_HEADS * IDX_DIM
COL_CV = COL_CQ + C_WIDTH
MIX_BF16_WIDTH = COL_CV + C_WIDTH
COL32_CG = 0
COL32_CK = COL32_CG + C_WIDTH
COL32_AKV = COL32_CK + C_WIDTH
COL32_BKV = COL32_AKV + 2 * KV_WIDTH
COL32_BIK = COL32_BKV + DSA_RANK
MIX_F32_WIDTH = COL32_BIK + LANES

DSA_QBLOCK = 256
DSA_KCHUNK = 256
BISECT_ITERS = 18
VMEM_LIMIT = 56 * 1024 * 1024


def _dot(a, b):
    return jnp.dot(a, b, preferred_element_type=F32)


def _dot_nt(a, b):
    return lax.dot_general(a, b, (((1,), (1,)), ((), ())), preferred_element_type=F32)


def _dot_tn(a, b):
    return lax.dot_general(a, b, (((0,), (0,)), ((), ())), preferred_element_type=F32)


def _silu(x):
    return x / (1.0 + jnp.exp(-x))


def _rms(x, g):
    return x * lax.rsqrt(jnp.mean(x * x, axis=-1, keepdims=True) + EPS) * g


def _alibi_slope(h, n):
    return 2.0 ** (-8.0 * (h + 1) / n)


def _ada_kernel(c_ref, w_ref, b_ref, o_ref):
    cond = _silu(c_ref[...]).astype(BF16)
    o_ref[...] = _dot(cond, w_ref[...].astype(BF16)) + b_ref[...]


def _ada_mod(c, ada_w, ada_b):
    depth, d, n = ada_w.shape
    b = c.shape[0]
    tn = 1024
    return pl.pallas_call(
        _ada_kernel,
        grid=(depth, n // tn),
        in_specs=[
            pl.BlockSpec((b, d), lambda l, j: (0, 0)),
            pl.BlockSpec((None, d, tn), lambda l, j: (l, 0, j)),
            pl.BlockSpec((None, 1, tn), lambda l, j: (l, 0, j)),
        ],
        out_specs=pl.BlockSpec((None, b, tn), lambda l, j: (l, 0, j)),
        out_shape=jax.ShapeDtypeStruct((depth, b, n), F32),
        compiler_params=pltpu.CompilerParams(
            dimension_semantics=("arbitrary", "arbitrary"), vmem_limit_bytes=VMEM_LIMIT),
        name="ada_mod",
    )(c, ada_w, ada_b.reshape(depth, 1, n))


def _mod_spec(layer, k, d, grid_rank):
    if grid_rank == 2:
        return pl.BlockSpec((None, None, None, 1, d), lambda b, i: (layer, b, k, 0, 0))
    return pl.BlockSpec((None, None, None, 1, d), lambda b, i, j: (layer, b, k, 0, 0))


def _gain_spec(layer, k, d, grid_rank):
    if grid_rank == 2:
        return pl.BlockSpec((None, None, 1, d), lambda b, i: (layer, k, 0, 0))
    return pl.BlockSpec((None, None, 1, d), lambda b, i, j: (layer, k, 0, 0))


ROW_CHUNK = 16


def _row_chunks(n_rows, body):
    def step(i, carry):
        body(pl.ds(pl.multiple_of(i * ROW_CHUNK, ROW_CHUNK), ROW_CHUNK))
        return carry
    lax.fori_loop(0, n_rows // ROW_CHUNK, step, 0, unroll=True)


def _inv_rms_rows(x_ref, inv_ref):
    def body(r):
        x = x_ref[r, :]
        inv_ref[r, :] = lax.rsqrt(jnp.mean(x * x, axis=-1, keepdims=True) + EPS)
    _row_chunks(x_ref.shape[0], body)


def _prenorm_mod_rows(x_ref, gain_ref, shift_ref, inv_ref, h_ref):
    _inv_rms_rows(x_ref, inv_ref)

    def body(r):
        h_ref[r, :] = (x_ref[r, :] * inv_ref[r, :] * gain_ref[...] + shift_ref[...]).astype(h_ref.dtype)
    _row_chunks(x_ref.shape[0], body)


def _postnorm_residual_rows(y_ref, x_ref, gain_ref, inv_ref, o_ref):
    _inv_rms_rows(y_ref, inv_ref)

    def body(r):
        o_ref[r, :] = x_ref[r, :] + y_ref[r, :] * inv_ref[r, :] * gain_ref[...]
    _row_chunks(y_ref.shape[0], body)


def _ffn_kernel(res_w, tail, x_ref, sh_ref, sc_ref, gt_ref, gpre_ref, gpost_ref, wg_ref, wu_ref, wo_ref,
                o_ref, h_scr, gain_scr, inv_scr):
    f = pl.program_id(2)
    last = pl.num_programs(2) - 1
    tf = wo_ref.shape[1]

    def step(lo, first):
        h = h_scr[...]
        act = _silu(_dot(h, wg_ref[0, :, lo:])) * _dot(h, wu_ref[0, :, lo:])
        out = _dot(act.astype(BF16), wo_ref[0, lo:, :])
        if first:
            o_ref[...] = out
        else:
            o_ref[...] += out

    @pl.when(f == 0)
    def _():
        gain_scr[...] = gpre_ref[...] * (1.0 + sc_ref[...])
        _prenorm_mod_rows(x_ref, gain_scr, sh_ref, inv_scr, h_scr)
        step(0, True)

    @pl.when((f > 0) & (f < last))
    def _():
        step(0, False)

    @pl.when(f == last)
    def _():
        step(tf - tail, False)
        gain_scr[...] = res_w * gt_ref[...] * gpost_ref[...]
        _postnorm_residual_rows(o_ref, x_ref, gain_scr, inv_scr, o_ref)


def _ffn_block(x, mod, norm_g, layer, sub, w_in, w_out, res_w, tm, tf):
    b, s, d = x.shape
    d_ff = w_out.shape[1]
    nf = pl.cdiv(d_ff, tf)
    tail = d_ff - (nf - 1) * tf
    assert tail % LANES == 0 and d_ff >= tf and nf >= 2
    mk = 3 * sub
    start = lambda j, base: pl.multiple_of(base + jnp.minimum(j * tf, d_ff - tf), LANES)
    col = lambda base: (lambda b, i, j: (layer, 0, start(j, base)))
    one = pl.Element(1)
    return pl.pallas_call(
        functools.partial(_ffn_kernel, res_w, tail),
        grid=(b, s // tm, nf),
        in_specs=[
            pl.BlockSpec((None, tm, d), lambda b, i, j: (b, i, 0)),
            _mod_spec(layer, mk, d, 3), _mod_spec(layer, mk + 1, d, 3), _mod_spec(layer, mk + 2, d, 3),
            _gain_spec(layer, 2 * sub, d, 3), _gain_spec(layer, 2 * sub + 1, d, 3),
            pl.BlockSpec((one, pl.Element(d), pl.Element(tf)), col(0)),
            pl.BlockSpec((one, pl.Element(d), pl.Element(tf)), col(d_ff)),
            pl.BlockSpec((one, pl.Element(tf), pl.Element(d)), lambda b, i, j: (layer, start(j, 0), 0)),
        ],
        out_specs=pl.BlockSpec((None, tm, d), lambda b, i, j: (b, i, 0)),
        out_shape=jax.ShapeDtypeStruct((b, s, d), F32),
        scratch_shapes=[pltpu.VMEM((tm, d), BF16), pltpu.VMEM((1, d), F32), pltpu.VMEM((tm, 1), F32)],
        compiler_params=pltpu.CompilerParams(
            dimension_semantics=("parallel", "parallel", "arbitrary"), vmem_limit_bytes=FFN_VMEM_LIMIT),
        name=f"ffn_l{layer}_s{sub}",
    )(x, mod, mod, mod, norm_g, norm_g, w_in, w_in, w_out)


def _mix_in_kernel(n16, x_ref, sh_ref, sc_ref, gpre_ref, w_ref, o16_ref, o32_ref, h_scr, gain_scr, inv_scr):
    j = pl.program_id(2)

    @pl.when(j == 0)
    def _():
        gain_scr[...] = gpre_ref[...] * (1.0 + sc_ref[...])
        _prenorm_mod_rows(x_ref, gain_scr, sh_ref, inv_scr, h_scr)

    @pl.when(j < n16)
    def _():
        o16_ref[...] = _dot(h_scr[...], w_ref[...]).astype(o16_ref.dtype)

    @pl.when(j >= n16)
    def _():
        o32_ref[...] = _dot(h_scr[...], w_ref[...])


def _mix_in(x, mod, norm_g, layer, w, tm, tn):
    b, s, d = x.shape
    assert MIX_BF16_WIDTH % tn == 0 and MIX_F32_WIDTH % tn == 0
    n16, n32 = MIX_BF16_WIDTH // tn, MIX_F32_WIDTH // tn
    return pl.pallas_call(
        functools.partial(_mix_in_kernel, n16),
        grid=(b, s // tm, n16 + n32),
        in_specs=[
            pl.BlockSpec((None, tm, d), lambda b, i, j: (b, i, 0)),
            _mod_spec(layer, 3, d, 3), _mod_spec(layer, 4, d, 3),
            _gain_spec(layer, 2, d, 3),
            pl.BlockSpec((None, d, tn), lambda b, i, j: (layer, 0, j)),
        ],
        out_specs=[
            pl.BlockSpec((None, tm, tn), lambda b, i, j: (b, i, jnp.minimum(j, n16 - 1))),
            pl.BlockSpec((None, tm, tn), lambda b, i, j: (b, i, jnp.maximum(j - n16, 0))),
        ],
        out_shape=[jax.ShapeDtypeStruct((b, s, MIX_BF16_WIDTH), BF16),
                   jax.ShapeDtypeStruct((b, s, MIX_F32_WIDTH), F32)],
        scratch_shapes=[pltpu.VMEM((tm, d), BF16), pltpu.VMEM((1, d), F32), pltpu.VMEM((tm, 1), F32)],
        compiler_params=pltpu.CompilerParams(
            dimension_semantics=("parallel", "parallel", "arbitrary"), vmem_limit_bytes=VMEM_LIMIT),
        name=f"mix_in_l{layer}",
    )(x, mod, mod, norm_g, w)


def _lane_halves(x, first_half_holds_data):
    lane = lax.broadcasted_iota(jnp.int32, x.shape, 1)
    if first_half_holds_data:
        lo = jnp.where(lane < HEAD_DIM, x, 0.0)
        return lo, pltpu.roll(lo, HEAD_DIM, 1)
    hi = jnp.where(lane >= HEAD_DIM, x, 0.0)
    return pltpu.roll(hi, HEAD_DIM, 1), hi


def _swa_kernel(sinks_ref, q_ref, kvc_ref, kvp_ref, o_ref):
    n = pl.program_id(1)
    w = BLOCK
    kcat = jnp.concatenate([kvp_ref[:, :KV_WIDTH], kvc_ref[:, :KV_WIDTH]], axis=0)
    vcat = jnp.concatenate([kvp_ref[:, KV_WIDTH:], kvc_ref[:, KV_WIDTH:]], axis=0)
    k_halves = [tuple(t.astype(BF16) for t in _lane_halves(kcat, hk == 0)) for hk in range(SWA_KV_HEADS)]
    v_halves = [tuple(t.astype(BF16) for t in _lane_halves(vcat, hk == 0)) for hk in range(SWA_KV_HEADS)]

    bands = SWA_HEADS // SWA_KV_HEADS // 2
    rows = bands * w
    r = lax.broadcasted_iota(jnp.int32, (rows, 2 * w), 0)
    j = lax.broadcasted_iota(jnp.int32, (rows, 2 * w), 1)
    dist = (r & (w - 1)) - j + w
    valid = (dist >= 0) & (dist < w) & ((n - 1) * w + j >= 0)
    band_step = _alibi_slope(2, SWA_HEADS) / _alibi_slope(0, SWA_HEADS)
    band_scale = jnp.ones((rows, 2 * w), F32)
    for band in range(1, bands):
        band_scale = jnp.where(r >= band * w, band_step ** band, band_scale)
    dist_scaled = dist.astype(F32) * band_scale
    band_col = lax.broadcasted_iota(jnp.int32, (rows, 1), 0)
    for hk in range(SWA_KV_HEADS):
        q = jnp.concatenate([q_ref[:, (bands * hk + band) * LANES:(bands * hk + band + 1) * LANES]
                             for band in range(bands)], axis=0).astype(BF16)
        out = None
        for par in range(2):
            head0 = 2 * bands * hk + par
            sink = jnp.full((rows, 1), sinks_ref[head0], F32)
            for band in range(1, bands):
                sink = jnp.where(band_col >= band * w, sinks_ref[head0 + 2 * band], sink)
            sc = _dot_nt(q, k_halves[hk][par]) * HEAD_DIM ** -0.5
            sc = jnp.where(valid, sc - _alibi_slope(head0, SWA_HEADS) * dist_scaled, -jnp.inf)
            m = jnp.maximum(jnp.max(sc, axis=-1, keepdims=True), sink)
            e = jnp.exp(sc - m)
            p = e / (jnp.sum(e, axis=-1, keepdims=True) + jnp.exp(sink - m))
            pv = _dot(p.astype(BF16), v_halves[hk][par])
            out = pv if out is None else out + pv
        for band in range(bands):
            tile = bands * hk + band
            o_ref[:, tile * LANES:(tile + 1) * LANES] = out[band * w:(band + 1) * w].astype(o_ref.dtype)


def _swa(proj16, proj32, sinks, layer):
    b, s, _ = proj16.shape
    nb = s // BLOCK
    kv_blk = COL32_AKV // (2 * KV_WIDTH)
    return pl.pallas_call(
        _swa_kernel,
        grid=(b, nb),
        in_specs=[
            pl.BlockSpec(memory_space=pltpu.SMEM),
            pl.BlockSpec((None, BLOCK, A_WIDTH), lambda b, n: (b, n, COL_AQ // A_WIDTH)),
            pl.BlockSpec((None, BLOCK, 2 * KV_WIDTH), lambda b, n: (b, n, kv_blk)),
            pl.BlockSpec((None, BLOCK, 2 * KV_WIDTH), lambda b, n: (b, jnp.maximum(n - 1, 0), kv_blk)),
        ],
        out_specs=pl.BlockSpec((None, BLOCK, A_WIDTH), lambda b, n: (b, n, 0)),
        out_shape=jax.ShapeDtypeStruct((b, s, A_WIDTH), BF16),
        compiler_params=pltpu.CompilerParams(
            dimension_semantics=("parallel", "arbitrary"), vmem_limit_bytes=VMEM_LIMIT),
        name=f"swa_l{layer}",
    )(sinks, proj16, proj32, proj32)


def _dsa_kernel(topk, q_ref, iq_ref, kv_ref, ikw_ref, kvn_ref, wuk_ref, wuv_ref, o_ref,
                ckv_scr, ik_scr, iqh_scr, score_scr, bias_scr, qlat_scr, s_scr, mrun_scr, m_scr,
                lpart_scr, acc_scr):
    n = pl.program_id(1)
    qb = DSA_QBLOCK
    kc = DSA_KCHUNK
    nchunks = lax.div(n * qb + qb - 1, kc) + 1

    @pl.when(n == 0)
    def _():
        ckv_scr[...] = _rms(kv_ref[...], kvn_ref[...]).astype(BF16)
        ik_scr[...] = ikw_ref[...].astype(BF16)

    row0 = pl.multiple_of(n * qb, qb)

    iw_t = ikw_ref[pl.ds(row0, qb), :].T
    iw_scale = IDX_HEADS ** -0.5 * IDX_DIM ** -0.5
    w_rows = [iw_t[IDX_DIM + h:IDX_DIM + h + 1, :] * iw_scale for h in range(IDX_HEADS)]
    lane = lax.broadcasted_iota(jnp.int32, (qb, LANES), 1)
    for h in range(IDX_HEADS):
        iq = iq_ref[:, (h // 2) * LANES:(h // 2 + 1) * LANES].astype(F32)
        if h % 2 == 0:
            iq = jnp.where(lane < IDX_DIM, iq, 0.0)
        else:
            iq = pltpu.roll(jnp.where(lane >= IDX_DIM, iq, 0.0), IDX_DIM, 1)
        iqh_scr[h] = iq.astype(BF16)

    key_row = lax.broadcasted_iota(jnp.int32, (kc, qb), 0)
    t_pos = n * qb + lax.broadcasted_iota(jnp.int32, (kc, qb), 1)

    def score_chunk(c, carry):
        off = pl.multiple_of(c * kc, kc)
        ik = ik_scr[pl.ds(off, kc), :]
        sc = jnp.zeros((kc, qb), F32)
        for h in range(IDX_HEADS):
            sc = sc + w_rows[h] * jnp.maximum(_dot_nt(ik, iqh_scr[h]), 0.0)
        score_scr[c] = jnp.where(off + key_row <= t_pos, sc, -jnp.inf)
        return carry

    lax.fori_loop(0, nchunks, score_chunk, 0)

    sub = 8

    def fold_rows(x, op):
        out = x[:sub]
        for g in range(1, x.shape[0] // sub):
            out = op(out, x[g * sub:(g + 1) * sub])
        return out

    def fold_lanes(x, op):
        out = x[:, :LANES]
        for t in range(1, x.shape[1] // LANES):
            out = op(out, x[:, t * LANES:(t + 1) * LANES])
        return out

    def count_ge(thr, strict=False):
        def body(c, cnt):
            x = score_scr[c]
            hit = (x > thr) if strict else (x >= thr)
            return cnt + fold_rows(jnp.where(hit, 1.0, 0.0), jnp.add)
        part = lax.fori_loop(0, nchunks, body, jnp.zeros((sub, qb), F32))
        return jnp.sum(part, axis=0, keepdims=True)

    def max_at_most(bound):
        def body(c, mx):
            x = score_scr[c]
            return jnp.maximum(mx, fold_rows(jnp.where(x <= bound, x, -jnp.inf), jnp.maximum))
        part = lax.fori_loop(0, nchunks, body, jnp.full((sub, qb), -jnp.inf, F32))
        return jnp.max(part, axis=0, keepdims=True)

    def store_bias(c, t, bias_t):
        for g in range(qb // LANES):
            bias_scr[c, g * LANES:(g + 1) * LANES, t * LANES:(t + 1) * LANES] = (
                bias_t[:, g * LANES:(g + 1) * LANES].T)

    @pl.when(n * qb + qb <= topk)
    def _():
        def body(c, carry):
            x = score_scr[c]
            for t in range(kc // LANES):
                store_bias(c, t, jnp.where(x[t * LANES:(t + 1) * LANES] == -jnp.inf, NEG, 0.0))
            return carry
        lax.fori_loop(0, nchunks, body, 0)

    @pl.when(n * qb + qb > topk)
    def _():
        kf = float(topk)

        def minmax(c, carry):
            mn, mx = carry
            x = score_scr[c]
            mn = jnp.minimum(mn, fold_rows(jnp.where(x == -jnp.inf, jnp.inf, x), jnp.minimum))
            return mn, jnp.maximum(mx, fold_rows(x, jnp.maximum))

        lo, hi = lax.fori_loop(0, nchunks, minmax,
                               (jnp.full((sub, qb), jnp.inf, F32), jnp.full((sub, qb), -jnp.inf, F32)))
        lo = jnp.min(lo, axis=0, keepdims=True)
        hi = jnp.max(hi, axis=0, keepdims=True)

        def bisect(_, carry):
            lo, hi = carry
            mid = 0.5 * (lo + hi)
            ge = count_ge(mid) >= kf
            return jnp.where(ge, mid, lo), jnp.where(ge, hi, mid)

        lo, hi = lax.fori_loop(0, BISECT_ITERS, bisect, (lo, hi))

        def walk_cond(carry):
            return carry[1] > 0.0

        def count_and_next(thr):
            def body(c, carry):
                cnt, mx = carry
                x = score_scr[c]
                ge = x >= thr
                cnt = cnt + fold_rows(jnp.where(ge, 1.0, 0.0), jnp.add)
                return cnt, jnp.maximum(mx, fold_rows(jnp.where(ge, -jnp.inf, x), jnp.maximum))
            cnt, mx = lax.fori_loop(0, nchunks, body, (jnp.zeros((sub, qb), F32),
                                                       jnp.full((sub, qb), -jnp.inf, F32)))
            return jnp.sum(cnt, axis=0, keepdims=True), jnp.max(mx, axis=0, keepdims=True)

        def walk(carry):
            thr, _ = carry
            cnt, below = count_and_next(thr)
            short = cnt < kf
            return jnp.where(short, below, thr), jnp.max(jnp.where(short, 1.0, 0.0))

        thr, _ = lax.while_loop(walk_cond, walk, (max_at_most(hi), jnp.float32(1.0)))

        need = kf - count_ge(thr, strict=True)

        r = lax.broadcasted_iota(jnp.int32, (LANES, LANES), 0)
        col = lax.broadcasted_iota(jnp.int32, (LANES, LANES), 1)
        tri = jnp.where(col <= r, 1.0, 0.0).astype(BF16)

        def select(c, seen):
            x = score_scr[c]
            for t in range(kc // LANES):
                xt = x[t * LANES:(t + 1) * LANES]
                eq = xt == thr
                eqf = jnp.where(eq, 1.0, 0.0)
                rank = seen + _dot(tri, eqf.astype(BF16))
                seen = seen + jnp.sum(eqf, axis=0, keepdims=True)
                sel = (xt > thr) | (eq & (rank <= need))
                store_bias(c, t, jnp.where(sel, 0.0, NEG))
            return seen

        lax.fori_loop(0, nchunks, select, jnp.zeros((1, qb), F32))

    for h in range(DSA_HEADS):
        tile = h // 2
        rows = slice(h * qb, (h + 1) * qb)
        q = q_ref[:, tile * LANES:(tile + 1) * LANES].astype(BF16)
        qlat_scr[rows, :] = (_dot(q, wuk_ref[h]) * HEAD_DIM ** -0.5).astype(BF16)
    mrun_scr[...] = jnp.full(mrun_scr.shape, NEG, F32)
    key_lane = lax.broadcasted_iota(jnp.int32, (1, kc), 1)

    def logits(c, carry):
        off = pl.multiple_of(c * kc, kc)
        s = _dot_nt(qlat_scr[...], ckv_scr[pl.ds(off, kc), :])
        bias = bias_scr[c]
        key_pos = (off + key_lane).astype(F32)
        for h in range(DSA_HEADS):
            rows = slice(h * qb, (h + 1) * qb)
            sh = s[rows] + (bias + _alibi_slope(h, DSA_HEADS) * key_pos)
            s_scr[c, rows, :] = sh
            mrun_scr[rows, :] = jnp.maximum(mrun_scr[rows, :], fold_lanes(sh, jnp.maximum))
        return carry

    lax.fori_loop(0, nchunks, logits, 0)
    m_scr[...] = jnp.broadcast_to(jnp.max(mrun_scr[...], axis=-1, keepdims=True), m_scr.shape)
    lpart_scr[...] = jnp.zeros_like(lpart_scr)
    acc_scr[...] = jnp.zeros_like(acc_scr)

    def attend(c, carry):
        off = pl.multiple_of(c * kc, kc)
        m = m_scr[...]
        p = [jnp.exp(s_scr[c, :, t * LANES:(t + 1) * LANES] - m) for t in range(kc // LANES)]
        lpart_scr[...] += functools.reduce(jnp.add, p)
        acc_scr[...] += _dot(jnp.concatenate(p, axis=1).astype(BF16), ckv_scr[pl.ds(off, kc), :])
        return carry

    lax.fori_loop(0, nchunks, attend, 0)

    o_all = (acc_scr[...] / jnp.sum(lpart_scr[...], axis=-1, keepdims=True)).astype(BF16)
    for tile in range(DSA_HEADS // 2):
        h = 2 * tile
        out = _dot(o_all[h * qb:(h + 1) * qb], wuv_ref[h]) + _dot(o_all[(h + 1) * qb:(h + 2) * qb], wuv_ref[h + 1])
        o_ref[:, tile * LANES:(tile + 1) * LANES] = out.astype(o_ref.dtype)


def _dsa(proj16, proj32, kv_norm, wuk_pad, wuv_pad, layer):
    b, s, _ = proj16.shape
    qb = DSA_QBLOCK
    nb = s // qb
    topk = min(DSA_TOPK_MAX, s // 4)
    assert topk % qb == 0
    nck = s // DSA_KCHUNK
    rows = DSA_HEADS * qb
    return pl.pallas_call(
        functools.partial(_dsa_kernel, topk),
        grid=(b, nb),
        in_specs=[
            pl.BlockSpec((None, qb, B_WIDTH), lambda b, n: (b, n, COL_BQ // B_WIDTH)),
            pl.BlockSpec((None, qb, B_WIDTH), lambda b, n: (b, n, COL_BIQ // B_WIDTH)),
            pl.BlockSpec((None, s, DSA_RANK), lambda b, n: (b, 0, COL32_BKV // DSA_RANK)),
            pl.BlockSpec((None, s, LANES), lambda b, n: (b, 0, COL32_BIK // LANES)),
            pl.BlockSpec((1, DSA_RANK), lambda b, n: (0, 0)),
            pl.BlockSpec((DSA_HEADS, LANES, DSA_RANK), lambda b, n: (0, 0, 0)),
            pl.BlockSpec((DSA_HEADS, DSA_RANK, LANES), lambda b, n: (0, 0, 0)),
        ],
        out_specs=pl.BlockSpec((None, qb, B_WIDTH), lambda b, n: (b, n, 0)),
        out_shape=jax.ShapeDtypeStruct((b, s, B_WIDTH), BF16),
        scratch_shapes=[
            pltpu.VMEM((s, DSA_RANK), BF16),
            pltpu.VMEM((s, LANES), BF16),
            pltpu.VMEM((IDX_HEADS, qb, LANES), BF16),
            pltpu.VMEM((nck, DSA_KCHUNK, qb), F32),
            pltpu.VMEM((nck, qb, DSA_KCHUNK), F32),
            pltpu.VMEM((rows, DSA_RANK), BF16),
            pltpu.VMEM((nck, rows, DSA_KCHUNK), F32),
            pltpu.VMEM((rows, LANES), F32),
            pltpu.VMEM((rows, LANES), F32),
            pltpu.VMEM((rows, LANES), F32),
            pltpu.VMEM((rows, DSA_RANK), F32),
        ],
        compiler_params=pltpu.CompilerParams(
            dimension_semantics=("parallel", "arbitrary"), vmem_limit_bytes=VMEM_LIMIT),
        name=f"dsa_l{layer}",
    )(proj16, proj16, proj32, proj32, kv_norm, wuk_pad, wuv_pad)


def _ret_kernel(q_ref, k_ref, v_ref, g_ref, gn_ref, o_ref, state_scr):
    cs = BLOCK

    @pl.when(pl.program_id(1) == 0)
    def _():
        state_scr[...] = jnp.zeros_like(state_scr)

    row = lax.broadcasted_iota(jnp.int32, (cs, LANES), 0)
    lane = lax.broadcasted_iota(jnp.int32, (cs, LANES), 1)
    first = lane < HEAD_DIM
    rowf = row.astype(F32)
    diff = (row - lane).astype(F32)
    same_head = (row < HEAD_DIM) == first
    seg_mean = jnp.where(same_head, 1.0 / HEAD_DIM, 0.0).astype(BF16)

    def seg_mean_dot(x):
        x_hi = x.astype(BF16)
        x_lo = (x - x_hi.astype(F32)).astype(BF16)
        return _dot(x_hi, seg_mean) + _dot(x_lo, seg_mean)

    tiles = RET_HEADS // 2
    outs = []
    for bi in range(q_ref.shape[0]):
        for tile in range(tiles):
            cols = slice(tile * LANES, (tile + 1) * LANES)
            lg = [math.log(1.0 - 2.0 ** (-5.0 - (2 * tile + par))) for par in range(2)]
            lg_lane = jnp.where(first, lg[0], lg[1])
            q = q_ref[bi, :, cols].astype(BF16)
            k = k_ref[bi, :, cols].astype(F32) * HEAD_DIM ** -0.5
            v = v_ref[bi, :, cols].astype(F32)
            state = state_scr[bi, tile]
            out = _dot(q, state.astype(BF16)) * jnp.exp(lg_lane * (rowf + 1.0))
            for par in range(2):
                keep = first if par == 0 else ~first
                decay = jnp.where(diff >= 0, jnp.exp(lg[par] * jnp.maximum(diff, 0.0)), 0.0)
                inner = _dot_nt(q, jnp.where(keep, k, 0.0).astype(BF16)) * decay
                out = out + _dot(inner.astype(BF16), jnp.where(keep, v, 0.0).astype(BF16))
            k_dec = (k * jnp.exp(lg_lane * (cs - 1.0 - rowf))).astype(BF16)
            kv = _dot_tn(k_dec, v.astype(BF16))
            state_scr[bi, tile] = state * jnp.exp(lg_lane * cs) + jnp.where(same_head, kv, 0.0)
            outs.append(out)

    out = jnp.concatenate(outs, axis=0)
    mu = seg_mean_dot(out)
    cen = out - mu
    var = seg_mean_dot(cen * cen)
    yn = cen * lax.rsqrt(var + EPS)
    for bi in range(q_ref.shape[0]):
        for tile in range(tiles):
            cols = slice(tile * LANES, (tile + 1) * LANES)
            y = yn[(bi * tiles + tile) * cs:(bi * tiles + tile + 1) * cs] * gn_ref[:, cols]
            o_ref[bi, :, cols] = (y * _silu(g_ref[bi, :, cols])).astype(o_ref.dtype)


RET_BATCH = 4


def _retention(proj16, proj32, ret_norm, layer):
    b, s, _ = proj16.shape
    nc = s // BLOCK
    rb = math.gcd(b, RET_BATCH)
    spec = lambda col: pl.BlockSpec((rb, BLOCK, C_WIDTH), lambda b, c: (b, c, col // C_WIDTH))
    return pl.pallas_call(
        _ret_kernel,
        grid=(b // rb, nc),
        in_specs=[spec(COL_CQ), spec(COL32_CK), spec(COL_CV), spec(COL32_CG),
                  pl.BlockSpec((1, C_WIDTH), lambda b, c: (0, 0))],
        out_specs=pl.BlockSpec((rb, BLOCK, C_WIDTH), lambda b, c: (b, c, 0)),
        out_shape=jax.ShapeDtypeStruct((b, s, C_WIDTH), BF16),
        scratch_shapes=[pltpu.VMEM((rb, RET_HEADS // 2, LANES, LANES), F32)],
        compiler_params=pltpu.CompilerParams(
            dimension_semantics=("parallel", "arbitrary"), vmem_limit_bytes=VMEM_LIMIT),
        name=f"ret_l{layer}",
    )(proj16, proj32, proj16, proj32, ret_norm)


def _mix_out_kernel(x_ref, gt_ref, gpost_ref, oa_ref, ob_ref, oc_ref, w_ref, o_ref, gain_scr, inv_scr):
    y = _dot(oa_ref[...], w_ref[:A_WIDTH, :])
    y = y + _dot(ob_ref[...], w_ref[A_WIDTH:A_WIDTH + B_WIDTH, :])
    o_ref[...] = y + _dot(oc_ref[...], w_ref[A_WIDTH + B_WIDTH:, :])
    gain_scr[...] = gt_ref[...] * gpost_ref[...]
    _postnorm_residual_rows(o_ref, x_ref, gain_scr, inv_scr, o_ref)


def _mix_out(x, mod, norm_g, layer, oa, ob, oc, w, tm):
    b, s, d = x.shape
    row = lambda width: pl.BlockSpec((None, tm, width), lambda b, i: (b, i, 0))
    return pl.pallas_call(
        _mix_out_kernel,
        grid=(b, s // tm),
        in_specs=[
            row(d), _mod_spec(layer, 5, d, 2), _gain_spec(layer, 3, d, 2),
            row(A_WIDTH), row(B_WIDTH), row(C_WIDTH),
            pl.BlockSpec((None,) + w.shape[1:], lambda b, i: (layer, 0, 0)),
        ],
        out_specs=row(d),
        out_shape=jax.ShapeDtypeStruct((b, s, d), F32),
        scratch_shapes=[pltpu.VMEM((1, d), F32), pltpu.VMEM((tm, 1), F32)],
        compiler_params=pltpu.CompilerParams(
            dimension_semantics=("parallel", "parallel"), vmem_limit_bytes=VMEM_LIMIT),
        name=f"mix_out_l{layer}",
    )(x, mod, norm_g, oa, ob, oc, w)


def _prep_mix_in(w):
    w = w.astype(BF16)
    sizes = [A_WIDTH, KV_WIDTH, KV_WIDTH, B_WIDTH, DSA_RANK, IDX_HEADS * IDX_DIM, IDX_DIM, IDX_HEADS,
             C_WIDTH, C_WIDTH, C_WIDTH, C_WIDTH]
    starts = [0]
    for sz in sizes:
        starts.append(starts[-1] + sz)
    aq, ak, av, bq, bkv, biq, bik, biw, cq, ck, cv, cg = [w[..., a:a + sz] for a, sz in zip(starts, sizes)]
    tail = jnp.zeros(w.shape[:-1] + (LANES - IDX_DIM - IDX_HEADS,), w.dtype)
    return jnp.concatenate([aq, bq, biq, cq, cv, cg, ck, ak, av, bkv, bik, biw, tail], axis=-1)


def _prep_dsa_up(w_uk, w_uv):
    r, h, dh = w_uk.shape
    uk = jnp.transpose(w_uk, (1, 2, 0))
    uv = jnp.transpose(w_uv, (1, 0, 2))
    odd = (jnp.arange(h) % 2 == 1)[:, None, None]
    zk = jnp.zeros_like(uk)
    zv = jnp.zeros_like(uv)
    uk_pad = jnp.where(odd, jnp.concatenate([zk, uk], axis=1), jnp.concatenate([uk, zk], axis=1))
    uv_pad = jnp.where(odd, jnp.concatenate([zv, uv], axis=2), jnp.concatenate([uv, zv], axis=2))
    return uk_pad.astype(BF16), uv_pad.astype(BF16)


FFN_TM = 1024
FFN_TF = 512
FFN_VMEM_LIMIT = 60 * 1024 * 1024
MIX_IN_TM = 1024
MIX_IN_TN = 1536
MIX_OUT_TM = 512


def kernel(x, c, ada_w, ada_b, norm_g, ffn1_w_in, ffn1_w_out, ffn2_w_in, ffn2_w_out,
           mix_w_in, mix_w_out, swa_sinks, dsa_kv_norm, dsa_w_uk, dsa_w_uv, ret_norm):
    depth = ada_w.shape[0]
    b, s, d = x.shape
    assert s % FFN_TM == 0 and s % DSA_KCHUNK == 0 and d % LANES == 0
    mod = _ada_mod(c, ada_w, ada_b).reshape(depth, b, N_MOD, 1, d)
    gains = norm_g.reshape(depth, norm_g.shape[1], 1, d)
    ffn1_in, ffn1_out = ffn1_w_in.astype(BF16), ffn1_w_out.astype(BF16)
    ffn2_in, ffn2_out = ffn2_w_in.astype(BF16), ffn2_w_out.astype(BF16)
    mix_out_w = mix_w_out.astype(BF16)
    mix_in_w = _prep_mix_in(mix_w_in)
    for l in range(depth):
        x = _ffn_block(x, mod, gains, l, 0, ffn1_in, ffn1_out, 0.5, FFN_TM, FFN_TF)

        proj16, proj32 = _mix_in(x, mod, gains, l, mix_in_w, MIX_IN_TM, MIX_IN_TN)
        oa = _swa(proj16, proj32, swa_sinks[l], l)
        uk_pad, uv_pad = _prep_dsa_up(dsa_w_uk[l], dsa_w_uv[l])
        ob = _dsa(proj16, proj32, dsa_kv_norm[l].reshape(1, -1), uk_pad, uv_pad, l)
        oc = _retention(proj16, proj32, ret_norm[l].reshape(1, -1), l)
        x = _mix_out(x, mod, gains, l, oa, ob, oc, mix_out_w, MIX_OUT_TM)

        x = _ffn_block(x, mod, gains, l, 2, ffn2_in, ffn2_out, 0.5, FFN_TM, FFN_TF)
    return x
```

```python
import functools
import math

import jax
import jax.numpy as jnp
from jax import lax
from jax.experimental import pallas as pl
from jax.experimental.pallas import tpu as pltpu

F32 = jnp.float32
BF16 = jnp.bfloat16

LANES = 128
HEAD_DIM = 64
SWA_HEADS = 16
SWA_KV_HEADS = 2
BLOCK = 128
DSA_HEADS = 8
DSA_RANK = 128
IDX_HEADS = 8
IDX_DIM = 64
DSA_TOPK_MAX = 256
RET_HEADS = 8
N_MOD = 9
EPS = 1e-6
NEG = -1e30

A_WIDTH = SWA_HEADS * HEAD_DIM
B_WIDTH = DSA_HEADS * HEAD_DIM
C_WIDTH = RET_HEADS * HEAD_DIM
KV_WIDTH = SWA_KV_HEADS * HEAD_DIM

COL_AQ = 0
COL_BQ = COL_AQ + A_WIDTH
COL_BIQ = COL_BQ + B_WIDTH
COL_CQ = COL_BIQ + IDX_HEADS * IDX_DIM
COL_CV = COL_CQ + C_WIDTH
MIX_BF16_WIDTH = COL_CV + C_WIDTH
COL32_CG = 0
COL32_CK = COL32_CG + C_WIDTH
COL32_AKV = COL32_CK + C_WIDTH
COL32_BKV = COL32_AKV + 2 * KV_WIDTH
COL32_BIK = COL32_BKV + DSA_RANK
MIX_F32_WIDTH = COL32_BIK + LANES

DSA_QBLOCK = 256
DSA_KCHUNK = 256
BISECT_ITERS = 18
VMEM_LIMIT = 56 * 1024 * 1024


def _dot(a, b):
    return jnp.dot(a, b, preferred_element_type=F32)


def _dot_nt(a, b):
    return lax.dot_general(a, b, (((1,), (1,)), ((), ())), preferred_element_type=F32)


def _dot_tn(a, b):
    return lax.dot_general(a, b, (((0,), (0,)), ((), ())), preferred_element_type=F32)


def _silu(x):
    return x / (1.0 + jnp.exp(-x))


def _rms(x, g):
    return x * lax.rsqrt(jnp.mean(x * x, axis=-1, keepdims=True) + EPS) * g


def _alibi_slope(h, n):
    return 2.0 ** (-8.0 * (h + 1) / n)


def _ada_kernel(c_ref, w_ref, b_ref, o_ref):
    cond = _silu(c_ref[...]).astype(BF16)
    o_ref[...] = _dot(cond, w_ref[...].astype(BF16)) + b_ref[...]


def _ada_mod(c, ada_w, ada_b):
    depth, d, n = ada_w.shape
    b = c.shape[0]
    tn = 1024
    return pl.pallas_call(
        _ada_kernel,
        grid=(depth, n // tn),
        in_specs=[
            pl.BlockSpec((b, d), lambda l, j: (0, 0)),
            pl.BlockSpec((None, d, tn), lambda l, j: (l, 0, j)),
            pl.BlockSpec((None, 1, tn), lambda l, j: (l, 0, j)),
        ],
        out_specs=pl.BlockSpec((None, b, tn), lambda l, j: (l, 0, j)),
        out_shape=jax.ShapeDtypeStruct((depth, b, n), F32),
        compiler_params=pltpu.CompilerParams(
            dimension_semantics=("arbitrary", "arbitrary"), vmem_limit_bytes=VMEM_LIMIT),
        name="ada_mod",
    )(c, ada_w, ada_b.reshape(depth, 1, n))


def _mod_spec(layer, k, d, grid_rank):
    if grid_rank == 2:
        return pl.BlockSpec((None, None, None, 1, d), lambda b, i: (layer, b, k, 0, 0))
    return pl.BlockSpec((None, None, None, 1, d), lambda b, i, j: (layer, b, k, 0, 0))


def _gain_spec(layer, k, d, grid_rank):
    if grid_rank == 2:
        return pl.BlockSpec((None, None, 1, d), lambda b, i: (layer, k, 0, 0))
    return pl.BlockSpec((None, None, 1, d), lambda b, i, j: (layer, k, 0, 0))


ROW_CHUNK = 16


def _row_chunks(n_rows, body):
    def step(i, carry):
        body(pl.ds(pl.multiple_of(i * ROW_CHUNK, ROW_CHUNK), ROW_CHUNK))
        return carry
    lax.fori_loop(0, n_rows // ROW_CHUNK, step, 0, unroll=True)


def _inv_rms_rows(x_ref, inv_ref):
    def body(r):
        x = x_ref[r, :]
        inv_ref[r, :] = lax.rsqrt(jnp.mean(x * x, axis=-1, keepdims=True) + EPS)
    _row_chunks(x_ref.shape[0], body)


def _prenorm_mod_rows(x_ref, gain_ref, shift_ref, inv_ref, h_ref):
    _inv_rms_rows(x_ref, inv_ref)

    def body(r):
        h_ref[r, :] = (x_ref[r, :] * inv_ref[r, :] * gain_ref[...] + shift_ref[...]).astype(h_ref.dtype)
    _row_chunks(x_ref.shape[0], body)


def _postnorm_residual_rows(y_ref, x_ref, gain_ref, inv_ref, o_ref):
    _inv_rms_rows(y_ref, inv_ref)

    def body(r):
        o_ref[r, :] = x_ref[r, :] + y_ref[r, :] * inv_ref[r, :] * gain_ref[...]
    _row_chunks(y_ref.shape[0], body)


def _ffn_kernel(res_w, tail, x_ref, sh_ref, sc_ref, gt_ref, gpre_ref, gpost_ref, wg_ref, wu_ref, wo_ref,
                o_ref, h_scr, gain_scr, inv_scr):
    f = pl.program_id(2)
    last = pl.num_programs(2) - 1
    tf = wo_ref.shape[1]

    def step(lo, first):
        h = h_scr[...]
        act = _silu(_dot(h, wg_ref[0, :, lo:])) * _dot(h, wu_ref[0, :, lo:])
        out = _dot(act.astype(BF16), wo_ref[0, lo:, :])
        if first:
            o_ref[...] = out
        else:
            o_ref[...] += out

    @pl.when(f == 0)
    def _():
        gain_scr[...] = gpre_ref[...] * (1.0 + sc_ref[...])
        _prenorm_mod_rows(x_ref, gain_scr, sh_ref, inv_scr, h_scr)
        step(0, True)

    @pl.when((f > 0) & (f < last))
    def _():
        step(0, False)

    @pl.when(f == last)
    def _():
        step(tf - tail, False)
        gain_scr[...] = res_w * gt_ref[...] * gpost_ref[...]
        _postnorm_residual_rows(o_ref, x_ref, gain_scr, inv_scr, o_ref)


def _ffn_block(x, mod, norm_g, layer, sub, w_in, w_out, res_w, tm, tf):
    b, s, d = x.shape
    d_ff = w_out.shape[1]
    nf = pl.cdiv(d_ff, tf)
    tail = d_ff - (nf - 1) * tf
    assert tail % LANES == 0 and d_ff >= tf and nf >= 2
    mk = 3 * sub
    start = lambda j, base: pl.multiple_of(base + jnp.minimum(j * tf, d_ff - tf), LANES)
    col = lambda base: (lambda b, i, j: (layer, 0, start(j, base)))
    one = pl.Element(1)
    return pl.pallas_call(
        functools.partial(_ffn_kernel, res_w, tail),
        grid=(b, s // tm, nf),
        in_specs=[
            pl.BlockSpec((None, tm, d), lambda b, i, j: (b, i, 0)),
            _mod_spec(layer, mk, d, 3), _mod_spec(layer, mk + 1, d, 3), _mod_spec(layer, mk + 2, d, 3),
            _gain_spec(layer, 2 * sub, d, 3), _gain_spec(layer, 2 * sub + 1, d, 3),
            pl.BlockSpec((one, pl.Element(d), pl.Element(tf)), col(0)),
            pl.BlockSpec((one, pl.Element(d), pl.Element(tf)), col(d_ff)),
            pl.BlockSpec((one, pl.Element(tf), pl.Element(d)), lambda b, i, j: (layer, start(j, 0), 0)),
        ],
        out_specs=pl.BlockSpec((None, tm, d), lambda b, i, j: (b, i, 0)),
        out_shape=jax.ShapeDtypeStruct((b, s, d), F32),
        scratch_shapes=[pltpu.VMEM((tm, d), BF16), pltpu.VMEM((1, d), F32), pltpu.VMEM((tm, 1), F32)],
        compiler_params=pltpu.CompilerParams(
            dimension_semantics=("parallel", "parallel", "arbitrary"), vmem_limit_bytes=FFN_VMEM_LIMIT),
        name=f"ffn_l{layer}_s{sub}",
    )(x, mod, mod, mod, norm_g, norm_g, w_in, w_in, w_out)


def _mix_in_kernel(n16, x_ref, sh_ref, sc_ref, gpre_ref, w_ref, o16_ref, o32_ref, h_scr, gain_scr, inv_scr):
    j = pl.program_id(2)

    @pl.when(j == 0)
    def _():
        gain_scr[...] = gpre_ref[...] * (1.0 + sc_ref[...])
        _prenorm_mod_rows(x_ref, gain_scr, sh_ref, inv_scr, h_scr)

    @pl.when(j < n16)
    def _():
        o16_ref[...] = _dot(h_scr[...], w_ref[...]).astype(o16_ref.dtype)

    @pl.when(j >= n16)
    def _():
        o32_ref[...] = _dot(h_scr[...], w_ref[...])


def _mix_in(x, mod, norm_g, layer, w, tm, tn):
    b, s, d = x.shape
    assert MIX_BF16_WIDTH % tn == 0 and MIX_F32_WIDTH % tn == 0
    n16, n32 = MIX_BF16_WIDTH // tn, MIX_F32_WIDTH // tn
    return pl.pallas_call(
        functools.partial(_mix_in_kernel, n16),
        grid=(b, s // tm, n16 + n32),
        in_specs=[
            pl.BlockSpec((None, tm, d), lambda b, i, j: (b, i, 0)),
            _mod_spec(layer, 3, d, 3), _mod_spec(layer, 4, d, 3),
            _gain_spec(layer, 2, d, 3),
            pl.BlockSpec((None, d, tn), lambda b, i, j: (layer, 0, j)),
        ],
        out_specs=[
            pl.BlockSpec((None, tm, tn), lambda b, i, j: (b, i, jnp.minimum(j, n16 - 1))),
            pl.BlockSpec((None, tm, tn), lambda b, i, j: (b, i, jnp.maximum(j - n16, 0))),
        ],
        out_shape=[jax.ShapeDtypeStruct((b, s, MIX_BF16_WIDTH), BF16),
                   jax.ShapeDtypeStruct((b, s, MIX_F32_WIDTH), F32)],
        scratch_shapes=[pltpu.VMEM((tm, d), BF16), pltpu.VMEM((1, d), F32), pltpu.VMEM((tm, 1), F32)],
        compiler_params=pltpu.CompilerParams(
            dimension_semantics=("parallel", "parallel", "arbitrary"), vmem_limit_bytes=VMEM_LIMIT),
        name=f"mix_in_l{layer}",
    )(x, mod, mod, norm_g, w)


def _lane_halves(x, first_half_holds_data):
    lane = lax.broadcasted_iota(jnp.int32, x.shape, 1)
    if first_half_holds_data:
        lo = jnp.where(lane < HEAD_DIM, x, 0.0)
        return lo, pltpu.roll(lo, HEAD_DIM, 1)
    hi = jnp.where(lane >= HEAD_DIM, x, 0.0)
    return pltpu.roll(hi, HEAD_DIM, 1), hi


def _swa_kernel(sinks_ref, q_ref, kvc_ref, kvp_ref, o_ref):
    n = pl.program_id(1)
    w = BLOCK
    bands = SWA_HEADS // SWA_KV_HEADS // 2
    rows = bands * w
    r = lax.broadcasted_iota(jnp.int32, (rows, w), 0)
    j = lax.broadcasted_iota(jnp.int32, (rows, w), 1)
    i = r & (w - 1)
    from_prev = j > i
    prev_f = jnp.where(from_prev, 1.0, 0.0)
    dist = i - j + jnp.where(from_prev, w, 0)
    no_key = jnp.where(from_prev & (n * w - w + j < 0), -jnp.inf, 0.0)
    band_step = _alibi_slope(2, SWA_HEADS) / _alibi_slope(0, SWA_HEADS)
    band_scale = jnp.ones((rows, w), F32)
    for band in range(1, bands):
        band_scale = jnp.where(r >= band * w, band_step ** band, band_scale)
    dist_scaled = dist.astype(F32) * band_scale
    band_col = lax.broadcasted_iota(jnp.int32, (rows, 1), 0)
    for bi in range(q_ref.shape[0]):
        kcat = jnp.concatenate([kvp_ref[bi, :, :KV_WIDTH], kvc_ref[bi, :, :KV_WIDTH]], axis=0)
        vcat = jnp.concatenate([kvp_ref[bi, :, KV_WIDTH:], kvc_ref[bi, :, KV_WIDTH:]], axis=0)
        for hk in range(SWA_KV_HEADS):
            k_halves = [t.astype(BF16) for t in _lane_halves(kcat, hk == 0)]
            v_halves = [t.astype(BF16) for t in _lane_halves(vcat, hk == 0)]
            q = jnp.concatenate([q_ref[bi, :, (bands * hk + band) * LANES:(bands * hk + band + 1) * LANES]
                                 for band in range(bands)], axis=0)
            q = (q * HEAD_DIM ** -0.5).astype(BF16)
            out = None
            for par in range(2):
                head0 = 2 * bands * hk + par
                sink = jnp.full((rows, 1), sinks_ref[head0], F32)
                for band in range(1, bands):
                    sink = jnp.where(band_col >= band * w, sinks_ref[head0 + 2 * band], sink)
                k_all, v_all = k_halves[par], v_halves[par]
                sc = jnp.where(from_prev, _dot_nt(q, k_all[:w]), _dot_nt(q, k_all[w:]))
                sc = sc - _alibi_slope(head0, SWA_HEADS) * dist_scaled + no_key
                m = jnp.maximum(jnp.max(sc, axis=-1, keepdims=True), sink)
                e = jnp.exp(sc - m)
                p = e / (jnp.sum(e, axis=-1, keepdims=True) + jnp.exp(sink - m))
                p_prev = p * prev_f
                pv = _dot(p_prev.astype(BF16), v_all[:w]) + _dot((p - p_prev).astype(BF16), v_all[w:])
                out = pv if out is None else out + pv
            for band in range(bands):
                tile = bands * hk + band
                o_ref[bi, :, tile * LANES:(tile + 1) * LANES] = out[band * w:(band + 1) * w].astype(o_ref.dtype)


SWA_BATCH = 4


def _swa(proj16, proj32, sinks, layer):
    b, s, _ = proj16.shape
    nb = s // BLOCK
    rb = math.gcd(b, SWA_BATCH)
    kv_blk = COL32_AKV // (2 * KV_WIDTH)
    return pl.pallas_call(
        _swa_kernel,
        grid=(b // rb, nb),
        in_specs=[
            pl.BlockSpec(memory_space=pltpu.SMEM),
            pl.BlockSpec((rb, BLOCK, A_WIDTH), lambda b, n: (b, n, COL_AQ // A_WIDTH)),
            pl.BlockSpec((rb, BLOCK, 2 * KV_WIDTH), lambda b, n: (b, n, kv_blk)),
            pl.BlockSpec((rb, BLOCK, 2 * KV_WIDTH), lambda b, n: (b, jnp.maximum(n - 1, 0), kv_blk)),
        ],
        out_specs=pl.BlockSpec((rb, BLOCK, A_WIDTH), lambda b, n: (b, n, 0)),
        out_shape=jax.ShapeDtypeStruct((b, s, A_WIDTH), BF16),
        compiler_params=pltpu.CompilerParams(
            dimension_semantics=("parallel", "arbitrary"), vmem_limit_bytes=VMEM_LIMIT),
        name=f"swa_l{layer}",
    )(sinks, proj16, proj32, proj32)


def _dsa_kernel(topk, q_ref, iq_ref, kv_ref, ikw_ref, kvn_ref, wuk_ref, wuv_ref, o_ref,
                ckv_scr, ik_scr, iqh_scr, score_scr, bias_scr, qlat_scr, s_scr, mrun_scr, m_scr,
                lpart_scr, acc_scr):
    n = pl.program_id(1)
    qb = DSA_QBLOCK
    kc = DSA_KCHUNK
    nchunks = lax.div(n * qb + qb - 1, kc) + 1

    @pl.when(n == 0)
    def _():
        ckv_scr[...] = _rms(kv_ref[...], kvn_ref[...]).astype(BF16)
        ik_scr[...] = ikw_ref[...].astype(BF16)

    row0 = pl.multiple_of(n * qb, qb)

    iw_t = ikw_ref[pl.ds(row0, qb), :].T
    iw_scale = IDX_HEADS ** -0.5 * IDX_DIM ** -0.5
    w_rows = [iw_t[IDX_DIM + h:IDX_DIM + h + 1, :] * iw_scale for h in range(IDX_HEADS)]
    lane = lax.broadcasted_iota(jnp.int32, (qb, LANES), 1)
    for h in range(IDX_HEADS):
        iq = iq_ref[:, (h // 2) * LANES:(h // 2 + 1) * LANES].astype(F32)
        if h % 2 == 0:
            iq = jnp.where(lane < IDX_DIM, iq, 0.0)
        else:
            iq = pltpu.roll(jnp.where(lane >= IDX_DIM, iq, 0.0), IDX_DIM, 1)
        iqh_scr[h] = iq.astype(BF16)

    key_row = lax.broadcasted_iota(jnp.int32, (kc, qb), 0)
    t_pos = n * qb + lax.broadcasted_iota(jnp.int32, (kc, qb), 1)

    def score_chunk(c, carry):
        off = pl.multiple_of(c * kc, kc)
        ik = ik_scr[pl.ds(off, kc), :]
        sc = jnp.zeros((kc, qb), F32)
        for h in range(IDX_HEADS):
            sc = sc + w_rows[h] * jnp.maximum(_dot_nt(ik, iqh_scr[h]), 0.0)
        score_scr[c] = jnp.where(off + key_row <= t_pos, sc, -jnp.inf)
        return carry

    lax.fori_loop(0, nchunks, score_chunk, 0)

    sub = 8

    def fold_rows(x, op):
        out = x[:sub]
        for g in range(1, x.shape[0] // sub):
            out = op(out, x[g * sub:(g + 1) * sub])
        return out

    def fold_lanes(x, op):
        out = x[:, :LANES]
        for t in range(1, x.shape[1] // LANES):
            out = op(out, x[:, t * LANES:(t + 1) * LANES])
        return out

    def count_ge(thr, strict=False):
        def body(c, cnt):
            x = score_scr[c]
            hit = (x > thr) if strict else (x >= thr)
            return cnt + fold_rows(jnp.where(hit, 1.0, 0.0), jnp.add)
        part = lax.fori_loop(0, nchunks, body, jnp.zeros((sub, qb), F32))
        return jnp.sum(part, axis=0, keepdims=True)

    def max_at_most(bound):
        def body(c, mx):
            x = score_scr[c]
            return jnp.maximum(mx, fold_rows(jnp.where(x <= bound, x, -jnp.inf), jnp.maximum))
        part = lax.fori_loop(0, nchunks, body, jnp.full((sub, qb), -jnp.inf, F32))
        return jnp.max(part, axis=0, keepdims=True)

    def store_bias(c, t, bias_t):
        for g in range(qb // LANES):
            bias_scr[c, g * LANES:(g + 1) * LANES, t * LANES:(t + 1) * LANES] = (
                bias_t[:, g * LANES:(g + 1) * LANES].T)

    @pl.when(n * qb + qb <= topk)
    def _():
        def body(c, carry):
            x = score_scr[c]
            for t in range(kc // LANES):
                store_bias(c, t, jnp.where(x[t * LANES:(t + 1) * LANES] == -jnp.inf, NEG, 0.0))
            return carry
        lax.fori_loop(0, nchunks, body, 0)

    @pl.when(n * qb + qb > topk)
    def _():
        kf = float(topk)

        def minmax(c, carry):
            mn, mx = carry
            x = score_scr[c]
            mn = jnp.minimum(mn, fold_rows(jnp.where(x == -jnp.inf, jnp.inf, x), jnp.minimum))
            return mn, jnp.maximum(mx, fold_rows(x, jnp.maximum))

        lo, hi = lax.fori_loop(0, nchunks, minmax,
                               (jnp.full((sub, qb), jnp.inf, F32), jnp.full((sub, qb), -jnp.inf, F32)))
        lo = jnp.min(lo, axis=0, keepdims=True)
        hi = jnp.max(hi, axis=0, keepdims=True)

        def bisect(_, carry):
            lo, hi = carry
            mid = 0.5 * (lo + hi)
            ge = count_ge(mid) >= kf
            return jnp.where(ge, mid, lo), jnp.where(ge, hi, mid)

        lo, hi = lax.fori_loop(0, BISECT_ITERS, bisect, (lo, hi))

        def walk_cond(carry):
            return carry[1] > 0.0

        def count_and_next(thr):
            def body(c, carry):
                cnt, mx = carry
                x = score_scr[c]
                ge = x >= thr
                cnt = cnt + fold_rows(jnp.where(ge, 1.0, 0.0), jnp.add)
                return cnt, jnp.maximum(mx, fold_rows(jnp.where(ge, -jnp.inf, x), jnp.maximum))
            cnt, mx = lax.fori_loop(0, nchunks, body, (jnp.zeros((sub, qb), F32),
                                                       jnp.full((sub, qb), -jnp.inf, F32)))
            return jnp.sum(cnt, axis=0, keepdims=True), jnp.max(mx, axis=0, keepdims=True)

        def walk(carry):
            thr, _ = carry
            cnt, below = count_and_next(thr)
            short = cnt < kf
            return jnp.where(short, below, thr), jnp.max(jnp.where(short, 1.0, 0.0))

        thr, _ = lax.while_loop(walk_cond, walk, (max_at_most(hi), jnp.float32(1.0)))

        need = kf - count_ge(thr, strict=True)

        r = lax.broadcasted_iota(jnp.int32, (LANES, LANES), 0)
        col = lax.broadcasted_iota(jnp.int32, (LANES, LANES), 1)
        tri = jnp.where(col <= r, 1.0, 0.0).astype(BF16)

        def select(c, seen):
            x = score_scr[c]
            for t in range(kc // LANES):
                xt = x[t * LANES:(t + 1) * LANES]
                eq = xt == thr
                eqf = jnp.where(eq, 1.0, 0.0)
                rank = seen + _dot(tri, eqf.astype(BF16))
                seen = seen + jnp.sum(eqf, axis=0, keepdims=True)
                sel = (xt > thr) | (eq & (rank <= need))
                store_bias(c, t, jnp.where(sel, 0.0, NEG))
            return seen

        lax.fori_loop(0, nchunks, select, jnp.zeros((1, qb), F32))

    for h in range(DSA_HEADS):
        tile = h // 2
        rows = slice(h * qb, (h + 1) * qb)
        q = q_ref[:, tile * LANES:(tile + 1) * LANES].astype(BF16)
        qlat_scr[rows, :] = (_dot(q, wuk_ref[h]) * HEAD_DIM ** -0.5).astype(BF16)
    mrun_scr[...] = jnp.full(mrun_scr.shape, NEG, F32)
    key_lane = lax.broadcasted_iota(jnp.int32, (1, kc), 1)

    def logits(c, carry):
        off = pl.multiple_of(c * kc, kc)
        s = _dot_nt(qlat_scr[...], ckv_scr[pl.ds(off, kc), :])
        bias = bias_scr[c]
        key_pos = (off + key_lane).astype(F32)
        for h in range(DSA_HEADS):
            rows = slice(h * qb, (h + 1) * qb)
            sh = s[rows] + (bias + _alibi_slope(h, DSA_HEADS) * key_pos)
            s_scr[c, rows, :] = sh
            mrun_scr[rows, :] = jnp.maximum(mrun_scr[rows, :], fold_lanes(sh, jnp.maximum))
        return carry

    lax.fori_loop(0, nchunks, logits, 0)
    m_scr[...] = jnp.broadcast_to(jnp.max(mrun_scr[...], axis=-1, keepdims=True), m_scr.shape)
    lpart_scr[...] = jnp.zeros_like(lpart_scr)
    acc_scr[...] = jnp.zeros_like(acc_scr)

    def attend(c, carry):
        off = pl.multiple_of(c * kc, kc)
        m = m_scr[...]
        p = [jnp.exp(s_scr[c, :, t * LANES:(t + 1) * LANES] - m) for t in range(kc // LANES)]
        lpart_scr[...] += functools.reduce(jnp.add, p)
        acc_scr[...] += _dot(jnp.concatenate(p, axis=1).astype(BF16), ckv_scr[pl.ds(off, kc), :])
        return carry

    lax.fori_loop(0, nchunks, attend, 0)

    o_all = (acc_scr[...] / jnp.sum(lpart_scr[...], axis=-1, keepdims=True)).astype(BF16)
    for tile in range(DSA_HEADS // 2):
        h = 2 * tile
        out = _dot(o_all[h * qb:(h + 1) * qb], wuv_ref[h]) + _dot(o_all[(h + 1) * qb:(h + 2) * qb], wuv_ref[h + 1])
        o_ref[:, tile * LANES:(tile + 1) * LANES] = out.astype(o_ref.dtype)


def _dsa(proj16, proj32, kv_norm, wuk_pad, wuv_pad, layer):
    b, s, _ = proj16.shape
    qb = DSA_QBLOCK
    nb = s // qb
    topk = min(DSA_TOPK_MAX, s // 4)
    assert topk % qb == 0
    nck = s // DSA_KCHUNK
    rows = DSA_HEADS * qb
    return pl.pallas_call(
        functools.partial(_dsa_kernel, topk),
        grid=(b, nb),
        in_specs=[
            pl.BlockSpec((None, qb, B_WIDTH), lambda b, n: (b, n, COL_BQ // B_WIDTH)),
            pl.BlockSpec((None, qb, B_WIDTH), lambda b, n: (b, n, COL_BIQ // B_WIDTH)),
            pl.BlockSpec((None, s, DSA_RANK), lambda b, n: (b, 0, COL32_BKV // DSA_RANK)),
            pl.BlockSpec((None, s, LANES), lambda b, n: (b, 0, COL32_BIK // LANES)),
            pl.BlockSpec((1, DSA_RANK), lambda b, n: (0, 0)),
            pl.BlockSpec((DSA_HEADS, LANES, DSA_RANK), lambda b, n: (0, 0, 0)),
            pl.BlockSpec((DSA_HEADS, DSA_RANK, LANES), lambda b, n: (0, 0, 0)),
        ],
        out_specs=pl.BlockSpec((None, qb, B_WIDTH), lambda b, n: (b, n, 0)),
        out_shape=jax.ShapeDtypeStruct((b, s, B_WIDTH), BF16),
        scratch_shapes=[
            pltpu.VMEM((s, DSA_RANK), BF16),
            pltpu.VMEM((s, LANES), BF16),
            pltpu.VMEM((IDX_HEADS, qb, LANES), BF16),
            pltpu.VMEM((nck, DSA_KCHUNK, qb), F32),
            pltpu.VMEM((nck, qb, DSA_KCHUNK), F32),
            pltpu.VMEM((rows, DSA_RANK), BF16),
            pltpu.VMEM((nck, rows, DSA_KCHUNK), F32),
            pltpu.VMEM((rows, LANES), F32),
            pltpu.VMEM((rows, LANES), F32),
            pltpu.VMEM((rows, LANES), F32),
            pltpu.VMEM((rows, DSA_RANK), F32),
        ],
        compiler_params=pltpu.CompilerParams(
            dimension_semantics=("parallel", "arbitrary"), vmem_limit_bytes=VMEM_LIMIT),
        name=f"dsa_l{layer}",
    )(proj16, proj16, proj32, proj32, kv_norm, wuk_pad, wuv_pad)


def _ret_kernel(q_ref, k_ref, v_ref, g_ref, gn_ref, o_ref, state_scr):
    cs = BLOCK

    @pl.when(pl.program_id(1) == 0)
    def _():
        state_scr[...] = jnp.zeros_like(state_scr)

    row = lax.broadcasted_iota(jnp.int32, (cs, LANES), 0)
    lane = lax.broadcasted_iota(jnp.int32, (cs, LANES), 1)
    first = lane < HEAD_DIM
    rowf = row.astype(F32)
    diff = (row - lane).astype(F32)
    same_head = (row < HEAD_DIM) == first
    seg_mean = jnp.where(same_head, 1.0 / HEAD_DIM, 0.0).astype(BF16)

    def seg_mean_dot(x):
        x_hi = x.astype(BF16)
        x_lo = (x - x_hi.astype(F32)).astype(BF16)
        return _dot(x_hi, seg_mean) + _dot(x_lo, seg_mean)

    tiles = RET_HEADS // 2
    outs = []
    for bi in range(q_ref.shape[0]):
        for tile in range(tiles):
            cols = slice(tile * LANES, (tile + 1) * LANES)
            lg = [math.log(1.0 - 2.0 ** (-5.0 - (2 * tile + par))) for par in range(2)]
            lg_lane = jnp.where(first, lg[0], lg[1])
            q = q_ref[bi, :, cols].astype(BF16)
            k = k_ref[bi, :, cols].astype(F32) * HEAD_DIM ** -0.5
            v = v_ref[bi, :, cols].astype(F32)
            state = state_scr[bi, tile]
            out = _dot(q, state.astype(BF16)) * jnp.exp(lg_lane * (rowf + 1.0))
            for par in range(2):
                keep = first if par == 0 else ~first
                decay = jnp.where(diff >= 0, jnp.exp(lg[par] * jnp.maximum(diff, 0.0)), 0.0)
                inner = _dot_nt(q, jnp.where(keep, k, 0.0).astype(BF16)) * decay
                out = out + _dot(inner.astype(BF16), jnp.where(keep, v, 0.0).astype(BF16))
            k_dec = (k * jnp.exp(lg_lane * (cs - 1.0 - rowf))).astype(BF16)
            kv = _dot_tn(k_dec, v.astype(BF16))
            state_scr[bi, tile] = state * jnp.exp(lg_lane * cs) + jnp.where(same_head, kv, 0.0)
            outs.append(out)

    out = jnp.concatenate(outs, axis=0)
    mu = seg_mean_dot(out)
    cen = out - mu
    var = seg_mean_dot(cen * cen)
    yn = cen * lax.rsqrt(var + EPS)
    for bi in range(q_ref.shape[0]):
        for tile in range(tiles):
            cols = slice(tile * LANES, (tile + 1) * LANES)
            y = yn[(bi * tiles + tile) * cs:(bi * tiles + tile + 1) * cs] * gn_ref[:, cols]
            o_ref[bi, :, cols] = (y * _silu(g_ref[bi, :, cols])).astype(o_ref.dtype)


RET_BATCH = 8


def _retention(proj16, proj32, ret_norm, layer):
    b, s, _ = proj16.shape
    nc = s // BLOCK
    rb = math.gcd(b, RET_BATCH)
    spec = lambda col: pl.BlockSpec((rb, BLOCK, C_WIDTH), lambda b, c: (b, c, col // C_WIDTH))
    return pl.pallas_call(
        _ret_kernel,
        grid=(b // rb, nc),
        in_specs=[spec(COL_CQ), spec(COL32_CK), spec(COL_CV), spec(COL32_CG),
                  pl.BlockSpec((1, C_WIDTH), lambda b, c: (0, 0))],
        out_specs=pl.BlockSpec((rb, BLOCK, C_WIDTH), lambda b, c: (b, c, 0)),
        out_shape=jax.ShapeDtypeStruct((b, s, C_WIDTH), BF16),
        scratch_shapes=[pltpu.VMEM((rb, RET_HEADS // 2, LANES, LANES), F32)],
        compiler_params=pltpu.CompilerParams(
            dimension_semantics=("parallel", "arbitrary"), vmem_limit_bytes=VMEM_LIMIT),
        name=f"ret_l{layer}",
    )(proj16, proj32, proj16, proj32, ret_norm)


def _mix_out_kernel(x_ref, gt_ref, gpost_ref, oa_ref, ob_ref, oc_ref, w_ref, o_ref, gain_scr, inv_scr):
    y = _dot(oa_ref[...], w_ref[:A_WIDTH, :])
    y = y + _dot(ob_ref[...], w_ref[A_WIDTH:A_WIDTH + B_WIDTH, :])
    o_ref[...] = y + _dot(oc_ref[...], w_ref[A_WIDTH + B_WIDTH:, :])
    gain_scr[...] = gt_ref[...] * gpost_ref[...]
    _postnorm_residual_rows(o_ref, x_ref, gain_scr, inv_scr, o_ref)


def _mix_out(x, mod, norm_g, layer, oa, ob, oc, w, tm):
    b, s, d = x.shape
    row = lambda width: pl.BlockSpec((None, tm, width), lambda b, i: (b, i, 0))
    return pl.pallas_call(
        _mix_out_kernel,
        grid=(b, s // tm),
        in_specs=[
            row(d), _mod_spec(layer, 5, d, 2), _gain_spec(layer, 3, d, 2),
            row(A_WIDTH), row(B_WIDTH), row(C_WIDTH),
            pl.BlockSpec((None,) + w.shape[1:], lambda b, i: (layer, 0, 0)),
        ],
        out_specs=row(d),
        out_shape=jax.ShapeDtypeStruct((b, s, d), F32),
        scratch_shapes=[pltpu.VMEM((1, d), F32), pltpu.VMEM((tm, 1), F32)],
        compiler_params=pltpu.CompilerParams(
            dimension_semantics=("parallel", "parallel"), vmem_limit_bytes=VMEM_LIMIT),
        name=f"mix_out_l{layer}",
    )(x, mod, norm_g, oa, ob, oc, w)


def _prep_mix_in(w):
    w = w.astype(BF16)
    sizes = [A_WIDTH, KV_WIDTH, KV_WIDTH, B_WIDTH, DSA_RANK, IDX_HEADS * IDX_DIM, IDX_DIM, IDX_HEADS,
             C_WIDTH, C_WIDTH, C_WIDTH, C_WIDTH]
    starts = [0]
    for sz in sizes:
        starts.append(starts[-1] + sz)
    aq, ak, av, bq, bkv, biq, bik, biw, cq, ck, cv, cg = [w[..., a:a + sz] for a, sz in zip(starts, sizes)]
    tail = jnp.zeros(w.shape[:-1] + (LANES - IDX_DIM - IDX_HEADS,), w.dtype)
    return jnp.concatenate([aq, bq, biq, cq, cv, cg, ck, ak, av, bkv, bik, biw, tail], axis=-1)


def _prep_dsa_up(w_uk, w_uv):
    r, h, dh = w_uk.shape
    uk = jnp.transpose(w_uk, (1, 2, 0))
    uv = jnp.transpose(w_uv, (1, 0, 2))
    odd = (jnp.arange(h) % 2 == 1)[:, None, None]
    zk = jnp.zeros_like(uk)
    zv = jnp.zeros_like(uv)
    uk_pad = jnp.where(odd, jnp.concatenate([zk, uk], axis=1), jnp.concatenate([uk, zk], axis=1))
    uv_pad = jnp.where(odd, jnp.concatenate([zv, uv], axis=2), jnp.concatenate([uv, zv], axis=2))
    return uk_pad.astype(BF16), uv_pad.astype(BF16)


FFN_TM = 1024
FFN_TF = 512
FFN_VMEM_LIMIT = 60 * 1024 * 1024
MIX_IN_TM = 1024
MIX_IN_TN = 1536
MIX_OUT_TM = 512


def kernel(x, c, ada_w, ada_b, norm_g, ffn1_w_in, ffn1_w_out, ffn2_w_in, ffn2_w_out,
           mix_w_in, mix_w_out, swa_sinks, dsa_kv_norm, dsa_w_uk, dsa_w_uv, ret_norm):
    depth = ada_w.shape[0]
    b, s, d = x.shape
    assert s % FFN_TM == 0 and s % DSA_KCHUNK == 0 and d % LANES == 0
    mod = _ada_mod(c, ada_w, ada_b).reshape(depth, b, N_MOD, 1, d)
    gains = norm_g.reshape(depth, norm_g.shape[1], 1, d)
    ffn1_in, ffn1_out = ffn1_w_in.astype(BF16), ffn1_w_out.astype(BF16)
    ffn2_in, ffn2_out = ffn2_w_in.astype(BF16), ffn2_w_out.astype(BF16)
    mix_out_w = mix_w_out.astype(BF16)
    mix_in_w = _prep_mix_in(mix_w_in)
    for l in range(depth):
        x = _ffn_block(x, mod, gains, l, 0, ffn1_in, ffn1_out, 0.5, FFN_TM, FFN_TF)

        proj16, proj32 = _mix_in(x, mod, gains, l, mix_in_w, MIX_IN_TM, MIX_IN_TN)
        oa = _swa(proj16, proj32, swa_sinks[l], l)
        uk_pad, uv_pad = _prep_dsa_up(dsa_w_uk[l], dsa_w_uv[l])
        ob = _dsa(proj16, proj32, dsa_kv_norm[l].reshape(1, -1), uk_pad, uv_pad, l)
        oc = _retention(proj16, proj32, ret_norm[l].reshape(1, -1), l)
        x = _mix_out(x, mod, gains, l, oa, ob, oc, mix_out_w, MIX_OUT_TM)

        x = _ffn_block(x, mod, gains, l, 2, ffn2_in, ffn2_out, 0.5, FFN_TM, FFN_TF)
    return x
```

```python
import functools
import math

import jax
import jax.numpy as jnp
from jax import lax
from jax.experimental import pallas as pl
from jax.experimental.pallas import tpu as pltpu

F32 = jnp.float32
BF16 = jnp.bfloat16

LANES = 128
HEAD_DIM = 64
SWA_HEADS = 16
SWA_KV_HEADS = 2
BLOCK = 128
DSA_HEADS = 8
DSA_RANK = 128
IDX_HEADS = 8
IDX_DIM = 64
DSA_TOPK_MAX = 256
RET_HEADS = 8
N_MOD = 9
EPS = 1e-6
NEG = -1e30

A_WIDTH = SWA_HEADS * HEAD_DIM
B_WIDTH = DSA_HEADS * HEAD_DIM
C_WIDTH = RET_HEADS * HEAD_DIM
KV_WIDTH = SWA_KV_HEADS * HEAD_DIM

COL_AQ = 0
COL_BQ = COL_AQ + A_WIDTH
COL_BIQ = COL_BQ + B_WIDTH
COL_CQ = COL_BIQ + IDX_HEADS * IDX_DIM
COL_CV = COL_CQ + C_WIDTH
MIX_BF16_WIDTH = COL_CV + C_WIDTH
COL32_CG = 0
COL32_CK = COL32_CG + C_WIDTH
COL32_AKV = COL32_CK + C_WIDTH
COL32_BKV = COL32_AKV + 2 * KV_WIDTH
COL32_BIK = COL32_BKV + DSA_RANK
MIX_F32_WIDTH = COL32_BIK + LANES

DSA_QBLOCK = 256
DSA_KCHUNK = 256
BISECT_ITERS = 14
VMEM_LIMIT = 56 * 1024 * 1024


def _dot(a, b):
    return jnp.dot(a, b, preferred_element_type=F32)


def _dot_nt(a, b):
    return lax.dot_general(a, b, (((1,), (1,)), ((), ())), preferred_element_type=F32)


def _dot_tn(a, b):
    return lax.dot_general(a, b, (((0,), (0,)), ((), ())), preferred_element_type=F32)


def _silu(x):
    return x / (1.0 + jnp.exp(-x))


def _rms(x, g):
    return x * lax.rsqrt(jnp.mean(x * x, axis=-1, keepdims=True) + EPS) * g


def _alibi_slope(h, n):
    return 2.0 ** (-8.0 * (h + 1) / n)


def _ada_kernel(c_ref, w_ref, b_ref, o_ref):
    cond = _silu(c_ref[...]).astype(BF16)
    o_ref[...] = _dot(cond, w_ref[...].astype(BF16)) + b_ref[...]


def _ada_mod(c, ada_w, ada_b):
    depth, d, n = ada_w.shape
    b = c.shape[0]
    tn = 1024
    return pl.pallas_call(
        _ada_kernel,
        grid=(depth, n // tn),
        in_specs=[
            pl.BlockSpec((b, d), lambda l, j: (0, 0)),
            pl.BlockSpec((None, d, tn), lambda l, j: (l, 0, j)),
            pl.BlockSpec((None, 1, tn), lambda l, j: (l, 0, j)),
        ],
        out_specs=pl.BlockSpec((None, b, tn), lambda l, j: (l, 0, j)),
        out_shape=jax.ShapeDtypeStruct((depth, b, n), F32),
        compiler_params=pltpu.CompilerParams(
            dimension_semantics=("arbitrary", "arbitrary"), vmem_limit_bytes=VMEM_LIMIT),
        name="ada_mod",
    )(c, ada_w, ada_b.reshape(depth, 1, n))


def _mod_spec(layer, k, d, grid_rank):
    if grid_rank == 2:
        return pl.BlockSpec((None, None, None, 1, d), lambda b, i: (layer, b, k, 0, 0))
    return pl.BlockSpec((None, None, None, 1, d), lambda b, i, j: (layer, b, k, 0, 0))


def _gain_spec(layer, k, d, grid_rank):
    if grid_rank == 2:
        return pl.BlockSpec((None, None, 1, d), lambda b, i: (layer, k, 0, 0))
    return pl.BlockSpec((None, None, 1, d), lambda b, i, j: (layer, k, 0, 0))


ROW_CHUNK = 16


def _row_chunks(n_rows, body):
    def step(i, carry):
        body(pl.ds(pl.multiple_of(i * ROW_CHUNK, ROW_CHUNK), ROW_CHUNK))
        return carry
    lax.fori_loop(0, n_rows // ROW_CHUNK, step, 0, unroll=True)


def _inv_rms_rows(x_ref, inv_ref):
    def body(r):
        x = x_ref[r, :]
        inv_ref[r, :] = lax.rsqrt(jnp.mean(x * x, axis=-1, keepdims=True) + EPS)
    _row_chunks(x_ref.shape[0], body)


def _prenorm_mod_rows(x_ref, gain_ref, shift_ref, inv_ref, h_ref):
    _inv_rms_rows(x_ref, inv_ref)

    def body(r):
        h_ref[r, :] = (x_ref[r, :] * inv_ref[r, :] * gain_ref[...] + shift_ref[...]).astype(h_ref.dtype)
    _row_chunks(x_ref.shape[0], body)


def _postnorm_residual_rows(y_ref, x_ref, gain_ref, inv_ref, o_ref):
    _inv_rms_rows(y_ref, inv_ref)

    def body(r):
        o_ref[r, :] = x_ref[r, :] + y_ref[r, :] * inv_ref[r, :] * gain_ref[...]
    _row_chunks(y_ref.shape[0], body)


def _ffn_kernel(res_w, tail, x_ref, sh_ref, sc_ref, gt_ref, gpre_ref, gpost_ref, wg_ref, wu_ref, wo_ref,
                o_ref, h_scr, gain_scr, inv_scr):
    f = pl.program_id(2)
    last = pl.num_programs(2) - 1
    tf = wo_ref.shape[1]

    def step(lo, first):
        h = h_scr[...]
        act = _silu(_dot(h, wg_ref[0, :, lo:])) * _dot(h, wu_ref[0, :, lo:])
        out = _dot(act.astype(BF16), wo_ref[0, lo:, :])
        if first:
            o_ref[...] = out
        else:
            o_ref[...] += out

    @pl.when(f == 0)
    def _():
        gain_scr[...] = gpre_ref[...] * (1.0 + sc_ref[...])
        _prenorm_mod_rows(x_ref, gain_scr, sh_ref, inv_scr, h_scr)
        step(0, True)

    @pl.when((f > 0) & (f < last))
    def _():
        step(0, False)

    @pl.when(f == last)
    def _():
        step(tf - tail, False)
        gain_scr[...] = res_w * gt_ref[...] * gpost_ref[...]
        _postnorm_residual_rows(o_ref, x_ref, gain_scr, inv_scr, o_ref)


def _ffn_block(x, mod, norm_g, layer, sub, w_in, w_out, res_w, tm, tf):
    b, s, d = x.shape
    d_ff = w_out.shape[1]
    nf = pl.cdiv(d_ff, tf)
    tail = d_ff - (nf - 1) * tf
    assert tail % LANES == 0 and d_ff >= tf and nf >= 2
    mk = 3 * sub
    start = lambda j, base: pl.multiple_of(base + jnp.minimum(j * tf, d_ff - tf), LANES)
    col = lambda base: (lambda b, i, j: (layer, 0, start(j, base)))
    one = pl.Element(1)
    return pl.pallas_call(
        functools.partial(_ffn_kernel, res_w, tail),
        grid=(b, s // tm, nf),
        in_specs=[
            pl.BlockSpec((None, tm, d), lambda b, i, j: (b, i, 0)),
            _mod_spec(layer, mk, d, 3), _mod_spec(layer, mk + 1, d, 3), _mod_spec(layer, mk + 2, d, 3),
            _gain_spec(layer, 2 * sub, d, 3), _gain_spec(layer, 2 * sub + 1, d, 3),
            pl.BlockSpec((one, pl.Element(d), pl.Element(tf)), col(0)),
            pl.BlockSpec((one, pl.Element(d), pl.Element(tf)), col(d_ff)),
            pl.BlockSpec((one, pl.Element(tf), pl.Element(d)), lambda b, i, j: (layer, start(j, 0), 0)),
        ],
        out_specs=pl.BlockSpec((None, tm, d), lambda b, i, j: (b, i, 0)),
        out_shape=jax.ShapeDtypeStruct((b, s, d), F32),
        scratch_shapes=[pltpu.VMEM((tm, d), BF16), pltpu.VMEM((1, d), F32), pltpu.VMEM((tm, 1), F32)],
        compiler_params=pltpu.CompilerParams(
            dimension_semantics=("parallel", "parallel", "arbitrary"), vmem_limit_bytes=FFN_VMEM_LIMIT),
        name=f"ffn_l{layer}_s{sub}",
    )(x, mod, mod, mod, norm_g, norm_g, w_in, w_in, w_out)


def _mix_in_kernel(n16, x_ref, sh_ref, sc_ref, gpre_ref, w_ref, o16_ref, o32_ref, h_scr, gain_scr, inv_scr):
    j = pl.program_id(2)

    @pl.when(j == 0)
    def _():
        gain_scr[...] = gpre_ref[...] * (1.0 + sc_ref[...])
        _prenorm_mod_rows(x_ref, gain_scr, sh_ref, inv_scr, h_scr)

    @pl.when(j < n16)
    def _():
        o16_ref[...] = _dot(h_scr[...], w_ref[...]).astype(o16_ref.dtype)

    @pl.when(j >= n16)
    def _():
        o32_ref[...] = _dot(h_scr[...], w_ref[...])


def _mix_in(x, mod, norm_g, layer, w, tm, tn):
    b, s, d = x.shape
    assert MIX_BF16_WIDTH % tn == 0 and MIX_F32_WIDTH % tn == 0
    n16, n32 = MIX_BF16_WIDTH // tn, MIX_F32_WIDTH // tn
    return pl.pallas_call(
        functools.partial(_mix_in_kernel, n16),
        grid=(b, s // tm, n16 + n32),
        in_specs=[
            pl.BlockSpec((None, tm, d), lambda b, i, j: (b, i, 0)),
            _mod_spec(layer, 3, d, 3), _mod_spec(layer, 4, d, 3),
            _gain_spec(layer, 2, d, 3),
            pl.BlockSpec((None, d, tn), lambda b, i, j: (layer, 0, j)),
        ],
        out_specs=[
            pl.BlockSpec((None, tm, tn), lambda b, i, j: (b, i, jnp.minimum(j, n16 - 1))),
            pl.BlockSpec((None, tm, tn), lambda b, i, j: (b, i, jnp.maximum(j - n16, 0))),
        ],
        out_shape=[jax.ShapeDtypeStruct((b, s, MIX_BF16_WIDTH), BF16),
                   jax.ShapeDtypeStruct((b, s, MIX_F32_WIDTH), F32)],
        scratch_shapes=[pltpu.VMEM((tm, d), BF16), pltpu.VMEM((1, d), F32), pltpu.VMEM((tm, 1), F32)],
        compiler_params=pltpu.CompilerParams(
            dimension_semantics=("parallel", "parallel", "arbitrary"), vmem_limit_bytes=VMEM_LIMIT),
        name=f"mix_in_l{layer}",
    )(x, mod, mod, norm_g, w)


def _lane_halves(x, first_half_holds_data):
    lane = lax.broadcasted_iota(jnp.int32, x.shape, 1)
    if first_half_holds_data:
        lo = jnp.where(lane < HEAD_DIM, x, 0.0)
        return lo, pltpu.roll(lo, HEAD_DIM, 1)
    hi = jnp.where(lane >= HEAD_DIM, x, 0.0)
    return pltpu.roll(hi, HEAD_DIM, 1), hi


def _swa_kernel(sinks_ref, q_ref, kvc_ref, kvp_ref, o_ref):
    n = pl.program_id(1)
    w = BLOCK
    bands = SWA_HEADS // SWA_KV_HEADS // 2
    rows = bands * w
    r = lax.broadcasted_iota(jnp.int32, (rows, w), 0)
    j = lax.broadcasted_iota(jnp.int32, (rows, w), 1)
    i = r & (w - 1)
    from_prev = j > i
    prev_f = jnp.where(from_prev, 1.0, 0.0)
    dist = i - j + jnp.where(from_prev, w, 0)
    no_key = jnp.where(from_prev & (n * w - w + j < 0), -jnp.inf, 0.0)
    band_step = _alibi_slope(2, SWA_HEADS) / _alibi_slope(0, SWA_HEADS)
    band_scale = jnp.ones((rows, w), F32)
    for band in range(1, bands):
        band_scale = jnp.where(r >= band * w, band_step ** band, band_scale)
    dist_scaled = dist.astype(F32) * band_scale
    band_col = lax.broadcasted_iota(jnp.int32, (rows, 1), 0)
    for bi in range(q_ref.shape[0]):
        kcat = jnp.concatenate([kvp_ref[bi, :, :KV_WIDTH], kvc_ref[bi, :, :KV_WIDTH]], axis=0)
        vcat = jnp.concatenate([kvp_ref[bi, :, KV_WIDTH:], kvc_ref[bi, :, KV_WIDTH:]], axis=0)
        for hk in range(SWA_KV_HEADS):
            k_halves = [t.astype(BF16) for t in _lane_halves(kcat, hk == 0)]
            v_halves = [t.astype(BF16) for t in _lane_halves(vcat, hk == 0)]
            q = jnp.concatenate([q_ref[bi, :, (bands * hk + band) * LANES:(bands * hk + band + 1) * LANES]
                                 for band in range(bands)], axis=0)
            q = (q * HEAD_DIM ** -0.5).astype(BF16)
            out = None
            for par in range(2):
                head0 = 2 * bands * hk + par
                sink = jnp.full((rows, 1), sinks_ref[head0], F32)
                for band in range(1, bands):
                    sink = jnp.where(band_col >= band * w, sinks_ref[head0 + 2 * band], sink)
                k_all, v_all = k_halves[par], v_halves[par]
                sc = jnp.where(from_prev, _dot_nt(q, k_all[:w]), _dot_nt(q, k_all[w:]))
                sc = sc - _alibi_slope(head0, SWA_HEADS) * dist_scaled + no_key
                m = jnp.maximum(jnp.max(sc, axis=-1, keepdims=True), sink)
                e = jnp.exp(sc - m)
                p = e / (jnp.sum(e, axis=-1, keepdims=True) + jnp.exp(sink - m))
                p_prev = p * prev_f
                pv = _dot(p_prev.astype(BF16), v_all[:w]) + _dot((p - p_prev).astype(BF16), v_all[w:])
                out = pv if out is None else out + pv
            for band in range(bands):
                tile = bands * hk + band
                o_ref[bi, :, tile * LANES:(tile + 1) * LANES] = out[band * w:(band + 1) * w].astype(o_ref.dtype)


SWA_BATCH = 8


def _swa(proj16, proj32, sinks, layer):
    b, s, _ = proj16.shape
    nb = s // BLOCK
    rb = math.gcd(b, SWA_BATCH)
    kv_blk = COL32_AKV // (2 * KV_WIDTH)
    return pl.pallas_call(
        _swa_kernel,
        grid=(b // rb, nb),
        in_specs=[
            pl.BlockSpec(memory_space=pltpu.SMEM),
            pl.BlockSpec((rb, BLOCK, A_WIDTH), lambda b, n: (b, n, COL_AQ // A_WIDTH)),
            pl.BlockSpec((rb, BLOCK, 2 * KV_WIDTH), lambda b, n: (b, n, kv_blk)),
            pl.BlockSpec((rb, BLOCK, 2 * KV_WIDTH), lambda b, n: (b, jnp.maximum(n - 1, 0), kv_blk)),
        ],
        out_specs=pl.BlockSpec((rb, BLOCK, A_WIDTH), lambda b, n: (b, n, 0)),
        out_shape=jax.ShapeDtypeStruct((b, s, A_WIDTH), BF16),
        compiler_params=pltpu.CompilerParams(
            dimension_semantics=("parallel", "arbitrary"), vmem_limit_bytes=VMEM_LIMIT),
        name=f"swa_l{layer}",
    )(sinks, proj16, proj32, proj32)


def _dsa_kernel(topk, q_ref, iq_ref, kv_ref, ikw_ref, kvn_ref, wuk_ref, wuv_ref, o_ref,
                ckv_scr, ik_scr, iqh_scr, score_scr, bias_scr, qlat_scr, s_scr, mrun_scr, m_scr,
                lpart_scr, acc_scr):
    n = pl.program_id(1)
    qb = DSA_QBLOCK
    kc = DSA_KCHUNK
    nchunks = lax.div(n * qb + qb - 1, kc) + 1

    @pl.when(n == 0)
    def _():
        ckv_scr[...] = _rms(kv_ref[...], kvn_ref[...]).astype(BF16)
        ik_scr[...] = ikw_ref[...].astype(BF16)

    row0 = pl.multiple_of(n * qb, qb)

    iw_t = ikw_ref[pl.ds(row0, qb), :].T
    iw_scale = IDX_HEADS ** -0.5 * IDX_DIM ** -0.5
    w_rows = [iw_t[IDX_DIM + h:IDX_DIM + h + 1, :] * iw_scale for h in range(IDX_HEADS)]
    lane = lax.broadcasted_iota(jnp.int32, (qb, LANES), 1)
    for h in range(IDX_HEADS):
        iq = iq_ref[:, (h // 2) * LANES:(h // 2 + 1) * LANES].astype(F32)
        if h % 2 == 0:
            iq = jnp.where(lane < IDX_DIM, iq, 0.0)
        else:
            iq = pltpu.roll(jnp.where(lane >= IDX_DIM, iq, 0.0), IDX_DIM, 1)
        iqh_scr[h] = iq.astype(BF16)

    key_row = lax.broadcasted_iota(jnp.int32, (kc, qb), 0)
    t_pos = n * qb + lax.broadcasted_iota(jnp.int32, (kc, qb), 1)

    def score_chunk(c, carry):
        off = pl.multiple_of(c * kc, kc)
        ik = ik_scr[pl.ds(off, kc), :]
        sc = jnp.zeros((kc, qb), F32)
        for h in range(IDX_HEADS):
            sc = sc + w_rows[h] * jnp.maximum(_dot_nt(ik, iqh_scr[h]), 0.0)
        score_scr[c] = jnp.where(off + key_row <= t_pos, sc, -jnp.inf)
        return carry

    lax.fori_loop(0, nchunks, score_chunk, 0)

    @pl.when(lax.rem(nchunks, 2) == 1)
    def _():
        score_scr[nchunks] = jnp.full((kc, qb), -jnp.inf, F32)

    npairs = lax.div(nchunks + 1, 2)

    def reduce_chunks(fn, init):
        def body(i, carry):
            for u in range(2):
                carry = fn(score_scr[2 * i + u], carry)
            return carry
        return lax.fori_loop(0, npairs, body, init)

    sub = 8

    def fold_rows(x, op):
        out = x[:sub]
        for g in range(1, x.shape[0] // sub):
            out = op(out, x[g * sub:(g + 1) * sub])
        return out

    def fold_lanes(x, op):
        out = x[:, :LANES]
        for t in range(1, x.shape[1] // LANES):
            out = op(out, x[:, t * LANES:(t + 1) * LANES])
        return out

    def count_ge(thr, strict=False):
        def body(x, cnt):
            hit = (x > thr) if strict else (x >= thr)
            return cnt + fold_rows(jnp.where(hit, 1.0, 0.0), jnp.add)
        return jnp.sum(reduce_chunks(body, jnp.zeros((sub, qb), F32)), axis=0, keepdims=True)

    def max_at_most(bound):
        def body(x, mx):
            return jnp.maximum(mx, fold_rows(jnp.where(x <= bound, x, -jnp.inf), jnp.maximum))
        return jnp.max(reduce_chunks(body, jnp.full((sub, qb), -jnp.inf, F32)), axis=0, keepdims=True)

    def store_bias(c, t, bias_t):
        for g in range(qb // LANES):
            bias_scr[c, g * LANES:(g + 1) * LANES, t * LANES:(t + 1) * LANES] = (
                bias_t[:, g * LANES:(g + 1) * LANES].T)

    @pl.when(n * qb + qb <= topk)
    def _():
        def body(c, carry):
            x = score_scr[c]
            for t in range(kc // LANES):
                store_bias(c, t, jnp.where(x[t * LANES:(t + 1) * LANES] == -jnp.inf, NEG, 0.0))
            return carry
        lax.fori_loop(0, nchunks, body, 0)

    @pl.when(n * qb + qb > topk)
    def _():
        kf = float(topk)

        def minmax(x, carry):
            mn, mx = carry
            mn = jnp.minimum(mn, fold_rows(jnp.where(x == -jnp.inf, jnp.inf, x), jnp.minimum))
            return mn, jnp.maximum(mx, fold_rows(x, jnp.maximum))

        lo, hi = reduce_chunks(minmax, (jnp.full((sub, qb), jnp.inf, F32), jnp.full((sub, qb), -jnp.inf, F32)))
        lo = jnp.min(lo, axis=0, keepdims=True)
        hi = jnp.max(hi, axis=0, keepdims=True)

        def bisect(_, carry):
            lo, hi = carry
            mid = 0.5 * (lo + hi)
            ge = count_ge(mid) >= kf
            return jnp.where(ge, mid, lo), jnp.where(ge, hi, mid)

        lo, hi = lax.fori_loop(0, BISECT_ITERS, bisect, (lo, hi))

        def walk_cond(carry):
            return carry[1] > 0.0

        def count_and_next(thr):
            def body(x, carry):
                cnt, mx = carry
                ge = x >= thr
                cnt = cnt + fold_rows(jnp.where(ge, 1.0, 0.0), jnp.add)
                return cnt, jnp.maximum(mx, fold_rows(jnp.where(ge, -jnp.inf, x), jnp.maximum))
            cnt, mx = reduce_chunks(body, (jnp.zeros((sub, qb), F32), jnp.full((sub, qb), -jnp.inf, F32)))
            return jnp.sum(cnt, axis=0, keepdims=True), jnp.max(mx, axis=0, keepdims=True)

        def walk(carry):
            thr, _ = carry
            cnt, below = count_and_next(thr)
            short = cnt < kf
            return jnp.where(short, below, thr), jnp.max(jnp.where(short, 1.0, 0.0))

        thr, _ = lax.while_loop(walk_cond, walk, (max_at_most(hi), jnp.float32(1.0)))

        need = kf - count_ge(thr, strict=True)

        r = lax.broadcasted_iota(jnp.int32, (LANES, LANES), 0)
        col = lax.broadcasted_iota(jnp.int32, (LANES, LANES), 1)
        tri = jnp.where(col <= r, 1.0, 0.0).astype(BF16)

        def select(c, seen):
            x = score_scr[c]
            for t in range(kc // LANES):
                xt = x[t * LANES:(t + 1) * LANES]
                eq = xt == thr
                eqf = jnp.where(eq, 1.0, 0.0)
                rank = seen + _dot(tri, eqf.astype(BF16))
                seen = seen + jnp.sum(eqf, axis=0, keepdims=True)
                sel = (xt > thr) | (eq & (rank <= need))
                store_bias(c, t, jnp.where(sel, 0.0, NEG))
            return seen

        lax.fori_loop(0, nchunks, select, jnp.zeros((1, qb), F32))

    for h in range(DSA_HEADS):
        tile = h // 2
        rows = slice(h * qb, (h + 1) * qb)
        q = q_ref[:, tile * LANES:(tile + 1) * LANES].astype(BF16)
        qlat_scr[rows, :] = (_dot(q, wuk_ref[h]) * HEAD_DIM ** -0.5).astype(BF16)
    mrun_scr[...] = jnp.full(mrun_scr.shape, NEG, F32)
    key_lane = lax.broadcasted_iota(jnp.int32, (1, kc), 1)

    def logits(c, carry):
        off = pl.multiple_of(c * kc, kc)
        s = _dot_nt(qlat_scr[...], ckv_scr[pl.ds(off, kc), :])
        bias = bias_scr[c]
        key_pos = (off + key_lane).astype(F32)
        for h in range(DSA_HEADS):
            rows = slice(h * qb, (h + 1) * qb)
            sh = s[rows] + (bias + _alibi_slope(h, DSA_HEADS) * key_pos)
            s_scr[c, rows, :] = sh
            mrun_scr[rows, :] = jnp.maximum(mrun_scr[rows, :], fold_lanes(sh, jnp.maximum))
        return carry

    lax.fori_loop(0, nchunks, logits, 0)
    m_scr[...] = jnp.broadcast_to(jnp.max(mrun_scr[...], axis=-1, keepdims=True), m_scr.shape)
    lpart_scr[...] = jnp.zeros_like(lpart_scr)
    acc_scr[...] = jnp.zeros_like(acc_scr)

    def attend(c, carry):
        off = pl.multiple_of(c * kc, kc)
        m = m_scr[...]
        p = [jnp.exp(s_scr[c, :, t * LANES:(t + 1) * LANES] - m) for t in range(kc // LANES)]
        lpart_scr[...] += functools.reduce(jnp.add, p)
        acc_scr[...] += _dot(jnp.concatenate(p, axis=1).astype(BF16), ckv_scr[pl.ds(off, kc), :])
        return carry

    lax.fori_loop(0, nchunks, attend, 0)

    o_all = (acc_scr[...] / jnp.sum(lpart_scr[...], axis=-1, keepdims=True)).astype(BF16)
    for tile in range(DSA_HEADS // 2):
        h = 2 * tile
        out = _dot(o_all[h * qb:(h + 1) * qb], wuv_ref[h]) + _dot(o_all[(h + 1) * qb:(h + 2) * qb], wuv_ref[h + 1])
        o_ref[:, tile * LANES:(tile + 1) * LANES] = out.astype(o_ref.dtype)


def _dsa(proj16, proj32, kv_norm, wuk_pad, wuv_pad, layer):
    b, s, _ = proj16.shape
    qb = DSA_QBLOCK
    nb = s // qb
    topk = min(DSA_TOPK_MAX, s // 4)
    assert topk % qb == 0
    assert (s // DSA_KCHUNK) % 2 == 0
    nck = s // DSA_KCHUNK
    rows = DSA_HEADS * qb
    return pl.pallas_call(
        functools.partial(_dsa_kernel, topk),
        grid=(b, nb),
        in_specs=[
            pl.BlockSpec((None, qb, B_WIDTH), lambda b, n: (b, n, COL_BQ // B_WIDTH)),
            pl.BlockSpec((None, qb, B_WIDTH), lambda b, n: (b, n, COL_BIQ // B_WIDTH)),
            pl.BlockSpec((None, s, DSA_RANK), lambda b, n: (b, 0, COL32_BKV // DSA_RANK)),
            pl.BlockSpec((None, s, LANES), lambda b, n: (b, 0, COL32_BIK // LANES)),
            pl.BlockSpec((1, DSA_RANK), lambda b, n: (0, 0)),
            pl.BlockSpec((DSA_HEADS, LANES, DSA_RANK), lambda b, n: (0, 0, 0)),
            pl.BlockSpec((DSA_HEADS, DSA_RANK, LANES), lambda b, n: (0, 0, 0)),
        ],
        out_specs=pl.BlockSpec((None, qb, B_WIDTH), lambda b, n: (b, n, 0)),
        out_shape=jax.ShapeDtypeStruct((b, s, B_WIDTH), BF16),
        scratch_shapes=[
            pltpu.VMEM((s, DSA_RANK), BF16),
            pltpu.VMEM((s, LANES), BF16),
            pltpu.VMEM((IDX_HEADS, qb, LANES), BF16),
            pltpu.VMEM((nck, DSA_KCHUNK, qb), F32),
            pltpu.VMEM((nck, qb, DSA_KCHUNK), F32),
            pltpu.VMEM((rows, DSA_RANK), BF16),
            pltpu.VMEM((nck, rows, DSA_KCHUNK), F32),
            pltpu.VMEM((rows, LANES), F32),
            pltpu.VMEM((rows, LANES), F32),
            pltpu.VMEM((rows, LANES), F32),
            pltpu.VMEM((rows, DSA_RANK), F32),
        ],
        compiler_params=pltpu.CompilerParams(
            dimension_semantics=("parallel", "arbitrary"), vmem_limit_bytes=VMEM_LIMIT),
        name=f"dsa_l{layer}",
    )(proj16, proj16, proj32, proj32, kv_norm, wuk_pad, wuv_pad)


def _ret_kernel(q_ref, k_ref, v_ref, g_ref, gn_ref, o_ref, state_scr):
    cs = BLOCK

    @pl.when(pl.program_id(1) == 0)
    def _():
        state_scr[...] = jnp.zeros_like(state_scr)

    row = lax.broadcasted_iota(jnp.int32, (cs, LANES), 0)
    lane = lax.broadcasted_iota(jnp.int32, (cs, LANES), 1)
    first = lane < HEAD_DIM
    rowf = row.astype(F32)
    diff = (row - lane).astype(F32)
    same_head = (row < HEAD_DIM) == first
    seg_mean = jnp.where(same_head, 1.0 / HEAD_DIM, 0.0).astype(BF16)

    def seg_mean_dot(x):
        x_hi = x.astype(BF16)
        x_lo = (x - x_hi.astype(F32)).astype(BF16)
        return _dot(x_hi, seg_mean) + _dot(x_lo, seg_mean)

    tiles = RET_HEADS // 2
    outs = []
    for bi in range(q_ref.shape[0]):
        for tile in range(tiles):
            cols = slice(tile * LANES, (tile + 1) * LANES)
            lg = [math.log(1.0 - 2.0 ** (-5.0 - (2 * tile + par))) for par in range(2)]
            lg_lane = jnp.where(first, lg[0], lg[1])
            q = q_ref[bi, :, cols].astype(BF16)
            k = k_ref[bi, :, cols].astype(F32) * HEAD_DIM ** -0.5
            v = v_ref[bi, :, cols].astype(F32)
            state = state_scr[bi, tile]
            out = _dot(q, state.astype(BF16)) * jnp.exp(lg_lane * (rowf + 1.0))
            for par in range(2):
                keep = first if par == 0 else ~first
                decay = jnp.where(diff >= 0, jnp.exp(lg[par] * jnp.maximum(diff, 0.0)), 0.0)
                inner = _dot_nt(q, jnp.where(keep, k, 0.0).astype(BF16)) * decay
                out = out + _dot(inner.astype(BF16), jnp.where(keep, v, 0.0).astype(BF16))
            k_dec = (k * jnp.exp(lg_lane * (cs - 1.0 - rowf))).astype(BF16)
            kv = _dot_tn(k_dec, v.astype(BF16))
            state_scr[bi, tile] = state * jnp.exp(lg_lane * cs) + jnp.where(same_head, kv, 0.0)
            outs.append(out)

    out = jnp.concatenate(outs, axis=0)
    mu = seg_mean_dot(out)
    cen = out - mu
    var = seg_mean_dot(cen * cen)
    yn = cen * lax.rsqrt(var + EPS)
    for bi in range(q_ref.shape[0]):
        for tile in range(tiles):
            cols = slice(tile * LANES, (tile + 1) * LANES)
            y = yn[(bi * tiles + tile) * cs:(bi * tiles + tile + 1) * cs] * gn_ref[:, cols]
            o_ref[bi, :, cols] = (y * _silu(g_ref[bi, :, cols])).astype(o_ref.dtype)


RET_BATCH = 8


def _retention(proj16, proj32, ret_norm, layer):
    b, s, _ = proj16.shape
    nc = s // BLOCK
    rb = math.gcd(b, RET_BATCH)
    spec = lambda col: pl.BlockSpec((rb, BLOCK, C_WIDTH), lambda b, c: (b, c, col // C_WIDTH))
    return pl.pallas_call(
        _ret_kernel,
        grid=(b // rb, nc),
        in_specs=[spec(COL_CQ), spec(COL32_CK), spec(COL_CV), spec(COL32_CG),
                  pl.BlockSpec((1, C_WIDTH), lambda b, c: (0, 0))],
        out_specs=pl.BlockSpec((rb, BLOCK, C_WIDTH), lambda b, c: (b, c, 0)),
        out_shape=jax.ShapeDtypeStruct((b, s, C_WIDTH), BF16),
        scratch_shapes=[pltpu.VMEM((rb, RET_HEADS // 2, LANES, LANES), F32)],
        compiler_params=pltpu.CompilerParams(
            dimension_semantics=("parallel", "arbitrary"), vmem_limit_bytes=VMEM_LIMIT),
        name=f"ret_l{layer}",
    )(proj16, proj32, proj16, proj32, ret_norm)


def _mix_out_kernel(x_ref, gt_ref, gpost_ref, oa_ref, ob_ref, oc_ref, w_ref, o_ref, gain_scr, inv_scr):
    y = _dot(oa_ref[...], w_ref[:A_WIDTH, :])
    y = y + _dot(ob_ref[...], w_ref[A_WIDTH:A_WIDTH + B_WIDTH, :])
    o_ref[...] = y + _dot(oc_ref[...], w_ref[A_WIDTH + B_WIDTH:, :])
    gain_scr[...] = gt_ref[...] * gpost_ref[...]
    _postnorm_residual_rows(o_ref, x_ref, gain_scr, inv_scr, o_ref)


def _mix_out(x, mod, norm_g, layer, oa, ob, oc, w, tm):
    b, s, d = x.shape
    row = lambda width: pl.BlockSpec((None, tm, width), lambda b, i: (b, i, 0))
    return pl.pallas_call(
        _mix_out_kernel,
        grid=(b, s // tm),
        in_specs=[
            row(d), _mod_spec(layer, 5, d, 2), _gain_spec(layer, 3, d, 2),
            row(A_WIDTH), row(B_WIDTH), row(C_WIDTH),
            pl.BlockSpec((None,) + w.shape[1:], lambda b, i: (layer, 0, 0)),
        ],
        out_specs=row(d),
        out_shape=jax.ShapeDtypeStruct((b, s, d), F32),
        scratch_shapes=[pltpu.VMEM((1, d), F32), pltpu.VMEM((tm, 1), F32)],
        compiler_params=pltpu.CompilerParams(
            dimension_semantics=("parallel", "parallel"), vmem_limit_bytes=VMEM_LIMIT),
        name=f"mix_out_l{layer}",
    )(x, mod, norm_g, oa, ob, oc, w)


def _prep_mix_in(w):
    w = w.astype(BF16)
    sizes = [A_WIDTH, KV_WIDTH, KV_WIDTH, B_WIDTH, DSA_RANK, IDX_HEADS * IDX_DIM, IDX_DIM, IDX_HEADS,
             C_WIDTH, C_WIDTH, C_WIDTH, C_WIDTH]
    starts = [0]
    for sz in sizes:
        starts.append(starts[-1] + sz)
    aq, ak, av, bq, bkv, biq, bik, biw, cq, ck, cv, cg = [w[..., a:a + sz] for a, sz in zip(starts, sizes)]
    tail = jnp.zeros(w.shape[:-1] + (LANES - IDX_DIM - IDX_HEADS,), w.dtype)
    return jnp.concatenate([aq, bq, biq, cq, cv, cg, ck, ak, av, bkv, bik, biw, tail], axis=-1)


def _prep_dsa_up(w_uk, w_uv):
    r, h, dh = w_uk.shape
    uk = jnp.transpose(w_uk, (1, 2, 0))
    uv = jnp.transpose(w_uv, (1, 0, 2))
    odd = (jnp.arange(h) % 2 == 1)[:, None, None]
    zk = jnp.zeros_like(uk)
    zv = jnp.zeros_like(uv)
    uk_pad = jnp.where(odd, jnp.concatenate([zk, uk], axis=1), jnp.concatenate([uk, zk], axis=1))
    uv_pad = jnp.where(odd, jnp.concatenate([zv, uv], axis=2), jnp.concatenate([uv, zv], axis=2))
    return uk_pad.astype(BF16), uv_pad.astype(BF16)


FFN_TM = 1024
FFN_TF = 512
FFN_VMEM_LIMIT = 60 * 1024 * 1024
MIX_IN_TM = 1024
MIX_IN_TN = 1536
MIX_OUT_TM = 512


def kernel(x, c, ada_w, ada_b, norm_g, ffn1_w_in, ffn1_w_out, ffn2_w_in, ffn2_w_out,
           mix_w_in, mix_w_out, swa_sinks, dsa_kv_norm, dsa_w_uk, dsa_w_uv, ret_norm):
    depth = ada_w.shape[0]
    b, s, d = x.shape
    assert s % FFN_TM == 0 and s % DSA_KCHUNK == 0 and d % LANES == 0
    mod = _ada_mod(c, ada_w, ada_b).reshape(depth, b, N_MOD, 1, d)
    gains = norm_g.reshape(depth, norm_g.shape[1], 1, d)
    ffn1_in, ffn1_out = ffn1_w_in.astype(BF16), ffn1_w_out.astype(BF16)
    ffn2_in, ffn2_out = ffn2_w_in.astype(BF16), ffn2_w_out.astype(BF16)
    mix_out_w = mix_w_out.astype(BF16)
    mix_in_w = _prep_mix_in(mix_w_in)
    for l in range(depth):
        x = _ffn_block(x, mod, gains, l, 0, ffn1_in, ffn1_out, 0.5, FFN_TM, FFN_TF)

        proj16, proj32 = _mix_in(x, mod, gains, l, mix_in_w, MIX_IN_TM, MIX_IN_TN)
        oa = _swa(proj16, proj32, swa_sinks[l], l)
        uk_pad, uv_pad = _prep_dsa_up(dsa_w_uk[l], dsa_w_uv[l])
        ob = _dsa(proj16, proj32, dsa_kv_norm[l].reshape(1, -1), uk_pad, uv_pad, l)
        oc = _retention(proj16, proj32, ret_norm[l].reshape(1, -1), l)
        x = _mix_out(x, mod, gains, l, oa, ob, oc, mix_out_w, MIX_OUT_TM)

        x = _ffn_block(x, mod, gains, l, 2, ffn2_in, ffn2_out, 0.5, FFN_TM, FFN_TF)
    return x
```

```python
import functools
import math

import jax
import jax.numpy as jnp
from jax import lax
from jax.experimental import pallas as pl
from jax.experimental.pallas import tpu as pltpu

F32 = jnp.float32
BF16 = jnp.bfloat16

LANES = 128
HEAD_DIM = 64
SWA_HEADS = 16
SWA_KV_HEADS = 2
BLOCK = 128
DSA_HEADS = 8
DSA_RANK = 128
IDX_HEADS = 8
IDX_DIM = 64
DSA_TOPK_MAX = 256
RET_HEADS = 8
N_MOD = 9
EPS = 1e-6
NEG = -1e30

A_WIDTH = SWA_HEADS * HEAD_DIM
B_WIDTH = DSA_HEADS * HEAD_DIM
C_WIDTH = RET_HEADS * HEAD_DIM
KV_WIDTH = SWA_KV_HEADS * HEAD_DIM

COL_AQ = 0
COL_BQ = COL_AQ + A_WIDTH
COL_BIQ = COL_BQ + B_WIDTH
COL_CQ = COL_BIQ + IDX_HEADS * IDX_DIM
COL_CV = COL_CQ + C_WIDTH
MIX_BF16_WIDTH = COL_CV + C_WIDTH
COL32_CG = 0
COL32_CK = COL32_CG + C_WIDTH
COL32_AKV = COL32_CK + C_WIDTH
COL32_BKV = COL32_AKV + 2 * KV_WIDTH
COL32_BIK = COL32_BKV + DSA_RANK
MIX_F32_WIDTH = COL32_BIK + LANES

DSA_QBLOCK = 256
DSA_KCHUNK = 256
BISECT_ITERS = 14
VMEM_LIMIT = 56 * 1024 * 1024


def _dot(a, b):
    return jnp.dot(a, b, preferred_element_type=F32)


def _dot_nt(a, b):
    return lax.dot_general(a, b, (((1,), (1,)), ((), ())), preferred_element_type=F32)


def _dot_tn(a, b):
    return lax.dot_general(a, b, (((0,), (0,)), ((), ())), preferred_element_type=F32)


def _silu(x):
    return x / (1.0 + jnp.exp(-x))


def _rms(x, g):
    return x * lax.rsqrt(jnp.mean(x * x, axis=-1, keepdims=True) + EPS) * g


def _alibi_slope(h, n):
    return 2.0 ** (-8.0 * (h + 1) / n)


def _ada_kernel(c_ref, w_ref, b_ref, o_ref):
    cond = _silu(c_ref[...]).astype(BF16)
    o_ref[...] = _dot(cond, w_ref[...].astype(BF16)) + b_ref[...]


def _ada_mod(c, ada_w, ada_b):
    depth, d, n = ada_w.shape
    b = c.shape[0]
    tn = 1024
    return pl.pallas_call(
        _ada_kernel,
        grid=(depth, n // tn),
        in_specs=[
            pl.BlockSpec((b, d), lambda l, j: (0, 0)),
            pl.BlockSpec((None, d, tn), lambda l, j: (l, 0, j)),
            pl.BlockSpec((None, 1, tn), lambda l, j: (l, 0, j)),
        ],
        out_specs=pl.BlockSpec((None, b, tn), lambda l, j: (l, 0, j)),
        out_shape=jax.ShapeDtypeStruct((depth, b, n), F32),
        compiler_params=pltpu.CompilerParams(
            dimension_semantics=("arbitrary", "arbitrary"), vmem_limit_bytes=VMEM_LIMIT),
        name="ada_mod",
    )(c, ada_w, ada_b.reshape(depth, 1, n))


def _mod_spec(layer, k, d, grid_rank):
    if grid_rank == 2:
        return pl.BlockSpec((None, None, None, 1, d), lambda b, i: (layer, b, k, 0, 0))
    return pl.BlockSpec((None, None, None, 1, d), lambda b, i, j: (layer, b, k, 0, 0))


def _gain_spec(layer, k, d, grid_rank):
    if grid_rank == 2:
        return pl.BlockSpec((None, None, 1, d), lambda b, i: (layer, k, 0, 0))
    return pl.BlockSpec((None, None, 1, d), lambda b, i, j: (layer, k, 0, 0))


ROW_CHUNK = 16


def _row_chunks(n_rows, body):
    def step(i, carry):
        body(pl.ds(pl.multiple_of(i * ROW_CHUNK, ROW_CHUNK), ROW_CHUNK))
        return carry
    lax.fori_loop(0, n_rows // ROW_CHUNK, step, 0, unroll=True)


def _inv_rms_rows(x_ref, inv_ref):
    def body(r):
        x = x_ref[r, :]
        inv_ref[r, :] = lax.rsqrt(jnp.mean(x * x, axis=-1, keepdims=True) + EPS)
    _row_chunks(x_ref.shape[0], body)


def _prenorm_mod_rows(x_ref, gain_ref, shift_ref, inv_ref, h_ref):
    _inv_rms_rows(x_ref, inv_ref)

    def body(r):
        h_ref[r, :] = (x_ref[r, :] * inv_ref[r, :] * gain_ref[...] + shift_ref[...]).astype(h_ref.dtype)
    _row_chunks(x_ref.shape[0], body)


def _postnorm_residual_rows(y_ref, x_ref, gain_ref, inv_ref, o_ref):
    _inv_rms_rows(y_ref, inv_ref)

    def body(r):
        o_ref[r, :] = x_ref[r, :] + y_ref[r, :] * inv_ref[r, :] * gain_ref[...]
    _row_chunks(y_ref.shape[0], body)


def _ffn_kernel(res_w, tail, x_ref, sh_ref, sc_ref, gt_ref, gpre_ref, gpost_ref, wg_ref, wu_ref, wo_ref,
                o_ref, h_scr, gain_scr, inv_scr):
    f = pl.program_id(2)
    last = pl.num_programs(2) - 1
    tf = wo_ref.shape[1]

    def step(lo, first, rows=slice(None)):
        h = h_scr[rows, :]
        act = _silu(_dot(h, wg_ref[0, :, lo:])) * _dot(h, wu_ref[0, :, lo:])
        out = _dot(act.astype(BF16), wo_ref[0, lo:, :])
        if first:
            o_ref[rows, :] = out
        else:
            o_ref[rows, :] += out

    @pl.when(f == 0)
    def _():
        gain_scr[...] = gpre_ref[...] * (1.0 + sc_ref[...])
        _prenorm_mod_rows(x_ref, gain_scr, sh_ref, inv_scr, h_scr)
        step(0, True)

    @pl.when((f > 0) & (f < last))
    def _():
        step(0, False)

    @pl.when(f == last)
    def _():
        gain_scr[...] = res_w * gt_ref[...] * gpost_ref[...]
        half = o_ref.shape[0] // 2
        for g in range(2):
            rows = pl.ds(g * half, half)
            step(tf - tail, False, rows)
            _postnorm_residual_rows(o_ref.at[rows], x_ref.at[rows], gain_scr, inv_scr.at[rows], o_ref.at[rows])


def _ffn_block(x, mod, norm_g, layer, sub, w_in, w_out, res_w, tm, tf):
    b, s, d = x.shape
    d_ff = w_out.shape[1]
    nf = pl.cdiv(d_ff, tf)
    tail = d_ff - (nf - 1) * tf
    assert tail % LANES == 0 and d_ff >= tf and nf >= 2
    mk = 3 * sub
    start = lambda j, base: pl.multiple_of(base + jnp.minimum(j * tf, d_ff - tf), LANES)
    col = lambda base: (lambda b, i, j: (layer, 0, start(j, base)))
    one = pl.Element(1)
    return pl.pallas_call(
        functools.partial(_ffn_kernel, res_w, tail),
        grid=(b, s // tm, nf),
        in_specs=[
            pl.BlockSpec((None, tm, d), lambda b, i, j: (b, i, 0)),
            _mod_spec(layer, mk, d, 3), _mod_spec(layer, mk + 1, d, 3), _mod_spec(layer, mk + 2, d, 3),
            _gain_spec(layer, 2 * sub, d, 3), _gain_spec(layer, 2 * sub + 1, d, 3),
            pl.BlockSpec((one, pl.Element(d), pl.Element(tf)), col(0)),
            pl.BlockSpec((one, pl.Element(d), pl.Element(tf)), col(d_ff)),
            pl.BlockSpec((one, pl.Element(tf), pl.Element(d)), lambda b, i, j: (layer, start(j, 0), 0)),
        ],
        out_specs=pl.BlockSpec((None, tm, d), lambda b, i, j: (b, i, 0)),
        out_shape=jax.ShapeDtypeStruct((b, s, d), F32),
        scratch_shapes=[pltpu.VMEM((tm, d), BF16), pltpu.VMEM((1, d), F32), pltpu.VMEM((tm, 1), F32)],
        compiler_params=pltpu.CompilerParams(
            dimension_semantics=("parallel", "parallel", "arbitrary"), vmem_limit_bytes=FFN_VMEM_LIMIT),
        name=f"ffn_l{layer}_s{sub}",
    )(x, mod, mod, mod, norm_g, norm_g, w_in, w_in, w_out)


def _mix_in_kernel(n16, x_ref, sh_ref, sc_ref, gpre_ref, w_ref, o16_ref, o32_ref, h_scr, gain_scr, inv_scr):
    j = pl.program_id(2)

    @pl.when(j == 0)
    def _():
        gain_scr[...] = gpre_ref[...] * (1.0 + sc_ref[...])
        _prenorm_mod_rows(x_ref, gain_scr, sh_ref, inv_scr, h_scr)

    @pl.when(j < n16)
    def _():
        o16_ref[...] = _dot(h_scr[...], w_ref[...]).astype(o16_ref.dtype)

    @pl.when(j >= n16)
    def _():
        o32_ref[...] = _dot(h_scr[...], w_ref[...])


def _mix_in(x, mod, norm_g, layer, w, tm, tn):
    b, s, d = x.shape
    assert MIX_BF16_WIDTH % tn == 0 and MIX_F32_WIDTH % tn == 0
    n16, n32 = MIX_BF16_WIDTH // tn, MIX_F32_WIDTH // tn
    return pl.pallas_call(
        functools.partial(_mix_in_kernel, n16),
        grid=(b, s // tm, n16 + n32),
        in_specs=[
            pl.BlockSpec((None, tm, d), lambda b, i, j: (b, i, 0)),
            _mod_spec(layer, 3, d, 3), _mod_spec(layer, 4, d, 3),
            _gain_spec(layer, 2, d, 3),
            pl.BlockSpec((None, d, tn), lambda b, i, j: (layer, 0, j)),
        ],
        out_specs=[
            pl.BlockSpec((None, tm, tn), lambda b, i, j: (b, i, jnp.minimum(j, n16 - 1))),
            pl.BlockSpec((None, tm, tn), lambda b, i, j: (b, i, jnp.maximum(j - n16, 0))),
        ],
        out_shape=[jax.ShapeDtypeStruct((b, s, MIX_BF16_WIDTH), BF16),
                   jax.ShapeDtypeStruct((b, s, MIX_F32_WIDTH), F32)],
        scratch_shapes=[pltpu.VMEM((tm, d), BF16), pltpu.VMEM((1, d), F32), pltpu.VMEM((tm, 1), F32)],
        compiler_params=pltpu.CompilerParams(
            dimension_semantics=("parallel", "parallel", "arbitrary"), vmem_limit_bytes=VMEM_LIMIT),
        name=f"mix_in_l{layer}",
    )(x, mod, mod, norm_g, w)


def _lane_halves(x, first_half_holds_data):
    lane = lax.broadcasted_iota(jnp.int32, x.shape, 1)
    if first_half_holds_data:
        lo = jnp.where(lane < HEAD_DIM, x, 0.0)
        return lo, pltpu.roll(lo, HEAD_DIM, 1)
    hi = jnp.where(lane >= HEAD_DIM, x, 0.0)
    return pltpu.roll(hi, HEAD_DIM, 1), hi


def _swa_kernel(sinks_ref, q_ref, kvc_ref, kvp_ref, o_ref):
    n = pl.program_id(1)
    w = BLOCK
    bands = SWA_HEADS // SWA_KV_HEADS // 2
    rows = bands * w
    r = lax.broadcasted_iota(jnp.int32, (rows, w), 0)
    j = lax.broadcasted_iota(jnp.int32, (rows, w), 1)
    i = r & (w - 1)
    from_prev = j > i
    prev_f = jnp.where(from_prev, 1.0, 0.0)
    dist = i - j + jnp.where(from_prev, w, 0)
    no_key = jnp.where(from_prev & (n * w - w + j < 0), -jnp.inf, 0.0)
    band_step = _alibi_slope(2, SWA_HEADS) / _alibi_slope(0, SWA_HEADS)
    band_scale = jnp.ones((rows, w), F32)
    for band in range(1, bands):
        band_scale = jnp.where(r >= band * w, band_step ** band, band_scale)
    dist_scaled = dist.astype(F32) * band_scale
    band_col = lax.broadcasted_iota(jnp.int32, (rows, 1), 0)
    for bi in range(q_ref.shape[0]):
        kcat = jnp.concatenate([kvp_ref[bi, :, :KV_WIDTH], kvc_ref[bi, :, :KV_WIDTH]], axis=0)
        vcat = jnp.concatenate([kvp_ref[bi, :, KV_WIDTH:], kvc_ref[bi, :, KV_WIDTH:]], axis=0)
        for hk in range(SWA_KV_HEADS):
            k_halves = [t.astype(BF16) for t in _lane_halves(kcat, hk == 0)]
            v_halves = [t.astype(BF16) for t in _lane_halves(vcat, hk == 0)]
            q = jnp.concatenate([q_ref[bi, :, (bands * hk + band) * LANES:(bands * hk + band + 1) * LANES]
                                 for band in range(bands)], axis=0)
            q = (q * HEAD_DIM ** -0.5).astype(BF16)
            out = None
            for par in range(2):
                head0 = 2 * bands * hk + par
                sink = jnp.full((rows, 1), sinks_ref[head0], F32)
                for band in range(1, bands):
                    sink = jnp.where(band_col >= band * w, sinks_ref[head0 + 2 * band], sink)
                k_all, v_all = k_halves[par], v_halves[par]
                sc = jnp.where(from_prev, _dot_nt(q, k_all[:w]), _dot_nt(q, k_all[w:]))
                sc = sc - _alibi_slope(head0, SWA_HEADS) * dist_scaled + no_key
                m = jnp.maximum(jnp.max(sc, axis=-1, keepdims=True), sink)
                e = jnp.exp(sc - m)
                p = e / (jnp.sum(e, axis=-1, keepdims=True) + jnp.exp(sink - m))
                p_prev = p * prev_f
                pv = _dot(p_prev.astype(BF16), v_all[:w]) + _dot((p - p_prev).astype(BF16), v_all[w:])
                out = pv if out is None else out + pv
            for band in range(bands):
                tile = bands * hk + band
                o_ref[bi, :, tile * LANES:(tile + 1) * LANES] = out[band * w:(band + 1) * w].astype(o_ref.dtype)


SWA_BATCH = 8


def _swa(proj16, proj32, sinks, layer):
    b, s, _ = proj16.shape
    nb = s // BLOCK
    rb = math.gcd(b, SWA_BATCH)
    kv_blk = COL32_AKV // (2 * KV_WIDTH)
    return pl.pallas_call(
        _swa_kernel,
        grid=(b // rb, nb),
        in_specs=[
            pl.BlockSpec(memory_space=pltpu.SMEM),
            pl.BlockSpec((rb, BLOCK, A_WIDTH), lambda b, n: (b, n, COL_AQ // A_WIDTH)),
            pl.BlockSpec((rb, BLOCK, 2 * KV_WIDTH), lambda b, n: (b, n, kv_blk)),
            pl.BlockSpec((rb, BLOCK, 2 * KV_WIDTH), lambda b, n: (b, jnp.maximum(n - 1, 0), kv_blk)),
        ],
        out_specs=pl.BlockSpec((rb, BLOCK, A_WIDTH), lambda b, n: (b, n, 0)),
        out_shape=jax.ShapeDtypeStruct((b, s, A_WIDTH), BF16),
        compiler_params=pltpu.CompilerParams(
            dimension_semantics=("parallel", "arbitrary"), vmem_limit_bytes=VMEM_LIMIT),
        name=f"swa_l{layer}",
    )(sinks, proj16, proj32, proj32)


def _dsa_kernel(topk, q_ref, iq_ref, kv_ref, ikw_ref, kvn_ref, wuk_ref, wuv_ref, o_ref,
                ckv_scr, ik_scr, iqh_scr, score_scr, bias_scr, qlat_scr, s_scr, mrun_scr, m_scr,
                lpart_scr, acc_scr):
    n = pl.program_id(1)
    qb = DSA_QBLOCK
    kc = DSA_KCHUNK
    nchunks = lax.div(n * qb + qb - 1, kc) + 1

    @pl.when(n == 0)
    def _():
        ckv_scr[...] = _rms(kv_ref[...], kvn_ref[...]).astype(BF16)
        ik_scr[...] = ikw_ref[...].astype(BF16)

    row0 = pl.multiple_of(n * qb, qb)

    iw_t = ikw_ref[pl.ds(row0, qb), :].T
    iw_scale = IDX_HEADS ** -0.5 * IDX_DIM ** -0.5
    w_rows = [iw_t[IDX_DIM + h:IDX_DIM + h + 1, :] * iw_scale for h in range(IDX_HEADS)]
    lane = lax.broadcasted_iota(jnp.int32, (qb, LANES), 1)
    for h in range(IDX_HEADS):
        iq = iq_ref[:, (h // 2) * LANES:(h // 2 + 1) * LANES].astype(F32)
        if h % 2 == 0:
            iq = jnp.where(lane < IDX_DIM, iq, 0.0)
        else:
            iq = pltpu.roll(jnp.where(lane >= IDX_DIM, iq, 0.0), IDX_DIM, 1)
        iqh_scr[h] = iq.astype(BF16)

    key_row = lax.broadcasted_iota(jnp.int32, (kc, qb), 0)
    t_pos = n * qb + lax.broadcasted_iota(jnp.int32, (kc, qb), 1)

    def score_chunk(c, carry):
        off = pl.multiple_of(c * kc, kc)
        ik = ik_scr[pl.ds(off, kc), :]
        sc = jnp.zeros((kc, qb), F32)
        for h in range(IDX_HEADS):
            sc = sc + w_rows[h] * jnp.maximum(_dot_nt(ik, iqh_scr[h]), 0.0)
        score_scr[c] = jnp.where(off + key_row <= t_pos, sc, -jnp.inf)
        return carry

    lax.fori_loop(0, nchunks, score_chunk, 0)

    @pl.when(lax.rem(nchunks, 2) == 1)
    def _():
        score_scr[nchunks] = jnp.full((kc, qb), -jnp.inf, F32)

    npairs = lax.div(nchunks + 1, 2)

    def reduce_chunks(fn, init):
        def body(i, carry):
            for u in range(2):
                carry = fn(score_scr[2 * i + u], carry)
            return carry
        return lax.fori_loop(0, npairs, body, init)

    sub = 8

    def fold_rows(x, op):
        out = x[:sub]
        for g in range(1, x.shape[0] // sub):
            out = op(out, x[g * sub:(g + 1) * sub])
        return out

    def fold_lanes(x, op):
        out = x[:, :LANES]
        for t in range(1, x.shape[1] // LANES):
            out = op(out, x[:, t * LANES:(t + 1) * LANES])
        return out

    def count_ge(thr, strict=False):
        def body(x, cnt):
            hit = (x > thr) if strict else (x >= thr)
            return cnt + fold_rows(jnp.where(hit, 1.0, 0.0), jnp.add)
        return jnp.sum(reduce_chunks(body, jnp.zeros((sub, qb), F32)), axis=0, keepdims=True)

    def max_at_most(bound):
        def body(x, mx):
            return jnp.maximum(mx, fold_rows(jnp.where(x <= bound, x, -jnp.inf), jnp.maximum))
        return jnp.max(reduce_chunks(body, jnp.full((sub, qb), -jnp.inf, F32)), axis=0, keepdims=True)

    def store_bias(c, t, bias_t):
        for g in range(qb // LANES):
            bias_scr[c, g * LANES:(g + 1) * LANES, t * LANES:(t + 1) * LANES] = (
                bias_t[:, g * LANES:(g + 1) * LANES].T)

    @pl.when(n * qb + qb <= topk)
    def _():
        def body(c, carry):
            x = score_scr[c]
            for t in range(kc // LANES):
                store_bias(c, t, jnp.where(x[t * LANES:(t + 1) * LANES] == -jnp.inf, NEG, 0.0))
            return carry
        lax.fori_loop(0, nchunks, body, 0)

    @pl.when(n * qb + qb > topk)
    def _():
        kf = float(topk)

        def minmax(x, carry):
            mn, mx = carry
            mn = jnp.minimum(mn, fold_rows(jnp.where(x == -jnp.inf, jnp.inf, x), jnp.minimum))
            return mn, jnp.maximum(mx, fold_rows(x, jnp.maximum))

        lo, hi = reduce_chunks(minmax, (jnp.full((sub, qb), jnp.inf, F32), jnp.full((sub, qb), -jnp.inf, F32)))
        lo = jnp.min(lo, axis=0, keepdims=True)
        hi = jnp.max(hi, axis=0, keepdims=True)

        def bisect(_, carry):
            lo, hi = carry
            mid = 0.5 * (lo + hi)
            ge = count_ge(mid) >= kf
            return jnp.where(ge, mid, lo), jnp.where(ge, hi, mid)

        lo, hi = lax.fori_loop(0, BISECT_ITERS, bisect, (lo, hi))

        def walk_cond(carry):
            return carry[1] > 0.0

        def count_and_next(thr):
            def body(x, carry):
                cnt, mx = carry
                ge = x >= thr
                cnt = cnt + fold_rows(jnp.where(ge, 1.0, 0.0), jnp.add)
                return cnt, jnp.maximum(mx, fold_rows(jnp.where(ge, -jnp.inf, x), jnp.maximum))
            cnt, mx = reduce_chunks(body, (jnp.zeros((sub, qb), F32), jnp.full((sub, qb), -jnp.inf, F32)))
            return jnp.sum(cnt, axis=0, keepdims=True), jnp.max(mx, axis=0, keepdims=True)

        def walk(carry):
            thr, _ = carry
            cnt, below = count_and_next(thr)
            short = cnt < kf
            return jnp.where(short, below, thr), jnp.max(jnp.where(short, 1.0, 0.0))

        thr, _ = lax.while_loop(walk_cond, walk, (max_at_most(hi), jnp.float32(1.0)))

        need = kf - count_ge(thr, strict=True)

        r = lax.broadcasted_iota(jnp.int32, (LANES, LANES), 0)
        col = lax.broadcasted_iota(jnp.int32, (LANES, LANES), 1)
        tri = jnp.where(col <= r, 1.0, 0.0).astype(BF16)

        def select(i, seen):
            for u in range(2):
                c = 2 * i + u
                x = score_scr[c]
                for t in range(kc // LANES):
                    xt = x[t * LANES:(t + 1) * LANES]
                    eq = xt == thr
                    eqf = jnp.where(eq, 1.0, 0.0)
                    rank = seen + _dot(tri, eqf.astype(BF16))
                    seen = seen + jnp.sum(eqf, axis=0, keepdims=True)
                    sel = (xt > thr) | (eq & (rank <= need))
                    store_bias(c, t, jnp.where(sel, 0.0, NEG))
            return seen

        lax.fori_loop(0, npairs, select, jnp.zeros((1, qb), F32))

    for h in range(DSA_HEADS):
        tile = h // 2
        rows = slice(h * qb, (h + 1) * qb)
        q = q_ref[:, tile * LANES:(tile + 1) * LANES].astype(BF16)
        qlat_scr[rows, :] = (_dot(q, wuk_ref[h]) * HEAD_DIM ** -0.5).astype(BF16)
    mrun_scr[...] = jnp.full(mrun_scr.shape, NEG, F32)
    key_lane = lax.broadcasted_iota(jnp.int32, (1, kc), 1)

    def logits(c, carry):
        off = pl.multiple_of(c * kc, kc)
        s = _dot_nt(qlat_scr[...], ckv_scr[pl.ds(off, kc), :])
        bias = bias_scr[c]
        key_pos = (off + key_lane).astype(F32)
        for h in range(DSA_HEADS):
            rows = slice(h * qb, (h + 1) * qb)
            sh = s[rows] + (bias + _alibi_slope(h, DSA_HEADS) * key_pos)
            s_scr[c, rows, :] = sh
            mrun_scr[rows, :] = jnp.maximum(mrun_scr[rows, :], fold_lanes(sh, jnp.maximum))
        return carry

    lax.fori_loop(0, nchunks, logits, 0)
    m_scr[...] = jnp.broadcast_to(jnp.max(mrun_scr[...], axis=-1, keepdims=True), m_scr.shape)
    lpart_scr[...] = jnp.zeros_like(lpart_scr)
    acc_scr[...] = jnp.zeros_like(acc_scr)

    def attend(c, carry):
        off = pl.multiple_of(c * kc, kc)
        m = m_scr[...]
        p = [jnp.exp(s_scr[c, :, t * LANES:(t + 1) * LANES] - m) for t in range(kc // LANES)]
        lpart_scr[...] += functools.reduce(jnp.add, p)
        acc_scr[...] += _dot(jnp.concatenate(p, axis=1).astype(BF16), ckv_scr[pl.ds(off, kc), :])
        return carry

    lax.fori_loop(0, nchunks, attend, 0)

    o_all = (acc_scr[...] / jnp.sum(lpart_scr[...], axis=-1, keepdims=True)).astype(BF16)
    for tile in range(DSA_HEADS // 2):
        h = 2 * tile
        out = _dot(o_all[h * qb:(h + 1) * qb], wuv_ref[h]) + _dot(o_all[(h + 1) * qb:(h + 2) * qb], wuv_ref[h + 1])
        o_ref[:, tile * LANES:(tile + 1) * LANES] = out.astype(o_ref.dtype)


def _dsa(proj16, proj32, kv_norm, wuk_pad, wuv_pad, layer):
    b, s, _ = proj16.shape
    qb = DSA_QBLOCK
    nb = s // qb
    topk = min(DSA_TOPK_MAX, s // 4)
    assert topk % qb == 0
    assert (s // DSA_KCHUNK) % 2 == 0
    nck = s // DSA_KCHUNK
    rows = DSA_HEADS * qb
    return pl.pallas_call(
        functools.partial(_dsa_kernel, topk),
        grid=(b, nb),
        in_specs=[
            pl.BlockSpec((None, qb, B_WIDTH), lambda b, n: (b, n, COL_BQ // B_WIDTH)),
            pl.BlockSpec((None, qb, B_WIDTH), lambda b, n: (b, n, COL_BIQ // B_WIDTH)),
            pl.BlockSpec((None, s, DSA_RANK), lambda b, n: (b, 0, COL32_BKV // DSA_RANK)),
            pl.BlockSpec((None, s, LANES), lambda b, n: (b, 0, COL32_BIK // LANES)),
            pl.BlockSpec((1, DSA_RANK), lambda b, n: (0, 0)),
            pl.BlockSpec((DSA_HEADS, LANES, DSA_RANK), lambda b, n: (0, 0, 0)),
            pl.BlockSpec((DSA_HEADS, DSA_RANK, LANES), lambda b, n: (0, 0, 0)),
        ],
        out_specs=pl.BlockSpec((None, qb, B_WIDTH), lambda b, n: (b, n, 0)),
        out_shape=jax.ShapeDtypeStruct((b, s, B_WIDTH), BF16),
        scratch_shapes=[
            pltpu.VMEM((s, DSA_RANK), BF16),
            pltpu.VMEM((s, LANES), BF16),
            pltpu.VMEM((IDX_HEADS, qb, LANES), BF16),
            pltpu.VMEM((nck, DSA_KCHUNK, qb), F32),
            pltpu.VMEM((nck, qb, DSA_KCHUNK), F32),
            pltpu.VMEM((rows, DSA_RANK), BF16),
            pltpu.VMEM((nck, rows, DSA_KCHUNK), F32),
            pltpu.VMEM((rows, LANES), F32),
            pltpu.VMEM((rows, LANES), F32),
            pltpu.VMEM((rows, LANES), F32),
            pltpu.VMEM((rows, DSA_RANK), F32),
        ],
        compiler_params=pltpu.CompilerParams(
            dimension_semantics=("parallel", "arbitrary"), vmem_limit_bytes=VMEM_LIMIT),
        name=f"dsa_l{layer}",
    )(proj16, proj16, proj32, proj32, kv_norm, wuk_pad, wuv_pad)


def _ret_kernel(q_ref, k_ref, v_ref, g_ref, gn_ref, o_ref, state_scr):
    cs = BLOCK

    @pl.when(pl.program_id(1) == 0)
    def _():
        state_scr[...] = jnp.zeros_like(state_scr)

    row = lax.broadcasted_iota(jnp.int32, (cs, LANES), 0)
    lane = lax.broadcasted_iota(jnp.int32, (cs, LANES), 1)
    first = lane < HEAD_DIM
    rowf = row.astype(F32)
    diff = (row - lane).astype(F32)
    same_head = (row < HEAD_DIM) == first
    seg_mean = jnp.where(same_head, 1.0 / HEAD_DIM, 0.0).astype(BF16)

    def seg_mean_dot(x):
        x_hi = x.astype(BF16)
        x_lo = (x - x_hi.astype(F32)).astype(BF16)
        return _dot(x_hi, seg_mean) + _dot(x_lo, seg_mean)

    tiles = RET_HEADS // 2
    outs = []
    for bi in range(q_ref.shape[0]):
        for tile in range(tiles):
            cols = slice(tile * LANES, (tile + 1) * LANES)
            lg = [math.log(1.0 - 2.0 ** (-5.0 - (2 * tile + par))) for par in range(2)]
            lg_lane = jnp.where(first, lg[0], lg[1])
            q = q_ref[bi, :, cols].astype(BF16)
            k = k_ref[bi, :, cols].astype(F32) * HEAD_DIM ** -0.5
            v = v_ref[bi, :, cols].astype(F32)
            state = state_scr[bi, tile]
            out = _dot(q, state.astype(BF16)) * jnp.exp(lg_lane * (rowf + 1.0))
            for par in range(2):
                keep = first if par == 0 else ~first
                decay = jnp.where(diff >= 0, jnp.exp(lg[par] * jnp.maximum(diff, 0.0)), 0.0)
                inner = _dot_nt(q, jnp.where(keep, k, 0.0).astype(BF16)) * decay
                out = out + _dot(inner.astype(BF16), jnp.where(keep, v, 0.0).astype(BF16))
            k_dec = (k * jnp.exp(lg_lane * (cs - 1.0 - rowf))).astype(BF16)
            kv = _dot_tn(k_dec, v.astype(BF16))
            state_scr[bi, tile] = state * jnp.exp(lg_lane * cs) + jnp.where(same_head, kv, 0.0)
            outs.append(out)

    out = jnp.concatenate(outs, axis=0)
    mu = seg_mean_dot(out)
    cen = out - mu
    var = seg_mean_dot(cen * cen)
    yn = cen * lax.rsqrt(var + EPS)
    for bi in range(q_ref.shape[0]):
        for tile in range(tiles):
            cols = slice(tile * LANES, (tile + 1) * LANES)
            y = yn[(bi * tiles + tile) * cs:(bi * tiles + tile + 1) * cs] * gn_ref[:, cols]
            o_ref[bi, :, cols] = (y * _silu(g_ref[bi, :, cols])).astype(o_ref.dtype)


RET_BATCH = 8


def _retention(proj16, proj32, ret_norm, layer):
    b, s, _ = proj16.shape
    nc = s // BLOCK
    rb = math.gcd(b, RET_BATCH)
    spec = lambda col: pl.BlockSpec((rb, BLOCK, C_WIDTH), lambda b, c: (b, c, col // C_WIDTH))
    return pl.pallas_call(
        _ret_kernel,
        grid=(b // rb, nc),
        in_specs=[spec(COL_CQ), spec(COL32_CK), spec(COL_CV), spec(COL32_CG),
                  pl.BlockSpec((1, C_WIDTH), lambda b, c: (0, 0))],
        out_specs=pl.BlockSpec((rb, BLOCK, C_WIDTH), lambda b, c: (b, c, 0)),
        out_shape=jax.ShapeDtypeStruct((b, s, C_WIDTH), BF16),
        scratch_shapes=[pltpu.VMEM((rb, RET_HEADS // 2, LANES, LANES), F32)],
        compiler_params=pltpu.CompilerParams(
            dimension_semantics=("parallel", "arbitrary"), vmem_limit_bytes=VMEM_LIMIT),
        name=f"ret_l{layer}",
    )(proj16, proj32, proj16, proj32, ret_norm)


def _mix_out_kernel(x_ref, gt_ref, gpost_ref, oa_ref, ob_ref, oc_ref, w_ref, o_ref, gain_scr, inv_scr):
    gain_scr[...] = gt_ref[...] * gpost_ref[...]
    half = o_ref.shape[0] // 2
    for g in range(2):
        rows = pl.ds(g * half, half)
        y = _dot(oa_ref[rows, :], w_ref[:A_WIDTH, :])
        y = y + _dot(ob_ref[rows, :], w_ref[A_WIDTH:A_WIDTH + B_WIDTH, :])
        o_ref[rows, :] = y + _dot(oc_ref[rows, :], w_ref[A_WIDTH + B_WIDTH:, :])
        _postnorm_residual_rows(o_ref.at[rows], x_ref.at[rows], gain_scr, inv_scr.at[rows], o_ref.at[rows])


def _mix_out(x, mod, norm_g, layer, oa, ob, oc, w, tm):
    b, s, d = x.shape
    row = lambda width: pl.BlockSpec((None, tm, width), lambda b, i: (b, i, 0))
    return pl.pallas_call(
        _mix_out_kernel,
        grid=(b, s // tm),
        in_specs=[
            row(d), _mod_spec(layer, 5, d, 2), _gain_spec(layer, 3, d, 2),
            row(A_WIDTH), row(B_WIDTH), row(C_WIDTH),
            pl.BlockSpec((None,) + w.shape[1:], lambda b, i: (layer, 0, 0)),
        ],
        out_specs=row(d),
        out_shape=jax.ShapeDtypeStruct((b, s, d), F32),
        scratch_shapes=[pltpu.VMEM((1, d), F32), pltpu.VMEM((tm, 1), F32)],
        compiler_params=pltpu.CompilerParams(
            dimension_semantics=("parallel", "parallel"), vmem_limit_bytes=VMEM_LIMIT),
        name=f"mix_out_l{layer}",
    )(x, mod, norm_g, oa, ob, oc, w)


def _prep_mix_in(w):
    w = w.astype(BF16)
    sizes = [A_WIDTH, KV_WIDTH, KV_WIDTH, B_WIDTH, DSA_RANK, IDX_HEADS * IDX_DIM, IDX_DIM, IDX_HEADS,
             C_WIDTH, C_WIDTH, C_WIDTH, C_WIDTH]
    starts = [0]
    for sz in sizes:
        starts.append(starts[-1] + sz)
    aq, ak, av, bq, bkv, biq, bik, biw, cq, ck, cv, cg = [w[..., a:a + sz] for a, sz in zip(starts, sizes)]
    tail = jnp.zeros(w.shape[:-1] + (LANES - IDX_DIM - IDX_HEADS,), w.dtype)
    return jnp.concatenate([aq, bq, biq, cq, cv, cg, ck, ak, av, bkv, bik, biw, tail], axis=-1)


def _prep_dsa_up(w_uk, w_uv):
    r, h, dh = w_uk.shape
    uk = jnp.transpose(w_uk, (1, 2, 0))
    uv = jnp.transpose(w_uv, (1, 0, 2))
    odd = (jnp.arange(h) % 2 == 1)[:, None, None]
    zk = jnp.zeros_like(uk)
    zv = jnp.zeros_like(uv)
    uk_pad = jnp.where(odd, jnp.concatenate([zk, uk], axis=1), jnp.concatenate([uk, zk], axis=1))
    uv_pad = jnp.where(odd, jnp.concatenate([zv, uv], axis=2), jnp.concatenate([uv, zv], axis=2))
    return uk_pad.astype(BF16), uv_pad.astype(BF16)


FFN_TM = 1024
FFN_TF = 512
FFN_VMEM_LIMIT = 60 * 1024 * 1024
MIX_IN_TM = 1024
MIX_IN_TN = 1536
MIX_OUT_TM = 512


def kernel(x, c, ada_w, ada_b, norm_g, ffn1_w_in, ffn1_w_out, ffn2_w_in, ffn2_w_out,
           mix_w_in, mix_w_out, swa_sinks, dsa_kv_norm, dsa_w_uk, dsa_w_uv, ret_norm):
    depth = ada_w.shape[0]
    b, s, d = x.shape
    assert s % FFN_TM == 0 and s % DSA_KCHUNK == 0 and d % LANES == 0
    mod = _ada_mod(c, ada_w, ada_b).reshape(depth, b, N_MOD, 1, d)
    gains = norm_g.reshape(depth, norm_g.shape[1], 1, d)
    ffn1_in, ffn1_out = ffn1_w_in.astype(BF16), ffn1_w_out.astype(BF16)
    ffn2_in, ffn2_out = ffn2_w_in.astype(BF16), ffn2_w_out.astype(BF16)
    mix_out_w = mix_w_out.astype(BF16)
    mix_in_w = _prep_mix_in(mix_w_in)
    for l in range(depth):
        x = _ffn_block(x, mod, gains, l, 0, ffn1_in, ffn1_out, 0.5, FFN_TM, FFN_TF)

        proj16, proj32 = _mix_in(x, mod, gains, l, mix_in_w, MIX_IN_TM, MIX_IN_TN)
        oa = _swa(proj16, proj32, swa_sinks[l], l)
        uk_pad, uv_pad = _prep_dsa_up(dsa_w_uk[l], dsa_w_uv[l])
        ob = _dsa(proj16, proj32, dsa_kv_norm[l].reshape(1, -1), uk_pad, uv_pad, l)
        oc = _retention(proj16, proj32, ret_norm[l].reshape(1, -1), l)
        x = _mix_out(x, mod, gains, l, oa, ob, oc, mix_out_w, MIX_OUT_TM)

        x = _ffn_block(x, mod, gains, l, 2, ffn2_in, ffn2_out, 0.5, FFN_TM, FFN_TF)
    return x
```

```python
import functools
import math

import jax
import jax.numpy as jnp
from jax import lax
from jax.experimental import pallas as pl
from jax.experimental.pallas import tpu as pltpu

F32 = jnp.float32
BF16 = jnp.bfloat16

LANES = 128
HEAD_DIM = 64
SWA_HEADS = 16
SWA_KV_HEADS = 2
BLOCK = 128
DSA_HEADS = 8
DSA_RANK = 128
IDX_HEADS = 8
IDX_DIM = 64
DSA_TOPK_MAX = 256
RET_HEADS = 8
N_MOD = 9
EPS = 1e-6
NEG = -1e30

A_WIDTH = SWA_HEADS * HEAD_DIM
B_WIDTH = DSA_HEADS * HEAD_DIM
C_WIDTH = RET_HEADS * HEAD_DIM
KV_WIDTH = SWA_KV_HEADS * HEAD_DIM

COL_AQ = 0
COL_BQ = COL_AQ + A_WIDTH
COL_BIQ = COL_BQ + B_WIDTH
COL_CQ = COL_BIQ + IDX_HEADS * IDX_DIM
COL_CV = COL_CQ + C_WIDTH
MIX_BF16_WIDTH = COL_CV + C_WIDTH
COL32_CG = 0
COL32_CK = COL32_CG + C_WIDTH
COL32_AKV = COL32_CK + C_WIDTH
COL32_BKV = COL32_AKV + 2 * KV_WIDTH
COL32_BIK = COL32_BKV + DSA_RANK
MIX_F32_WIDTH = COL32_BIK + LANES

DSA_QBLOCK = 256
DSA_KCHUNK = 256
BISECT_ITERS = 14
VMEM_LIMIT = 56 * 1024 * 1024


def _dot(a, b):
    return jnp.dot(a, b, preferred_element_type=F32)


def _dot_nt(a, b):
    return lax.dot_general(a, b, (((1,), (1,)), ((), ())), preferred_element_type=F32)


def _dot_tn(a, b):
    return lax.dot_general(a, b, (((0,), (0,)), ((), ())), preferred_element_type=F32)


def _silu(x):
    return x / (1.0 + jnp.exp(-x))


def _rms(x, g):
    return x * lax.rsqrt(jnp.mean(x * x, axis=-1, keepdims=True) + EPS) * g


def _alibi_slope(h, n):
    return 2.0 ** (-8.0 * (h + 1) / n)


def _ada_kernel(c_ref, w_ref, b_ref, o_ref):
    cond = _silu(c_ref[...]).astype(BF16)
    o_ref[...] = _dot(cond, w_ref[...].astype(BF16)) + b_ref[...]


def _ada_mod(c, ada_w, ada_b):
    depth, d, n = ada_w.shape
    b = c.shape[0]
    tn = 1024
    return pl.pallas_call(
        _ada_kernel,
        grid=(depth, n // tn),
        in_specs=[
            pl.BlockSpec((b, d), lambda l, j: (0, 0)),
            pl.BlockSpec((None, d, tn), lambda l, j: (l, 0, j)),
            pl.BlockSpec((None, 1, tn), lambda l, j: (l, 0, j)),
        ],
        out_specs=pl.BlockSpec((None, b, tn), lambda l, j: (l, 0, j)),
        out_shape=jax.ShapeDtypeStruct((depth, b, n), F32),
        compiler_params=pltpu.CompilerParams(
            dimension_semantics=("arbitrary", "arbitrary"), vmem_limit_bytes=VMEM_LIMIT),
        name="ada_mod",
    )(c, ada_w, ada_b.reshape(depth, 1, n))


def _mod_spec(layer, k, d, grid_rank):
    if grid_rank == 2:
        return pl.BlockSpec((None, None, None, 1, d), lambda b, i: (layer, b, k, 0, 0))
    return pl.BlockSpec((None, None, None, 1, d), lambda b, i, j: (layer, b, k, 0, 0))


def _gain_spec(layer, k, d, grid_rank):
    if grid_rank == 2:
        return pl.BlockSpec((None, None, 1, d), lambda b, i: (layer, k, 0, 0))
    return pl.BlockSpec((None, None, 1, d), lambda b, i, j: (layer, k, 0, 0))


ROW_CHUNK = 16


def _row_chunks(n_rows, body):
    def step(i, carry):
        body(pl.ds(pl.multiple_of(i * ROW_CHUNK, ROW_CHUNK), ROW_CHUNK))
        return carry
    lax.fori_loop(0, n_rows // ROW_CHUNK, step, 0, unroll=True)


def _inv_rms_rows(x_ref, inv_ref):
    def body(r):
        x = x_ref[r, :]
        inv_ref[r, :] = lax.rsqrt(jnp.mean(x * x, axis=-1, keepdims=True) + EPS)
    _row_chunks(x_ref.shape[0], body)


def _prenorm_mod_rows(x_ref, gain_ref, shift_ref, inv_ref, h_ref):
    _inv_rms_rows(x_ref, inv_ref)

    def body(r):
        h_ref[r, :] = (x_ref[r, :] * inv_ref[r, :] * gain_ref[...] + shift_ref[...]).astype(h_ref.dtype)
    _row_chunks(x_ref.shape[0], body)


def _postnorm_residual_rows(y_ref, x_ref, gain_ref, inv_ref, o_ref):
    _inv_rms_rows(y_ref, inv_ref)

    def body(r):
        o_ref[r, :] = x_ref[r, :] + y_ref[r, :] * inv_ref[r, :] * gain_ref[...]
    _row_chunks(y_ref.shape[0], body)


def _ffn_kernel(res_w, tail, x_ref, sh_ref, sc_ref, gt_ref, gpre_ref, gpost_ref, wg_ref, wu_ref, wo_ref,
                o_ref, h_scr, gain_scr, inv_scr):
    f = pl.program_id(2)
    last = pl.num_programs(2) - 1
    tf = wo_ref.shape[1]

    def step(lo, first):
        h = h_scr[...]
        act = _silu(_dot(h, wg_ref[0, :, lo:])) * _dot(h, wu_ref[0, :, lo:])
        out = _dot(act.astype(BF16), wo_ref[0, lo:, :])
        if first:
            o_ref[...] = out
        else:
            o_ref[...] += out

    @pl.when(f == 0)
    def _():
        gain_scr[...] = gpre_ref[...] * (1.0 + sc_ref[...])
        _prenorm_mod_rows(x_ref, gain_scr, sh_ref, inv_scr, h_scr)
        step(0, True)

    @pl.when((f > 0) & (f < last))
    def _():
        step(0, False)

    @pl.when(f == last)
    def _():
        step(tf - tail, False)
        gain_scr[...] = res_w * gt_ref[...] * gpost_ref[...]
        _postnorm_residual_rows(o_ref, x_ref, gain_scr, inv_scr, o_ref)


def _ffn_block(x, mod, norm_g, layer, sub, w_in, w_out, res_w, tm, tf):
    b, s, d = x.shape
    d_ff = w_out.shape[1]
    nf = pl.cdiv(d_ff, tf)
    tail = d_ff - (nf - 1) * tf
    assert tail % LANES == 0 and d_ff >= tf and nf >= 2
    mk = 3 * sub
    start = lambda j, base: pl.multiple_of(base + jnp.minimum(j * tf, d_ff - tf), LANES)
    col = lambda base: (lambda b, i, j: (layer, 0, start(j, base)))
    one = pl.Element(1)
    return pl.pallas_call(
        functools.partial(_ffn_kernel, res_w, tail),
        grid=(b, s // tm, nf),
        in_specs=[
            pl.BlockSpec((None, tm, d), lambda b, i, j: (b, i, 0)),
            _mod_spec(layer, mk, d, 3), _mod_spec(layer, mk + 1, d, 3), _mod_spec(layer, mk + 2, d, 3),
            _gain_spec(layer, 2 * sub, d, 3), _gain_spec(layer, 2 * sub + 1, d, 3),
            pl.BlockSpec((one, pl.Element(d), pl.Element(tf)), col(0)),
            pl.BlockSpec((one, pl.Element(d), pl.Element(tf)), col(d_ff)),
            pl.BlockSpec((one, pl.Element(tf), pl.Element(d)), lambda b, i, j: (layer, start(j, 0), 0)),
        ],
        out_specs=pl.BlockSpec((None, tm, d), lambda b, i, j: (b, i, 0)),
        out_shape=jax.ShapeDtypeStruct((b, s, d), F32),
        scratch_shapes=[pltpu.VMEM((tm, d), BF16), pltpu.VMEM((1, d), F32), pltpu.VMEM((tm, 1), F32)],
        compiler_params=pltpu.CompilerParams(
            dimension_semantics=("parallel", "parallel", "arbitrary"), vmem_limit_bytes=FFN_VMEM_LIMIT),
        name=f"ffn_l{layer}_s{sub}",
    )(x, mod, mod, mod, norm_g, norm_g, w_in, w_in, w_out)


def _mix_in_kernel(n16, x_ref, sh_ref, sc_ref, gpre_ref, w_ref, o16_ref, o32_ref, h_scr, gain_scr, inv_scr):
    j = pl.program_id(2)

    @pl.when(j == 0)
    def _():
        gain_scr[...] = gpre_ref[...] * (1.0 + sc_ref[...])
        _prenorm_mod_rows(x_ref, gain_scr, sh_ref, inv_scr, h_scr)

    @pl.when(j < n16)
    def _():
        o16_ref[...] = _dot(h_scr[...], w_ref[...]).astype(o16_ref.dtype)

    @pl.when(j >= n16)
    def _():
        o32_ref[...] = _dot(h_scr[...], w_ref[...])


def _mix_in(x, mod, norm_g, layer, w, tm, tn):
    b, s, d = x.shape
    assert MIX_BF16_WIDTH % tn == 0 and MIX_F32_WIDTH % tn == 0
    n16, n32 = MIX_BF16_WIDTH // tn, MIX_F32_WIDTH // tn
    return pl.pallas_call(
        functools.partial(_mix_in_kernel, n16),
        grid=(b, s // tm, n16 + n32),
        in_specs=[
            pl.BlockSpec((None, tm, d), lambda b, i, j: (b, i, 0)),
            _mod_spec(layer, 3, d, 3), _mod_spec(layer, 4, d, 3),
            _gain_spec(layer, 2, d, 3),
            pl.BlockSpec((None, d, tn), lambda b, i, j: (layer, 0, j)),
        ],
        out_specs=[
            pl.BlockSpec((None, tm, tn), lambda b, i, j: (b, i, jnp.minimum(j, n16 - 1))),
            pl.BlockSpec((None, tm, tn), lambda b, i, j: (b, i, jnp.maximum(j - n16, 0))),
        ],
        out_shape=[jax.ShapeDtypeStruct((b, s, MIX_BF16_WIDTH), BF16),
                   jax.ShapeDtypeStruct((b, s, MIX_F32_WIDTH), F32)],
        scratch_shapes=[pltpu.VMEM((tm, d), BF16), pltpu.VMEM((1, d), F32), pltpu.VMEM((tm, 1), F32)],
        compiler_params=pltpu.CompilerParams(
            dimension_semantics=("parallel", "parallel", "arbitrary"), vmem_limit_bytes=VMEM_LIMIT),
        name=f"mix_in_l{layer}",
    )(x, mod, mod, norm_g, w)


def _lane_halves(x, first_half_holds_data):
    lane = lax.broadcasted_iota(jnp.int32, x.shape, 1)
    if first_half_holds_data:
        lo = jnp.where(lane < HEAD_DIM, x, 0.0)
        return lo, pltpu.roll(lo, HEAD_DIM, 1)
    hi = jnp.where(lane >= HEAD_DIM, x, 0.0)
    return pltpu.roll(hi, HEAD_DIM, 1), hi


def _swa_kernel(sinks_ref, q_ref, kvc_ref, kvp_ref, o_ref):
    n = pl.program_id(1)
    w = BLOCK
    bands = SWA_HEADS // SWA_KV_HEADS // 2
    rows = bands * w
    r = lax.broadcasted_iota(jnp.int32, (rows, w), 0)
    j = lax.broadcasted_iota(jnp.int32, (rows, w), 1)
    i = r & (w - 1)
    from_prev = j > i
    prev_f = jnp.where(from_prev, 1.0, 0.0)
    dist = i - j + jnp.where(from_prev, w, 0)
    no_key = jnp.where(from_prev & (n * w - w + j < 0), -jnp.inf, 0.0)
    band_step = _alibi_slope(2, SWA_HEADS) / _alibi_slope(0, SWA_HEADS)
    band_scale = jnp.ones((rows, w), F32)
    for band in range(1, bands):
        band_scale = jnp.where(r >= band * w, band_step ** band, band_scale)
    dist_scaled = dist.astype(F32) * band_scale
    band_col = lax.broadcasted_iota(jnp.int32, (rows, 1), 0)
    for bi in range(q_ref.shape[0]):
        kcat = jnp.concatenate([kvp_ref[bi, :, :KV_WIDTH], kvc_ref[bi, :, :KV_WIDTH]], axis=0)
        vcat = jnp.concatenate([kvp_ref[bi, :, KV_WIDTH:], kvc_ref[bi, :, KV_WIDTH:]], axis=0)
        for hk in range(SWA_KV_HEADS):
            k_halves = [t.astype(BF16) for t in _lane_halves(kcat, hk == 0)]
            v_halves = [t.astype(BF16) for t in _lane_halves(vcat, hk == 0)]
            q = jnp.concatenate([q_ref[bi, :, (bands * hk + band) * LANES:(bands * hk + band + 1) * LANES]
                                 for band in range(bands)], axis=0)
            q = (q * HEAD_DIM ** -0.5).astype(BF16)
            out = None
            for par in range(2):
                head0 = 2 * bands * hk + par
                sink = jnp.full((rows, 1), sinks_ref[head0], F32)
                for band in range(1, bands):
                    sink = jnp.where(band_col >= band * w, sinks_ref[head0 + 2 * band], sink)
                k_all, v_all = k_halves[par], v_halves[par]
                sc = jnp.where(from_prev, _dot_nt(q, k_all[:w]), _dot_nt(q, k_all[w:]))
                sc = sc - _alibi_slope(head0, SWA_HEADS) * dist_scaled + no_key
                m = jnp.maximum(jnp.max(sc, axis=-1, keepdims=True), sink)
                e = jnp.exp(sc - m)
                p = e / (jnp.sum(e, axis=-1, keepdims=True) + jnp.exp(sink - m))
                p_prev = p * prev_f
                pv = _dot(p_prev.astype(BF16), v_all[:w]) + _dot((p - p_prev).astype(BF16), v_all[w:])
                out = pv if out is None else out + pv
            for band in range(bands):
                tile = bands * hk + band
                o_ref[bi, :, tile * LANES:(tile + 1) * LANES] = out[band * w:(band + 1) * w].astype(o_ref.dtype)


SWA_BATCH = 8


def _swa(proj16, proj32, sinks, layer):
    b, s, _ = proj16.shape
    nb = s // BLOCK
    rb = math.gcd(b, SWA_BATCH)
    kv_blk = COL32_AKV // (2 * KV_WIDTH)
    return pl.pallas_call(
        _swa_kernel,
        grid=(b // rb, nb),
        in_specs=[
            pl.BlockSpec(memory_space=pltpu.SMEM),
            pl.BlockSpec((rb, BLOCK, A_WIDTH), lambda b, n: (b, n, COL_AQ // A_WIDTH)),
            pl.BlockSpec((rb, BLOCK, 2 * KV_WIDTH), lambda b, n: (b, n, kv_blk)),
            pl.BlockSpec((rb, BLOCK, 2 * KV_WIDTH), lambda b, n: (b, jnp.maximum(n - 1, 0), kv_blk)),
        ],
        out_specs=pl.BlockSpec((rb, BLOCK, A_WIDTH), lambda b, n: (b, n, 0)),
        out_shape=jax.ShapeDtypeStruct((b, s, A_WIDTH), BF16),
        compiler_params=pltpu.CompilerParams(
            dimension_semantics=("parallel", "arbitrary"), vmem_limit_bytes=VMEM_LIMIT),
        name=f"swa_l{layer}",
    )(sinks, proj16, proj32, proj32)


def _dsa_kernel(topk, q_ref, iq_ref, kv_ref, ikw_ref, kvn_ref, wuk_ref, wuv_ref, o_ref,
                ckv_scr, ik_scr, iqh_scr, score_scr, bias_scr, qlat_scr, s_scr, mrun_scr, m_scr,
                lpart_scr, acc_scr):
    n = pl.program_id(1)
    qb = DSA_QBLOCK
    kc = DSA_KCHUNK
    nchunks = lax.div(n * qb + qb - 1, kc) + 1

    @pl.when(n == 0)
    def _():
        ckv_scr[...] = _rms(kv_ref[...], kvn_ref[...]).astype(BF16)
        ik_scr[...] = ikw_ref[...].astype(BF16)

    row0 = pl.multiple_of(n * qb, qb)

    iw_t = ikw_ref[pl.ds(row0, qb), :].T
    iw_scale = IDX_HEADS ** -0.5 * IDX_DIM ** -0.5
    w_rows = [iw_t[IDX_DIM + h:IDX_DIM + h + 1, :] * iw_scale for h in range(IDX_HEADS)]
    lane = lax.broadcasted_iota(jnp.int32, (qb, LANES), 1)
    for h in range(IDX_HEADS):
        iq = iq_ref[:, (h // 2) * LANES:(h // 2 + 1) * LANES].astype(F32)
        if h % 2 == 0:
            iq = jnp.where(lane < IDX_DIM, iq, 0.0)
        else:
            iq = pltpu.roll(jnp.where(lane >= IDX_DIM, iq, 0.0), IDX_DIM, 1)
        iqh_scr[h] = iq.astype(BF16)

    key_row = lax.broadcasted_iota(jnp.int32, (kc, qb), 0)
    t_pos = n * qb + lax.broadcasted_iota(jnp.int32, (kc, qb), 1)

    def score_chunk(c, carry):
        off = pl.multiple_of(c * kc, kc)
        ik = ik_scr[pl.ds(off, kc), :]
        sc = jnp.zeros((kc, qb), F32)
        for h in range(IDX_HEADS):
            sc = sc + w_rows[h] * jnp.maximum(_dot_nt(ik, iqh_scr[h]), 0.0)
        score_scr[c] = jnp.where(off + key_row <= t_pos, sc, -jnp.inf)
        return carry

    lax.fori_loop(0, nchunks, score_chunk, 0)

    @pl.when(lax.rem(nchunks, 2) == 1)
    def _():
        score_scr[nchunks] = jnp.full((kc, qb), -jnp.inf, F32)

    npairs = lax.div(nchunks + 1, 2)

    def reduce_chunks(fn, init):
        def body(i, carry):
            for u in range(2):
                carry = fn(score_scr[2 * i + u], carry)
            return carry
        return lax.fori_loop(0, npairs, body, init)

    sub = 8

    def fold_rows(x, op):
        out = x[:sub]
        for g in range(1, x.shape[0] // sub):
            out = op(out, x[g * sub:(g + 1) * sub])
        return out

    def fold_lanes(x, op):
        out = x[:, :LANES]
        for t in range(1, x.shape[1] // LANES):
            out = op(out, x[:, t * LANES:(t + 1) * LANES])
        return out

    def count_ge(thr, strict=False):
        def body(x, cnt):
            hit = (x > thr) if strict else (x >= thr)
            return cnt + fold_rows(jnp.where(hit, 1.0, 0.0), jnp.add)
        return jnp.sum(reduce_chunks(body, jnp.zeros((sub, qb), F32)), axis=0, keepdims=True)

    def max_at_most(bound):
        def body(x, mx):
            return jnp.maximum(mx, fold_rows(jnp.where(x <= bound, x, -jnp.inf), jnp.maximum))
        return jnp.max(reduce_chunks(body, jnp.full((sub, qb), -jnp.inf, F32)), axis=0, keepdims=True)

    def store_bias(c, t, bias_t):
        for g in range(qb // LANES):
            bias_scr[c, g * LANES:(g + 1) * LANES, t * LANES:(t + 1) * LANES] = (
                bias_t[:, g * LANES:(g + 1) * LANES].T)

    @pl.when(n * qb + qb <= topk)
    def _():
        def body(c, carry):
            x = score_scr[c]
            for t in range(kc // LANES):
                store_bias(c, t, jnp.where(x[t * LANES:(t + 1) * LANES] == -jnp.inf, NEG, 0.0))
            return carry
        lax.fori_loop(0, nchunks, body, 0)

    @pl.when(n * qb + qb > topk)
    def _():
        kf = float(topk)

        def minmax(x, carry):
            mn, mx = carry
            mn = jnp.minimum(mn, fold_rows(jnp.where(x == -jnp.inf, jnp.inf, x), jnp.minimum))
            return mn, jnp.maximum(mx, fold_rows(x, jnp.maximum))

        lo, hi = reduce_chunks(minmax, (jnp.full((sub, qb), jnp.inf, F32), jnp.full((sub, qb), -jnp.inf, F32)))
        lo = jnp.min(lo, axis=0, keepdims=True)
        hi = jnp.max(hi, axis=0, keepdims=True)

        def bisect(_, carry):
            lo, hi = carry
            mid = 0.5 * (lo + hi)
            ge = count_ge(mid) >= kf
            return jnp.where(ge, mid, lo), jnp.where(ge, hi, mid)

        lo, hi = lax.fori_loop(0, BISECT_ITERS, bisect, (lo, hi))

        def walk_cond(carry):
            return carry[1] > 0.0

        def count_and_next(thr):
            def body(x, carry):
                cnt, mx = carry
                ge = x >= thr
                cnt = cnt + fold_rows(jnp.where(ge, 1.0, 0.0), jnp.add)
                return cnt, jnp.maximum(mx, fold_rows(jnp.where(ge, -jnp.inf, x), jnp.maximum))
            cnt, mx = reduce_chunks(body, (jnp.zeros((sub, qb), F32), jnp.full((sub, qb), -jnp.inf, F32)))
            return jnp.sum(cnt, axis=0, keepdims=True), jnp.max(mx, axis=0, keepdims=True)

        def walk(carry):
            thr, _ = carry
            cnt, below = count_and_next(thr)
            short = cnt < kf
            return jnp.where(short, below, thr), jnp.max(jnp.where(short, 1.0, 0.0))

        thr, _ = lax.while_loop(walk_cond, walk, (max_at_most(hi), jnp.float32(1.0)))

        need = kf - count_ge(thr, strict=True)

        r = lax.broadcasted_iota(jnp.int32, (LANES, LANES), 0)
        col = lax.broadcasted_iota(jnp.int32, (LANES, LANES), 1)
        tri = jnp.where(col <= r, 1.0, 0.0).astype(BF16)

        def select(i, seen):
            for u in range(2):
                c = 2 * i + u
                x = score_scr[c]
                for t in range(kc // LANES):
                    xt = x[t * LANES:(t + 1) * LANES]
                    eq = xt == thr
                    eqf = jnp.where(eq, 1.0, 0.0)
                    rank = seen + _dot(tri, eqf.astype(BF16))
                    seen = seen + jnp.sum(eqf, axis=0, keepdims=True)
                    sel = (xt > thr) | (eq & (rank <= need))
                    store_bias(c, t, jnp.where(sel, 0.0, NEG))
            return seen

        lax.fori_loop(0, npairs, select, jnp.zeros((1, qb), F32))

    for h in range(DSA_HEADS):
        tile = h // 2
        rows = slice(h * qb, (h + 1) * qb)
        q = q_ref[:, tile * LANES:(tile + 1) * LANES].astype(BF16)
        qlat_scr[rows, :] = (_dot(q, wuk_ref[h]) * HEAD_DIM ** -0.5).astype(BF16)
    mrun_scr[...] = jnp.full(mrun_scr.shape, NEG, F32)
    key_lane = lax.broadcasted_iota(jnp.int32, (1, kc), 1)

    def logits(c, carry):
        off = pl.multiple_of(c * kc, kc)
        s = _dot_nt(qlat_scr[...], ckv_scr[pl.ds(off, kc), :])
        bias = bias_scr[c]
        key_pos = (off + key_lane).astype(F32)
        for h in range(DSA_HEADS):
            rows = slice(h * qb, (h + 1) * qb)
            sh = s[rows] + (bias + _alibi_slope(h, DSA_HEADS) * key_pos)
            s_scr[c, rows, :] = sh
            mrun_scr[rows, :] = jnp.maximum(mrun_scr[rows, :], fold_lanes(sh, jnp.maximum))
        return carry

    lax.fori_loop(0, nchunks, logits, 0)
    m_scr[...] = jnp.broadcast_to(jnp.max(mrun_scr[...], axis=-1, keepdims=True), m_scr.shape)
    lpart_scr[...] = jnp.zeros_like(lpart_scr)
    acc_scr[...] = jnp.zeros_like(acc_scr)

    def attend(c, carry):
        off = pl.multiple_of(c * kc, kc)
        m = m_scr[...]
        p = [jnp.exp(s_scr[c, :, t * LANES:(t + 1) * LANES] - m) for t in range(kc // LANES)]
        lpart_scr[...] += functools.reduce(jnp.add, p)
        acc_scr[...] += _dot(jnp.concatenate(p, axis=1).astype(BF16), ckv_scr[pl.ds(off, kc), :])
        return carry

    lax.fori_loop(0, nchunks, attend, 0)

    o_all = (acc_scr[...] / jnp.sum(lpart_scr[...], axis=-1, keepdims=True)).astype(BF16)
    for tile in range(DSA_HEADS // 2):
        h = 2 * tile
        out = _dot(o_all[h * qb:(h + 1) * qb], wuv_ref[h]) + _dot(o_all[(h + 1) * qb:(h + 2) * qb], wuv_ref[h + 1])
        o_ref[:, tile * LANES:(tile + 1) * LANES] = out.astype(o_ref.dtype)


def _dsa(proj16, proj32, kv_norm, wuk_pad, wuv_pad, layer):
    b, s, _ = proj16.shape
    qb = DSA_QBLOCK
    nb = s // qb
    topk = min(DSA_TOPK_MAX, s // 4)
    assert topk % qb == 0
    assert (s // DSA_KCHUNK) % 2 == 0
    nck = s // DSA_KCHUNK
    rows = DSA_HEADS * qb
    return pl.pallas_call(
        functools.partial(_dsa_kernel, topk),
        grid=(b, nb),
        in_specs=[
            pl.BlockSpec((None, qb, B_WIDTH), lambda b, n: (b, n, COL_BQ // B_WIDTH)),
            pl.BlockSpec((None, qb, B_WIDTH), lambda b, n: (b, n, COL_BIQ // B_WIDTH)),
            pl.BlockSpec((None, s, DSA_RANK), lambda b, n: (b, 0, COL32_BKV // DSA_RANK)),
            pl.BlockSpec((None, s, LANES), lambda b, n: (b, 0, COL32_BIK // LANES)),
            pl.BlockSpec((1, DSA_RANK), lambda b, n: (0, 0)),
            pl.BlockSpec((DSA_HEADS, LANES, DSA_RANK), lambda b, n: (0, 0, 0)),
            pl.BlockSpec((DSA_HEADS, DSA_RANK, LANES), lambda b, n: (0, 0, 0)),
        ],
        out_specs=pl.BlockSpec((None, qb, B_WIDTH), lambda b, n: (b, n, 0)),
        out_shape=jax.ShapeDtypeStruct((b, s, B_WIDTH), BF16),
        scratch_shapes=[
            pltpu.VMEM((s, DSA_RANK), BF16),
            pltpu.VMEM((s, LANES), BF16),
            pltpu.VMEM((IDX_HEADS, qb, LANES), BF16),
            pltpu.VMEM((nck, DSA_KCHUNK, qb), F32),
            pltpu.VMEM((nck, qb, DSA_KCHUNK), F32),
            pltpu.VMEM((rows, DSA_RANK), BF16),
            pltpu.VMEM((nck, rows, DSA_KCHUNK), F32),
            pltpu.VMEM((rows, LANES), F32),
            pltpu.VMEM((rows, LANES), F32),
            pltpu.VMEM((rows, LANES), F32),
            pltpu.VMEM((rows, DSA_RANK), F32),
        ],
        compiler_params=pltpu.CompilerParams(
            dimension_semantics=("parallel", "arbitrary"), vmem_limit_bytes=VMEM_LIMIT),
        name=f"dsa_l{layer}",
    )(proj16, proj16, proj32, proj32, kv_norm, wuk_pad, wuv_pad)


def _ret_kernel(q_ref, k_ref, v_ref, g_ref, gn_ref, o_ref, state_scr):
    cs = BLOCK

    @pl.when(pl.program_id(1) == 0)
    def _():
        state_scr[...] = jnp.zeros_like(state_scr)

    row = lax.broadcasted_iota(jnp.int32, (cs, LANES), 0)
    lane = lax.broadcasted_iota(jnp.int32, (cs, LANES), 1)
    first = lane < HEAD_DIM
    rowf = row.astype(F32)
    diff = (row - lane).astype(F32)
    same_head = (row < HEAD_DIM) == first
    seg_mean = jnp.where(same_head, 1.0 / HEAD_DIM, 0.0).astype(BF16)

    def seg_mean_dot(x):
        x_hi = x.astype(BF16)
        x_lo = (x - x_hi.astype(F32)).astype(BF16)
        return _dot(x_hi, seg_mean) + _dot(x_lo, seg_mean)

    tiles = RET_HEADS // 2
    outs = []
    for bi in range(q_ref.shape[0]):
        for tile in range(tiles):
            cols = slice(tile * LANES, (tile + 1) * LANES)
            lg = [math.log(1.0 - 2.0 ** (-5.0 - (2 * tile + par))) for par in range(2)]
            lg_lane = jnp.where(first, lg[0], lg[1])
            q = q_ref[bi, :, cols].astype(BF16)
            k = k_ref[bi, :, cols].astype(F32) * HEAD_DIM ** -0.5
            v = v_ref[bi, :, cols].astype(F32)
            state = state_scr[bi, tile]
            out = _dot(q, state.astype(BF16)) * jnp.exp(lg_lane * (rowf + 1.0))
            for par in range(2):
                keep = first if par == 0 else ~first
                decay = jnp.where(diff >= 0, jnp.exp(lg[par] * jnp.maximum(diff, 0.0)), 0.0)
                inner = _dot_nt(q, jnp.where(keep, k, 0.0).astype(BF16)) * decay
                out = out + _dot(inner.astype(BF16), jnp.where(keep, v, 0.0).astype(BF16))
            k_dec = (k * jnp.exp(lg_lane * (cs - 1.0 - rowf))).astype(BF16)
            kv = _dot_tn(k_dec, v.astype(BF16))
            state_scr[bi, tile] = state * jnp.exp(lg_lane * cs) + jnp.where(same_head, kv, 0.0)
            outs.append(out)

    out = jnp.concatenate(outs, axis=0)
    mu = seg_mean_dot(out)
    cen = out - mu
    var = seg_mean_dot(cen * cen)
    yn = cen * lax.rsqrt(var + EPS)
    for bi in range(q_ref.shape[0]):
        for tile in range(tiles):
            cols = slice(tile * LANES, (tile + 1) * LANES)
            y = yn[(bi * tiles + tile) * cs:(bi * tiles + tile + 1) * cs] * gn_ref[:, cols]
            o_ref[bi, :, cols] = (y * _silu(g_ref[bi, :, cols])).astype(o_ref.dtype)


RET_BATCH = 8


def _retention(proj16, proj32, ret_norm, layer):
    b, s, _ = proj16.shape
    nc = s // BLOCK
    rb = math.gcd(b, RET_BATCH)
    spec = lambda col: pl.BlockSpec((rb, BLOCK, C_WIDTH), lambda b, c: (b, c, col // C_WIDTH))
    return pl.pallas_call(
        _ret_kernel,
        grid=(b // rb, nc),
        in_specs=[spec(COL_CQ), spec(COL32_CK), spec(COL_CV), spec(COL32_CG),
                  pl.BlockSpec((1, C_WIDTH), lambda b, c: (0, 0))],
        out_specs=pl.BlockSpec((rb, BLOCK, C_WIDTH), lambda b, c: (b, c, 0)),
        out_shape=jax.ShapeDtypeStruct((b, s, C_WIDTH), BF16),
        scratch_shapes=[pltpu.VMEM((rb, RET_HEADS // 2, LANES, LANES), F32)],
        compiler_params=pltpu.CompilerParams(
            dimension_semantics=("parallel", "arbitrary"), vmem_limit_bytes=VMEM_LIMIT),
        name=f"ret_l{layer}",
    )(proj16, proj32, proj16, proj32, ret_norm)


def _mix_out_kernel(x_ref, gt_ref, gpost_ref, oa_ref, ob_ref, oc_ref, w_ref, o_ref, gain_scr, inv_scr):
    y = _dot(oa_ref[...], w_ref[:A_WIDTH, :])
    y = y + _dot(ob_ref[...], w_ref[A_WIDTH:A_WIDTH + B_WIDTH, :])
    o_ref[...] = y + _dot(oc_ref[...], w_ref[A_WIDTH + B_WIDTH:, :])
    gain_scr[...] = gt_ref[...] * gpost_ref[...]
    _postnorm_residual_rows(o_ref, x_ref, gain_scr, inv_scr, o_ref)


def _mix_out(x, mod, norm_g, layer, oa, ob, oc, w, tm):
    b, s, d = x.shape
    row = lambda width: pl.BlockSpec((None, tm, width), lambda b, i: (b, i, 0))
    return pl.pallas_call(
        _mix_out_kernel,
        grid=(b, s // tm),
        in_specs=[
            row(d), _mod_spec(layer, 5, d, 2), _gain_spec(layer, 3, d, 2),
            row(A_WIDTH), row(B_WIDTH), row(C_WIDTH),
            pl.BlockSpec((None,) + w.shape[1:], lambda b, i: (layer, 0, 0)),
        ],
        out_specs=row(d),
        out_shape=jax.ShapeDtypeStruct((b, s, d), F32),
        scratch_shapes=[pltpu.VMEM((1, d), F32), pltpu.VMEM((tm, 1), F32)],
        compiler_params=pltpu.CompilerParams(
            dimension_semantics=("parallel", "parallel"), vmem_limit_bytes=VMEM_LIMIT),
        name=f"mix_out_l{layer}",
    )(x, mod, norm_g, oa, ob, oc, w)


def _prep_mix_in(w):
    w = w.astype(BF16)
    sizes = [A_WIDTH, KV_WIDTH, KV_WIDTH, B_WIDTH, DSA_RANK, IDX_HEADS * IDX_DIM, IDX_DIM, IDX_HEADS,
             C_WIDTH, C_WIDTH, C_WIDTH, C_WIDTH]
    starts = [0]
    for sz in sizes:
        starts.append(starts[-1] + sz)
    aq, ak, av, bq, bkv, biq, bik, biw, cq, ck, cv, cg = [w[..., a:a + sz] for a, sz in zip(starts, sizes)]
    tail = jnp.zeros(w.shape[:-1] + (LANES - IDX_DIM - IDX_HEADS,), w.dtype)
    return jnp.concatenate([aq, bq, biq, cq, cv, cg, ck, ak, av, bkv, bik, biw, tail], axis=-1)


def _prep_dsa_up(w_uk, w_uv):
    r, h, dh = w_uk.shape
    uk = jnp.transpose(w_uk, (1, 2, 0))
    uv = jnp.transpose(w_uv, (1, 0, 2))
    odd = (jnp.arange(h) % 2 == 1)[:, None, None]
    zk = jnp.zeros_like(uk)
    zv = jnp.zeros_like(uv)
    uk_pad = jnp.where(odd, jnp.concatenate([zk, uk], axis=1), jnp.concatenate([uk, zk], axis=1))
    uv_pad = jnp.where(odd, jnp.concatenate([zv, uv], axis=2), jnp.concatenate([uv, zv], axis=2))
    return uk_pad.astype(BF16), uv_pad.astype(BF16)


FFN_TM = 1024
FFN_TF = 512
FFN_VMEM_LIMIT = 60 * 1024 * 1024
MIX_IN_TM = 1024
MIX_IN_TN = 1536
MIX_OUT_TM = 512


def kernel(x, c, ada_w, ada_b, norm_g, ffn1_w_in, ffn1_w_out, ffn2_w_in, ffn2_w_out,
           mix_w_in, mix_w_out, swa_sinks, dsa_kv_norm, dsa_w_uk, dsa_w_uv, ret_norm):
    depth = ada_w.shape[0]
    b, s, d = x.shape
    assert s % FFN_TM == 0 and s % DSA_KCHUNK == 0 and d % LANES == 0
    mod = _ada_mod(c, ada_w, ada_b).reshape(depth, b, N_MOD, 1, d)
    gains = norm_g.reshape(depth, norm_g.shape[1], 1, d)
    ffn1_in, ffn1_out = ffn1_w_in.astype(BF16), ffn1_w_out.astype(BF16)
    ffn2_in, ffn2_out = ffn2_w_in.astype(BF16), ffn2_w_out.astype(BF16)
    mix_out_w = mix_w_out.astype(BF16)
    mix_in_w = _prep_mix_in(mix_w_in)
    for l in range(depth):
        x = _ffn_block(x, mod, gains, l, 0, ffn1_in, ffn1_out, 0.5, FFN_TM, FFN_TF)

        proj16, proj32 = _mix_in(x, mod, gains, l, mix_in_w, MIX_IN_TM, MIX_IN_TN)
        oa = _swa(proj16, proj32, swa_sinks[l], l)
        uk_pad, uv_pad = _prep_dsa_up(dsa_w_uk[l], dsa_w_uv[l])
        ob = _dsa(proj16, proj32, dsa_kv_norm[l].reshape(1, -1), uk_pad, uv_pad, l)
        oc = _retention(proj16, proj32, ret_norm[l].reshape(1, -1), l)
        x = _mix_out(x, mod, gains, l, oa, ob, oc, mix_out_w, MIX_OUT_TM)

        x = _ffn_block(x, mod, gains, l, 2, ffn2_in, ffn2_out, 0.5, FFN_TM, FFN_TF)
    return x
```

```python
import functools
import math

import jax
import jax.numpy as jnp
from jax import lax
from jax.experimental import pallas as pl
from jax.experimental.pallas import tpu as pltpu

F32 = jnp.float32
BF16 = jnp.bfloat16

LANES = 128
HEAD_DIM = 64
SWA_HEADS = 16
SWA_KV_HEADS = 2
BLOCK = 128
DSA_HEADS = 8
DSA_RANK = 128
IDX_HEADS = 8
IDX_DIM = 64
DSA_TOPK_MAX = 256
RET_HEADS = 8
N_MOD = 9
EPS = 1e-6
NEG = -1e30

A_WIDTH = SWA_HEADS * HEAD_DIM
B_WIDTH = DSA_HEADS * HEAD_DIM
C_WIDTH = RET_HEADS * HEAD_DIM
KV_WIDTH = SWA_KV_HEADS * HEAD_DIM

COL_AQ = 0
COL_BQ = COL_AQ + A_WIDTH
COL_BIQ = COL_BQ + B_WIDTH
COL_CQ = COL_BIQ + IDX_HEADS * IDX_DIM
COL_CV = COL_CQ + C_WIDTH
MIX_BF16_WIDTH = COL_CV + C_WIDTH
COL32_CG = 0
COL32_CK = COL32_CG + C_WIDTH
COL32_AKV = COL32_CK + C_WIDTH
COL32_BKV = COL32_AKV + 2 * KV_WIDTH
COL32_BIK = COL32_BKV + DSA_RANK
MIX_F32_WIDTH = COL32_BIK + LANES

DSA_QBLOCK = 256
DSA_KCHUNK = 256
BISECT_ITERS = 14
VMEM_LIMIT = 56 * 1024 * 1024


def _dot(a, b):
    return jnp.dot(a, b, preferred_element_type=F32)


def _dot_nt(a, b):
    return lax.dot_general(a, b, (((1,), (1,)), ((), ())), preferred_element_type=F32)


def _dot_tn(a, b):
    return lax.dot_general(a, b, (((0,), (0,)), ((), ())), preferred_element_type=F32)


def _silu(x):
    return x / (1.0 + jnp.exp(-x))


def _rms(x, g):
    return x * lax.rsqrt(jnp.mean(x * x, axis=-1, keepdims=True) + EPS) * g


def _alibi_slope(h, n):
    return 2.0 ** (-8.0 * (h + 1) / n)


def _ada_kernel(c_ref, w_ref, b_ref, o_ref):
    cond = _silu(c_ref[...]).astype(BF16)
    o_ref[...] = _dot(cond, w_ref[...].astype(BF16)) + b_ref[...]


def _ada_mod(c, ada_w, ada_b):
    depth, d, n = ada_w.shape
    b = c.shape[0]
    tn = 1024
    return pl.pallas_call(
        _ada_kernel,
        grid=(depth, n // tn),
        in_specs=[
            pl.BlockSpec((b, d), lambda l, j: (0, 0)),
            pl.BlockSpec((None, d, tn), lambda l, j: (l, 0, j)),
            pl.BlockSpec((None, 1, tn), lambda l, j: (l, 0, j)),
        ],
        out_specs=pl.BlockSpec((None, b, tn), lambda l, j: (l, 0, j)),
        out_shape=jax.ShapeDtypeStruct((depth, b, n), F32),
        compiler_params=pltpu.CompilerParams(
            dimension_semantics=("arbitrary", "arbitrary"), vmem_limit_bytes=VMEM_LIMIT),
        name="ada_mod",
    )(c, ada_w, ada_b.reshape(depth, 1, n))


def _mod_spec(layer, k, d, grid_rank):
    if grid_rank == 2:
        return pl.BlockSpec((None, None, None, 1, d), lambda b, i: (layer, b, k, 0, 0))
    return pl.BlockSpec((None, None, None, 1, d), lambda b, i, j: (layer, b, k, 0, 0))


def _gain_spec(layer, k, d, grid_rank):
    if grid_rank == 2:
        return pl.BlockSpec((None, None, 1, d), lambda b, i: (layer, k, 0, 0))
    return pl.BlockSpec((None, None, 1, d), lambda b, i, j: (layer, k, 0, 0))


ROW_CHUNK = 16


def _row_chunks(n_rows, body):
    def step(i, carry):
        body(pl.ds(pl.multiple_of(i * ROW_CHUNK, ROW_CHUNK), ROW_CHUNK))
        return carry
    lax.fori_loop(0, n_rows // ROW_CHUNK, step, 0, unroll=True)


def _inv_rms_rows(x_ref, inv_ref):
    def body(r):
        x = x_ref[r, :]
        inv_ref[r, :] = lax.rsqrt(jnp.mean(x * x, axis=-1, keepdims=True) + EPS)
    _row_chunks(x_ref.shape[0], body)


def _prenorm_mod_rows(x_ref, gain_ref, shift_ref, inv_ref, h_ref):
    _inv_rms_rows(x_ref, inv_ref)

    def body(r):
        h_ref[r, :] = (x_ref[r, :] * inv_ref[r, :] * gain_ref[...] + shift_ref[...]).astype(h_ref.dtype)
    _row_chunks(x_ref.shape[0], body)


def _postnorm_residual_rows(y_ref, x_ref, gain_ref, inv_ref, o_ref):
    _inv_rms_rows(y_ref, inv_ref)

    def body(r):
        o_ref[r, :] = x_ref[r, :] + y_ref[r, :] * inv_ref[r, :] * gain_ref[...]
    _row_chunks(y_ref.shape[0], body)


def _ffn_kernel(res_w, tail, x_ref, sh_ref, sc_ref, gt_ref, gpre_ref, gpost_ref, wg_ref, wu_ref, wo_ref,
                o_ref, h_scr, gain_scr, inv_scr):
    f = pl.program_id(2)
    last = pl.num_programs(2) - 1
    tf = wo_ref.shape[1]

    def step(lo, first):
        h = h_scr[...]
        act = _silu(_dot(h, wg_ref[0, :, lo:])) * _dot(h, wu_ref[0, :, lo:])
        out = _dot(act.astype(BF16), wo_ref[0, lo:, :])
        if first:
            o_ref[...] = out
        else:
            o_ref[...] += out

    @pl.when(f == 0)
    def _():
        gain_scr[...] = gpre_ref[...] * (1.0 + sc_ref[...])
        _prenorm_mod_rows(x_ref, gain_scr, sh_ref, inv_scr, h_scr)
        step(0, True)

    @pl.when((f > 0) & (f < last))
    def _():
        step(0, False)

    @pl.when(f == last)
    def _():
        step(tf - tail, False)
        gain_scr[...] = res_w * gt_ref[...] * gpost_ref[...]
        _postnorm_residual_rows(o_ref, x_ref, gain_scr, inv_scr, o_ref)


def _ffn_block(x, mod, norm_g, layer, sub, w_in, w_out, res_w, tm, tf):
    b, s, d = x.shape
    d_ff = w_out.shape[1]
    nf = pl.cdiv(d_ff, tf)
    tail = d_ff - (nf - 1) * tf
    assert tail % LANES == 0 and d_ff >= tf and nf >= 2
    mk = 3 * sub
    start = lambda j, base: pl.multiple_of(base + jnp.minimum(j * tf, d_ff - tf), LANES)
    col = lambda base: (lambda b, i, j: (layer, 0, start(j, base)))
    one = pl.Element(1)
    return pl.pallas_call(
        functools.partial(_ffn_kernel, res_w, tail),
        grid=(b, s // tm, nf),
        in_specs=[
            pl.BlockSpec((None, tm, d), lambda b, i, j: (b, i, 0)),
            _mod_spec(layer, mk, d, 3), _mod_spec(layer, mk + 1, d, 3), _mod_spec(layer, mk + 2, d, 3),
            _gain_spec(layer, 2 * sub, d, 3), _gain_spec(layer, 2 * sub + 1, d, 3),
            pl.BlockSpec((one, pl.Element(d), pl.Element(tf)), col(0)),
            pl.BlockSpec((one, pl.Element(d), pl.Element(tf)), col(d_ff)),
            pl.BlockSpec((one, pl.Element(tf), pl.Element(d)), lambda b, i, j: (layer, start(j, 0), 0)),
        ],
        out_specs=pl.BlockSpec((None, tm, d), lambda b, i, j: (b, i, 0)),
        out_shape=jax.ShapeDtypeStruct((b, s, d), F32),
        scratch_shapes=[pltpu.VMEM((tm, d), BF16), pltpu.VMEM((1, d), F32), pltpu.VMEM((tm, 1), F32)],
        compiler_params=pltpu.CompilerParams(
            dimension_semantics=("parallel", "parallel", "arbitrary"), vmem_limit_bytes=FFN_VMEM_LIMIT),
        name=f"ffn_l{layer}_s{sub}",
    )(x, mod, mod, mod, norm_g, norm_g, w_in, w_in, w_out)


def _mix_in_kernel(n16, x_ref, sh_ref, sc_ref, gpre_ref, w_ref, o16_ref, o32_ref, h_scr, gain_scr, inv_scr):
    j = pl.program_id(2)

    @pl.when(j == 0)
    def _():
        gain_scr[...] = gpre_ref[...] * (1.0 + sc_ref[...])
        _prenorm_mod_rows(x_ref, gain_scr, sh_ref, inv_scr, h_scr)
        o16_ref[...] = _dot(h_scr[...], w_ref[...]).astype(o16_ref.dtype)

    @pl.when((j > 0) & (j < n16))
    def _():
        o16_ref[...] = _dot(h_scr[...], w_ref[...]).astype(o16_ref.dtype)

    @pl.when(j >= n16)
    def _():
        o32_ref[...] = _dot(h_scr[...], w_ref[...])


def _mix_in(x, mod, norm_g, layer, w, tm, tn):
    b, s, d = x.shape
    assert MIX_BF16_WIDTH % tn == 0 and MIX_F32_WIDTH % tn == 0
    n16, n32 = MIX_BF16_WIDTH // tn, MIX_F32_WIDTH // tn
    return pl.pallas_call(
        functools.partial(_mix_in_kernel, n16),
        grid=(b, s // tm, n16 + n32),
        in_specs=[
            pl.BlockSpec((None, tm, d), lambda b, i, j: (b, i, 0)),
            _mod_spec(layer, 3, d, 3), _mod_spec(layer, 4, d, 3),
            _gain_spec(layer, 2, d, 3),
            pl.BlockSpec((None, d, tn), lambda b, i, j: (layer, 0, j)),
        ],
        out_specs=[
            pl.BlockSpec((None, tm, tn), lambda b, i, j: (b, i, jnp.minimum(j, n16 - 1))),
            pl.BlockSpec((None, tm, tn), lambda b, i, j: (b, i, jnp.maximum(j - n16, 0))),
        ],
        out_shape=[jax.ShapeDtypeStruct((b, s, MIX_BF16_WIDTH), BF16),
                   jax.ShapeDtypeStruct((b, s, MIX_F32_WIDTH), F32)],
        scratch_shapes=[pltpu.VMEM((tm, d), BF16), pltpu.VMEM((1, d), F32), pltpu.VMEM((tm, 1), F32)],
        compiler_params=pltpu.CompilerParams(
            dimension_semantics=("parallel", "parallel", "arbitrary"), vmem_limit_bytes=VMEM_LIMIT),
        name=f"mix_in_l{layer}",
    )(x, mod, mod, norm_g, w)


def _lane_halves(x, first_half_holds_data):
    lane = lax.broadcasted_iota(jnp.int32, x.shape, 1)
    if first_half_holds_data:
        lo = jnp.where(lane < HEAD_DIM, x, 0.0)
        return lo, pltpu.roll(lo, HEAD_DIM, 1)
    hi = jnp.where(lane >= HEAD_DIM, x, 0.0)
    return pltpu.roll(hi, HEAD_DIM, 1), hi


def _swa_kernel(sinks_ref, q_ref, kvc_ref, kvp_ref, o_ref):
    n = pl.program_id(1)
    w = BLOCK
    bands = SWA_HEADS // SWA_KV_HEADS // 2
    rows = bands * w
    r = lax.broadcasted_iota(jnp.int32, (rows, w), 0)
    j = lax.broadcasted_iota(jnp.int32, (rows, w), 1)
    i = r & (w - 1)
    from_prev = j > i
    prev_f = jnp.where(from_prev, 1.0, 0.0)
    dist = i - j + jnp.where(from_prev, w, 0)
    no_key = jnp.where(from_prev & (n * w - w + j < 0), -jnp.inf, 0.0)
    band_step = _alibi_slope(2, SWA_HEADS) / _alibi_slope(0, SWA_HEADS)
    band_scale = jnp.ones((rows, w), F32)
    for band in range(1, bands):
        band_scale = jnp.where(r >= band * w, band_step ** band, band_scale)
    dist_scaled = dist.astype(F32) * band_scale
    band_col = lax.broadcasted_iota(jnp.int32, (rows, 1), 0)
    for bi in range(q_ref.shape[0]):
        kcat = jnp.concatenate([kvp_ref[bi, :, :KV_WIDTH], kvc_ref[bi, :, :KV_WIDTH]], axis=0)
        vcat = jnp.concatenate([kvp_ref[bi, :, KV_WIDTH:], kvc_ref[bi, :, KV_WIDTH:]], axis=0)
        for hk in range(SWA_KV_HEADS):
            k_halves = [t.astype(BF16) for t in _lane_halves(kcat, hk == 0)]
            v_halves = [t.astype(BF16) for t in _lane_halves(vcat, hk == 0)]
            q = jnp.concatenate([q_ref[bi, :, (bands * hk + band) * LANES:(bands * hk + band + 1) * LANES]
                                 for band in range(bands)], axis=0)
            q = (q * HEAD_DIM ** -0.5).astype(BF16)
            out = None
            for par in range(2):
                head0 = 2 * bands * hk + par
                sink = jnp.full((rows, 1), sinks_ref[head0], F32)
                for band in range(1, bands):
                    sink = jnp.where(band_col >= band * w, sinks_ref[head0 + 2 * band], sink)
                k_all, v_all = k_halves[par], v_halves[par]
                sc = jnp.where(from_prev, _dot_nt(q, k_all[:w]), _dot_nt(q, k_all[w:]))
                sc = sc - _alibi_slope(head0, SWA_HEADS) * dist_scaled + no_key
                m = jnp.maximum(jnp.max(sc, axis=-1, keepdims=True), sink)
                e = jnp.exp(sc - m)
                p = e / (jnp.sum(e, axis=-1, keepdims=True) + jnp.exp(sink - m))
                p_prev = p * prev_f
                pv = _dot(p_prev.astype(BF16), v_all[:w]) + _dot((p - p_prev).astype(BF16), v_all[w:])
                out = pv if out is None else out + pv
            for band in range(bands):
                tile = bands * hk + band
                o_ref[bi, :, tile * LANES:(tile + 1) * LANES] = out[band * w:(band + 1) * w].astype(o_ref.dtype)


SWA_BATCH = 8


def _swa(proj16, proj32, sinks, layer):
    b, s, _ = proj16.shape
    nb = s // BLOCK
    rb = math.gcd(b, SWA_BATCH)
    kv_blk = COL32_AKV // (2 * KV_WIDTH)
    return pl.pallas_call(
        _swa_kernel,
        grid=(b // rb, nb),
        in_specs=[
            pl.BlockSpec(memory_space=pltpu.SMEM),
            pl.BlockSpec((rb, BLOCK, A_WIDTH), lambda b, n: (b, n, COL_AQ // A_WIDTH)),
            pl.BlockSpec((rb, BLOCK, 2 * KV_WIDTH), lambda b, n: (b, n, kv_blk)),
            pl.BlockSpec((rb, BLOCK, 2 * KV_WIDTH), lambda b, n: (b, jnp.maximum(n - 1, 0), kv_blk)),
        ],
        out_specs=pl.BlockSpec((rb, BLOCK, A_WIDTH), lambda b, n: (b, n, 0)),
        out_shape=jax.ShapeDtypeStruct((b, s, A_WIDTH), BF16),
        compiler_params=pltpu.CompilerParams(
            dimension_semantics=("parallel", "arbitrary"), vmem_limit_bytes=VMEM_LIMIT),
        name=f"swa_l{layer}",
    )(sinks, proj16, proj32, proj32)


def _dsa_kernel(topk, q_ref, iq_ref, kv_ref, ikw_ref, kvn_ref, wuk_ref, wuv_ref, o_ref,
                ckv_scr, ik_scr, iqh_scr, score_scr, bias_scr, qlat_scr, s_scr, mrun_scr, m_scr,
                lpart_scr, acc_scr):
    n = pl.program_id(1)
    qb = DSA_QBLOCK
    kc = DSA_KCHUNK
    nchunks = lax.div(n * qb + qb - 1, kc) + 1

    @pl.when(n == 0)
    def _():
        ckv_scr[...] = _rms(kv_ref[...], kvn_ref[...]).astype(BF16)
        ik_scr[...] = ikw_ref[...].astype(BF16)

    row0 = pl.multiple_of(n * qb, qb)

    iw_t = ikw_ref[pl.ds(row0, qb), :].T
    iw_scale = IDX_HEADS ** -0.5 * IDX_DIM ** -0.5
    w_rows = [iw_t[IDX_DIM + h:IDX_DIM + h + 1, :] * iw_scale for h in range(IDX_HEADS)]
    lane = lax.broadcasted_iota(jnp.int32, (qb, LANES), 1)
    for h in range(IDX_HEADS):
        iq = iq_ref[:, (h // 2) * LANES:(h // 2 + 1) * LANES].astype(F32)
        if h % 2 == 0:
            iq = jnp.where(lane < IDX_DIM, iq, 0.0)
        else:
            iq = pltpu.roll(jnp.where(lane >= IDX_DIM, iq, 0.0), IDX_DIM, 1)
        iqh_scr[h] = iq.astype(BF16)

    key_row = lax.broadcasted_iota(jnp.int32, (kc, qb), 0)
    t_pos = n * qb + lax.broadcasted_iota(jnp.int32, (kc, qb), 1)

    def score_chunk(c, carry):
        off = pl.multiple_of(c * kc, kc)
        ik = ik_scr[pl.ds(off, kc), :]
        sc = jnp.zeros((kc, qb), F32)
        for h in range(IDX_HEADS):
            sc = sc + w_rows[h] * jnp.maximum(_dot_nt(ik, iqh_scr[h]), 0.0)
        score_scr[c] = jnp.where(off + key_row <= t_pos, sc, -jnp.inf)
        return carry

    lax.fori_loop(0, nchunks, score_chunk, 0)

    @pl.when(lax.rem(nchunks, 2) == 1)
    def _():
        score_scr[nchunks] = jnp.full((kc, qb), -jnp.inf, F32)

    npairs = lax.div(nchunks + 1, 2)

    def reduce_chunks(fn, init):
        def body(i, carry):
            for u in range(2):
                carry = fn(score_scr[2 * i + u], carry)
            return carry
        return lax.fori_loop(0, npairs, body, init)

    sub = 8

    def fold_rows(x, op):
        out = x[:sub]
        for g in range(1, x.shape[0] // sub):
            out = op(out, x[g * sub:(g + 1) * sub])
        return out

    def fold_lanes(x, op):
        out = x[:, :LANES]
        for t in range(1, x.shape[1] // LANES):
            out = op(out, x[:, t * LANES:(t + 1) * LANES])
        return out

    def count_ge(thr, strict=False):
        def body(x, cnt):
            hit = (x > thr) if strict else (x >= thr)
            return cnt + fold_rows(jnp.where(hit, 1.0, 0.0), jnp.add)
        return jnp.sum(reduce_chunks(body, jnp.zeros((sub, qb), F32)), axis=0, keepdims=True)

    def max_at_most(bound):
        def body(x, mx):
            return jnp.maximum(mx, fold_rows(jnp.where(x <= bound, x, -jnp.inf), jnp.maximum))
        return jnp.max(reduce_chunks(body, jnp.full((sub, qb), -jnp.inf, F32)), axis=0, keepdims=True)

    def store_bias(c, t, bias_t):
        for g in range(qb // LANES):
            bias_scr[c, g * LANES:(g + 1) * LANES, t * LANES:(t + 1) * LANES] = (
                bias_t[:, g * LANES:(g + 1) * LANES].T)

    @pl.when(n * qb + qb <= topk)
    def _():
        def body(c, carry):
            x = score_scr[c]
            for t in range(kc // LANES):
                store_bias(c, t, jnp.where(x[t * LANES:(t + 1) * LANES] == -jnp.inf, NEG, 0.0))
            return carry
        lax.fori_loop(0, nchunks, body, 0)

    @pl.when(n * qb + qb > topk)
    def _():
        kf = float(topk)

        def minmax(x, carry):
            mn, mx = carry
            mn = jnp.minimum(mn, fold_rows(jnp.where(x == -jnp.inf, jnp.inf, x), jnp.minimum))
            return mn, jnp.maximum(mx, fold_rows(x, jnp.maximum))

        lo, hi = reduce_chunks(minmax, (jnp.full((sub, qb), jnp.inf, F32), jnp.full((sub, qb), -jnp.inf, F32)))
        lo = jnp.min(lo, axis=0, keepdims=True)
        hi = jnp.max(hi, axis=0, keepdims=True)

        def bisect(_, carry):
            lo, hi = carry
            mid = 0.5 * (lo + hi)
            ge = count_ge(mid) >= kf
            return jnp.where(ge, mid, lo), jnp.where(ge, hi, mid)

        lo, hi = lax.fori_loop(0, BISECT_ITERS, bisect, (lo, hi))

        def walk_cond(carry):
            return carry[1] > 0.0

        def count_and_next(thr):
            def body(x, carry):
                cnt, mx = carry
                ge = x >= thr
                cnt = cnt + fold_rows(jnp.where(ge, 1.0, 0.0), jnp.add)
                return cnt, jnp.maximum(mx, fold_rows(jnp.where(ge, -jnp.inf, x), jnp.maximum))
            cnt, mx = reduce_chunks(body, (jnp.zeros((sub, qb), F32), jnp.full((sub, qb), -jnp.inf, F32)))
            return jnp.sum(cnt, axis=0, keepdims=True), jnp.max(mx, axis=0, keepdims=True)

        def walk(carry):
            thr, _ = carry
            cnt, below = count_and_next(thr)
            short = cnt < kf
            return jnp.where(short, below, thr), jnp.max(jnp.where(short, 1.0, 0.0))

        thr, _ = lax.while_loop(walk_cond, walk, (max_at_most(hi), jnp.float32(1.0)))

        need = kf - count_ge(thr, strict=True)

        r = lax.broadcasted_iota(jnp.int32, (LANES, LANES), 0)
        col = lax.broadcasted_iota(jnp.int32, (LANES, LANES), 1)
        tri = jnp.where(col <= r, 1.0, 0.0).astype(BF16)

        def select(i, seen):
            for u in range(2):
                c = 2 * i + u
                x = score_scr[c]
                for t in range(kc // LANES):
                    xt = x[t * LANES:(t + 1) * LANES]
                    eq = xt == thr
                    eqf = jnp.where(eq, 1.0, 0.0)
                    rank = seen + _dot(tri, eqf.astype(BF16))
                    seen = seen + jnp.sum(eqf, axis=0, keepdims=True)
                    sel = (xt > thr) | (eq & (rank <= need))
                    store_bias(c, t, jnp.where(sel, 0.0, NEG))
            return seen

        lax.fori_loop(0, npairs, select, jnp.zeros((1, qb), F32))

    for h in range(DSA_HEADS):
        tile = h // 2
        rows = slice(h * qb, (h + 1) * qb)
        q = q_ref[:, tile * LANES:(tile + 1) * LANES].astype(BF16)
        qlat_scr[rows, :] = (_dot(q, wuk_ref[h]) * HEAD_DIM ** -0.5).astype(BF16)
    mrun_scr[...] = jnp.full(mrun_scr.shape, NEG, F32)
    key_lane = lax.broadcasted_iota(jnp.int32, (1, kc), 1)

    def logits(c, carry):
        off = pl.multiple_of(c * kc, kc)
        s = _dot_nt(qlat_scr[...], ckv_scr[pl.ds(off, kc), :])
        bias = bias_scr[c]
        key_pos = (off + key_lane).astype(F32)
        for h in range(DSA_HEADS):
            rows = slice(h * qb, (h + 1) * qb)
            sh = s[rows] + (bias + _alibi_slope(h, DSA_HEADS) * key_pos)
            s_scr[c, rows, :] = sh
            mrun_scr[rows, :] = jnp.maximum(mrun_scr[rows, :], fold_lanes(sh, jnp.maximum))
        return carry

    lax.fori_loop(0, nchunks, logits, 0)
    m_scr[...] = jnp.broadcast_to(jnp.max(mrun_scr[...], axis=-1, keepdims=True), m_scr.shape)
    lpart_scr[...] = jnp.zeros_like(lpart_scr)
    acc_scr[...] = jnp.zeros_like(acc_scr)

    def attend(c, carry):
        off = pl.multiple_of(c * kc, kc)
        m = m_scr[...]
        p = [jnp.exp(s_scr[c, :, t * LANES:(t + 1) * LANES] - m) for t in range(kc // LANES)]
        lpart_scr[...] += functools.reduce(jnp.add, p)
        acc_scr[...] += _dot(jnp.concatenate(p, axis=1).astype(BF16), ckv_scr[pl.ds(off, kc), :])
        return carry

    lax.fori_loop(0, nchunks, attend, 0)

    o_all = (acc_scr[...] / jnp.sum(lpart_scr[...], axis=-1, keepdims=True)).astype(BF16)
    for tile in range(DSA_HEADS // 2):
        h = 2 * tile
        out = _dot(o_all[h * qb:(h + 1) * qb], wuv_ref[h]) + _dot(o_all[(h + 1) * qb:(h + 2) * qb], wuv_ref[h + 1])
        o_ref[:, tile * LANES:(tile + 1) * LANES] = out.astype(o_ref.dtype)


def _dsa(proj16, proj32, kv_norm, wuk_pad, wuv_pad, layer):
    b, s, _ = proj16.shape
    qb = DSA_QBLOCK
    nb = s // qb
    topk = min(DSA_TOPK_MAX, s // 4)
    assert topk % qb == 0
    assert (s // DSA_KCHUNK) % 2 == 0
    nck = s // DSA_KCHUNK
    rows = DSA_HEADS * qb
    return pl.pallas_call(
        functools.partial(_dsa_kernel, topk),
        grid=(b, nb),
        in_specs=[
            pl.BlockSpec((None, qb, B_WIDTH), lambda b, n: (b, n, COL_BQ // B_WIDTH)),
            pl.BlockSpec((None, qb, B_WIDTH), lambda b, n: (b, n, COL_BIQ // B_WIDTH)),
            pl.BlockSpec((None, s, DSA_RANK), lambda b, n: (b, 0, COL32_BKV // DSA_RANK)),
            pl.BlockSpec((None, s, LANES), lambda b, n: (b, 0, COL32_BIK // LANES)),
            pl.BlockSpec((1, DSA_RANK), lambda b, n: (0, 0)),
            pl.BlockSpec((DSA_HEADS, LANES, DSA_RANK), lambda b, n: (0, 0, 0)),
            pl.BlockSpec((DSA_HEADS, DSA_RANK, LANES), lambda b, n: (0, 0, 0)),
        ],
        out_specs=pl.BlockSpec((None, qb, B_WIDTH), lambda b, n: (b, n, 0)),
        out_shape=jax.ShapeDtypeStruct((b, s, B_WIDTH), BF16),
        scratch_shapes=[
            pltpu.VMEM((s, DSA_RANK), BF16),
            pltpu.VMEM((s, LANES), BF16),
            pltpu.VMEM((IDX_HEADS, qb, LANES), BF16),
            pltpu.VMEM((nck, DSA_KCHUNK, qb), F32),
            pltpu.VMEM((nck, qb, DSA_KCHUNK), F32),
            pltpu.VMEM((rows, DSA_RANK), BF16),
            pltpu.VMEM((nck, rows, DSA_KCHUNK), F32),
            pltpu.VMEM((rows, LANES), F32),
            pltpu.VMEM((rows, LANES), F32),
            pltpu.VMEM((rows, LANES), F32),
            pltpu.VMEM((rows, DSA_RANK), F32),
        ],
        compiler_params=pltpu.CompilerParams(
            dimension_semantics=("parallel", "arbitrary"), vmem_limit_bytes=VMEM_LIMIT),
        name=f"dsa_l{layer}",
    )(proj16, proj16, proj32, proj32, kv_norm, wuk_pad, wuv_pad)


def _ret_kernel(q_ref, k_ref, v_ref, g_ref, gn_ref, o_ref, state_scr):
    cs = BLOCK

    @pl.when(pl.program_id(1) == 0)
    def _():
        state_scr[...] = jnp.zeros_like(state_scr)

    row = lax.broadcasted_iota(jnp.int32, (cs, LANES), 0)
    lane = lax.broadcasted_iota(jnp.int32, (cs, LANES), 1)
    first = lane < HEAD_DIM
    rowf = row.astype(F32)
    diff = (row - lane).astype(F32)
    same_head = (row < HEAD_DIM) == first
    seg_mean = jnp.where(same_head, 1.0 / HEAD_DIM, 0.0).astype(BF16)

    def seg_mean_dot(x):
        x_hi = x.astype(BF16)
        x_lo = (x - x_hi.astype(F32)).astype(BF16)
        return _dot(x_hi, seg_mean) + _dot(x_lo, seg_mean)

    tiles = RET_HEADS // 2
    outs = []
    for bi in range(q_ref.shape[0]):
        for tile in range(tiles):
            cols = slice(tile * LANES, (tile + 1) * LANES)
            lg = [math.log(1.0 - 2.0 ** (-5.0 - (2 * tile + par))) for par in range(2)]
            lg_lane = jnp.where(first, lg[0], lg[1])
            q = q_ref[bi, :, cols].astype(BF16)
            k = k_ref[bi, :, cols].astype(F32) * HEAD_DIM ** -0.5
            v = v_ref[bi, :, cols].astype(F32)
            state = state_scr[bi, tile]
            out = _dot(q, state.astype(BF16)) * jnp.exp(lg_lane * (rowf + 1.0))
            for par in range(2):
                keep = first if par == 0 else ~first
                decay = jnp.where(diff >= 0, jnp.exp(lg[par] * jnp.maximum(diff, 0.0)), 0.0)
                inner = _dot_nt(q, jnp.where(keep, k, 0.0).astype(BF16)) * decay
                out = out + _dot(inner.astype(BF16), jnp.where(keep, v, 0.0).astype(BF16))
            k_dec = (k * jnp.exp(lg_lane * (cs - 1.0 - rowf))).astype(BF16)
            kv = _dot_tn(k_dec, v.astype(BF16))
            state_scr[bi, tile] = state * jnp.exp(lg_lane * cs) + jnp.where(same_head, kv, 0.0)
            outs.append(out)

    out = jnp.concatenate(outs, axis=0)
    mu = seg_mean_dot(out)
    cen = out - mu
    var = seg_mean_dot(cen * cen)
    yn = cen * lax.rsqrt(var + EPS)
    for bi in range(q_ref.shape[0]):
        for tile in range(tiles):
            cols = slice(tile * LANES, (tile + 1) * LANES)
            y = yn[(bi * tiles + tile) * cs:(bi * tiles + tile + 1) * cs] * gn_ref[:, cols]
            o_ref[bi, :, cols] = (y * _silu(g_ref[bi, :, cols])).astype(o_ref.dtype)


RET_BATCH = 8


def _retention(proj16, proj32, ret_norm, layer):
    b, s, _ = proj16.shape
    nc = s // BLOCK
    rb = math.gcd(b, RET_BATCH)
    spec = lambda col: pl.BlockSpec((rb, BLOCK, C_WIDTH), lambda b, c: (b, c, col // C_WIDTH))
    return pl.pallas_call(
        _ret_kernel,
        grid=(b // rb, nc),
        in_specs=[spec(COL_CQ), spec(COL32_CK), spec(COL_CV), spec(COL32_CG),
                  pl.BlockSpec((1, C_WIDTH), lambda b, c: (0, 0))],
        out_specs=pl.BlockSpec((rb, BLOCK, C_WIDTH), lambda b, c: (b, c, 0)),
        out_shape=jax.ShapeDtypeStruct((b, s, C_WIDTH), BF16),
        scratch_shapes=[pltpu.VMEM((rb, RET_HEADS // 2, LANES, LANES), F32)],
        compiler_params=pltpu.CompilerParams(
            dimension_semantics=("parallel", "arbitrary"), vmem_limit_bytes=VMEM_LIMIT),
        name=f"ret_l{layer}",
    )(proj16, proj32, proj16, proj32, ret_norm)


def _mix_out_kernel(x_ref, gt_ref, gpost_ref, oa_ref, ob_ref, oc_ref, w_ref, o_ref, gain_scr, inv_scr):
    y = _dot(oa_ref[...], w_ref[:A_WIDTH, :])
    y = y + _dot(ob_ref[...], w_ref[A_WIDTH:A_WIDTH + B_WIDTH, :])
    o_ref[...] = y + _dot(oc_ref[...], w_ref[A_WIDTH + B_WIDTH:, :])
    gain_scr[...] = gt_ref[...] * gpost_ref[...]
    _postnorm_residual_rows(o_ref, x_ref, gain_scr, inv_scr, o_ref)


def _mix_out(x, mod, norm_g, layer, oa, ob, oc, w, tm):
    b, s, d = x.shape
    row = lambda width: pl.BlockSpec((None, tm, width), lambda b, i: (b, i, 0))
    return pl.pallas_call(
        _mix_out_kernel,
        grid=(b, s // tm),
        in_specs=[
            row(d), _mod_spec(layer, 5, d, 2), _gain_spec(layer, 3, d, 2),
            row(A_WIDTH), row(B_WIDTH), row(C_WIDTH),
            pl.BlockSpec((None,) + w.shape[1:], lambda b, i: (layer, 0, 0)),
        ],
        out_specs=row(d),
        out_shape=jax.ShapeDtypeStruct((b, s, d), F32),
        scratch_shapes=[pltpu.VMEM((1, d), F32), pltpu.VMEM((tm, 1), F32)],
        compiler_params=pltpu.CompilerParams(
            dimension_semantics=("parallel", "parallel"), vmem_limit_bytes=VMEM_LIMIT),
        name=f"mix_out_l{layer}",
    )(x, mod, norm_g, oa, ob, oc, w)


def _prep_mix_in(w):
    w = w.astype(BF16)
    sizes = [A_WIDTH, KV_WIDTH, KV_WIDTH, B_WIDTH, DSA_RANK, IDX_HEADS * IDX_DIM, IDX_DIM, IDX_HEADS,
             C_WIDTH, C_WIDTH, C_WIDTH, C_WIDTH]
    starts = [0]
    for sz in sizes:
        starts.append(starts[-1] + sz)
    aq, ak, av, bq, bkv, biq, bik, biw, cq, ck, cv, cg = [w[..., a:a + sz] for a, sz in zip(starts, sizes)]
    tail = jnp.zeros(w.shape[:-1] + (LANES - IDX_DIM - IDX_HEADS,), w.dtype)
    return jnp.concatenate([aq, bq, biq, cq, cv, cg, ck, ak, av, bkv, bik, biw, tail], axis=-1)


def _prep_dsa_up(w_uk, w_uv):
    r, h, dh = w_uk.shape
    uk = jnp.transpose(w_uk, (1, 2, 0))
    uv = jnp.transpose(w_uv, (1, 0, 2))
    odd = (jnp.arange(h) % 2 == 1)[:, None, None]
    zk = jnp.zeros_like(uk)
    zv = jnp.zeros_like(uv)
    uk_pad = jnp.where(odd, jnp.concatenate([zk, uk], axis=1), jnp.concatenate([uk, zk], axis=1))
    uv_pad = jnp.where(odd, jnp.concatenate([zv, uv], axis=2), jnp.concatenate([uv, zv], axis=2))
    return uk_pad.astype(BF16), uv_pad.astype(BF16)


FFN_TM = 1024
FFN_TF = 512
FFN_VMEM_LIMIT = 60 * 1024 * 1024
MIX_IN_TM = 1024
MIX_IN_TN = 1536
MIX_OUT_TM = 512


def kernel(x, c, ada_w, ada_b, norm_g, ffn1_w_in, ffn1_w_out, ffn2_w_in, ffn2_w_out,
           mix_w_in, mix_w_out, swa_sinks, dsa_kv_norm, dsa_w_uk, dsa_w_uv, ret_norm):
    depth = ada_w.shape[0]
    b, s, d = x.shape
    assert s % FFN_TM == 0 and s % DSA_KCHUNK == 0 and d % LANES == 0
    mod = _ada_mod(c, ada_w, ada_b).reshape(depth, b, N_MOD, 1, d)
    gains = norm_g.reshape(depth, norm_g.shape[1], 1, d)
    ffn1_in, ffn1_out = ffn1_w_in.astype(BF16), ffn1_w_out.astype(BF16)
    ffn2_in, ffn2_out = ffn2_w_in.astype(BF16), ffn2_w_out.astype(BF16)
    mix_out_w = mix_w_out.astype(BF16)
    mix_in_w = _prep_mix_in(mix_w_in)
    for l in range(depth):
        x = _ffn_block(x, mod, gains, l, 0, ffn1_in, ffn1_out, 0.5, FFN_TM, FFN_TF)

        proj16, proj32 = _mix_in(x, mod, gains, l, mix_in_w, MIX_IN_TM, MIX_IN_TN)
        oa = _swa(proj16, proj32, swa_sinks[l], l)
        uk_pad, uv_pad = _prep_dsa_up(dsa_w_uk[l], dsa_w_uv[l])
        ob = _dsa(proj16, proj32, dsa_kv_norm[l].reshape(1, -1), uk_pad, uv_pad, l)
        oc = _retention(proj16, proj32, ret_norm[l].reshape(1, -1), l)
        x = _mix_out(x, mod, gains, l, oa, ob, oc, mix_out_w, MIX_OUT_TM)

        x = _ffn_block(x, mod, gains, l, 2, ffn2_in, ffn2_out, 0.5, FFN_TM, FFN_TF)
    return x
```

```python
import functools
import math

import jax
import jax.numpy as jnp
from jax import lax
from jax.experimental import pallas as pl
from jax.experimental.pallas import tpu as pltpu

F32 = jnp.float32
BF16 = jnp.bfloat16

LANES = 128
HEAD_DIM = 64
SWA_HEADS = 16
SWA_KV_HEADS = 2
BLOCK = 128
DSA_HEADS = 8
DSA_RANK = 128
IDX_HEADS = 8
IDX_DIM = 64
DSA_TOPK_MAX = 256
RET_HEADS = 8
N_MOD = 9
EPS = 1e-6
NEG = -1e30

A_WIDTH = SWA_HEADS * HEAD_DIM
B_WIDTH = DSA_HEADS * HEAD_DIM
C_WIDTH = RET_HEADS * HEAD_DIM
KV_WIDTH = SWA_KV_HEADS * HEAD_DIM

COL_AQ = 0
COL_BQ = COL_AQ + A_WIDTH
COL_BIQ = COL_BQ + B_WIDTH
COL_CQ = COL_BIQ + IDX_HEADS * IDX_DIM
COL_CV = COL_CQ + C_WIDTH
MIX_BF16_WIDTH = COL_CV + C_WIDTH
COL32_CG = 0
COL32_CK = COL32_CG + C_WIDTH
COL32_AKV = COL32_CK + C_WIDTH
COL32_BKV = COL32_AKV + 2 * KV_WIDTH
COL32_BIK = COL32_BKV + DSA_RANK
MIX_F32_WIDTH = COL32_BIK + LANES

DSA_QBLOCK = 256
DSA_KCHUNK = 256
BISECT_ITERS = 14
VMEM_LIMIT = 56 * 1024 * 1024


def _dot(a, b):
    return jnp.dot(a, b, preferred_element_type=F32)


def _dot_nt(a, b):
    return lax.dot_general(a, b, (((1,), (1,)), ((), ())), preferred_element_type=F32)


def _dot_tn(a, b):
    return lax.dot_general(a, b, (((0,), (0,)), ((), ())), preferred_element_type=F32)


def _silu(x):
    return x / (1.0 + jnp.exp(-x))


def _rms(x, g):
    return x * lax.rsqrt(jnp.mean(x * x, axis=-1, keepdims=True) + EPS) * g


def _alibi_slope(h, n):
    return 2.0 ** (-8.0 * (h + 1) / n)


def _ada_kernel(c_ref, w_ref, b_ref, o_ref):
    cond = _silu(c_ref[...]).astype(BF16)
    o_ref[...] = _dot(cond, w_ref[...].astype(BF16)) + b_ref[...]


def _ada_mod(c, ada_w, ada_b):
    depth, d, n = ada_w.shape
    b = c.shape[0]
    tn = 1024
    return pl.pallas_call(
        _ada_kernel,
        grid=(depth, n // tn),
        in_specs=[
            pl.BlockSpec((b, d), lambda l, j: (0, 0)),
            pl.BlockSpec((None, d, tn), lambda l, j: (l, 0, j)),
            pl.BlockSpec((None, 1, tn), lambda l, j: (l, 0, j)),
        ],
        out_specs=pl.BlockSpec((None, b, tn), lambda l, j: (l, 0, j)),
        out_shape=jax.ShapeDtypeStruct((depth, b, n), F32),
        compiler_params=pltpu.CompilerParams(
            dimension_semantics=("arbitrary", "arbitrary"), vmem_limit_bytes=VMEM_LIMIT),
        name="ada_mod",
    )(c, ada_w, ada_b.reshape(depth, 1, n))


def _mod_spec(layer, k, d, grid_rank):
    if grid_rank == 2:
        return pl.BlockSpec((None, None, None, 1, d), lambda b, i: (layer, b, k, 0, 0))
    return pl.BlockSpec((None, None, None, 1, d), lambda b, i, j: (layer, b, k, 0, 0))


def _gain_spec(layer, k, d, grid_rank):
    if grid_rank == 2:
        return pl.BlockSpec((None, None, 1, d), lambda b, i: (layer, k, 0, 0))
    return pl.BlockSpec((None, None, 1, d), lambda b, i, j: (layer, k, 0, 0))


ROW_CHUNK = 16


def _row_chunks(n_rows, body):
    def step(i, carry):
        body(pl.ds(pl.multiple_of(i * ROW_CHUNK, ROW_CHUNK), ROW_CHUNK))
        return carry
    lax.fori_loop(0, n_rows // ROW_CHUNK, step, 0, unroll=True)


def _inv_rms_rows(x_ref, inv_ref):
    def body(r):
        x = x_ref[r, :]
        inv_ref[r, :] = lax.rsqrt(jnp.mean(x * x, axis=-1, keepdims=True) + EPS)
    _row_chunks(x_ref.shape[0], body)


def _prenorm_mod_rows(x_ref, gain_ref, shift_ref, inv_ref, h_ref):
    _inv_rms_rows(x_ref, inv_ref)

    def body(r):
        h_ref[r, :] = (x_ref[r, :] * inv_ref[r, :] * gain_ref[...] + shift_ref[...]).astype(h_ref.dtype)
    _row_chunks(x_ref.shape[0], body)


def _postnorm_residual_rows(y_ref, x_ref, gain_ref, inv_ref, o_ref):
    _inv_rms_rows(y_ref, inv_ref)

    def body(r):
        o_ref[r, :] = x_ref[r, :] + y_ref[r, :] * inv_ref[r, :] * gain_ref[...]
    _row_chunks(y_ref.shape[0], body)


def _ffn_kernel(res_w, tail, x_ref, sh_ref, sc_ref, gt_ref, gpre_ref, gpost_ref, wg_ref, wu_ref, wo_ref,
                o_ref, h_scr, gain_scr, inv_scr):
    f = pl.program_id(2)
    last = pl.num_programs(2) - 1
    tf = wo_ref.shape[1]

    def step(lo, first):
        h = h_scr[...]
        act = _silu(_dot(h, wg_ref[0, :, lo:])) * _dot(h, wu_ref[0, :, lo:])
        out = _dot(act.astype(BF16), wo_ref[0, lo:, :])
        if first:
            o_ref[...] = out
        else:
            o_ref[...] += out

    @pl.when(f == 0)
    def _():
        gain_scr[...] = gpre_ref[...] * (1.0 + sc_ref[...])
        _prenorm_mod_rows(x_ref, gain_scr, sh_ref, inv_scr, h_scr)
        step(0, True)

    @pl.when((f > 0) & (f < last))
    def _():
        step(0, False)

    @pl.when(f == last)
    def _():
        step(tf - tail, False)
        gain_scr[...] = res_w * gt_ref[...] * gpost_ref[...]
        _postnorm_residual_rows(o_ref, x_ref, gain_scr, inv_scr, o_ref)


def _ffn_block(x, mod, norm_g, layer, sub, w_in, w_out, res_w, tm, tf):
    b, s, d = x.shape
    d_ff = w_out.shape[1]
    nf = pl.cdiv(d_ff, tf)
    tail = d_ff - (nf - 1) * tf
    assert tail % LANES == 0 and d_ff >= tf and nf >= 2
    mk = 3 * sub
    start = lambda j, base: pl.multiple_of(base + jnp.minimum(j * tf, d_ff - tf), LANES)
    col = lambda base: (lambda b, i, j: (layer, 0, start(j, base)))
    one = pl.Element(1)
    return pl.pallas_call(
        functools.partial(_ffn_kernel, res_w, tail),
        grid=(b, s // tm, nf),
        in_specs=[
            pl.BlockSpec((None, tm, d), lambda b, i, j: (b, i, 0)),
            _mod_spec(layer, mk, d, 3), _mod_spec(layer, mk + 1, d, 3), _mod_spec(layer, mk + 2, d, 3),
            _gain_spec(layer, 2 * sub, d, 3), _gain_spec(layer, 2 * sub + 1, d, 3),
            pl.BlockSpec((one, pl.Element(d), pl.Element(tf)), col(0)),
            pl.BlockSpec((one, pl.Element(d), pl.Element(tf)), col(d_ff)),
            pl.BlockSpec((one, pl.Element(tf), pl.Element(d)), lambda b, i, j: (layer, start(j, 0), 0)),
        ],
        out_specs=pl.BlockSpec((None, tm, d), lambda b, i, j: (b, i, 0)),
        out_shape=jax.ShapeDtypeStruct((b, s, d), F32),
        scratch_shapes=[pltpu.VMEM((tm, d), BF16), pltpu.VMEM((1, d), F32), pltpu.VMEM((tm, 1), F32)],
        compiler_params=pltpu.CompilerParams(
            dimension_semantics=("parallel", "parallel", "arbitrary"), vmem_limit_bytes=FFN_VMEM_LIMIT),
        name=f"ffn_l{layer}_s{sub}",
    )(x, mod, mod, mod, norm_g, norm_g, w_in, w_in, w_out)


def _mix_in_kernel(n16, x_ref, sh_ref, sc_ref, gpre_ref, w_ref, o16_ref, o32_ref, h_scr, gain_scr, inv_scr):
    j = pl.program_id(2)

    @pl.when(j == 0)
    def _():
        gain_scr[...] = gpre_ref[...] * (1.0 + sc_ref[...])
        _prenorm_mod_rows(x_ref, gain_scr, sh_ref, inv_scr, h_scr)
        o16_ref[...] = _dot(h_scr[...], w_ref[...]).astype(o16_ref.dtype)

    @pl.when((j > 0) & (j < n16))
    def _():
        o16_ref[...] = _dot(h_scr[...], w_ref[...]).astype(o16_ref.dtype)

    @pl.when(j >= n16)
    def _():
        o32_ref[...] = _dot(h_scr[...], w_ref[...])


def _mix_in(x, mod, norm_g, layer, w, tm, tn):
    b, s, d = x.shape
    assert MIX_BF16_WIDTH % tn == 0 and MIX_F32_WIDTH % tn == 0
    n16, n32 = MIX_BF16_WIDTH // tn, MIX_F32_WIDTH // tn
    return pl.pallas_call(
        functools.partial(_mix_in_kernel, n16),
        grid=(b, s // tm, n16 + n32),
        in_specs=[
            pl.BlockSpec((None, tm, d), lambda b, i, j: (b, i, 0)),
            _mod_spec(layer, 3, d, 3), _mod_spec(layer, 4, d, 3),
            _gain_spec(layer, 2, d, 3),
            pl.BlockSpec((None, d, tn), lambda b, i, j: (layer, 0, j)),
        ],
        out_specs=[
            pl.BlockSpec((None, tm, tn), lambda b, i, j: (b, i, jnp.minimum(j, n16 - 1))),
            pl.BlockSpec((None, tm, tn), lambda b, i, j: (b, i, jnp.maximum(j - n16, 0))),
        ],
        out_shape=[jax.ShapeDtypeStruct((b, s, MIX_BF16_WIDTH), BF16),
                   jax.ShapeDtypeStruct((b, s, MIX_F32_WIDTH), F32)],
        scratch_shapes=[pltpu.VMEM((tm, d), BF16), pltpu.VMEM((1, d), F32), pltpu.VMEM((tm, 1), F32)],
        compiler_params=pltpu.CompilerParams(
            dimension_semantics=("parallel", "parallel", "arbitrary"), vmem_limit_bytes=VMEM_LIMIT),
        name=f"mix_in_l{layer}",
    )(x, mod, mod, norm_g, w)


def _lane_halves(x, first_half_holds_data):
    lane = lax.broadcasted_iota(jnp.int32, x.shape, 1)
    if first_half_holds_data:
        lo = jnp.where(lane < HEAD_DIM, x, 0.0)
        return lo, pltpu.roll(lo, HEAD_DIM, 1)
    hi = jnp.where(lane >= HEAD_DIM, x, 0.0)
    return pltpu.roll(hi, HEAD_DIM, 1), hi


def _swa_kernel(sinks_ref, q_ref, kvc_ref, kvp_ref, o_ref):
    n = pl.program_id(1)
    w = BLOCK
    bands = SWA_HEADS // SWA_KV_HEADS // 2
    rows = bands * w
    r = lax.broadcasted_iota(jnp.int32, (rows, w), 0)
    j = lax.broadcasted_iota(jnp.int32, (rows, w), 1)
    i = r & (w - 1)
    from_prev = j > i
    prev_f = jnp.where(from_prev, 1.0, 0.0)
    dist = i - j + jnp.where(from_prev, w, 0)
    no_key = jnp.where(from_prev & (n * w - w + j < 0), -jnp.inf, 0.0)
    band_step = _alibi_slope(2, SWA_HEADS) / _alibi_slope(0, SWA_HEADS)
    band_scale = jnp.ones((rows, w), F32)
    for band in range(1, bands):
        band_scale = jnp.where(r >= band * w, band_step ** band, band_scale)
    dist_scaled = dist.astype(F32) * band_scale
    band_col = lax.broadcasted_iota(jnp.int32, (rows, 1), 0)
    for bi in range(q_ref.shape[0]):
        kcat = jnp.concatenate([kvp_ref[bi, :, :KV_WIDTH], kvc_ref[bi, :, :KV_WIDTH]], axis=0)
        vcat = jnp.concatenate([kvp_ref[bi, :, KV_WIDTH:], kvc_ref[bi, :, KV_WIDTH:]], axis=0)
        for hk in range(SWA_KV_HEADS):
            k_halves = [t.astype(BF16) for t in _lane_halves(kcat, hk == 0)]
            v_halves = [t.astype(BF16) for t in _lane_halves(vcat, hk == 0)]
            q = jnp.concatenate([q_ref[bi, :, (bands * hk + band) * LANES:(bands * hk + band + 1) * LANES]
                                 for band in range(bands)], axis=0)
            q = (q * HEAD_DIM ** -0.5).astype(BF16)
            out = None
            for par in range(2):
                head0 = 2 * bands * hk + par
                sink = jnp.full((rows, 1), sinks_ref[head0], F32)
                for band in range(1, bands):
                    sink = jnp.where(band_col >= band * w, sinks_ref[head0 + 2 * band], sink)
                k_all, v_all = k_halves[par], v_halves[par]
                sc = jnp.where(from_prev, _dot_nt(q, k_all[:w]), _dot_nt(q, k_all[w:]))
                sc = sc - _alibi_slope(head0, SWA_HEADS) * dist_scaled + no_key
                m = jnp.maximum(jnp.max(sc, axis=-1, keepdims=True), sink)
                e = jnp.exp(sc - m)
                p = e / (jnp.sum(e, axis=-1, keepdims=True) + jnp.exp(sink - m))
                p_prev = p * prev_f
                pv = _dot(p_prev.astype(BF16), v_all[:w]) + _dot((p - p_prev).astype(BF16), v_all[w:])
                out = pv if out is None else out + pv
            for band in range(bands):
                tile = bands * hk + band
                o_ref[bi, :, tile * LANES:(tile + 1) * LANES] = out[band * w:(band + 1) * w].astype(o_ref.dtype)


SWA_BATCH = 8


def _swa(proj16, proj32, sinks, layer):
    b, s, _ = proj16.shape
    nb = s // BLOCK
    rb = math.gcd(b, SWA_BATCH)
    kv_blk = COL32_AKV // (2 * KV_WIDTH)
    return pl.pallas_call(
        _swa_kernel,
        grid=(b // rb, nb),
        in_specs=[
            pl.BlockSpec(memory_space=pltpu.SMEM),
            pl.BlockSpec((rb, BLOCK, A_WIDTH), lambda b, n: (b, n, COL_AQ // A_WIDTH)),
            pl.BlockSpec((rb, BLOCK, 2 * KV_WIDTH), lambda b, n: (b, n, kv_blk)),
            pl.BlockSpec((rb, BLOCK, 2 * KV_WIDTH), lambda b, n: (b, jnp.maximum(n - 1, 0), kv_blk)),
        ],
        out_specs=pl.BlockSpec((rb, BLOCK, A_WIDTH), lambda b, n: (b, n, 0)),
        out_shape=jax.ShapeDtypeStruct((b, s, A_WIDTH), BF16),
        compiler_params=pltpu.CompilerParams(
            dimension_semantics=("parallel", "arbitrary"), vmem_limit_bytes=VMEM_LIMIT),
        name=f"swa_l{layer}",
    )(sinks, proj16, proj32, proj32)


def _dsa_kernel(topk, q_ref, iq_ref, kv_ref, ikw_ref, kvn_ref, wuk_ref, wuv_ref, o_ref,
                ckv_scr, iklo_scr, ikhi_scr, score_scr, bias_scr, qlat_scr, s_scr, mrun_scr, m_scr,
                lpart_scr, acc_scr):
    n = pl.program_id(1)
    qb = DSA_QBLOCK
    kc = DSA_KCHUNK
    nchunks = lax.div(n * qb + qb - 1, kc) + 1

    @pl.when(n == 0)
    def _():
        ckv_scr[...] = _rms(kv_ref[...], kvn_ref[...]).astype(BF16)
        lo, hi = _lane_halves(ikw_ref[...], True)
        iklo_scr[...] = lo.astype(BF16)
        ikhi_scr[...] = hi.astype(BF16)

    row0 = pl.multiple_of(n * qb, qb)

    iw_t = ikw_ref[pl.ds(row0, qb), :].T
    iw_scale = IDX_HEADS ** -0.5 * IDX_DIM ** -0.5
    w_rows = [iw_t[IDX_DIM + h:IDX_DIM + h + 1, :] * iw_scale for h in range(IDX_HEADS)]

    key_row = lax.broadcasted_iota(jnp.int32, (kc, qb), 0)
    t_pos = n * qb + lax.broadcasted_iota(jnp.int32, (kc, qb), 1)

    def score_chunk(c, carry):
        off = pl.multiple_of(c * kc, kc)
        ik_halves = (iklo_scr[pl.ds(off, kc), :], ikhi_scr[pl.ds(off, kc), :])
        sc = jnp.zeros((kc, qb), F32)
        for h in range(IDX_HEADS):
            iq = iq_ref[:, (h // 2) * LANES:(h // 2 + 1) * LANES].astype(BF16)
            sc = sc + w_rows[h] * jnp.maximum(_dot_nt(ik_halves[h % 2], iq), 0.0)
        score_scr[c] = jnp.where(off + key_row <= t_pos, sc, -jnp.inf)
        return carry

    lax.fori_loop(0, nchunks, score_chunk, 0)

    @pl.when(lax.rem(nchunks, 2) == 1)
    def _():
        score_scr[nchunks] = jnp.full((kc, qb), -jnp.inf, F32)

    npairs = lax.div(nchunks + 1, 2)

    def reduce_chunks(fn, init):
        def body(i, carry):
            for u in range(2):
                carry = fn(score_scr[2 * i + u], carry)
            return carry
        return lax.fori_loop(0, npairs, body, init)

    sub = 8

    def fold_rows(x, op):
        out = x[:sub]
        for g in range(1, x.shape[0] // sub):
            out = op(out, x[g * sub:(g + 1) * sub])
        return out

    def fold_lanes(x, op):
        out = x[:, :LANES]
        for t in range(1, x.shape[1] // LANES):
            out = op(out, x[:, t * LANES:(t + 1) * LANES])
        return out

    def count_ge(thr, strict=False):
        def body(x, cnt):
            hit = (x > thr) if strict else (x >= thr)
            return cnt + fold_rows(jnp.where(hit, 1.0, 0.0), jnp.add)
        return jnp.sum(reduce_chunks(body, jnp.zeros((sub, qb), F32)), axis=0, keepdims=True)

    def max_at_most(bound):
        def body(x, mx):
            return jnp.maximum(mx, fold_rows(jnp.where(x <= bound, x, -jnp.inf), jnp.maximum))
        return jnp.max(reduce_chunks(body, jnp.full((sub, qb), -jnp.inf, F32)), axis=0, keepdims=True)

    def store_bias(c, t, bias_t):
        for g in range(qb // LANES):
            bias_scr[c, g * LANES:(g + 1) * LANES, t * LANES:(t + 1) * LANES] = (
                bias_t[:, g * LANES:(g + 1) * LANES].T)

    @pl.when(n * qb + qb <= topk)
    def _():
        def body(c, carry):
            x = score_scr[c]
            for t in range(kc // LANES):
                store_bias(c, t, jnp.where(x[t * LANES:(t + 1) * LANES] == -jnp.inf, NEG, 0.0))
            return carry
        lax.fori_loop(0, nchunks, body, 0)

    @pl.when(n * qb + qb > topk)
    def _():
        kf = float(topk)

        def minmax(x, carry):
            mn, mx = carry
            mn = jnp.minimum(mn, fold_rows(jnp.where(x == -jnp.inf, jnp.inf, x), jnp.minimum))
            return mn, jnp.maximum(mx, fold_rows(x, jnp.maximum))

        lo, hi = reduce_chunks(minmax, (jnp.full((sub, qb), jnp.inf, F32), jnp.full((sub, qb), -jnp.inf, F32)))
        lo = jnp.min(lo, axis=0, keepdims=True)
        hi = jnp.max(hi, axis=0, keepdims=True)

        def bisect(_, carry):
            lo, hi = carry
            mid = 0.5 * (lo + hi)
            ge = count_ge(mid) >= kf
            return jnp.where(ge, mid, lo), jnp.where(ge, hi, mid)

        lo, hi = lax.fori_loop(0, BISECT_ITERS, bisect, (lo, hi))

        def walk_cond(carry):
            return carry[1] > 0.0

        def count_and_next(thr):
            def body(x, carry):
                cnt, mx = carry
                ge = x >= thr
                cnt = cnt + fold_rows(jnp.where(ge, 1.0, 0.0), jnp.add)
                return cnt, jnp.maximum(mx, fold_rows(jnp.where(ge, -jnp.inf, x), jnp.maximum))
            cnt, mx = reduce_chunks(body, (jnp.zeros((sub, qb), F32), jnp.full((sub, qb), -jnp.inf, F32)))
            return jnp.sum(cnt, axis=0, keepdims=True), jnp.max(mx, axis=0, keepdims=True)

        def walk(carry):
            thr, _ = carry
            cnt, below = count_and_next(thr)
            short = cnt < kf
            return jnp.where(short, below, thr), jnp.max(jnp.where(short, 1.0, 0.0))

        thr, _ = lax.while_loop(walk_cond, walk, (max_at_most(hi), jnp.float32(1.0)))

        need = kf - count_ge(thr, strict=True)

        r = lax.broadcasted_iota(jnp.int32, (LANES, LANES), 0)
        col = lax.broadcasted_iota(jnp.int32, (LANES, LANES), 1)
        tri = jnp.where(col <= r, 1.0, 0.0).astype(BF16)

        def select(i, seen):
            for u in range(2):
                c = 2 * i + u
                x = score_scr[c]
                for t in range(kc // LANES):
                    xt = x[t * LANES:(t + 1) * LANES]
                    eq = xt == thr
                    eqf = jnp.where(eq, 1.0, 0.0)
                    rank = seen + _dot(tri, eqf.astype(BF16))
                    seen = seen + jnp.sum(eqf, axis=0, keepdims=True)
                    sel = (xt > thr) | (eq & (rank <= need))
                    store_bias(c, t, jnp.where(sel, 0.0, NEG))
            return seen

        lax.fori_loop(0, npairs, select, jnp.zeros((1, qb), F32))

    for h in range(DSA_HEADS):
        tile = h // 2
        rows = slice(h * qb, (h + 1) * qb)
        q = q_ref[:, tile * LANES:(tile + 1) * LANES].astype(BF16)
        qlat_scr[rows, :] = (_dot(q, wuk_ref[h]) * HEAD_DIM ** -0.5).astype(BF16)
    mrun_scr[...] = jnp.full(mrun_scr.shape, NEG, F32)
    key_lane = lax.broadcasted_iota(jnp.int32, (1, kc), 1)

    def logits(c, carry):
        off = pl.multiple_of(c * kc, kc)
        s = _dot_nt(qlat_scr[...], ckv_scr[pl.ds(off, kc), :])
        bias = bias_scr[c]
        key_pos = (off + key_lane).astype(F32)
        for h in range(DSA_HEADS):
            rows = slice(h * qb, (h + 1) * qb)
            sh = s[rows] + (bias + _alibi_slope(h, DSA_HEADS) * key_pos)
            s_scr[c, rows, :] = sh
            mrun_scr[rows, :] = jnp.maximum(mrun_scr[rows, :], fold_lanes(sh, jnp.maximum))
        return carry

    lax.fori_loop(0, nchunks, logits, 0)
    m_scr[...] = jnp.broadcast_to(jnp.max(mrun_scr[...], axis=-1, keepdims=True), m_scr.shape)
    lpart_scr[...] = jnp.zeros_like(lpart_scr)
    acc_scr[...] = jnp.zeros_like(acc_scr)

    def attend(c, carry):
        off = pl.multiple_of(c * kc, kc)
        m = m_scr[...]
        p = [jnp.exp(s_scr[c, :, t * LANES:(t + 1) * LANES] - m) for t in range(kc // LANES)]
        lpart_scr[...] += functools.reduce(jnp.add, p)
        acc_scr[...] += _dot(jnp.concatenate(p, axis=1).astype(BF16), ckv_scr[pl.ds(off, kc), :])
        return carry

    lax.fori_loop(0, nchunks, attend, 0)

    o_all = (acc_scr[...] / jnp.sum(lpart_scr[...], axis=-1, keepdims=True)).astype(BF16)
    for tile in range(DSA_HEADS // 2):
        h = 2 * tile
        out = _dot(o_all[h * qb:(h + 1) * qb], wuv_ref[h]) + _dot(o_all[(h + 1) * qb:(h + 2) * qb], wuv_ref[h + 1])
        o_ref[:, tile * LANES:(tile + 1) * LANES] = out.astype(o_ref.dtype)


def _dsa(proj16, proj32, kv_norm, wuk_pad, wuv_pad, layer):
    b, s, _ = proj16.shape
    qb = DSA_QBLOCK
    nb = s // qb
    topk = min(DSA_TOPK_MAX, s // 4)
    assert topk % qb == 0
    assert (s // DSA_KCHUNK) % 2 == 0
    nck = s // DSA_KCHUNK
    rows = DSA_HEADS * qb
    return pl.pallas_call(
        functools.partial(_dsa_kernel, topk),
        grid=(b, nb),
        in_specs=[
            pl.BlockSpec((None, qb, B_WIDTH), lambda b, n: (b, n, COL_BQ // B_WIDTH)),
            pl.BlockSpec((None, qb, B_WIDTH), lambda b, n: (b, n, COL_BIQ // B_WIDTH)),
            pl.BlockSpec((None, s, DSA_RANK), lambda b, n: (b, 0, COL32_BKV // DSA_RANK)),
            pl.BlockSpec((None, s, LANES), lambda b, n: (b, 0, COL32_BIK // LANES)),
            pl.BlockSpec((1, DSA_RANK), lambda b, n: (0, 0)),
            pl.BlockSpec((DSA_HEADS, LANES, DSA_RANK), lambda b, n: (0, 0, 0)),
            pl.BlockSpec((DSA_HEADS, DSA_RANK, LANES), lambda b, n: (0, 0, 0)),
        ],
        out_specs=pl.BlockSpec((None, qb, B_WIDTH), lambda b, n: (b, n, 0)),
        out_shape=jax.ShapeDtypeStruct((b, s, B_WIDTH), BF16),
        scratch_shapes=[
            pltpu.VMEM((s, DSA_RANK), BF16),
            pltpu.VMEM((s, LANES), BF16),
            pltpu.VMEM((s, LANES), BF16),
            pltpu.VMEM((nck, DSA_KCHUNK, qb), F32),
            pltpu.VMEM((nck, qb, DSA_KCHUNK), F32),
            pltpu.VMEM((rows, DSA_RANK), BF16),
            pltpu.VMEM((nck, rows, DSA_KCHUNK), F32),
            pltpu.VMEM((rows, LANES), F32),
            pltpu.VMEM((rows, LANES), F32),
            pltpu.VMEM((rows, LANES), F32),
            pltpu.VMEM((rows, DSA_RANK), F32),
        ],
        compiler_params=pltpu.CompilerParams(
            dimension_semantics=("parallel", "arbitrary"), vmem_limit_bytes=VMEM_LIMIT),
        name=f"dsa_l{layer}",
    )(proj16, proj16, proj32, proj32, kv_norm, wuk_pad, wuv_pad)


def _ret_kernel(q_ref, k_ref, v_ref, g_ref, gn_ref, o_ref, state_scr):
    cs = BLOCK

    @pl.when(pl.program_id(1) == 0)
    def _():
        state_scr[...] = jnp.zeros_like(state_scr)

    row = lax.broadcasted_iota(jnp.int32, (cs, LANES), 0)
    lane = lax.broadcasted_iota(jnp.int32, (cs, LANES), 1)
    first = lane < HEAD_DIM
    rowf = row.astype(F32)
    diff = (row - lane).astype(F32)
    same_head = (row < HEAD_DIM) == first
    seg_mean = jnp.where(same_head, 1.0 / HEAD_DIM, 0.0).astype(BF16)

    def seg_mean_dot(x):
        x_hi = x.astype(BF16)
        x_lo = (x - x_hi.astype(F32)).astype(BF16)
        return _dot(x_hi, seg_mean) + _dot(x_lo, seg_mean)

    tiles = RET_HEADS // 2
    outs = []
    for bi in range(q_ref.shape[0]):
        for tile in range(tiles):
            cols = slice(tile * LANES, (tile + 1) * LANES)
            lg = [math.log(1.0 - 2.0 ** (-5.0 - (2 * tile + par))) for par in range(2)]
            lg_lane = jnp.where(first, lg[0], lg[1])
            q = q_ref[bi, :, cols].astype(BF16)
            k = k_ref[bi, :, cols].astype(F32) * HEAD_DIM ** -0.5
            v = v_ref[bi, :, cols].astype(F32)
            state = state_scr[bi, tile]
            out = _dot(q, state.astype(BF16)) * jnp.exp(lg_lane * (rowf + 1.0))
            for par in range(2):
                keep = first if par == 0 else ~first
                decay = jnp.where(diff >= 0, jnp.exp(lg[par] * jnp.maximum(diff, 0.0)), 0.0)
                inner = _dot_nt(q, jnp.where(keep, k, 0.0).astype(BF16)) * decay
                out = out + _dot(inner.astype(BF16), jnp.where(keep, v, 0.0).astype(BF16))
            k_dec = (k * jnp.exp(lg_lane * (cs - 1.0 - rowf))).astype(BF16)
            kv = _dot_tn(k_dec, v.astype(BF16))
            state_scr[bi, tile] = state * jnp.exp(lg_lane * cs) + jnp.where(same_head, kv, 0.0)
            outs.append(out)

    out = jnp.concatenate(outs, axis=0)
    mu = seg_mean_dot(out)
    cen = out - mu
    var = seg_mean_dot(cen * cen)
    yn = cen * lax.rsqrt(var + EPS)
    for bi in range(q_ref.shape[0]):
        for tile in range(tiles):
            cols = slice(tile * LANES, (tile + 1) * LANES)
            y = yn[(bi * tiles + tile) * cs:(bi * tiles + tile + 1) * cs] * gn_ref[:, cols]
            o_ref[bi, :, cols] = (y * _silu(g_ref[bi, :, cols])).astype(o_ref.dtype)


RET_BATCH = 8


def _retention(proj16, proj32, ret_norm, layer):
    b, s, _ = proj16.shape
    nc = s // BLOCK
    rb = math.gcd(b, RET_BATCH)
    spec = lambda col: pl.BlockSpec((rb, BLOCK, C_WIDTH), lambda b, c: (b, c, col // C_WIDTH))
    return pl.pallas_call(
        _ret_kernel,
        grid=(b // rb, nc),
        in_specs=[spec(COL_CQ), spec(COL32_CK), spec(COL_CV), spec(COL32_CG),
                  pl.BlockSpec((1, C_WIDTH), lambda b, c: (0, 0))],
        out_specs=pl.BlockSpec((rb, BLOCK, C_WIDTH), lambda b, c: (b, c, 0)),
        out_shape=jax.ShapeDtypeStruct((b, s, C_WIDTH), BF16),
        scratch_shapes=[pltpu.VMEM((rb, RET_HEADS // 2, LANES, LANES), F32)],
        compiler_params=pltpu.CompilerParams(
            dimension_semantics=("parallel", "arbitrary"), vmem_limit_bytes=VMEM_LIMIT),
        name=f"ret_l{layer}",
    )(proj16, proj32, proj16, proj32, ret_norm)


def _mix_out_kernel(x_ref, gt_ref, gpost_ref, oa_ref, ob_ref, oc_ref, w_ref, o_ref, gain_scr, inv_scr):
    y = _dot(oa_ref[...], w_ref[:A_WIDTH, :])
    y = y + _dot(ob_ref[...], w_ref[A_WIDTH:A_WIDTH + B_WIDTH, :])
    o_ref[...] = y + _dot(oc_ref[...], w_ref[A_WIDTH + B_WIDTH:, :])
    gain_scr[...] = gt_ref[...] * gpost_ref[...]
    _postnorm_residual_rows(o_ref, x_ref, gain_scr, inv_scr, o_ref)


def _mix_out(x, mod, norm_g, layer, oa, ob, oc, w, tm):
    b, s, d = x.shape
    row = lambda width: pl.BlockSpec((None, tm, width), lambda b, i: (b, i, 0))
    return pl.pallas_call(
        _mix_out_kernel,
        grid=(b, s // tm),
        in_specs=[
            row(d), _mod_spec(layer, 5, d, 2), _gain_spec(layer, 3, d, 2),
            row(A_WIDTH), row(B_WIDTH), row(C_WIDTH),
            pl.BlockSpec((None,) + w.shape[1:], lambda b, i: (layer, 0, 0)),
        ],
        out_specs=row(d),
        out_shape=jax.ShapeDtypeStruct((b, s, d), F32),
        scratch_shapes=[pltpu.VMEM((1, d), F32), pltpu.VMEM((tm, 1), F32)],
        compiler_params=pltpu.CompilerParams(
            dimension_semantics=("parallel", "parallel"), vmem_limit_bytes=VMEM_LIMIT),
        name=f"mix_out_l{layer}",
    )(x, mod, norm_g, oa, ob, oc, w)


def _prep_mix_in(w):
    w = w.astype(BF16)
    sizes = [A_WIDTH, KV_WIDTH, KV_WIDTH, B_WIDTH, DSA_RANK, IDX_HEADS * IDX_DIM, IDX_DIM, IDX_HEADS,
             C_WIDTH, C_WIDTH, C_WIDTH, C_WIDTH]
    starts = [0]
    for sz in sizes:
        starts.append(starts[-1] + sz)
    aq, ak, av, bq, bkv, biq, bik, biw, cq, ck, cv, cg = [w[..., a:a + sz] for a, sz in zip(starts, sizes)]
    tail = jnp.zeros(w.shape[:-1] + (LANES - IDX_DIM - IDX_HEADS,), w.dtype)
    return jnp.concatenate([aq, bq, biq, cq, cv, cg, ck, ak, av, bkv, bik, biw, tail], axis=-1)


def _prep_dsa_up(w_uk, w_uv):
    r, h, dh = w_uk.shape
    uk = jnp.transpose(w_uk, (1, 2, 0))
    uv = jnp.transpose(w_uv, (1, 0, 2))
    odd = (jnp.arange(h) % 2 == 1)[:, None, None]
    zk = jnp.zeros_like(uk)
    zv = jnp.zeros_like(uv)
    uk_pad = jnp.where(odd, jnp.concatenate([zk, uk], axis=1), jnp.concatenate([uk, zk], axis=1))
    uv_pad = jnp.where(odd, jnp.concatenate([zv, uv], axis=2), jnp.concatenate([uv, zv], axis=2))
    return uk_pad.astype(BF16), uv_pad.astype(BF16)


FFN_TM = 1024
FFN_TF = 512
FFN_VMEM_LIMIT = 60 * 1024 * 1024
MIX_IN_TM = 1024
MIX_IN_TN = 1536
MIX_OUT_TM = 512


def kernel(x, c, ada_w, ada_b, norm_g, ffn1_w_in, ffn1_w_out, ffn2_w_in, ffn2_w_out,
           mix_w_in, mix_w_out, swa_sinks, dsa_kv_norm, dsa_w_uk, dsa_w_uv, ret_norm):
    depth = ada_w.shape[0]
    b, s, d = x.shape
    assert s % FFN_TM == 0 and s % DSA_KCHUNK == 0 and d % LANES == 0
    mod = _ada_mod(c, ada_w, ada_b).reshape(depth, b, N_MOD, 1, d)
    gains = norm_g.reshape(depth, norm_g.shape[1], 1, d)
    ffn1_in, ffn1_out = ffn1_w_in.astype(BF16), ffn1_w_out.astype(BF16)
    ffn2_in, ffn2_out = ffn2_w_in.astype(BF16), ffn2_w_out.astype(BF16)
    mix_out_w = mix_w_out.astype(BF16)
    mix_in_w = _prep_mix_in(mix_w_in)
    for l in range(depth):
        x = _ffn_block(x, mod, gains, l, 0, ffn1_in, ffn1_out, 0.5, FFN_TM, FFN_TF)

        proj16, proj32 = _mix_in(x, mod, gains, l, mix_in_w, MIX_IN_TM, MIX_IN_TN)
        oa = _swa(proj16, proj32, swa_sinks[l], l)
        uk_pad, uv_pad = _prep_dsa_up(dsa_w_uk[l], dsa_w_uv[l])
        ob = _dsa(proj16, proj32, dsa_kv_norm[l].reshape(1, -1), uk_pad, uv_pad, l)
        oc = _retention(proj16, proj32, ret_norm[l].reshape(1, -1), l)
        x = _mix_out(x, mod, gains, l, oa, ob, oc, mix_out_w, MIX_OUT_TM)

        x = _ffn_block(x, mod, gains, l, 2, ffn2_in, ffn2_out, 0.5, FFN_TM, FFN_TF)
    return x
```
